```python
import jax, jax.numpy as jnp
from jax import lax
import numpy as np

D_MODEL = 1024
BATCH = 8
SEQ = 2048
DEPTH = 2

D_FF = 2816
CHUNK = 128
A_HEADS = 4
A_HEAD_DIM = 128
D_A = A_HEADS * A_HEAD_DIM
B_GROUPS = 8
B_GROUP_DIM = 64
D_B = B_GROUPS * B_GROUP_DIM
D_MIX = D_A + D_B
D_IN_AB = 2 * D_A + 3 * D_B
CONV_W = 3
POOL_WINDOWS = (2, 4, 8, 16)
POOL_GROUPS = len(POOL_WINDOWS)
POOL_GROUP_DIM = D_MODEL // POOL_GROUPS
N_SUB = 3
N_EVEN = (DEPTH + 1) // 2
N_ODD = DEPTH // 2
EPS = 1e-6

kernel_name = "hybrid_gmlp_shortconv_pool_macaron_adaln"


def rmsnorm(x, g):
    xf = x.astype(jnp.float32)
    y = xf * lax.rsqrt(jnp.mean(xf * xf, axis=-1, keepdims=True) + EPS)
    return (y * g.astype(jnp.float32)).astype(x.dtype)


def layernorm(x, g):
    xf = x.astype(jnp.float32)
    mu = jnp.mean(xf, axis=-1, keepdims=True)
    var = jnp.mean(jnp.square(xf - mu), axis=-1, keepdims=True)
    y = (xf - mu) * lax.rsqrt(var + EPS)
    return (y * g.astype(jnp.float32)).astype(x.dtype)


def modulate(x, g, mod):
    shift, scale, gate = jnp.split(mod, 3, axis=-1)
    h = rmsnorm(x, g) * (1.0 + scale[:, None, :]) + shift[:, None, :]
    return h, gate[:, None, :]


def swiglu(h, w_in, w_out):
    gu = h @ w_in
    g, u = jnp.split(gu, 2, axis=-1)
    return (jax.nn.silu(g) * u) @ w_out


def spatial_gating(u, v, norm_v, w_s, b_s):
    bsz, s, _ = v.shape
    n_chunks = s // CHUNK
    v = layernorm(v, norm_v)
    vc = v.reshape(bsz, n_chunks, CHUNK, A_HEADS, A_HEAD_DIM)
    mask = jnp.tril(jnp.ones((CHUNK, CHUNK), dtype=w_s.dtype))
    z = jnp.einsum('hts,bnshd->bnthd', w_s * mask[None], vc)
    z = z + jnp.transpose(b_s)[None, None, :, :, None]
    return u * z.reshape(bsz, s, D_A)


def causal_short_conv(x, w):
    s = x.shape[1]
    xp = jnp.pad(x, ((0, 0), (CONV_W - 1, 0), (0, 0)))
    y = w[0] * xp[:, 0:s]
    for k in range(1, CONV_W):
        y = y + w[k] * xp[:, k:k + s]
    return y


def mixer_ab(h, w_in, norm_v, w_s, b_s, conv_w, w_out):
    proj = h @ w_in
    u, v, bg, cg, xb = jnp.split(
        proj, [D_A, 2 * D_A, 2 * D_A + D_B, 2 * D_A + 2 * D_B], axis=-1)
    y_a = spatial_gating(jax.nn.gelu(u), jax.nn.gelu(v), norm_v, w_s, b_s)
    y_b = bg * causal_short_conv(cg * xb, conv_w)
    return jnp.concatenate([y_a, y_b], axis=-1) @ w_out


def mixer_pool(h, w_grp, scale):
    s = h.shape[1]
    cum = jnp.cumsum(h.astype(jnp.float32), axis=1)
    t = jnp.arange(s)
    outs = []
    for i, w in enumerate(POOL_WINDOWS):
        sl = slice(i * POOL_GROUP_DIM, (i + 1) * POOL_GROUP_DIM)
        cg = cum[..., sl]
        prev = jnp.pad(cg, ((0, 0), (w, 0), (0, 0)))[:, :s]
        cnt = jnp.minimum(t + 1, w).astype(jnp.float32)[None, :, None]
        p = ((cg - prev) / cnt).astype(h.dtype) - h[..., sl]
        outs.append(p @ w_grp[i])
    return jnp.concatenate(outs, axis=-1) * scale


def _fwd_setup_inputs(seed: int = 0) -> dict:
    key = jax.random.key(seed)
    ks = jax.random.split(key, 20)
    f32 = jnp.float32
    nrm = lambda k, shape, s: (jax.random.normal(k, shape, f32) * s)
    x = jax.random.normal(ks[0], (BATCH, SEQ, D_MODEL), f32)
    c = jax.random.normal(ks[1], (BATCH, D_MODEL), f32)
    norm_g = 1.0 + nrm(ks[2], (DEPTH, N_SUB, D_MODEL), 0.02)
    w_mod = nrm(ks[3], (DEPTH, D_MODEL, N_SUB * 3 * D_MODEL), 0.5 * D_MODEL ** -0.5)
    b_mod = nrm(ks[4], (DEPTH, N_SUB * 3 * D_MODEL), 0.01)
    w_ffn_in = nrm(ks[5], (DEPTH, 2, D_MODEL, 2 * D_FF), D_MODEL ** -0.5)
    w_ffn_out = nrm(ks[6], (DEPTH, 2, D_FF, D_MODEL), D_FF ** -0.5)
    ab_w_in = nrm(ks[7], (N_EVEN, D_MODEL, D_IN_AB), D_MODEL ** -0.5)
    ab_norm_v = 1.0 + nrm(ks[8], (N_EVEN, D_A), 0.02)
    ab_w_s = nrm(ks[9], (N_EVEN, A_HEADS, CHUNK, CHUNK), CHUNK ** -0.5)
    ab_b_s = 1.0 + nrm(ks[10], (N_EVEN, A_HEADS, CHUNK), 0.02)
    ab_conv_w = nrm(ks[11], (N_EVEN, CONV_W, D_B), CONV_W ** -0.5)
    ab_w_out = nrm(ks[12], (N_EVEN, D_MIX, D_MODEL), D_MIX ** -0.5)
    pool_w_grp = nrm(ks[13], (N_ODD, POOL_GROUPS, POOL_GROUP_DIM, POOL_GROUP_DIM), POOL_GROUP_DIM ** -0.5)
    pool_scale = 1.0 + nrm(ks[14], (N_ODD, D_MODEL), 0.1)
    final_g = 1.0 + nrm(ks[15], (D_MODEL,), 0.02)
    return {"x": x, "c": c, "norm_g": norm_g, "w_mod": w_mod, "b_mod": b_mod,
            "w_ffn_in": w_ffn_in, "w_ffn_out": w_ffn_out,
            "ab_w_in": ab_w_in, "ab_norm_v": ab_norm_v, "ab_w_s": ab_w_s, "ab_b_s": ab_b_s,
            "ab_conv_w": ab_conv_w, "ab_w_out": ab_w_out,
            "pool_w_grp": pool_w_grp, "pool_scale": pool_scale, "final_g": final_g}


def _fwd_reference(x, c, norm_g, w_mod, b_mod, w_ffn_in, w_ffn_out,
              ab_w_in, ab_norm_v, ab_w_s, ab_b_s, ab_conv_w, ab_w_out,
              pool_w_grp, pool_scale, final_g):
    c_act = jax.nn.silu(c)
    for l in range(DEPTH):
        mod = c_act @ w_mod[l] + b_mod[l]
        mod_f1, mod_mx, mod_f2 = jnp.split(mod, N_SUB, axis=-1)
        h, gate = modulate(x, norm_g[l, 0], mod_f1)
        x = x + 0.5 * gate * swiglu(h, w_ffn_in[l, 0], w_ffn_out[l, 0])
        h, gate = modulate(x, norm_g[l, 1], mod_mx)
        if l % 2 == 0:
            j = l // 2
            y = mixer_ab(h, ab_w_in[j], ab_norm_v[j], ab_w_s[j], ab_b_s[j],
                         ab_conv_w[j], ab_w_out[j])
        else:
            j = l // 2
            y = mixer_pool(h, pool_w_grp[j], pool_scale[j])
        x = x + gate * y
        h, gate = modulate(x, norm_g[l, 2], mod_f2)
        x = x + 0.5 * gate * swiglu(h, w_ffn_in[l, 1], w_ffn_out[l, 1])
    return rmsnorm(x, final_g)


import jax as _jax
import jax.numpy as _jnp

TWIN_FORMAT = 'train_step'
FWD_PARAMS = ['x', 'c', 'norm_g', 'w_mod', 'b_mod', 'w_ffn_in', 'w_ffn_out', 'ab_w_in', 'ab_norm_v', 'ab_w_s', 'ab_b_s', 'ab_conv_w', 'ab_w_out', 'pool_w_grp', 'pool_scale', 'final_g']
TWIN_WEIGHTS = ['norm_g', 'w_mod', 'b_mod', 'w_ffn_in', 'w_ffn_out', 'ab_w_in', 'ab_norm_v', 'ab_w_s', 'ab_b_s', 'ab_conv_w', 'ab_w_out', 'pool_w_grp', 'pool_scale', 'final_g']
TWIN_DIFF_INPUT = 'x'
TWIN_INPUTS = ['x', 'c', 'norm_g', 'w_mod', 'b_mod', 'w_ffn_in', 'w_ffn_out', 'ab_w_in', 'ab_norm_v', 'ab_w_s', 'ab_b_s', 'ab_conv_w', 'ab_w_out', 'pool_w_grp', 'pool_scale', 'final_g', 'loss_target', 'm_norm_g', 'm_w_mod', 'm_b_mod', 'm_w_ffn_in', 'm_w_ffn_out', 'm_ab_w_in', 'm_ab_norm_v', 'm_ab_w_s', 'm_ab_b_s', 'm_ab_conv_w', 'm_ab_w_out', 'm_pool_w_grp', 'm_pool_scale', 'm_final_g', 'v_norm_g', 'v_w_mod', 'v_b_mod', 'v_w_ffn_in', 'v_w_ffn_out', 'v_ab_w_in', 'v_ab_norm_v', 'v_ab_w_s', 'v_ab_b_s', 'v_ab_conv_w', 'v_ab_w_out', 'v_pool_w_grp', 'v_pool_scale', 'v_final_g']
TWIN_OUTPUTS = ['loss', 'grad_x', 'grad_norm_g', 'grad_w_mod', 'grad_b_mod', 'grad_w_ffn_in', 'grad_w_ffn_out', 'grad_ab_w_in', 'grad_ab_norm_v', 'grad_ab_w_s', 'grad_ab_b_s', 'grad_ab_conv_w', 'grad_ab_w_out', 'grad_pool_w_grp', 'grad_pool_scale', 'grad_final_g', 'delta_norm_g', 'delta_w_mod', 'delta_b_mod', 'delta_w_ffn_in', 'delta_w_ffn_out', 'delta_ab_w_in', 'delta_ab_norm_v', 'delta_ab_w_s', 'delta_ab_b_s', 'delta_ab_conv_w', 'delta_ab_w_out', 'delta_pool_w_grp', 'delta_pool_scale', 'delta_final_g', 'new_m_norm_g', 'new_m_w_mod', 'new_m_b_mod', 'new_m_w_ffn_in', 'new_m_w_ffn_out', 'new_m_ab_w_in', 'new_m_ab_norm_v', 'new_m_ab_w_s', 'new_m_ab_b_s', 'new_m_ab_conv_w', 'new_m_ab_w_out', 'new_m_pool_w_grp', 'new_m_pool_scale', 'new_m_final_g', 'new_v_norm_g', 'new_v_w_mod', 'new_v_b_mod', 'new_v_w_ffn_in', 'new_v_w_ffn_out', 'new_v_ab_w_in', 'new_v_ab_norm_v', 'new_v_ab_w_s', 'new_v_ab_b_s', 'new_v_ab_conv_w', 'new_v_ab_w_out', 'new_v_pool_w_grp', 'new_v_pool_scale', 'new_v_final_g']
TWIN_LEAF_KINDS = {'loss': 'loss', 'grad_x': 'grad_x', 'grad_norm_g': 'grad_w', 'grad_w_mod': 'grad_w', 'grad_b_mod': 'grad_w', 'grad_w_ffn_in': 'grad_w', 'grad_w_ffn_out': 'grad_w', 'grad_ab_w_in': 'grad_w', 'grad_ab_norm_v': 'grad_w', 'grad_ab_w_s': 'grad_w', 'grad_ab_b_s': 'grad_w', 'grad_ab_conv_w': 'grad_w', 'grad_ab_w_out': 'grad_w', 'grad_pool_w_grp': 'grad_w', 'grad_pool_scale': 'grad_w', 'grad_final_g': 'grad_w', 'delta_norm_g': 'delta_w', 'delta_w_mod': 'delta_w', 'delta_b_mod': 'delta_w', 'delta_w_ffn_in': 'delta_w', 'delta_w_ffn_out': 'delta_w', 'delta_ab_w_in': 'delta_w', 'delta_ab_norm_v': 'delta_w', 'delta_ab_w_s': 'delta_w', 'delta_ab_b_s': 'delta_w', 'delta_ab_conv_w': 'delta_w', 'delta_ab_w_out': 'delta_w', 'delta_pool_w_grp': 'delta_w', 'delta_pool_scale': 'delta_w', 'delta_final_g': 'delta_w', 'new_m_norm_g': 'new_m', 'new_m_w_mod': 'new_m', 'new_m_b_mod': 'new_m', 'new_m_w_ffn_in': 'new_m', 'new_m_w_ffn_out': 'new_m', 'new_m_ab_w_in': 'new_m', 'new_m_ab_norm_v': 'new_m', 'new_m_ab_w_s': 'new_m', 'new_m_ab_b_s': 'new_m', 'new_m_ab_conv_w': 'new_m', 'new_m_ab_w_out': 'new_m', 'new_m_pool_w_grp': 'new_m', 'new_m_pool_scale': 'new_m', 'new_m_final_g': 'new_m', 'new_v_norm_g': 'new_v', 'new_v_w_mod': 'new_v', 'new_v_b_mod': 'new_v', 'new_v_w_ffn_in': 'new_v', 'new_v_w_ffn_out': 'new_v', 'new_v_ab_w_in': 'new_v', 'new_v_ab_norm_v': 'new_v', 'new_v_ab_w_s': 'new_v', 'new_v_ab_b_s': 'new_v', 'new_v_ab_conv_w': 'new_v', 'new_v_ab_w_out': 'new_v', 'new_v_pool_w_grp': 'new_v', 'new_v_pool_scale': 'new_v', 'new_v_final_g': 'new_v'}


def _forward(args):
    return _fwd_reference(*[args[k] for k in FWD_PARAMS])


def _output_shape():
    out = _jax.eval_shape(lambda: _forward(_fwd_setup_inputs(0)))
    return out.shape, out.dtype

N_MICROBATCH = 1
ADAM_LR = 0.001
ADAM_B1 = 0.9
ADAM_B2 = 0.999
ADAM_EPS = 1e-08
ADAM_WD = 0.01
ADAM_STEP = 10
PER_EXAMPLE_BATCH_AXIS = {'x': 0, 'c': 0, 'loss_target': 0}
SHARED_INPUTS = []
_WEIGHT_DTYPES = {'norm_g': _jnp.float32, 'w_mod': _jnp.float32, 'b_mod': _jnp.float32, 'w_ffn_in': _jnp.float32, 'w_ffn_out': _jnp.float32, 'ab_w_in': _jnp.float32, 'ab_norm_v': _jnp.float32, 'ab_w_s': _jnp.float32, 'ab_b_s': _jnp.float32, 'ab_conv_w': _jnp.float32, 'ab_w_out': _jnp.float32, 'pool_w_grp': _jnp.float32, 'pool_scale': _jnp.float32, 'final_g': _jnp.float32}
MOMENT_SCALE = {'norm_g': 3.220391e-02, 'w_mod': 3.419234e-02, 'b_mod': 5.711524e-02, 'w_ffn_in': 8.191901e-03, 'w_ffn_out': 1.335633e-02, 'ab_w_in': 4.085740e-02, 'ab_norm_v': 1.825645e-02, 'ab_w_s': 1.870293e-02, 'ab_b_s': 2.661307e-02, 'ab_conv_w': 4.827121e-02, 'ab_w_out': 4.043877e-02, 'pool_w_grp': 3.245126e-02, 'pool_scale': 5.083530e-02, 'final_g': 1.607915e+01}


def _to_microbatches(a, axis):
    t = _jnp.moveaxis(a, axis, 0)
    t = t.reshape((N_MICROBATCH, t.shape[0] // N_MICROBATCH) + t.shape[1:])
    return _jnp.moveaxis(t, 1, axis + 1)


def setup_inputs(seed: int = 0) -> dict:
    inp = _fwd_setup_inputs(seed)
    key = _jax.random.fold_in(_jax.random.key(seed), 7919)
    shape, _ = _output_shape()
    out = dict(inp)
    out["loss_target"] = _jax.random.normal(_jax.random.fold_in(key, 0), shape, _jnp.float32)
    for i, name in enumerate(TWIN_WEIGHTS):
        w = inp[name].astype(_jnp.float32)
        if MOMENT_SCALE is None:
            s = _jnp.sqrt(_jnp.mean(_jnp.square(w)) + 1e-30)
        else:
            s = MOMENT_SCALE[name]
        km, kv = _jax.random.split(_jax.random.fold_in(key, i + 1))
        out[name] = w
        out["m_" + name] = s * _jax.random.normal(km, w.shape, _jnp.float32)
        out["v_" + name] = (s * s) * _jax.random.uniform(kv, w.shape, _jnp.float32, 0.5, 1.5)
    if N_MICROBATCH > 1:
        for name, axis in PER_EXAMPLE_BATCH_AXIS.items():
            out[name] = _to_microbatches(out[name], axis)
    return {'x': out['x'], 'c': out['c'], 'norm_g': out['norm_g'], 'w_mod': out['w_mod'], 'b_mod': out['b_mod'], 'w_ffn_in': out['w_ffn_in'], 'w_ffn_out': out['w_ffn_out'], 'ab_w_in': out['ab_w_in'], 'ab_norm_v': out['ab_norm_v'], 'ab_w_s': out['ab_w_s'], 'ab_b_s': out['ab_b_s'], 'ab_conv_w': out['ab_conv_w'], 'ab_w_out': out['ab_w_out'], 'pool_w_grp': out['pool_w_grp'], 'pool_scale': out['pool_scale'], 'final_g': out['final_g'], 'loss_target': out['loss_target'], 'm_norm_g': out['m_norm_g'], 'm_w_mod': out['m_w_mod'], 'm_b_mod': out['m_b_mod'], 'm_w_ffn_in': out['m_w_ffn_in'], 'm_w_ffn_out': out['m_w_ffn_out'], 'm_ab_w_in': out['m_ab_w_in'], 'm_ab_norm_v': out['m_ab_norm_v'], 'm_ab_w_s': out['m_ab_w_s'], 'm_ab_b_s': out['m_ab_b_s'], 'm_ab_conv_w': out['m_ab_conv_w'], 'm_ab_w_out': out['m_ab_w_out'], 'm_pool_w_grp': out['m_pool_w_grp'], 'm_pool_scale': out['m_pool_scale'], 'm_final_g': out['m_final_g'], 'v_norm_g': out['v_norm_g'], 'v_w_mod': out['v_w_mod'], 'v_b_mod': out['v_b_mod'], 'v_w_ffn_in': out['v_w_ffn_in'], 'v_w_ffn_out': out['v_w_ffn_out'], 'v_ab_w_in': out['v_ab_w_in'], 'v_ab_norm_v': out['v_ab_norm_v'], 'v_ab_w_s': out['v_ab_w_s'], 'v_ab_b_s': out['v_ab_b_s'], 'v_ab_conv_w': out['v_ab_conv_w'], 'v_ab_w_out': out['v_ab_w_out'], 'v_pool_w_grp': out['v_pool_w_grp'], 'v_pool_scale': out['v_pool_scale'], 'v_final_g': out['v_final_g']}


def _loss(weights, diff, rest, loss_target):
    with _jax.named_scope("forward"):
        args = {**rest, TWIN_DIFF_INPUT: diff, **{k: w.astype(_WEIGHT_DTYPES[k]) for k, w in weights.items()}}
        y = _forward(args)
    with _jax.named_scope("loss_head"):
        err = _jnp.square(y.astype(_jnp.float32) - loss_target)
        return 0.5 * _jnp.sum(_jnp.mean(err, axis=-1)) if err.ndim else 0.5 * err


def _adamw(w, g, m, v):
    m = ADAM_B1 * m + (1.0 - ADAM_B1) * g
    v = ADAM_B2 * v + (1.0 - ADAM_B2) * _jnp.square(g)
    m_hat = m / (1.0 - ADAM_B1 ** ADAM_STEP)
    v_hat = v / (1.0 - ADAM_B2 ** ADAM_STEP)
    delta = -ADAM_LR * (m_hat / (_jnp.sqrt(v_hat) + ADAM_EPS) + ADAM_WD * w)
    return delta, m, v


def reference(x, c, norm_g, w_mod, b_mod, w_ffn_in, w_ffn_out, ab_w_in, ab_norm_v, ab_w_s, ab_b_s, ab_conv_w, ab_w_out, pool_w_grp, pool_scale, final_g, loss_target, m_norm_g, m_w_mod, m_b_mod, m_w_ffn_in, m_w_ffn_out, m_ab_w_in, m_ab_norm_v, m_ab_w_s, m_ab_b_s, m_ab_conv_w, m_ab_w_out, m_pool_w_grp, m_pool_scale, m_final_g, v_norm_g, v_w_mod, v_b_mod, v_w_ffn_in, v_w_ffn_out, v_ab_w_in, v_ab_norm_v, v_ab_w_s, v_ab_b_s, v_ab_conv_w, v_ab_w_out, v_pool_w_grp, v_pool_scale, v_final_g):
    given = dict(x=x, c=c, norm_g=norm_g, w_mod=w_mod, b_mod=b_mod, w_ffn_in=w_ffn_in, w_ffn_out=w_ffn_out, ab_w_in=ab_w_in, ab_norm_v=ab_norm_v, ab_w_s=ab_w_s, ab_b_s=ab_b_s, ab_conv_w=ab_conv_w, ab_w_out=ab_w_out, pool_w_grp=pool_w_grp, pool_scale=pool_scale, final_g=final_g, loss_target=loss_target, m_norm_g=m_norm_g, m_w_mod=m_w_mod, m_b_mod=m_b_mod, m_w_ffn_in=m_w_ffn_in, m_w_ffn_out=m_w_ffn_out, m_ab_w_in=m_ab_w_in, m_ab_norm_v=m_ab_norm_v, m_ab_w_s=m_ab_w_s, m_ab_b_s=m_ab_b_s, m_ab_conv_w=m_ab_conv_w, m_ab_w_out=m_ab_w_out, m_pool_w_grp=m_pool_w_grp, m_pool_scale=m_pool_scale, m_final_g=m_final_g, v_norm_g=v_norm_g, v_w_mod=v_w_mod, v_b_mod=v_b_mod, v_w_ffn_in=v_w_ffn_in, v_w_ffn_out=v_w_ffn_out, v_ab_w_in=v_ab_w_in, v_ab_norm_v=v_ab_norm_v, v_ab_w_s=v_ab_w_s, v_ab_b_s=v_ab_b_s, v_ab_conv_w=v_ab_conv_w, v_ab_w_out=v_ab_w_out, v_pool_w_grp=v_pool_w_grp, v_pool_scale=v_pool_scale, v_final_g=v_final_g)
    weights = {n: given[n] for n in TWIN_WEIGHTS}
    shared = {n: given[n] for n in SHARED_INPUTS}
    per_example = {n: given[n] for n in ['x', 'c']}
    grad_fn = _jax.value_and_grad(_loss, argnums=(0, 1))

    def one_microbatch(ex, loss_target):
        ex = dict(ex)
        diff = ex.pop(TWIN_DIFF_INPUT)
        return grad_fn(weights, diff, {**shared, **ex}, loss_target)

    if N_MICROBATCH == 1:
        loss, (grad_w, grad_x) = one_microbatch(per_example, given["loss_target"])
    else:
        def body(carry, xs):
            loss_sum, grad_sum = carry
            l_k, (gw_k, gx_k) = one_microbatch(xs[0], xs[1])
            with _jax.named_scope("update"):
                return (loss_sum + l_k, _jax.tree.map(_jnp.add, grad_sum, gw_k)), gx_k

        init = (_jnp.zeros((), _jnp.float32), _jax.tree.map(_jnp.zeros_like, weights))
        (loss, grad_w), grad_x = _jax.lax.scan(body, init, (per_example, given["loss_target"]))
    with _jax.named_scope("update"):
        delta_w, new_m, new_v = {}, {}, {}
        for n in TWIN_WEIGHTS:
            delta_w[n], new_m[n], new_v[n] = _adamw(weights[n], grad_w[n], given["m_" + n], given["v_" + n])
    return (loss, grad_x, *[grad_w[n] for n in TWIN_WEIGHTS], *[delta_w[n] for n in TWIN_WEIGHTS],
            *[new_m[n] for n in TWIN_WEIGHTS], *[new_v[n] for n in TWIN_WEIGHTS])
```

```python
import functools
import math

import jax
import jax.numpy as jnp
from jax import lax
from jax.experimental import pallas as pl
from jax.experimental.pallas import tpu as pltpu

F32 = jnp.float32
BF16 = jnp.bfloat16

N_DEV = 8
D = 1024
DFF = 2816
HC = 256
NCH = DFF // HC
D_A = 512
D_B = 512
D_AB = 2 * D_A + 3 * D_B
CHUNK = 128
A_HEADS = 4
POOL_G = 256
POOL_HALO = 16
CONV_HALO = 8
EPS = 1e-6
TM = 256
GELU_K = math.sqrt(2.0 / math.pi)
GELU_C = 0.044715

ADAM_LR = 0.001
ADAM_B1 = 0.9
ADAM_B2 = 0.999
ADAM_EPS = 1e-08
ADAM_WD = 0.01
ADAM_STEP = 10

VMEM_LIMIT = 56 * 1024 * 1024
MESH_ID = pl.DeviceIdType.MESH
ANY = pl.BlockSpec(memory_space=pl.ANY)
VMEM_SPEC = pl.BlockSpec(memory_space=pltpu.VMEM)
ARB1 = pltpu.CompilerParams(dimension_semantics=("arbitrary",), vmem_limit_bytes=VMEM_LIMIT)


def _dot_nt(a, b):
    return lax.dot_general(a, b, (((1,), (1,)), ((), ())), preferred_element_type=F32)


def _dot_nn(a, b):
    return lax.dot_general(a, b, (((1,), (0,)), ((), ())), preferred_element_type=F32)


def _dot_tn(a, b):
    return lax.dot_general(a, b, (((0,), (0,)), ((), ())), preferred_element_type=F32)


def _colsum8(v):
    r, n = v.shape
    return jnp.sum(v.reshape(r // 8, 8, n), axis=0)


def _gelu(x):
    return 0.5 * x * (1.0 + jnp.tanh(GELU_K * (x + GELU_C * x * x * x)))


def _gelu_grad(x):
    t = jnp.tanh(GELU_K * (x + GELU_C * x * x * x))
    return 0.5 * (1.0 + t) + 0.5 * x * (1.0 - t * t) * (GELU_K * (1.0 + 3.0 * GELU_C * x * x))


def _mod_rows(p_ref):
    return p_ref[0:1, :], p_ref[1:2, :], p_ref[2:3, :], p_ref[3:4, :]


def _modulate(x, gn, sh, sc):
    r = lax.rsqrt(jnp.mean(x * x, axis=-1, keepdims=True) + EPS)
    return ((x * r) * gn) * (1.0 + sc) + sh


def _modulate_bwd(dh, x, gn, sc, stats):
    r = lax.rsqrt(jnp.mean(x * x, axis=-1, keepdims=True) + EPS)
    xn = x * r
    stats[0] += _colsum8(dh)
    stats[1] += _colsum8(dh * (xn * gn))
    dy0 = dh * (1.0 + sc)
    stats[3] += _colsum8(dy0 * xn)
    dxn = dy0 * gn
    return r * (dxn - xn * jnp.mean(dxn * xn, axis=-1, keepdims=True))


def _stats_out(stats, out_ref):
    rows = [jnp.sum(stats[k], axis=0, keepdims=True) for k in range(4)]
    out_ref[...] = jnp.concatenate(rows + [jnp.zeros((4, stats.shape[-1]), F32)], axis=0)


def _shift_down(v, k, prev):
    n = v.shape[0]
    row = lax.broadcasted_iota(jnp.int32, v.shape, 0)
    out = pltpu.roll(v, k, 0)
    for j in range(k):
        out = jnp.where(row == j, prev[prev.shape[0] - k + j:prev.shape[0] - k + j + 1, :], out)
    return out


def _shift_up(v, k, nxt):
    n = v.shape[0]
    row = lax.broadcasted_iota(jnp.int32, v.shape, 0)
    out = pltpu.roll(v, n - k, 0)
    for j in range(k):
        out = jnp.where(row == n - k + j, nxt[j:j + 1, :], out)
    return out


def _load_rows(w_hbm, sel, dst, sems, base):
    n = dst.shape[0] // N_DEV
    cps = []
    for k in range(N_DEV):
        src = w_hbm.at[k] if sel is None else w_hbm.at[k, sel]
        cps.append(pltpu.make_async_copy(src, dst.at[pl.ds(k * n, n)], sems.at[base + k]))
    return cps


def _my_pos():
    return lax.axis_index("x"), lax.axis_index("y"), lax.axis_index("c")


def _peer(j):
    x, y, c = _my_pos()
    return (1 - x if j & 4 else x, 1 - y if j & 2 else y, 1 - c if j & 1 else c)


def _index(pos):
    return 4 * pos[0] + 2 * pos[1] + pos[2]


def _exchange_small(v, name, with_sum):
    rows = v.shape[0]

    def body(v_ref, *refs):
        if with_sum:
            out_ref, sum_ref, send_sems, recv_sems, local_sem = refs
        else:
            out_ref, send_sems, recv_sems, local_sem = refs
        me = _index(_my_pos())

        def copy(j, slot):
            return pltpu.make_async_remote_copy(
                src_ref=v_ref, dst_ref=out_ref.at[slot], send_sem=send_sems.at[j - 1], recv_sem=recv_sems.at[j - 1],
                device_id=_peer(j), device_id_type=MESH_ID)

        mine = pltpu.make_async_copy(v_ref, out_ref.at[me], local_sem)
        mine.start()
        sends = [copy(j, me) for j in range(1, N_DEV)]
        for cp in sends:
            cp.start()
        for j in range(1, N_DEV):
            copy(j, _index(_peer(j))).wait_recv()
        for cp in sends:
            cp.wait_send()
        mine.wait()
        if with_sum:
            acc = out_ref[0]
            for k in range(1, N_DEV):
                acc = acc + out_ref[k]
            sum_ref[...] = acc

    out_shape = [jax.ShapeDtypeStruct((N_DEV, rows, 128), F32)]
    out_specs = [VMEM_SPEC]
    if with_sum:
        out_shape.append(jax.ShapeDtypeStruct((rows, 128), F32))
        out_specs.append(VMEM_SPEC)
    res = pl.pallas_call(
        body, name=name, out_shape=out_shape, in_specs=[VMEM_SPEC], out_specs=out_specs,
        scratch_shapes=[pltpu.SemaphoreType.DMA((N_DEV - 1,)), pltpu.SemaphoreType.DMA((N_DEV - 1,)),
                        pltpu.SemaphoreType.DMA(())],
    )(v)
    return res if with_sum else res[0]


def _gather_weights(shards):
    n = len(shards)

    def body(*refs):
        ins, outs = refs[:n], refs[n:2 * n]
        send_sems, recv_sems, local_sems = refs[2 * n:]
        me = _index(_my_pos())

        def copy(a, j, slot):
            s = a * (N_DEV - 1) + j - 1
            return pltpu.make_async_remote_copy(
                src_ref=ins[a], dst_ref=outs[a].at[slot], send_sem=send_sems.at[s], recv_sem=recv_sems.at[s],
                device_id=_peer(j), device_id_type=MESH_ID)

        mine = [pltpu.make_async_copy(ins[a], outs[a].at[me], local_sems.at[a]) for a in range(n)]
        sends = [copy(a, j, me) for j in range(1, N_DEV) for a in range(n)]
        for cp in mine + sends:
            cp.start()
        for j in range(1, N_DEV):
            for a in range(n):
                copy(a, j, _index(_peer(j))).wait_recv()
        for cp in sends:
            cp.wait_send()
        for cp in mine:
            cp.wait()

    return pl.pallas_call(
        body, name="gather_weights",
        out_shape=[jax.ShapeDtypeStruct((N_DEV,) + s.shape, s.dtype) for s in shards],
        in_specs=[ANY] * n, out_specs=[ANY] * n,
        scratch_shapes=[pltpu.SemaphoreType.DMA((n * (N_DEV - 1),)), pltpu.SemaphoreType.DMA((n * (N_DEV - 1),)),
                        pltpu.SemaphoreType.DMA((n,))],
    )(*shards)


def _scatter_grads(grads, slots, out_shapes):
    ng, no = len(grads), len(out_shapes)

    def body(*refs):
        ins, outs = refs[:ng], refs[ng:ng + no]
        send_sems, recv_sems, local_sems = refs[ng + no:]
        me = _index(_my_pos())

        def part(g, k):
            if ins[g].ndim == 3:
                n = ins[g].shape[1] // N_DEV
                return ins[g].at[:, pl.ds(pl.multiple_of(k * n, 16), n)]
            n = ins[g].shape[0] // N_DEV
            return ins[g].at[pl.ds(pl.multiple_of(k * n, 16), n)]

        def land(g, k):
            o, f = slots[g]
            return outs[o].at[k] if f is None else outs[o].at[k, f]

        def copy(g, j, to, src_dev):
            s = g * (N_DEV - 1) + j - 1
            return pltpu.make_async_remote_copy(
                src_ref=part(g, to), dst_ref=land(g, src_dev), send_sem=send_sems.at[s], recv_sem=recv_sems.at[s],
                device_id=_peer(j), device_id_type=MESH_ID)

        mine = [pltpu.make_async_copy(part(g, me), land(g, me), local_sems.at[g]) for g in range(ng)]
        sends = [copy(g, j, _index(_peer(j)), me) for j in range(1, N_DEV) for g in range(ng)]
        for cp in mine + sends:
            cp.start()
        for j in range(1, N_DEV):
            for g in range(ng):
                copy(g, j, me, _index(_peer(j))).wait_recv()
        for cp in sends:
            cp.wait_send()
        for cp in mine:
            cp.wait()

    return pl.pallas_call(
        body, name="scatter_grads",
        out_shape=[jax.ShapeDtypeStruct(s, BF16) for s in out_shapes],
        in_specs=[ANY] * ng, out_specs=[ANY] * no,
        scratch_shapes=[pltpu.SemaphoreType.DMA((ng * (N_DEV - 1),)), pltpu.SemaphoreType.DMA((ng * (N_DEV - 1),)),
                        pltpu.SemaphoreType.DMA((ng,))],
    )(*grads)


def _mod_fwd(c_all, w_mod):
    ncol = w_mod.shape[-1]

    def body(c_ref, w_ref, act_ref, out_ref):
        c = c_ref[...]
        act = c * jax.nn.sigmoid(c)
        act_ref[...] = act
        out_ref[0] = _dot_nn(act.astype(BF16), w_ref[0].astype(BF16))

    return pl.pallas_call(
        body, name="mod_fwd", grid=(2,),
        out_shape=[jax.ShapeDtypeStruct((N_DEV, D), F32), jax.ShapeDtypeStruct((2, N_DEV, ncol), F32)],
        in_specs=[pl.BlockSpec((N_DEV, D), lambda l: (0, 0)), pl.BlockSpec((1, D, ncol), lambda l: (l, 0, 0))],
        out_specs=[pl.BlockSpec((N_DEV, D), lambda l: (0, 0)), pl.BlockSpec((1, N_DEV, ncol), lambda l: (l, 0, 0))],
        compiler_params=ARB1,
    )(c_all, w_mod)


def _ffn_fwd(x, p, win_all, wout_all, f):
    s = x.shape[0]
    nt = s // TM

    def body(x_ref, p_ref, win_hbm, wout_hbm, xo_ref, h_ref, g_ref, u_ref, y_ref, win, wout, acc, sems):
        @pl.when(pl.program_id(0) == 0)
        def _():
            cps = _load_rows(win_hbm, f, win, sems, 0) + _load_rows(wout_hbm, f, wout, sems, N_DEV)
            for cp in cps:
                cp.start()
            for cp in cps:
                cp.wait()

        sh, sc, gate, gn = _mod_rows(p_ref)
        x = x_ref[...]
        hb = _modulate(x, gn, sh, sc).astype(BF16)
        h_ref[...] = hb
        acc[...] = jnp.zeros_like(acc)

        def chunk(c, carry):
            off = pl.multiple_of(c * HC, HC)
            g = _dot_nt(hb, win[pl.ds(off, HC), :])
            u = _dot_nt(hb, win[pl.ds(DFF + off, HC), :])
            g_ref[c] = g.astype(BF16)
            u_ref[c] = u.astype(BF16)
            a = (g * jax.nn.sigmoid(g)) * u
            acc[...] += _dot_nn(a.astype(BF16), wout[pl.ds(off, HC), :])
            return carry

        lax.fori_loop(0, NCH, chunk, 0)
        y = acc[...]
        y_ref[...] = y
        xo_ref[...] = x + (0.5 * gate) * y

    tile = pl.BlockSpec((TM, D), lambda i: (i, 0))
    chunks = pl.BlockSpec((NCH, TM, HC), lambda i: (0, i, 0))
    return pl.pallas_call(
        body, name=f"ffn_fwd_{f}", grid=(nt,),
        out_shape=[jax.ShapeDtypeStruct((s, D), F32), jax.ShapeDtypeStruct((s, D), BF16),
                   jax.ShapeDtypeStruct((NCH, s, HC), BF16), jax.ShapeDtypeStruct((NCH, s, HC), BF16),
                   jax.ShapeDtypeStruct((s, D), F32)],
        in_specs=[tile, pl.BlockSpec((8, D), lambda i: (0, 0)), ANY, ANY],
        out_specs=[tile, tile, chunks, chunks, tile],
        scratch_shapes=[pltpu.VMEM((2 * DFF, D), BF16), pltpu.VMEM((DFF, D), BF16), pltpu.VMEM((TM, D), F32),
                        pltpu.SemaphoreType.DMA((2 * N_DEV,))],
        compiler_params=ARB1,
    )(x, p, win_all, wout_all)


def _ffn_bwd(dxo, x, p, g3, u3, y, win_all, wout_all, f):
    s = x.shape[0]
    nt = s // TM

    def body(dxo_ref, x_ref, p_ref, g_ref, u_ref, y_ref, win_hbm, wout_hbm,
             dx_ref, dgu_ref, a_ref, dy_ref, st_ref, win, wout, acc, stats, sems):
        i = pl.program_id(0)

        @pl.when(i == 0)
        def _():
            cps = _load_rows(win_hbm, f, win, sems, 0) + _load_rows(wout_hbm, f, wout, sems, N_DEV)
            for cp in cps:
                cp.start()
            stats[...] = jnp.zeros_like(stats)
            for cp in cps:
                cp.wait()

        sh, sc, gate, gn = _mod_rows(p_ref)
        x = x_ref[...]
        dxo = dxo_ref[...]
        dyb = ((0.5 * gate) * dxo).astype(BF16)
        dy_ref[...] = dyb
        stats[2] += _colsum8((0.5 * dxo) * y_ref[...])
        acc[...] = jnp.zeros_like(acc)

        def chunk(c, carry):
            off = pl.multiple_of(c * HC, HC)
            da = _dot_nt(dyb, wout[pl.ds(off, HC), :])
            g = g_ref[c].astype(F32)
            u = u_ref[c].astype(F32)
            sg = jax.nn.sigmoid(g)
            si = g * sg
            dg = ((da * u) * (sg * (1.0 + g * (1.0 - sg)))).astype(BF16)
            du = (da * si).astype(BF16)
            a_ref[c] = (si * u).astype(BF16)
            dgu_ref[c] = dg
            dgu_ref[NCH + c] = du
            acc[...] += _dot_nn(dg, win[pl.ds(off, HC), :]) + _dot_nn(du, win[pl.ds(DFF + off, HC), :])
            return carry

        lax.fori_loop(0, NCH, chunk, 0)
        dx_ref[...] = dxo + _modulate_bwd(acc[...], x, gn, sc, stats)

        @pl.when(i == nt - 1)
        def _():
            _stats_out(stats, st_ref)

    tile = pl.BlockSpec((TM, D), lambda i: (i, 0))
    chunks = pl.BlockSpec((NCH, TM, HC), lambda i: (0, i, 0))
    small = pl.BlockSpec((8, D), lambda i: (0, 0))
    return pl.pallas_call(
        body, name=f"ffn_bwd_{f}", grid=(nt,),
        out_shape=[jax.ShapeDtypeStruct((s, D), F32), jax.ShapeDtypeStruct((2 * NCH, s, HC), BF16),
                   jax.ShapeDtypeStruct((NCH, s, HC), BF16), jax.ShapeDtypeStruct((s, D), BF16),
                   jax.ShapeDtypeStruct((8, D), F32)],
        in_specs=[tile, tile, small, chunks, chunks, tile, ANY, ANY],
        out_specs=[tile, pl.BlockSpec((2 * NCH, TM, HC), lambda i: (0, i, 0)), chunks, tile, small],
        scratch_shapes=[pltpu.VMEM((2 * DFF, D), BF16), pltpu.VMEM((DFF, D), BF16), pltpu.VMEM((TM, D), F32),
                        pltpu.VMEM((4, 8, D), F32), pltpu.SemaphoreType.DMA((2 * N_DEV,))],
        compiler_params=ARB1,
    )(dxo, x, p, g3, u3, y, win_all, wout_all)


def _wgrad(lhs3, rhs, name):
    nj, s, _ = lhs3.shape
    n = rhs.shape[1]

    def body(l_ref, r_ref, o_ref):
        o_ref[...] = _dot_tn(l_ref[0], r_ref[...]).astype(BF16)

    return pl.pallas_call(
        body, name=name, grid=(nj,),
        out_shape=jax.ShapeDtypeStruct((nj * HC, n), BF16),
        in_specs=[pl.BlockSpec((1, s, HC), lambda j: (j, 0, 0)), pl.BlockSpec((s, n), lambda j: (0, 0))],
        out_specs=pl.BlockSpec((HC, n), lambda j: (j, 0)),
        compiler_params=ARB1,
    )(lhs3, rhs)


def _gating(proj, nv, ws_ref, bst):
    u, v = proj[:, 0:D_A], proj[:, D_A:2 * D_A]
    gu, gv = _gelu(u), _gelu(v)
    mu = jnp.mean(gv, axis=-1, keepdims=True)
    dv = gv - mu
    rstd = lax.rsqrt(jnp.mean(dv * dv, axis=-1, keepdims=True) + EPS)
    vhat = dv * rstd
    vn = vhat * nv
    r = lax.broadcasted_iota(jnp.int32, (CHUNK, CHUNK), 0)
    c = lax.broadcasted_iota(jnp.int32, (CHUNK, CHUNK), 1)
    wm = [jnp.where(r >= c, ws_ref[hd], 0.0).astype(BF16) for hd in range(A_HEADS)]
    vnb = vn.astype(BF16)
    rows = []
    for n in range(proj.shape[0] // CHUNK):
        blocks = []
        for hd in range(A_HEADS):
            blk = vnb[n * CHUNK:(n + 1) * CHUNK, hd * CHUNK:(hd + 1) * CHUNK]
            blocks.append(_dot_nn(wm[hd], blk) + bst[:, hd:hd + 1])
        rows.append(jnp.concatenate(blocks, axis=1))
    z = jnp.concatenate(rows, axis=0)
    return u, v, gu, rstd, vhat, vnb, wm, z


def _conv(proj, cw, prev_xp):
    bg = proj[:, 2 * D_A:2 * D_A + D_B]
    cg = proj[:, 2 * D_A + D_B:2 * D_A + 2 * D_B]
    xb = proj[:, 2 * D_A + 2 * D_B:]
    xp = cg * xb
    x1 = _shift_down(xp, 1, prev_xp)
    x2 = _shift_down(xp, 2, prev_xp)
    conv = cw[0:1, :] * x2 + cw[1:2, :] * x1 + cw[2:3, :] * xp
    return bg, cg, xb, xp, x1, x2, conv


def _ab_fwd(x, p, abin_all, about_all, nv, ws, bst, cw):
    s = x.shape[0]
    nt = s // TM

    def body(x_ref, p_ref, abin_hbm, about_hbm, nv_ref, ws_ref, bst_ref, cw_ref,
             xo_ref, h_ref, proj_ref, out_ref, abin, about, prev, sems):
        @pl.when(pl.program_id(0) == 0)
        def _():
            cps = _load_rows(abin_hbm, None, abin, sems, 0) + _load_rows(about_hbm, None, about, sems, N_DEV)
            for cp in cps:
                cp.start()
            prev[...] = jnp.zeros_like(prev)
            for cp in cps:
                cp.wait()

        sh, sc, gate, gn = _mod_rows(p_ref)
        x = x_ref[...]
        hb = _modulate(x, gn, sh, sc).astype(BF16)
        h_ref[...] = hb
        proj = _dot_nt(hb, abin[...])
        proj_ref[...] = proj
        _, _, gu, _, _, _, _, z = _gating(proj, nv_ref[...], ws_ref, bst_ref[...])
        bg, _, _, xp, _, _, conv = _conv(proj, cw_ref[...], prev[...])
        prev[...] = xp[TM - CONV_HALO:, :]
        cat = jnp.concatenate([gu * z, bg * conv], axis=1).astype(BF16)
        out = _dot_nn(cat, about[...])
        out_ref[...] = out
        xo_ref[...] = x + gate * out

    tile = pl.BlockSpec((TM, D), lambda i: (i, 0))
    full = lambda a: pl.BlockSpec(a.shape, lambda i: (0,) * a.ndim)
    return pl.pallas_call(
        body, name="ab_fwd", grid=(nt,),
        out_shape=[jax.ShapeDtypeStruct((s, D), F32), jax.ShapeDtypeStruct((s, D), BF16),
                   jax.ShapeDtypeStruct((s, D_AB), F32), jax.ShapeDtypeStruct((s, D), F32)],
        in_specs=[tile, pl.BlockSpec((8, D), lambda i: (0, 0)), ANY, ANY, full(nv), full(ws), full(bst), full(cw)],
        out_specs=[tile, tile, pl.BlockSpec((TM, D_AB), lambda i: (i, 0)), tile],
        scratch_shapes=[pltpu.VMEM((D_AB, D), BF16), pltpu.VMEM((D, D), BF16), pltpu.VMEM((CONV_HALO, D_B), F32),
                        pltpu.SemaphoreType.DMA((2 * N_DEV,))],
        compiler_params=ARB1,
    )(x, p, abin_all, about_all, nv, ws, bst, cw)


def _ab_bwd(dxo, x, p, proj, out, abin_all, about_all, nv, ws, bst, cw):
    s = x.shape[0]
    nt = s // TM
    npj = D_AB // HC

    def body(dxo_ref, x_ref, p_ref, proj_ref, halo_ref, out_ref, abin_hbm, about_hbm, nv_ref, ws_ref, bst_ref, cw_ref,
             dx_ref, dproj_ref, cat_ref, dy_ref, st_ref, dnv_ref, dws_ref, dbs_ref, dcw_ref,
             abin, about, nxt, stats, dnv, dws, dbs, dcw, sems):
        i = pl.program_id(0)
        ti = nt - 1 - i

        @pl.when(i == 0)
        def _():
            cps = _load_rows(abin_hbm, None, abin, sems, 0) + _load_rows(about_hbm, None, about, sems, N_DEV)
            for cp in cps:
                cp.start()
            for z in (nxt, stats, dnv, dws, dbs, dcw):
                z[...] = jnp.zeros_like(z)
            for cp in cps:
                cp.wait()

        sh, sc, gate, gn = _mod_rows(p_ref)
        x = x_ref[...]
        dxo = dxo_ref[...]
        dyb = (gate * dxo).astype(BF16)
        dy_ref[...] = dyb
        stats[2] += _colsum8(dxo * out_ref[...])
        dcat = _dot_nt(dyb, about[...])
        dya, dyb2 = dcat[:, 0:D_A], dcat[:, D_A:]

        proj = proj_ref[...]
        nvv = nv_ref[...]
        u, v, gu, rstd, vhat, vnb, wm, z = _gating(proj, nvv, ws_ref, bst_ref[...])
        dgu = dya * z
        dzb = (dya * gu).astype(BF16)
        dz32 = dya * gu
        rows = []
        for n in range(TM // CHUNK):
            blocks = []
            for hd in range(A_HEADS):
                sl = (slice(n * CHUNK, (n + 1) * CHUNK), slice(hd * CHUNK, (hd + 1) * CHUNK))
                dbs[hd] += dz32[sl]
                dws[hd] += _dot_nt(dzb[sl], vnb[sl])
                blocks.append(_dot_tn(wm[hd], dzb[sl]))
            rows.append(jnp.concatenate(blocks, axis=1))
        dvn = jnp.concatenate(rows, axis=0)
        dnv[...] += _colsum8(dvn * vhat)
        dvh = dvn * nvv
        dgv = rstd * (dvh - jnp.mean(dvh, axis=-1, keepdims=True) - vhat * jnp.mean(dvh * vhat, axis=-1, keepdims=True))
        du = dgu * _gelu_grad(u)
        dv = dgv * _gelu_grad(v)

        halo = halo_ref[...]
        prev_xp = jnp.where(ti > 0, halo[:, 2 * D_A + D_B:2 * D_A + 2 * D_B] * halo[:, 2 * D_A + 2 * D_B:], 0.0)
        cwv = cw_ref[...]
        bg, cg, xb, xp, x1, x2, conv = _conv(proj, cwv, prev_xp)
        dbg = dyb2 * conv
        dconv = dyb2 * bg
        dcw[...] += jnp.concatenate(
            [jnp.sum(_colsum8(dconv * t), axis=0, keepdims=True) for t in (x2, x1, xp)] + [jnp.zeros((5, D_B), F32)], axis=0)
        nx = nxt[...]
        dxp = cwv[2:3, :] * dconv + cwv[1:2, :] * _shift_up(dconv, 1, nx) + cwv[0:1, :] * _shift_up(dconv, 2, nx)
        nxt[...] = dconv[0:CONV_HALO, :]
        dcg = dxp * xb
        dxb = dxp * cg

        dproj = jnp.concatenate([du, dv, dbg, dcg, dxb], axis=1).astype(BF16)
        for k in range(npj):
            dproj_ref[k] = dproj[:, k * HC:(k + 1) * HC]
        cat = jnp.concatenate([gu * z, bg * conv], axis=1).astype(BF16)
        for k in range(D // HC):
            cat_ref[k] = cat[:, k * HC:(k + 1) * HC]
        dh = _dot_nn(dproj, abin[...])
        dx_ref[...] = dxo + _modulate_bwd(dh, x, gn, sc, stats)

        @pl.when(i == nt - 1)
        def _():
            _stats_out(stats, st_ref)
            dnv_ref[...] = jnp.concatenate([jnp.sum(dnv[...], axis=0, keepdims=True), jnp.zeros((7, D_A), F32)], axis=0)
            r = lax.broadcasted_iota(jnp.int32, (CHUNK, CHUNK), 0)
            c = lax.broadcasted_iota(jnp.int32, (CHUNK, CHUNK), 1)
            for hd in range(A_HEADS):
                dws_ref[hd] = jnp.where(r >= c, dws[hd], 0.0)
                dbs_ref[hd] = jnp.broadcast_to(jnp.sum(dbs[hd], axis=1, keepdims=True), (CHUNK, CHUNK))
            dcw_ref[...] = dcw[...]

    rev = pl.BlockSpec((TM, D), lambda i: (nt - 1 - i, 0))
    small = pl.BlockSpec((8, D), lambda i: (0, 0))
    full = lambda a: pl.BlockSpec(a.shape, lambda i: (0,) * a.ndim)
    hpt = TM // CONV_HALO
    fixed = lambda shape: pl.BlockSpec(shape, lambda i: (0,) * len(shape))
    return pl.pallas_call(
        body, name="ab_bwd", grid=(nt,),
        out_shape=[jax.ShapeDtypeStruct((s, D), F32), jax.ShapeDtypeStruct((npj, s, HC), BF16),
                   jax.ShapeDtypeStruct((D // HC, s, HC), BF16), jax.ShapeDtypeStruct((s, D), BF16),
                   jax.ShapeDtypeStruct((8, D), F32), jax.ShapeDtypeStruct((8, D_A), F32),
                   jax.ShapeDtypeStruct((A_HEADS, CHUNK, CHUNK), F32), jax.ShapeDtypeStruct((A_HEADS, CHUNK, CHUNK), F32),
                   jax.ShapeDtypeStruct((8, D_B), F32)],
        in_specs=[rev, rev, small,
                  pl.BlockSpec((TM, D_AB), lambda i: (nt - 1 - i, 0)),
                  pl.BlockSpec((CONV_HALO, D_AB), lambda i: (jnp.maximum((nt - 1 - i) * hpt - 1, 0), 0)),
                  rev, ANY, ANY, full(nv), full(ws), full(bst), full(cw)],
        out_specs=[rev, pl.BlockSpec((npj, TM, HC), lambda i: (0, nt - 1 - i, 0)),
                   pl.BlockSpec((D // HC, TM, HC), lambda i: (0, nt - 1 - i, 0)), rev,
                   small, fixed((8, D_A)), fixed((A_HEADS, CHUNK, CHUNK)), fixed((A_HEADS, CHUNK, CHUNK)), fixed((8, D_B))],
        scratch_shapes=[pltpu.VMEM((D_AB, D), BF16), pltpu.VMEM((D, D), BF16), pltpu.VMEM((CONV_HALO, D_B), F32),
                        pltpu.VMEM((4, 8, D), F32), pltpu.VMEM((8, D_A), F32),
                        pltpu.VMEM((A_HEADS, CHUNK, CHUNK), F32), pltpu.VMEM((A_HEADS, CHUNK, CHUNK), F32),
                        pltpu.VMEM((8, D_B), F32), pltpu.SemaphoreType.DMA((2 * N_DEV,))],
        compiler_params=ARB1,
    )(dxo, x, p, proj, proj, out, abin_all, about_all, nv, ws, bst, cw)


def _pool_counts(first_token, rows):
    t = (first_token + lax.broadcasted_iota(jnp.int32, (rows, 1), 0) + 1).astype(F32)
    lane = lax.broadcasted_iota(jnp.int32, (1, D), 1)
    w = jnp.where(lane < POOL_G, 2.0, jnp.where(lane < 2 * POOL_G, 4.0, jnp.where(lane < 3 * POOL_G, 8.0, 16.0)))
    return jnp.minimum(t, w)


def _window_sums(ext, n_keep, lead, back):
    n = ext.shape[0]
    sh = (lambda v, k: pltpu.roll(v, k, 0)) if back else (lambda v, k: pltpu.roll(v, n - k, 0))
    s2 = ext + sh(ext, 1)
    s4 = s2[:, POOL_G:] + sh(s2[:, POOL_G:], 2)
    s8 = s4[:, POOL_G:] + sh(s4[:, POOL_G:], 4)
    s16 = s8[:, POOL_G:] + sh(s8[:, POOL_G:], 8)
    keep = slice(lead, lead + n_keep)
    return jnp.concatenate([s2[keep, 0:POOL_G], s4[keep, 0:POOL_G], s8[keep, 0:POOL_G], s16[keep, :]], axis=1)


def _pool_fwd(x, p, pool_all, pscale):
    s = x.shape[0]
    nt = s // TM
    ng = D // POOL_G

    def body(x_ref, p_ref, wg_ref, ps_ref, xo_ref, pb_ref, op_ref, prev):
        i = pl.program_id(0)

        @pl.when(i == 0)
        def _():
            prev[...] = jnp.zeros_like(prev)

        sh, sc, gate, gn = _mod_rows(p_ref)
        x = x_ref[...]
        h = _modulate(x, gn, sh, sc)
        win = _window_sums(jnp.concatenate([prev[...], h], axis=0), TM, POOL_HALO, True)
        prev[...] = h[TM - POOL_HALO:, :]
        pb = (win / _pool_counts(i * TM, TM) - h).astype(BF16)
        pb_ref[...] = pb
        op = jnp.concatenate(
            [_dot_nn(pb[:, g * POOL_G:(g + 1) * POOL_G], wg_ref[:, g].reshape(POOL_G, POOL_G)) for g in range(ng)], axis=1)
        op_ref[...] = op
        xo_ref[...] = x + gate * (op * ps_ref[...])

    tile = pl.BlockSpec((TM, D), lambda i: (i, 0))
    return pl.pallas_call(
        body, name="pool_fwd", grid=(nt,),
        out_shape=[jax.ShapeDtypeStruct((s, D), F32), jax.ShapeDtypeStruct((s, D), BF16), jax.ShapeDtypeStruct((s, D), F32)],
        in_specs=[tile, pl.BlockSpec((8, D), lambda i: (0, 0)),
                  pl.BlockSpec(pool_all.shape, lambda i: (0, 0, 0, 0)), pl.BlockSpec((1, D), lambda i: (0, 0))],
        out_specs=[tile, tile, tile],
        scratch_shapes=[pltpu.VMEM((POOL_HALO, D), F32)],
        compiler_params=ARB1,
    )(x, p, pool_all, pscale)


def _pool_bwd(dxo, x, p, pb, op, pool_all, pscale):
    s = x.shape[0]
    nt = s // TM
    ng = D // POOL_G

    def body(dxo_ref, x_ref, p_ref, pb_ref, op_ref, wg_ref, ps_ref,
             dx_ref, st_ref, dps_ref, dwg_ref, nxt, stats, dps, dwg):
        i = pl.program_id(0)
        ti = nt - 1 - i

        @pl.when(i == 0)
        def _():
            for z in (nxt, stats, dps, dwg):
                z[...] = jnp.zeros_like(z)

        sh, sc, gate, gn = _mod_rows(p_ref)
        x = x_ref[...]
        dxo = dxo_ref[...]
        ps = ps_ref[...]
        op = op_ref[...]
        dmo = gate * dxo
        stats[2] += _colsum8(dxo * (op * ps))
        dps[...] += _colsum8(dmo * op)
        dopb = (dmo * ps).astype(BF16)
        pbv = pb_ref[...]
        dps_parts = []
        for g in range(ng):
            sl = slice(g * POOL_G, (g + 1) * POOL_G)
            dps_parts.append(_dot_nt(dopb[:, sl], wg_ref[:, g].reshape(POOL_G, POOL_G)))
            dwg[g] += _dot_tn(pbv[:, sl], dopb[:, sl])
        dp = jnp.concatenate(dps_parts, axis=1)
        q = dp / _pool_counts(ti * TM, TM)
        wsum = _window_sums(jnp.concatenate([q, nxt[...]], axis=0), TM, 0, False)
        nxt[...] = q[0:POOL_HALO, :]
        dx_ref[...] = dxo + _modulate_bwd(wsum - dp, x, gn, sc, stats)

        @pl.when(i == nt - 1)
        def _():
            _stats_out(stats, st_ref)
            dps_ref[...] = jnp.concatenate([jnp.sum(dps[...], axis=0, keepdims=True), jnp.zeros((7, D), F32)], axis=0)
            dwg_ref[...] = dwg[...].astype(BF16)

    rev = pl.BlockSpec((TM, D), lambda i: (nt - 1 - i, 0))
    small = pl.BlockSpec((8, D), lambda i: (0, 0))
    return pl.pallas_call(
        body, name="pool_bwd", grid=(nt,),
        out_shape=[jax.ShapeDtypeStruct((s, D), F32), jax.ShapeDtypeStruct((8, D), F32), jax.ShapeDtypeStruct((8, D), F32),
                   jax.ShapeDtypeStruct((ng, POOL_G, POOL_G), BF16)],
        in_specs=[rev, rev, small, rev, rev,
                  pl.BlockSpec(pool_all.shape, lambda i: (0, 0, 0, 0)), pl.BlockSpec((1, D), lambda i: (0, 0))],
        out_specs=[rev, small, small, pl.BlockSpec((ng, POOL_G, POOL_G), lambda i: (0, 0, 0))],
        scratch_shapes=[pltpu.VMEM((POOL_HALO, D), F32), pltpu.VMEM((4, 8, D), F32), pltpu.VMEM((8, D), F32),
                        pltpu.VMEM((ng, POOL_G, POOL_G), F32)],
        compiler_params=ARB1,
    )(dxo, x, p, pb, op, pool_all, pscale)


def _head(x, fg, tgt):
    s = x.shape[0]
    nt = s // TM

    def body(x_ref, fg_ref, t_ref, dx_ref, loss_ref, dfg_ref, sq, dfg):
        i = pl.program_id(0)

        @pl.when(i == 0)
        def _():
            sq[...] = jnp.zeros_like(sq)
            dfg[...] = jnp.zeros_like(dfg)

        x = x_ref[...]
        g = fg_ref[...]
        r = lax.rsqrt(jnp.mean(x * x, axis=-1, keepdims=True) + EPS)
        xn = x * r
        e = xn * g - t_ref[...]
        sq[...] += _colsum8(e * e)
        dy = e * (1.0 / D)
        dfg[...] += _colsum8(dy * xn)
        dxn = dy * g
        dx_ref[...] = r * (dxn - xn * jnp.mean(dxn * xn, axis=-1, keepdims=True))

        @pl.when(i == nt - 1)
        def _():
            total = jnp.sum(jnp.sum(sq[...], axis=0, keepdims=True), axis=1, keepdims=True)
            loss_ref[...] = jnp.broadcast_to(total * (0.5 / D), loss_ref.shape)
            dfg_ref[...] = jnp.concatenate([jnp.sum(dfg[...], axis=0, keepdims=True), jnp.zeros((7, D), F32)], axis=0)

    tile = pl.BlockSpec((TM, D), lambda i: (i, 0))
    return pl.pallas_call(
        body, name="head", grid=(nt,),
        out_shape=[jax.ShapeDtypeStruct((s, D), F32), jax.ShapeDtypeStruct((8, 128), F32), jax.ShapeDtypeStruct((8, D), F32)],
        in_specs=[tile, pl.BlockSpec((1, D), lambda i: (0, 0)), tile],
        out_specs=[tile, pl.BlockSpec((8, 128), lambda i: (0, 0)), pl.BlockSpec((8, D), lambda i: (0, 0))],
        scratch_shapes=[pltpu.VMEM((8, D), F32), pltpu.VMEM((8, D), F32)],
        compiler_params=ARB1,
    )(x, fg, tgt)


def _adamw_math(w, g, m, v):
    m = ADAM_B1 * m + (1.0 - ADAM_B1) * g
    v = ADAM_B2 * v + (1.0 - ADAM_B2) * (g * g)
    m_hat = m / (1.0 - ADAM_B1 ** ADAM_STEP)
    v_hat = v / (1.0 - ADAM_B2 ** ADAM_STEP)
    delta = -ADAM_LR * (m_hat / (jnp.sqrt(v_hat) + ADAM_EPS) + ADAM_WD * w)
    return delta, m, v


def _finish(parts, w, m, v, transposed, rb, name):
    nf, r, c = w.shape

    def body(p_ref, w_ref, m_ref, v_ref, g_ref, d_ref, mo_ref, vo_ref):
        g = p_ref[0, 0].astype(F32)
        for k in range(1, N_DEV):
            g = g + p_ref[k, 0].astype(F32)
        if transposed:
            g = g.T
        g_ref[0] = g
        d_ref[0], mo_ref[0], vo_ref[0] = _adamw_math(w_ref[0], g, m_ref[0], v_ref[0])

    blk = pl.BlockSpec((1, rb, c), lambda f, i: (f, i, 0))
    if transposed:
        pblk = pl.BlockSpec((N_DEV, 1, c, rb), lambda f, i: (0, f, 0, i))
    else:
        pblk = pl.BlockSpec((N_DEV, 1, rb, c), lambda f, i: (0, f, i, 0))
    return pl.pallas_call(
        body, name=name, grid=(nf, r // rb),
        out_shape=[jax.ShapeDtypeStruct(w.shape, F32)] * 4,
        in_specs=[pblk, blk, blk, blk], out_specs=[blk] * 4,
        compiler_params=pltpu.CompilerParams(dimension_semantics=("arbitrary", "arbitrary"), vmem_limit_bytes=VMEM_LIMIT),
    )(parts, w, m, v)


def _adamw(w, g, m, v, name):
    def body(w_ref, g_ref, m_ref, v_ref, d_ref, mo_ref, vo_ref):
        d_ref[...], mo_ref[...], vo_ref[...] = _adamw_math(w_ref[...], g_ref[...], m_ref[...], v_ref[...])

    return pl.pallas_call(
        body, name=name, out_shape=[jax.ShapeDtypeStruct(w.shape, F32)] * 3,
        in_specs=[VMEM_SPEC] * 4, out_specs=[VMEM_SPEC] * 3,
    )(w, g, m, v)


def _wmod_finish(act_t, dmod_cols, w, m, v):
    rb = 256
    ncol = w.shape[-1]

    def body(a_ref, dm_ref, w_ref, m_ref, v_ref, g_ref, d_ref, mo_ref, vo_ref):
        g = a_ref[:, 0:1] * dm_ref[0, 0:1, :]
        for k in range(1, N_DEV):
            g = g + a_ref[:, k:k + 1] * dm_ref[0, k:k + 1, :]
        g_ref[0] = g
        d_ref[0], mo_ref[0], vo_ref[0] = _adamw_math(w_ref[0], g, m_ref[0], v_ref[0])

    blk = pl.BlockSpec((1, rb, ncol), lambda l, i: (l, i, 0))
    return pl.pallas_call(
        body, name="wmod_finish", grid=(2, D // rb),
        out_shape=[jax.ShapeDtypeStruct(w.shape, F32)] * 4,
        in_specs=[pl.BlockSpec((rb, N_DEV), lambda l, i: (i, 0)), pl.BlockSpec((1, N_DEV, ncol), lambda l, i: (l, 0, 0)),
                  blk, blk, blk],
        out_specs=[blk] * 4,
        compiler_params=pltpu.CompilerParams(dimension_semantics=("arbitrary", "arbitrary"), vmem_limit_bytes=VMEM_LIMIT),
    )(act_t, dmod_cols, w, m, v)


def _pack(pieces):
    flat, offs, at = [], [], 0
    for a in pieces:
        a = a.reshape(-1)
        n = -(-a.shape[0] // 128) * 128
        flat.append(jnp.pad(a, (0, n - a.shape[0])))
        offs.append(at)
        at += n
    return jnp.concatenate(flat).reshape(-1, 128), offs


def _param_block(mod_l, sub, gn):
    return jnp.concatenate([mod_l[sub], gn[None, :], jnp.zeros((4, D), F32)], axis=0)


def kernel(x, c, norm_g, w_mod, b_mod, w_ffn_in, w_ffn_out, ab_w_in, ab_norm_v, ab_w_s, ab_b_s, ab_conv_w, ab_w_out, pool_w_grp, pool_scale, final_g, loss_target, m_norm_g, m_w_mod, m_b_mod, m_w_ffn_in, m_w_ffn_out, m_ab_w_in, m_ab_norm_v, m_ab_w_s, m_ab_b_s, m_ab_conv_w, m_ab_w_out, m_pool_w_grp, m_pool_scale, m_final_g, v_norm_g, v_w_mod, v_b_mod, v_w_ffn_in, v_w_ffn_out, v_ab_w_in, v_ab_norm_v, v_ab_w_s, v_ab_b_s, v_ab_conv_w, v_ab_w_out, v_pool_w_grp, v_pool_scale, v_final_g):
    me = 4 * lax.axis_index("x") + 2 * lax.axis_index("y") + lax.axis_index("c")
    x0 = x[0]
    tgt = loss_target[0]
    n_in = w_ffn_in.shape[-1]
    n_out = w_ffn_out.shape[-2]
    n_abin = ab_w_in.shape[-1]
    n_about = ab_w_out.shape[-2]
    n_pool = pool_w_grp.shape[-2]
    n_mod = w_mod.shape[-1]
    n_ng = norm_g.shape[-1]
    n_cw = ab_conv_w.shape[-1]
    n_ps = pool_scale.shape[-1]

    pack, offs = _pack([c, norm_g, ab_conv_w, pool_scale])
    got = _exchange_small(pack, "gather_small", False).reshape(N_DEV, -1)
    c_all = got[:, offs[0]:offs[0] + D]
    ng_full = got[:, offs[1]:offs[1] + 6 * n_ng].reshape(N_DEV, 2, 3, n_ng).transpose(1, 2, 0, 3).reshape(2, 3, D)
    cw_full = got[:, offs[2]:offs[2] + 3 * n_cw].reshape(N_DEV, 3, n_cw).transpose(1, 0, 2).reshape(3, D_B)
    ps_full = got[:, offs[3]:offs[3] + n_ps].reshape(1, D)

    act_all, mod_cols = _mod_fwd(c_all, w_mod)
    mod_got = _exchange_small(mod_cols.reshape(-1, 128), "gather_mod", False).reshape(N_DEV, 2, N_DEV, n_mod)
    mod = lax.dynamic_index_in_dim(mod_got, me, axis=2, keepdims=False).transpose(1, 0, 2).reshape(2, 9 * D) + b_mod
    mod = mod.reshape(2, 3, 3, D)

    win_all, wout_all, abin_all, about_all, pool_all = _gather_weights([
        jnp.swapaxes(w_ffn_in.reshape(4, D, n_in), 1, 2).astype(BF16),
        w_ffn_out.reshape(4, n_out, D).astype(BF16),
        ab_w_in[0].T.astype(BF16),
        ab_w_out[0].astype(BF16),
        pool_w_grp[0].astype(BF16),
    ])
    nv = ab_norm_v
    ws = ab_w_s[0]
    bst = ab_b_s[0].T
    cw8 = jnp.concatenate([cw_full, jnp.zeros((5, D_B), F32)], axis=0)

    pb = [[_param_block(mod[l], s, ng_full[l, s]) for s in range(3)] for l in range(2)]
    x1, h00, g00, u00, y00 = _ffn_fwd(x0, pb[0][0], win_all, wout_all, 0)
    x2, h01, proj, ab_out = _ab_fwd(x1, pb[0][1], abin_all, about_all, nv, ws, bst, cw8)
    x3, h02, g02, u02, y02 = _ffn_fwd(x2, pb[0][2], win_all, wout_all, 1)
    x4, h10, g10, u10, y10 = _ffn_fwd(x3, pb[1][0], win_all, wout_all, 2)
    x5, pooled, pool_out = _pool_fwd(x4, pb[1][1], pool_all, ps_full)
    x6, h12, g12, u12, y12 = _ffn_fwd(x5, pb[1][2], win_all, wout_all, 3)
    dx6, loss_blk, dfg = _head(x6, final_g.reshape(1, D), tgt)
    loss = lax.psum(loss_blk[0, 0], ("x", "y", "c"))

    dx5, dgu12, a12, dy12, st12 = _ffn_bwd(dx6, x5, pb[1][2], g12, u12, y12, win_all, wout_all, 3)
    gw_in3 = _wgrad(dgu12, h12, "wgrad_in_3")
    gw_out3 = _wgrad(a12, dy12, "wgrad_out_3")
    dx4, st11, dps, gw_pool = _pool_bwd(dx5, x4, pb[1][1], pooled, pool_out, pool_all, ps_full)
    dx3, dgu10, a10, dy10, st10 = _ffn_bwd(dx4, x3, pb[1][0], g10, u10, y10, win_all, wout_all, 2)
    gw_in2 = _wgrad(dgu10, h10, "wgrad_in_2")
    gw_out2 = _wgrad(a10, dy10, "wgrad_out_2")
    dx2, dgu02, a02, dy02, st02 = _ffn_bwd(dx3, x2, pb[0][2], g02, u02, y02, win_all, wout_all, 1)
    gw_in1 = _wgrad(dgu02, h02, "wgrad_in_1")
    gw_out1 = _wgrad(a02, dy02, "wgrad_out_1")
    dx1, dproj, cat, dy01, st01, dnv, dws, dbs, dcw = _ab_bwd(
        dx2, x1, pb[0][1], proj, ab_out, abin_all, about_all, nv, ws, bst, cw8)
    gw_abin = _wgrad(dproj, h01, "wgrad_ab_in")
    gw_about = _wgrad(cat, dy01, "wgrad_ab_out")
    dx0, dgu00, a00, dy00, st00 = _ffn_bwd(dx1, x0, pb[0][0], g00, u00, y00, win_all, wout_all, 0)
    gw_in0 = _wgrad(dgu00, h00, "wgrad_in_0")
    gw_out0 = _wgrad(a00, dy00, "wgrad_out_0")
    grad_x = dx0[None]

    parts_in, parts_out, parts_abin, parts_about, parts_pool = _scatter_grads(
        [gw_in0, gw_in1, gw_in2, gw_in3, gw_out0, gw_out1, gw_out2, gw_out3, gw_abin, gw_about, gw_pool],
        [(0, 0), (0, 1), (0, 2), (0, 3), (1, 0), (1, 1), (1, 2), (1, 3), (2, None), (3, None), (4, None)],
        [(N_DEV, 4, n_in, D), (N_DEV, 4, n_out, D), (N_DEV, n_abin, D), (N_DEV, n_about, D), (N_DEV, 4, n_pool, POOL_G)])
    shape_in, shape_out = w_ffn_in.shape, w_ffn_out.shape
    fin = lambda a: a.reshape(4, D, n_in)
    fout = lambda a: a.reshape(4, n_out, D)
    r_in = _finish(parts_in, fin(w_ffn_in), fin(m_w_ffn_in), fin(v_w_ffn_in), True, 256, "finish_ffn_in")
    r_out = _finish(parts_out, fout(w_ffn_out), fout(m_w_ffn_out), fout(v_w_ffn_out), False, n_out // 2, "finish_ffn_out")
    r_abin = _finish(parts_abin[:, None], ab_w_in, m_ab_w_in, v_ab_w_in, True, 256, "finish_ab_in")
    r_about = _finish(parts_about[:, None], ab_w_out, m_ab_w_out, v_ab_w_out, False, n_about, "finish_ab_out")
    r_pool = _finish(parts_pool, pool_w_grp[0], m_pool_w_grp[0], v_pool_w_grp[0], False, n_pool, "finish_pool")
    r_in = [a.reshape(shape_in) for a in r_in]
    r_out = [a.reshape(shape_out) for a in r_out]
    r_pool = [a[None] for a in r_pool]

    stats = [[st00, st01, st02], [st10, st11, st12]]
    dmod = jnp.stack([jnp.concatenate([stats[l][s][0:3].reshape(-1) for s in range(3)]) for l in range(2)])
    dng = jnp.stack([jnp.stack([stats[l][s][3] for s in range(3)]) for l in range(2)])
    spack, so = _pack([dmod, dng, dnv[0], dws, dbs[:, :, 0], dcw[0:3], dps[0], dfg[0]])
    sgot, ssum = _exchange_small(spack, "reduce_small", True)
    ssum = ssum.reshape(-1)
    take = lambda i, n: lax.dynamic_slice_in_dim(ssum, so[i], n)
    g_bmod = take(0, 2 * 9 * D).reshape(2, 9 * D)
    g_ng = lax.dynamic_slice_in_dim(take(1, 6 * D).reshape(2, 3, D), me * n_ng, n_ng, axis=2)
    g_nv = take(2, D_A).reshape(1, D_A)
    g_ws = take(3, A_HEADS * CHUNK * CHUNK).reshape(1, A_HEADS, CHUNK, CHUNK)
    g_bs = take(4, A_HEADS * CHUNK).reshape(1, A_HEADS, CHUNK)
    g_cw = lax.dynamic_slice_in_dim(take(5, 3 * D_B).reshape(1, 3, D_B), me * n_cw, n_cw, axis=2)
    g_ps = lax.dynamic_slice_in_dim(take(6, D).reshape(1, D), me * n_ps, n_ps, axis=1)
    g_fg = take(7, D)

    dmod_all = sgot.reshape(N_DEV, -1)[:, so[0]:so[0] + 2 * 9 * D].reshape(N_DEV, 2, 9 * D)
    dmod_cols = lax.dynamic_slice_in_dim(dmod_all, me * n_mod, n_mod, axis=2).transpose(1, 0, 2)
    r_wmod = _wmod_finish(act_all.T, dmod_cols, w_mod, m_w_mod, v_w_mod)

    small_w = [b_mod, norm_g, ab_norm_v, ab_w_s, ab_b_s, ab_conv_w, pool_scale, final_g]
    small_g = [g_bmod, g_ng, g_nv, g_ws, g_bs, g_cw, g_ps, g_fg]
    small_m = [m_b_mod, m_norm_g, m_ab_norm_v, m_ab_w_s, m_ab_b_s, m_ab_conv_w, m_pool_scale, m_final_g]
    small_v = [v_b_mod, v_norm_g, v_ab_norm_v, v_ab_w_s, v_ab_b_s, v_ab_conv_w, v_pool_scale, v_final_g]
    pw, po = _pack(small_w)
    pv = jnp.concatenate([jnp.pad(a.reshape(-1), (0, -a.size % 128), constant_values=1.0) for a in small_v]).reshape(-1, 128)
    sd, sm, sv = _adamw(pw, _pack(small_g)[0], _pack(small_m)[0], pv, "adamw_small")
    unpack = lambda packed: [packed.reshape(-1)[po[i]:po[i] + a.size].reshape(a.shape) for i, a in enumerate(small_w)]
    d_s, m_s, v_s = unpack(sd), unpack(sm), unpack(sv)

    def ordered(k, small):
        return [small[1], r_wmod[k], small[0], r_in[k], r_out[k], r_abin[k], small[2], small[3], small[4], small[5],
                r_about[k], r_pool[k], small[6], small[7]]

    grads = ordered(0, small_g)
    deltas = ordered(1, d_s)
    new_m = ordered(2, m_s)
    new_v = ordered(3, v_s)
    return (loss, grad_x, *grads, *deltas, *new_m, *new_v)
```

```python
import functools
import math

import jax
import jax.numpy as jnp
from jax import lax
from jax.experimental import pallas as pl
from jax.experimental.pallas import tpu as pltpu

F32 = jnp.float32
BF16 = jnp.bfloat16

N_DEV = 8
D = 1024
DFF = 2816
HC = 256
NCH = DFF // HC
D_A = 512
D_B = 512
D_AB = 2 * D_A + 3 * D_B
CHUNK = 128
A_HEADS = 4
POOL_G = 256
POOL_HALO = 16
CONV_HALO = 8
EPS = 1e-6
TM = 256
GELU_K = math.sqrt(2.0 / math.pi)
GELU_C = 0.044715

ADAM_LR = 0.001
ADAM_B1 = 0.9
ADAM_B2 = 0.999
ADAM_EPS = 1e-08
ADAM_WD = 0.01
ADAM_STEP = 10

VMEM_LIMIT = 56 * 1024 * 1024
MESH_ID = pl.DeviceIdType.MESH
ANY = pl.BlockSpec(memory_space=pl.ANY)
VMEM_SPEC = pl.BlockSpec(memory_space=pltpu.VMEM)
ARB1 = pltpu.CompilerParams(dimension_semantics=("arbitrary",), vmem_limit_bytes=VMEM_LIMIT)


def _dot_nt(a, b):
    return lax.dot_general(a, b, (((1,), (1,)), ((), ())), preferred_element_type=F32)


def _dot_nn(a, b):
    return lax.dot_general(a, b, (((1,), (0,)), ((), ())), preferred_element_type=F32)


def _dot_tn(a, b):
    return lax.dot_general(a, b, (((0,), (0,)), ((), ())), preferred_element_type=F32)


def _colsum8(v):
    r, n = v.shape
    return jnp.sum(v.reshape(r // 8, 8, n), axis=0)


def _gelu(x):
    return 0.5 * x * (1.0 + jnp.tanh(GELU_K * (x + GELU_C * x * x * x)))


def _gelu_grad(x):
    t = jnp.tanh(GELU_K * (x + GELU_C * x * x * x))
    return 0.5 * (1.0 + t) + 0.5 * x * (1.0 - t * t) * (GELU_K * (1.0 + 3.0 * GELU_C * x * x))


def _mod_rows(p_ref):
    return p_ref[0:1, :], p_ref[1:2, :], p_ref[2:3, :], p_ref[3:4, :]


def _modulate(x, gn, sh, sc):
    r = lax.rsqrt(jnp.mean(x * x, axis=-1, keepdims=True) + EPS)
    return ((x * r) * gn) * (1.0 + sc) + sh


def _modulate_bwd(dh, x, gn, sc, stats):
    r = lax.rsqrt(jnp.mean(x * x, axis=-1, keepdims=True) + EPS)
    xn = x * r
    stats[0] += _colsum8(dh)
    stats[1] += _colsum8(dh * (xn * gn))
    dy0 = dh * (1.0 + sc)
    stats[3] += _colsum8(dy0 * xn)
    dxn = dy0 * gn
    return r * (dxn - xn * jnp.mean(dxn * xn, axis=-1, keepdims=True))


def _stats_out(stats, out_ref):
    rows = [jnp.sum(stats[k], axis=0, keepdims=True) for k in range(4)]
    out_ref[...] = jnp.concatenate(rows + [jnp.zeros((4, stats.shape[-1]), F32)], axis=0)


def _shift_down(v, k, prev):
    n = v.shape[0]
    row = lax.broadcasted_iota(jnp.int32, v.shape, 0)
    out = pltpu.roll(v, k, 0)
    for j in range(k):
        out = jnp.where(row == j, prev[prev.shape[0] - k + j:prev.shape[0] - k + j + 1, :], out)
    return out


def _shift_up(v, k, nxt):
    n = v.shape[0]
    row = lax.broadcasted_iota(jnp.int32, v.shape, 0)
    out = pltpu.roll(v, n - k, 0)
    for j in range(k):
        out = jnp.where(row == n - k + j, nxt[j:j + 1, :], out)
    return out


def _load_rows(w_hbm, sel, dst, sems, base):
    n = dst.shape[0] // N_DEV
    cps = []
    for k in range(N_DEV):
        src = w_hbm.at[k] if sel is None else w_hbm.at[k, sel]
        cps.append(pltpu.make_async_copy(src, dst.at[pl.ds(k * n, n)], sems.at[base + k]))
    return cps


def _my_pos():
    return lax.axis_index("x"), lax.axis_index("y"), lax.axis_index("c")


def _peer(j):
    x, y, c = _my_pos()
    return (1 - x if j & 4 else x, 1 - y if j & 2 else y, 1 - c if j & 1 else c)


def _index(pos):
    return 4 * pos[0] + 2 * pos[1] + pos[2]


def _exchange_small(v, name, with_sum):
    rows = v.shape[0]

    def body(v_ref, *refs):
        if with_sum:
            out_ref, sum_ref, send_sems, recv_sems, local_sem = refs
        else:
            out_ref, send_sems, recv_sems, local_sem = refs
        me = _index(_my_pos())

        def copy(j, slot):
            return pltpu.make_async_remote_copy(
                src_ref=v_ref, dst_ref=out_ref.at[slot], send_sem=send_sems.at[j - 1], recv_sem=recv_sems.at[j - 1],
                device_id=_peer(j), device_id_type=MESH_ID)

        mine = pltpu.make_async_copy(v_ref, out_ref.at[me], local_sem)
        mine.start()
        sends = [copy(j, me) for j in range(1, N_DEV)]
        for cp in sends:
            cp.start()
        for j in range(1, N_DEV):
            copy(j, _index(_peer(j))).wait_recv()
        for cp in sends:
            cp.wait_send()
        mine.wait()
        if with_sum:
            acc = out_ref[0]
            for k in range(1, N_DEV):
                acc = acc + out_ref[k]
            sum_ref[...] = acc

    out_shape = [jax.ShapeDtypeStruct((N_DEV, rows, 128), F32)]
    out_specs = [VMEM_SPEC]
    if with_sum:
        out_shape.append(jax.ShapeDtypeStruct((rows, 128), F32))
        out_specs.append(VMEM_SPEC)
    res = pl.pallas_call(
        body, name=name, out_shape=out_shape, in_specs=[VMEM_SPEC], out_specs=out_specs,
        scratch_shapes=[pltpu.SemaphoreType.DMA((N_DEV - 1,)), pltpu.SemaphoreType.DMA((N_DEV - 1,)),
                        pltpu.SemaphoreType.DMA(())],
    )(v)
    return res if with_sum else res[0]


class _GatherRider:
    has_middle = True

    def __init__(self, shards):
        n = len(shards)
        self.inputs = list(shards)
        self.out_shapes = [jax.ShapeDtypeStruct((N_DEV,) + s.shape, s.dtype) for s in shards]
        self.scratch = [pltpu.SemaphoreType.DMA((7 * n,)), pltpu.SemaphoreType.DMA((7 * n,)), pltpu.SemaphoreType.DMA((n,))]

    def _ctx(self, outs, scr):
        send, recv, _ = scr
        x, y, c = _my_pos()
        chips = [(1 - x, y), (x, 1 - y), (1 - x, 1 - y)]

        def copy(a, k, block, to, src=None):
            slot = outs[a].at[_index(block)]
            return pltpu.make_async_remote_copy(
                src_ref=slot if src is None else src, dst_ref=slot, send_sem=send.at[7 * a + k], recv_sem=recv.at[7 * a + k],
                device_id=to, device_id_type=MESH_ID)

        return (x, y, c), (x, y, 1 - c), chips, copy

    def _sends(self, ins, outs, scr):
        me, sib, chips, copy = self._ctx(outs, scr)
        out = []
        for a in range(len(ins)):
            out.append(copy(a, 0, me, sib, src=ins[a]))
            out += [copy(a, 1 + j, me, (*chips[j], me[2]), src=ins[a]) for j in range(3)]
        return out

    def first(self, ins, outs, scr):
        me = _index(_my_pos())
        for a in range(len(ins)):
            pltpu.make_async_copy(ins[a], outs[a].at[me], scr[2].at[a]).start()
        for cp in self._sends(ins, outs, scr):
            cp.start()

    def middle(self, ins, outs, scr):
        me, sib, chips, copy = self._ctx(outs, scr)
        for j in range(3):
            for a in range(len(ins)):
                copy(a, 1 + j, (*chips[j], me[2]), me).wait_recv()
                copy(a, 4 + j, (*chips[j], me[2]), sib).start()

    def last(self, ins, outs, scr):
        me, sib, chips, copy = self._ctx(outs, scr)
        for a in range(len(ins)):
            copy(a, 0, sib, me).wait_recv()
            for j in range(3):
                copy(a, 4 + j, (*chips[j], sib[2]), me).wait_recv()
        for cp in self._sends(ins, outs, scr):
            cp.wait_send()
        for a in range(len(ins)):
            for j in range(3):
                copy(a, 4 + j, (*chips[j], me[2]), sib).wait_send()
            pltpu.make_async_copy(ins[a], outs[a].at[_index(me)], scr[2].at[a]).wait()


class _ScatterRider:
    has_middle = False

    def __init__(self, grads):
        n = len(grads)
        self.inputs = list(grads)
        self.out_shapes = []
        for g in grads:
            if g.ndim == 3:
                self.out_shapes.append(jax.ShapeDtypeStruct((N_DEV, g.shape[0], g.shape[1] // N_DEV, g.shape[2]), g.dtype))
            else:
                self.out_shapes.append(jax.ShapeDtypeStruct((N_DEV, g.shape[0] // N_DEV, g.shape[1]), g.dtype))
        self.scratch = [pltpu.SemaphoreType.DMA((7 * n,)), pltpu.SemaphoreType.DMA((7 * n,)), pltpu.SemaphoreType.DMA((n,))]

    @staticmethod
    def _part(ref, k):
        if ref.ndim == 3:
            n = ref.shape[1] // N_DEV
            return ref.at[:, pl.ds(pl.multiple_of(k * n, 16), n)]
        n = ref.shape[0] // N_DEV
        return ref.at[pl.ds(pl.multiple_of(k * n, 16), n)]

    def _copy(self, ins, outs, scr, g, j, to, src_dev):
        return pltpu.make_async_remote_copy(
            src_ref=self._part(ins[g], to), dst_ref=outs[g].at[src_dev], send_sem=scr[0].at[7 * g + j - 1],
            recv_sem=scr[1].at[7 * g + j - 1], device_id=_peer(j), device_id_type=MESH_ID)

    def first(self, ins, outs, scr):
        me = _index(_my_pos())
        for g in range(len(ins)):
            pltpu.make_async_copy(self._part(ins[g], me), outs[g].at[me], scr[2].at[g]).start()
        for j in range(1, N_DEV):
            for g in range(len(ins)):
                self._copy(ins, outs, scr, g, j, _index(_peer(j)), me).start()

    def last(self, ins, outs, scr):
        me = _index(_my_pos())
        for j in range(1, N_DEV):
            for g in range(len(ins)):
                self._copy(ins, outs, scr, g, j, me, _index(_peer(j))).wait_recv()
        for j in range(1, N_DEV):
            for g in range(len(ins)):
                self._copy(ins, outs, scr, g, j, _index(_peer(j)), me).wait_send()
        for g in range(len(ins)):
            pltpu.make_async_copy(self._part(ins[g], me), outs[g].at[me], scr[2].at[g]).wait()


def _run(body, *, name, grid, in_specs, out_specs, out_shape, scratch_shapes, args, rider=None, params=None):
    params = ARB1 if params is None else params
    if rider is None:
        outs = pl.pallas_call(body, name=name, grid=grid, in_specs=in_specs, out_specs=out_specs, out_shape=out_shape,
                              scratch_shapes=scratch_shapes, compiler_params=params)(*args)
        return list(outs), []
    ni, no, ns = len(in_specs), len(out_shape), len(scratch_shapes)
    ri, ro = len(rider.inputs), len(rider.out_shapes)
    steps = grid[0]

    def wrapped(*refs):
        cut = [ni, ni + ri, ni + ri + no, ni + ri + no + ro, ni + ri + no + ro + ns]
        a, b, c, d, e, f = (refs[lo:hi] for lo, hi in zip([0] + cut, cut + [len(refs)]))
        i = pl.program_id(0)

        @pl.when(i == 0)
        def _():
            rider.first(b, d, f)

        if rider.has_middle:
            @pl.when(i == steps - 1)
            def _():
                rider.middle(b, d, f)

        body(*a, *c, *e)

        @pl.when(i == steps - 1)
        def _():
            rider.last(b, d, f)

    outs = pl.pallas_call(
        wrapped, name=name, grid=grid, in_specs=list(in_specs) + [ANY] * ri, out_specs=list(out_specs) + [ANY] * ro,
        out_shape=list(out_shape) + rider.out_shapes, scratch_shapes=list(scratch_shapes) + rider.scratch,
        compiler_params=params)(*args, *rider.inputs)
    return list(outs[:no]), list(outs[no:])


def _run_rider(rider, name):
    ri, ro = len(rider.inputs), len(rider.out_shapes)

    def body(*refs):
        b, d, f = refs[:ri], refs[ri:ri + ro], refs[ri + ro:]
        rider.first(b, d, f)
        if rider.has_middle:
            rider.middle(b, d, f)
        rider.last(b, d, f)

    return list(pl.pallas_call(body, name=name, in_specs=[ANY] * ri, out_specs=[ANY] * ro, out_shape=rider.out_shapes,
                               scratch_shapes=rider.scratch)(*rider.inputs))


def _mod_fwd(c_all, w_mod):
    ncol = w_mod.shape[-1]

    def body(c_ref, w_ref, act_ref, out_ref):
        c = c_ref[...]
        act = c * jax.nn.sigmoid(c)
        act_ref[...] = act
        out_ref[0] = _dot_nn(act.astype(BF16), w_ref[0].astype(BF16))

    return pl.pallas_call(
        body, name="mod_fwd", grid=(2,),
        out_shape=[jax.ShapeDtypeStruct((N_DEV, D), F32), jax.ShapeDtypeStruct((2, N_DEV, ncol), F32)],
        in_specs=[pl.BlockSpec((N_DEV, D), lambda l: (0, 0)), pl.BlockSpec((1, D, ncol), lambda l: (l, 0, 0))],
        out_specs=[pl.BlockSpec((N_DEV, D), lambda l: (0, 0)), pl.BlockSpec((1, N_DEV, ncol), lambda l: (l, 0, 0))],
        compiler_params=ARB1,
    )(c_all, w_mod)


def _ffn_fwd(x, p, win_all, wout_all, f, rider=None):
    s = x.shape[0]
    nt = s // TM

    def body(x_ref, p_ref, win_hbm, wout_hbm, xo_ref, h_ref, g_ref, u_ref, y_ref, win, wout, acc, sems):
        @pl.when(pl.program_id(0) == 0)
        def _():
            cps = _load_rows(win_hbm, None, win, sems, 0) + _load_rows(wout_hbm, None, wout, sems, N_DEV)
            for cp in cps:
                cp.start()
            for cp in cps:
                cp.wait()

        sh, sc, gate, gn = _mod_rows(p_ref)
        x = x_ref[...]
        hb = _modulate(x, gn, sh, sc).astype(BF16)
        h_ref[...] = hb
        acc[...] = jnp.zeros_like(acc)

        def chunk(c, carry):
            off = pl.multiple_of(c * HC, HC)
            g = _dot_nt(hb, win[pl.ds(off, HC), :])
            u = _dot_nt(hb, win[pl.ds(DFF + off, HC), :])
            g_ref[c] = g.astype(BF16)
            u_ref[c] = u.astype(BF16)
            a = (g * jax.nn.sigmoid(g)) * u
            acc[...] += _dot_nn(a.astype(BF16), wout[pl.ds(off, HC), :])
            return carry

        lax.fori_loop(0, NCH, chunk, 0)
        y = acc[...]
        y_ref[...] = y
        xo_ref[...] = x + (0.5 * gate) * y

    tile = pl.BlockSpec((TM, D), lambda i: (i, 0))
    chunks = pl.BlockSpec((NCH, TM, HC), lambda i: (0, i, 0))
    return _run(
        body, name=f"ffn_fwd_{f}", grid=(nt,),
        out_shape=[jax.ShapeDtypeStruct((s, D), F32), jax.ShapeDtypeStruct((s, D), BF16),
                   jax.ShapeDtypeStruct((NCH, s, HC), BF16), jax.ShapeDtypeStruct((NCH, s, HC), BF16),
                   jax.ShapeDtypeStruct((s, D), F32)],
        in_specs=[tile, pl.BlockSpec((8, D), lambda i: (0, 0)), ANY, ANY],
        out_specs=[tile, tile, chunks, chunks, tile],
        scratch_shapes=[pltpu.VMEM((2 * DFF, D), BF16), pltpu.VMEM((DFF, D), BF16), pltpu.VMEM((TM, D), F32),
                        pltpu.SemaphoreType.DMA((2 * N_DEV,))],
        args=(x, p, win_all, wout_all), rider=rider)


def _ffn_bwd(dxo, x, p, g3, u3, y, win_all, wout_all, f, rider=None):
    s = x.shape[0]
    nt = s // TM

    def body(dxo_ref, x_ref, p_ref, g_ref, u_ref, y_ref, win_hbm, wout_hbm,
             dx_ref, dgu_ref, a_ref, dy_ref, st_ref, win, wout, acc, stats, sems):
        i = pl.program_id(0)

        @pl.when(i == 0)
        def _():
            cps = _load_rows(win_hbm, None, win, sems, 0) + _load_rows(wout_hbm, None, wout, sems, N_DEV)
            for cp in cps:
                cp.start()
            stats[...] = jnp.zeros_like(stats)
            for cp in cps:
                cp.wait()

        sh, sc, gate, gn = _mod_rows(p_ref)
        x = x_ref[...]
        dxo = dxo_ref[...]
        dyb = ((0.5 * gate) * dxo).astype(BF16)
        dy_ref[...] = dyb
        stats[2] += _colsum8((0.5 * dxo) * y_ref[...])
        acc[...] = jnp.zeros_like(acc)

        def chunk(c, carry):
            off = pl.multiple_of(c * HC, HC)
            da = _dot_nt(dyb, wout[pl.ds(off, HC), :])
            g = g_ref[c].astype(F32)
            u = u_ref[c].astype(F32)
            sg = jax.nn.sigmoid(g)
            si = g * sg
            dg = ((da * u) * (sg * (1.0 + g * (1.0 - sg)))).astype(BF16)
            du = (da * si).astype(BF16)
            a_ref[c] = (si * u).astype(BF16)
            dgu_ref[c] = dg
            dgu_ref[NCH + c] = du
            acc[...] += _dot_nn(dg, win[pl.ds(off, HC), :]) + _dot_nn(du, win[pl.ds(DFF + off, HC), :])
            return carry

        lax.fori_loop(0, NCH, chunk, 0)
        dx_ref[...] = dxo + _modulate_bwd(acc[...], x, gn, sc, stats)

        @pl.when(i == nt - 1)
        def _():
            _stats_out(stats, st_ref)

    tile = pl.BlockSpec((TM, D), lambda i: (i, 0))
    chunks = pl.BlockSpec((NCH, TM, HC), lambda i: (0, i, 0))
    small = pl.BlockSpec((8, D), lambda i: (0, 0))
    return _run(
        body, name=f"ffn_bwd_{f}", grid=(nt,),
        out_shape=[jax.ShapeDtypeStruct((s, D), F32), jax.ShapeDtypeStruct((2 * NCH, s, HC), BF16),
                   jax.ShapeDtypeStruct((NCH, s, HC), BF16), jax.ShapeDtypeStruct((s, D), BF16),
                   jax.ShapeDtypeStruct((8, D), F32)],
        in_specs=[tile, tile, small, chunks, chunks, tile, ANY, ANY],
        out_specs=[tile, pl.BlockSpec((2 * NCH, TM, HC), lambda i: (0, i, 0)), chunks, tile, small],
        scratch_shapes=[pltpu.VMEM((2 * DFF, D), BF16), pltpu.VMEM((DFF, D), BF16), pltpu.VMEM((TM, D), F32),
                        pltpu.VMEM((4, 8, D), F32), pltpu.SemaphoreType.DMA((2 * N_DEV,))],
        args=(dxo, x, p, g3, u3, y, win_all, wout_all), rider=rider)


def _wgrad(lhs3, rhs, name, rider=None):
    nj, s, _ = lhs3.shape
    n = rhs.shape[1]

    def body(l_ref, r_ref, o_ref):
        o_ref[...] = _dot_tn(l_ref[0], r_ref[...]).astype(BF16)

    outs, rode = _run(
        body, name=name, grid=(nj,),
        out_shape=[jax.ShapeDtypeStruct((nj * HC, n), BF16)],
        in_specs=[pl.BlockSpec((1, s, HC), lambda j: (j, 0, 0)), pl.BlockSpec((s, n), lambda j: (0, 0))],
        out_specs=[pl.BlockSpec((HC, n), lambda j: (j, 0))],
        scratch_shapes=[], args=(lhs3, rhs), rider=rider)
    return outs[0], rode


def _gating(proj, nv, ws_ref, bst):
    u, v = proj[:, 0:D_A], proj[:, D_A:2 * D_A]
    gu, gv = _gelu(u), _gelu(v)
    mu = jnp.mean(gv, axis=-1, keepdims=True)
    dv = gv - mu
    rstd = lax.rsqrt(jnp.mean(dv * dv, axis=-1, keepdims=True) + EPS)
    vhat = dv * rstd
    vn = vhat * nv
    r = lax.broadcasted_iota(jnp.int32, (CHUNK, CHUNK), 0)
    c = lax.broadcasted_iota(jnp.int32, (CHUNK, CHUNK), 1)
    wm = [jnp.where(r >= c, ws_ref[hd], 0.0).astype(BF16) for hd in range(A_HEADS)]
    vnb = vn.astype(BF16)
    rows = []
    for n in range(proj.shape[0] // CHUNK):
        blocks = []
        for hd in range(A_HEADS):
            blk = vnb[n * CHUNK:(n + 1) * CHUNK, hd * CHUNK:(hd + 1) * CHUNK]
            blocks.append(_dot_nn(wm[hd], blk) + bst[:, hd:hd + 1])
        rows.append(jnp.concatenate(blocks, axis=1))
    z = jnp.concatenate(rows, axis=0)
    return u, v, gu, rstd, vhat, vnb, wm, z


def _conv(proj, cw, prev_xp):
    bg = proj[:, 2 * D_A:2 * D_A + D_B]
    cg = proj[:, 2 * D_A + D_B:2 * D_A + 2 * D_B]
    xb = proj[:, 2 * D_A + 2 * D_B:]
    xp = cg * xb
    x1 = _shift_down(xp, 1, prev_xp)
    x2 = _shift_down(xp, 2, prev_xp)
    conv = cw[0:1, :] * x2 + cw[1:2, :] * x1 + cw[2:3, :] * xp
    return bg, cg, xb, xp, x1, x2, conv


def _ab_fwd(x, p, abin_all, about_all, nv, ws, bst, cw, rider=None):
    s = x.shape[0]
    nt = s // TM

    def body(x_ref, p_ref, abin_hbm, about_hbm, nv_ref, ws_ref, bst_ref, cw_ref,
             xo_ref, h_ref, proj_ref, out_ref, abin, about, prev, sems):
        @pl.when(pl.program_id(0) == 0)
        def _():
            cps = _load_rows(abin_hbm, None, abin, sems, 0) + _load_rows(about_hbm, None, about, sems, N_DEV)
            for cp in cps:
                cp.start()
            prev[...] = jnp.zeros_like(prev)
            for cp in cps:
                cp.wait()

        sh, sc, gate, gn = _mod_rows(p_ref)
        x = x_ref[...]
        hb = _modulate(x, gn, sh, sc).astype(BF16)
        h_ref[...] = hb
        proj = _dot_nt(hb, abin[...])
        proj_ref[...] = proj
        _, _, gu, _, _, _, _, z = _gating(proj, nv_ref[...], ws_ref, bst_ref[...])
        bg, _, _, xp, _, _, conv = _conv(proj, cw_ref[...], prev[...])
        prev[...] = xp[TM - CONV_HALO:, :]
        cat = jnp.concatenate([gu * z, bg * conv], axis=1).astype(BF16)
        out = _dot_nn(cat, about[...])
        out_ref[...] = out
        xo_ref[...] = x + gate * out

    tile = pl.BlockSpec((TM, D), lambda i: (i, 0))
    full = lambda a: pl.BlockSpec(a.shape, lambda i: (0,) * a.ndim)
    return _run(
        body, name="ab_fwd", grid=(nt,),
        out_shape=[jax.ShapeDtypeStruct((s, D), F32), jax.ShapeDtypeStruct((s, D), BF16),
                   jax.ShapeDtypeStruct((s, D_AB), F32), jax.ShapeDtypeStruct((s, D), F32)],
        in_specs=[tile, pl.BlockSpec((8, D), lambda i: (0, 0)), ANY, ANY, full(nv), full(ws), full(bst), full(cw)],
        out_specs=[tile, tile, pl.BlockSpec((TM, D_AB), lambda i: (i, 0)), tile],
        scratch_shapes=[pltpu.VMEM((D_AB, D), BF16), pltpu.VMEM((D, D), BF16), pltpu.VMEM((CONV_HALO, D_B), F32),
                        pltpu.SemaphoreType.DMA((2 * N_DEV,))],
        args=(x, p, abin_all, about_all, nv, ws, bst, cw), rider=rider)


def _ab_bwd(dxo, x, p, proj, out, abin_all, about_all, nv, ws, bst, cw, rider=None):
    s = x.shape[0]
    nt = s // TM
    npj = D_AB // HC

    def body(dxo_ref, x_ref, p_ref, proj_ref, halo_ref, out_ref, abin_hbm, about_hbm, nv_ref, ws_ref, bst_ref, cw_ref,
             dx_ref, dproj_ref, cat_ref, dy_ref, st_ref, dnv_ref, dws_ref, dbs_ref, dcw_ref,
             abin, about, nxt, stats, dnv, dws, dbs, dcw, sems):
        i = pl.program_id(0)
        ti = nt - 1 - i

        @pl.when(i == 0)
        def _():
            cps = _load_rows(abin_hbm, None, abin, sems, 0) + _load_rows(about_hbm, None, about, sems, N_DEV)
            for cp in cps:
                cp.start()
            for z in (nxt, stats, dnv, dws, dbs, dcw):
                z[...] = jnp.zeros_like(z)
            for cp in cps:
                cp.wait()

        sh, sc, gate, gn = _mod_rows(p_ref)
        x = x_ref[...]
        dxo = dxo_ref[...]
        dyb = (gate * dxo).astype(BF16)
        dy_ref[...] = dyb
        stats[2] += _colsum8(dxo * out_ref[...])
        dcat = _dot_nt(dyb, about[...])
        dya, dyb2 = dcat[:, 0:D_A], dcat[:, D_A:]

        proj = proj_ref[...]
        nvv = nv_ref[...]
        u, v, gu, rstd, vhat, vnb, wm, z = _gating(proj, nvv, ws_ref, bst_ref[...])
        dgu = dya * z
        dzb = (dya * gu).astype(BF16)
        dz32 = dya * gu
        rows = []
        for n in range(TM // CHUNK):
            blocks = []
            for hd in range(A_HEADS):
                sl = (slice(n * CHUNK, (n + 1) * CHUNK), slice(hd * CHUNK, (hd + 1) * CHUNK))
                dbs[hd] += dz32[sl]
                dws[hd] += _dot_nt(dzb[sl], vnb[sl])
                blocks.append(_dot_tn(wm[hd], dzb[sl]))
            rows.append(jnp.concatenate(blocks, axis=1))
        dvn = jnp.concatenate(rows, axis=0)
        dnv[...] += _colsum8(dvn * vhat)
        dvh = dvn * nvv
        dgv = rstd * (dvh - jnp.mean(dvh, axis=-1, keepdims=True) - vhat * jnp.mean(dvh * vhat, axis=-1, keepdims=True))
        du = dgu * _gelu_grad(u)
        dv = dgv * _gelu_grad(v)

        halo = halo_ref[...]
        prev_xp = jnp.where(ti > 0, halo[:, 2 * D_A + D_B:2 * D_A + 2 * D_B] * halo[:, 2 * D_A + 2 * D_B:], 0.0)
        cwv = cw_ref[...]
        bg, cg, xb, xp, x1, x2, conv = _conv(proj, cwv, prev_xp)
        dbg = dyb2 * conv
        dconv = dyb2 * bg
        dcw[...] += jnp.concatenate(
            [jnp.sum(_colsum8(dconv * t), axis=0, keepdims=True) for t in (x2, x1, xp)] + [jnp.zeros((5, D_B), F32)], axis=0)
        nx = nxt[...]
        dxp = cwv[2:3, :] * dconv + cwv[1:2, :] * _shift_up(dconv, 1, nx) + cwv[0:1, :] * _shift_up(dconv, 2, nx)
        nxt[...] = dconv[0:CONV_HALO, :]
        dcg = dxp * xb
        dxb = dxp * cg

        dproj = jnp.concatenate([du, dv, dbg, dcg, dxb], axis=1).astype(BF16)
        for k in range(npj):
            dproj_ref[k] = dproj[:, k * HC:(k + 1) * HC]
        cat = jnp.concatenate([gu * z, bg * conv], axis=1).astype(BF16)
        for k in range(D // HC):
            cat_ref[k] = cat[:, k * HC:(k + 1) * HC]
        dh = _dot_nn(dproj, abin[...])
        dx_ref[...] = dxo + _modulate_bwd(dh, x, gn, sc, stats)

        @pl.when(i == nt - 1)
        def _():
            _stats_out(stats, st_ref)
            dnv_ref[...] = jnp.concatenate([jnp.sum(dnv[...], axis=0, keepdims=True), jnp.zeros((7, D_A), F32)], axis=0)
            r = lax.broadcasted_iota(jnp.int32, (CHUNK, CHUNK), 0)
            c = lax.broadcasted_iota(jnp.int32, (CHUNK, CHUNK), 1)
            for hd in range(A_HEADS):
                dws_ref[hd] = jnp.where(r >= c, dws[hd], 0.0)
                dbs_ref[hd] = jnp.broadcast_to(jnp.sum(dbs[hd], axis=1, keepdims=True), (CHUNK, CHUNK))
            dcw_ref[...] = dcw[...]

    rev = pl.BlockSpec((TM, D), lambda i: (nt - 1 - i, 0))
    small = pl.BlockSpec((8, D), lambda i: (0, 0))
    full = lambda a: pl.BlockSpec(a.shape, lambda i: (0,) * a.ndim)
    hpt = TM // CONV_HALO
    fixed = lambda shape: pl.BlockSpec(shape, lambda i: (0,) * len(shape))
    return _run(
        body, name="ab_bwd", grid=(nt,),
        out_shape=[jax.ShapeDtypeStruct((s, D), F32), jax.ShapeDtypeStruct((npj, s, HC), BF16),
                   jax.ShapeDtypeStruct((D // HC, s, HC), BF16), jax.ShapeDtypeStruct((s, D), BF16),
                   jax.ShapeDtypeStruct((8, D), F32), jax.ShapeDtypeStruct((8, D_A), F32),
                   jax.ShapeDtypeStruct((A_HEADS, CHUNK, CHUNK), F32), jax.ShapeDtypeStruct((A_HEADS, CHUNK, CHUNK), F32),
                   jax.ShapeDtypeStruct((8, D_B), F32)],
        in_specs=[rev, rev, small,
                  pl.BlockSpec((TM, D_AB), lambda i: (nt - 1 - i, 0)),
                  pl.BlockSpec((CONV_HALO, D_AB), lambda i: (jnp.maximum((nt - 1 - i) * hpt - 1, 0), 0)),
                  rev, ANY, ANY, full(nv), full(ws), full(bst), full(cw)],
        out_specs=[rev, pl.BlockSpec((npj, TM, HC), lambda i: (0, nt - 1 - i, 0)),
                   pl.BlockSpec((D // HC, TM, HC), lambda i: (0, nt - 1 - i, 0)), rev,
                   small, fixed((8, D_A)), fixed((A_HEADS, CHUNK, CHUNK)), fixed((A_HEADS, CHUNK, CHUNK)), fixed((8, D_B))],
        scratch_shapes=[pltpu.VMEM((D_AB, D), BF16), pltpu.VMEM((D, D), BF16), pltpu.VMEM((CONV_HALO, D_B), F32),
                        pltpu.VMEM((4, 8, D), F32), pltpu.VMEM((8, D_A), F32),
                        pltpu.VMEM((A_HEADS, CHUNK, CHUNK), F32), pltpu.VMEM((A_HEADS, CHUNK, CHUNK), F32),
                        pltpu.VMEM((8, D_B), F32), pltpu.SemaphoreType.DMA((2 * N_DEV,))],
        args=(dxo, x, p, proj, proj, out, abin_all, about_all, nv, ws, bst, cw), rider=rider)


def _pool_counts(first_token, rows):
    t = (first_token + lax.broadcasted_iota(jnp.int32, (rows, 1), 0) + 1).astype(F32)
    lane = lax.broadcasted_iota(jnp.int32, (1, D), 1)
    w = jnp.where(lane < POOL_G, 2.0, jnp.where(lane < 2 * POOL_G, 4.0, jnp.where(lane < 3 * POOL_G, 8.0, 16.0)))
    return jnp.minimum(t, w)


def _window_sums(ext, n_keep, lead, back):
    n = ext.shape[0]
    sh = (lambda v, k: pltpu.roll(v, k, 0)) if back else (lambda v, k: pltpu.roll(v, n - k, 0))
    s2 = ext + sh(ext, 1)
    s4 = s2[:, POOL_G:] + sh(s2[:, POOL_G:], 2)
    s8 = s4[:, POOL_G:] + sh(s4[:, POOL_G:], 4)
    s16 = s8[:, POOL_G:] + sh(s8[:, POOL_G:], 8)
    keep = slice(lead, lead + n_keep)
    return jnp.concatenate([s2[keep, 0:POOL_G], s4[keep, 0:POOL_G], s8[keep, 0:POOL_G], s16[keep, :]], axis=1)


def _pool_fwd(x, p, pool_all, pscale):
    s = x.shape[0]
    nt = s // TM
    ng = D // POOL_G

    def body(x_ref, p_ref, wg_ref, ps_ref, xo_ref, pb_ref, op_ref, prev):
        i = pl.program_id(0)

        @pl.when(i == 0)
        def _():
            prev[...] = jnp.zeros_like(prev)

        sh, sc, gate, gn = _mod_rows(p_ref)
        x = x_ref[...]
        h = _modulate(x, gn, sh, sc)
        win = _window_sums(jnp.concatenate([prev[...], h], axis=0), TM, POOL_HALO, True)
        prev[...] = h[TM - POOL_HALO:, :]
        pb = (win / _pool_counts(i * TM, TM) - h).astype(BF16)
        pb_ref[...] = pb
        op = jnp.concatenate(
            [_dot_nn(pb[:, g * POOL_G:(g + 1) * POOL_G], wg_ref[:, g].reshape(POOL_G, POOL_G)) for g in range(ng)], axis=1)
        op_ref[...] = op
        xo_ref[...] = x + gate * (op * ps_ref[...])

    tile = pl.BlockSpec((TM, D), lambda i: (i, 0))
    return pl.pallas_call(
        body, name="pool_fwd", grid=(nt,),
        out_shape=[jax.ShapeDtypeStruct((s, D), F32), jax.ShapeDtypeStruct((s, D), BF16), jax.ShapeDtypeStruct((s, D), F32)],
        in_specs=[tile, pl.BlockSpec((8, D), lambda i: (0, 0)),
                  pl.BlockSpec(pool_all.shape, lambda i: (0, 0, 0, 0)), pl.BlockSpec((1, D), lambda i: (0, 0))],
        out_specs=[tile, tile, tile],
        scratch_shapes=[pltpu.VMEM((POOL_HALO, D), F32)],
        compiler_params=ARB1,
    )(x, p, pool_all, pscale)


def _pool_bwd(dxo, x, p, pb, op, pool_all, pscale):
    s = x.shape[0]
    nt = s // TM
    ng = D // POOL_G

    def body(dxo_ref, x_ref, p_ref, pb_ref, op_ref, wg_ref, ps_ref,
             dx_ref, st_ref, dps_ref, dwg_ref, nxt, stats, dps, dwg):
        i = pl.program_id(0)
        ti = nt - 1 - i

        @pl.when(i == 0)
        def _():
            for z in (nxt, stats, dps, dwg):
                z[...] = jnp.zeros_like(z)

        sh, sc, gate, gn = _mod_rows(p_ref)
        x = x_ref[...]
        dxo = dxo_ref[...]
        ps = ps_ref[...]
        op = op_ref[...]
        dmo = gate * dxo
        stats[2] += _colsum8(dxo * (op * ps))
        dps[...] += _colsum8(dmo * op)
        dopb = (dmo * ps).astype(BF16)
        pbv = pb_ref[...]
        dps_parts = []
        for g in range(ng):
            sl = slice(g * POOL_G, (g + 1) * POOL_G)
            dps_parts.append(_dot_nt(dopb[:, sl], wg_ref[:, g].reshape(POOL_G, POOL_G)))
            dwg[g] += _dot_tn(pbv[:, sl], dopb[:, sl])
        dp = jnp.concatenate(dps_parts, axis=1)
        q = dp / _pool_counts(ti * TM, TM)
        wsum = _window_sums(jnp.concatenate([q, nxt[...]], axis=0), TM, 0, False)
        nxt[...] = q[0:POOL_HALO, :]
        dx_ref[...] = dxo + _modulate_bwd(wsum - dp, x, gn, sc, stats)

        @pl.when(i == nt - 1)
        def _():
            _stats_out(stats, st_ref)
            dps_ref[...] = jnp.concatenate([jnp.sum(dps[...], axis=0, keepdims=True), jnp.zeros((7, D), F32)], axis=0)
            dwg_ref[...] = dwg[...].astype(BF16)

    rev = pl.BlockSpec((TM, D), lambda i: (nt - 1 - i, 0))
    small = pl.BlockSpec((8, D), lambda i: (0, 0))
    return pl.pallas_call(
        body, name="pool_bwd", grid=(nt,),
        out_shape=[jax.ShapeDtypeStruct((s, D), F32), jax.ShapeDtypeStruct((8, D), F32), jax.ShapeDtypeStruct((8, D), F32),
                   jax.ShapeDtypeStruct((ng, POOL_G, POOL_G), BF16)],
        in_specs=[rev, rev, small, rev, rev,
                  pl.BlockSpec(pool_all.shape, lambda i: (0, 0, 0, 0)), pl.BlockSpec((1, D), lambda i: (0, 0))],
        out_specs=[rev, small, small, pl.BlockSpec((ng, POOL_G, POOL_G), lambda i: (0, 0, 0))],
        scratch_shapes=[pltpu.VMEM((POOL_HALO, D), F32), pltpu.VMEM((4, 8, D), F32), pltpu.VMEM((8, D), F32),
                        pltpu.VMEM((ng, POOL_G, POOL_G), F32)],
        compiler_params=ARB1,
    )(dxo, x, p, pb, op, pool_all, pscale)


def _head(x, fg, tgt):
    s = x.shape[0]
    nt = s // TM

    def body(x_ref, fg_ref, t_ref, dx_ref, loss_ref, dfg_ref, sq, dfg):
        i = pl.program_id(0)

        @pl.when(i == 0)
        def _():
            sq[...] = jnp.zeros_like(sq)
            dfg[...] = jnp.zeros_like(dfg)

        x = x_ref[...]
        g = fg_ref[...]
        r = lax.rsqrt(jnp.mean(x * x, axis=-1, keepdims=True) + EPS)
        xn = x * r
        e = xn * g - t_ref[...]
        sq[...] += _colsum8(e * e)
        dy = e * (1.0 / D)
        dfg[...] += _colsum8(dy * xn)
        dxn = dy * g
        dx_ref[...] = r * (dxn - xn * jnp.mean(dxn * xn, axis=-1, keepdims=True))

        @pl.when(i == nt - 1)
        def _():
            total = jnp.sum(jnp.sum(sq[...], axis=0, keepdims=True), axis=1, keepdims=True)
            loss_ref[...] = jnp.broadcast_to(total * (0.5 / D), loss_ref.shape)
            dfg_ref[...] = jnp.concatenate([jnp.sum(dfg[...], axis=0, keepdims=True), jnp.zeros((7, D), F32)], axis=0)

    tile = pl.BlockSpec((TM, D), lambda i: (i, 0))
    return pl.pallas_call(
        body, name="head", grid=(nt,),
        out_shape=[jax.ShapeDtypeStruct((s, D), F32), jax.ShapeDtypeStruct((8, 128), F32), jax.ShapeDtypeStruct((8, D), F32)],
        in_specs=[tile, pl.BlockSpec((1, D), lambda i: (0, 0)), tile],
        out_specs=[tile, pl.BlockSpec((8, 128), lambda i: (0, 0)), pl.BlockSpec((8, D), lambda i: (0, 0))],
        scratch_shapes=[pltpu.VMEM((8, D), F32), pltpu.VMEM((8, D), F32)],
        compiler_params=ARB1,
    )(x, fg, tgt)


def _adamw_math(w, g, m, v):
    m = ADAM_B1 * m + (1.0 - ADAM_B1) * g
    v = ADAM_B2 * v + (1.0 - ADAM_B2) * (g * g)
    m_hat = m / (1.0 - ADAM_B1 ** ADAM_STEP)
    v_hat = v / (1.0 - ADAM_B2 ** ADAM_STEP)
    delta = -ADAM_LR * (m_hat / (jnp.sqrt(v_hat) + ADAM_EPS) + ADAM_WD * w)
    return delta, m, v


def _finish(parts, w, m, v, transposed, rb, name):
    nf, r, c = w.shape

    def body(*refs):
        p_refs = refs[:nf]
        w_ref, m_ref, v_ref, g_ref, d_ref, mo_ref, vo_ref = refs[nf:]
        for f in range(nf):
            @pl.when(pl.program_id(0) == f)
            def _():
                g = p_refs[f][0].astype(F32)
                for k in range(1, N_DEV):
                    g = g + p_refs[f][k].astype(F32)
                if transposed:
                    g = g.T
                g_ref[0] = g
                d_ref[0], mo_ref[0], vo_ref[0] = _adamw_math(w_ref[0], g, m_ref[0], v_ref[0])

    blk = pl.BlockSpec((1, rb, c), lambda f, i: (f, i, 0))

    def pblk(mine):
        if transposed:
            return pl.BlockSpec((N_DEV, c, rb), lambda f, i: (0, 0, jnp.where(f == mine, i, 0)))
        return pl.BlockSpec((N_DEV, rb, c), lambda f, i: (0, jnp.where(f == mine, i, 0), 0))

    return pl.pallas_call(
        body, name=name, grid=(nf, r // rb),
        out_shape=[jax.ShapeDtypeStruct(w.shape, F32)] * 4,
        in_specs=[pblk(f) for f in range(nf)] + [blk, blk, blk], out_specs=[blk] * 4,
        compiler_params=pltpu.CompilerParams(dimension_semantics=("arbitrary", "arbitrary"), vmem_limit_bytes=VMEM_LIMIT),
    )(*parts, w, m, v)


def _adamw(w, g, m, v, name):
    def body(w_ref, g_ref, m_ref, v_ref, d_ref, mo_ref, vo_ref):
        d_ref[...], mo_ref[...], vo_ref[...] = _adamw_math(w_ref[...], g_ref[...], m_ref[...], v_ref[...])

    return pl.pallas_call(
        body, name=name, out_shape=[jax.ShapeDtypeStruct(w.shape, F32)] * 3,
        in_specs=[VMEM_SPEC] * 4, out_specs=[VMEM_SPEC] * 3,
    )(w, g, m, v)


def _wmod_finish(act_t, dmod_cols, w, m, v):
    rb = 256
    ncol = w.shape[-1]

    def body(a_ref, dm_ref, w_ref, m_ref, v_ref, g_ref, d_ref, mo_ref, vo_ref):
        g = a_ref[:, 0:1] * dm_ref[0, 0:1, :]
        for k in range(1, N_DEV):
            g = g + a_ref[:, k:k + 1] * dm_ref[0, k:k + 1, :]
        g_ref[0] = g
        d_ref[0], mo_ref[0], vo_ref[0] = _adamw_math(w_ref[0], g, m_ref[0], v_ref[0])

    blk = pl.BlockSpec((1, rb, ncol), lambda l, i: (l, i, 0))
    return pl.pallas_call(
        body, name="wmod_finish", grid=(2, D // rb),
        out_shape=[jax.ShapeDtypeStruct(w.shape, F32)] * 4,
        in_specs=[pl.BlockSpec((rb, N_DEV), lambda l, i: (i, 0)), pl.BlockSpec((1, N_DEV, ncol), lambda l, i: (l, 0, 0)),
                  blk, blk, blk],
        out_specs=[blk] * 4,
        compiler_params=pltpu.CompilerParams(dimension_semantics=("arbitrary", "arbitrary"), vmem_limit_bytes=VMEM_LIMIT),
    )(act_t, dmod_cols, w, m, v)


def _pack(pieces):
    flat, offs, at = [], [], 0
    for a in pieces:
        a = a.reshape(-1)
        n = -(-a.shape[0] // 128) * 128
        flat.append(jnp.pad(a, (0, n - a.shape[0])))
        offs.append(at)
        at += n
    return jnp.concatenate(flat).reshape(-1, 128), offs


def _param_block(mod_l, sub, gn):
    return jnp.concatenate([mod_l[sub], gn[None, :], jnp.zeros((4, D), F32)], axis=0)


def kernel(x, c, norm_g, w_mod, b_mod, w_ffn_in, w_ffn_out, ab_w_in, ab_norm_v, ab_w_s, ab_b_s, ab_conv_w, ab_w_out, pool_w_grp, pool_scale, final_g, loss_target, m_norm_g, m_w_mod, m_b_mod, m_w_ffn_in, m_w_ffn_out, m_ab_w_in, m_ab_norm_v, m_ab_w_s, m_ab_b_s, m_ab_conv_w, m_ab_w_out, m_pool_w_grp, m_pool_scale, m_final_g, v_norm_g, v_w_mod, v_b_mod, v_w_ffn_in, v_w_ffn_out, v_ab_w_in, v_ab_norm_v, v_ab_w_s, v_ab_b_s, v_ab_conv_w, v_ab_w_out, v_pool_w_grp, v_pool_scale, v_final_g):
    me = 4 * lax.axis_index("x") + 2 * lax.axis_index("y") + lax.axis_index("c")
    x0 = x[0]
    tgt = loss_target[0]
    n_in = w_ffn_in.shape[-1]
    n_out = w_ffn_out.shape[-2]
    n_abin = ab_w_in.shape[-1]
    n_about = ab_w_out.shape[-2]
    n_pool = pool_w_grp.shape[-2]
    n_mod = w_mod.shape[-1]
    n_ng = norm_g.shape[-1]
    n_cw = ab_conv_w.shape[-1]
    n_ps = pool_scale.shape[-1]

    pack, offs = _pack([c, norm_g, ab_conv_w, pool_scale])
    got = _exchange_small(pack, "gather_small", False).reshape(N_DEV, -1)
    c_all = got[:, offs[0]:offs[0] + D]
    ng_full = got[:, offs[1]:offs[1] + 6 * n_ng].reshape(N_DEV, 2, 3, n_ng).transpose(1, 2, 0, 3).reshape(2, 3, D)
    cw_full = got[:, offs[2]:offs[2] + 3 * n_cw].reshape(N_DEV, 3, n_cw).transpose(1, 0, 2).reshape(3, D_B)
    ps_full = got[:, offs[3]:offs[3] + n_ps].reshape(1, D)

    act_all, mod_cols = _mod_fwd(c_all, w_mod)
    mod_got = _exchange_small(mod_cols.reshape(-1, 128), "gather_mod", False).reshape(N_DEV, 2, N_DEV, n_mod)
    mod = lax.dynamic_index_in_dim(mod_got, me, axis=2, keepdims=False).transpose(1, 0, 2).reshape(2, 9 * D) + b_mod
    mod = mod.reshape(2, 3, 3, D)

    win_sh = jnp.swapaxes(w_ffn_in.reshape(4, D, n_in), 1, 2).astype(BF16)
    wout_sh = w_ffn_out.reshape(4, n_out, D).astype(BF16)
    abin_sh = ab_w_in[0].T.astype(BF16)
    about_sh = ab_w_out[0].astype(BF16)
    pool_sh = pool_w_grp[0].astype(BF16)
    nv = ab_norm_v
    ws = ab_w_s[0]
    bst = ab_b_s[0].T
    cw8 = jnp.concatenate([cw_full, jnp.zeros((5, D_B), F32)], axis=0)
    win, wout = [None] * 4, [None] * 4

    pb = [[_param_block(mod[l], s, ng_full[l, s]) for s in range(3)] for l in range(2)]
    win[0], wout[0] = _run_rider(_GatherRider([win_sh[0], wout_sh[0]]), "gather_ffn_0")
    (x1, h00, g00, u00, y00), (abin_all, about_all, win[1]) = _ffn_fwd(
        x0, pb[0][0], win[0], wout[0], 0, _GatherRider([abin_sh, about_sh, win_sh[1]]))
    (x2, h01, proj, ab_out), (wout[1],) = _ab_fwd(
        x1, pb[0][1], abin_all, about_all, nv, ws, bst, cw8, _GatherRider([wout_sh[1]]))
    (x3, h02, g02, u02, y02), (win[2], wout[2]) = _ffn_fwd(
        x2, pb[0][2], win[1], wout[1], 1, _GatherRider([win_sh[2], wout_sh[2]]))
    (x4, h10, g10, u10, y10), (pool_all, win[3], wout[3]) = _ffn_fwd(
        x3, pb[1][0], win[2], wout[2], 2, _GatherRider([pool_sh, win_sh[3], wout_sh[3]]))
    x5, pooled, pool_out = _pool_fwd(x4, pb[1][1], pool_all, ps_full)
    (x6, h12, g12, u12, y12), _ = _ffn_fwd(x5, pb[1][2], win[3], wout[3], 3)
    dx6, loss_blk, dfg = _head(x6, final_g.reshape(1, D), tgt)
    loss = lax.psum(loss_blk[0, 0], ("x", "y", "c"))

    p_in, p_out = [None] * 4, [None] * 4
    (dx5, dgu12, a12, dy12, st12), _ = _ffn_bwd(dx6, x5, pb[1][2], g12, u12, y12, win[3], wout[3], 3)
    gw_in3, _ = _wgrad(dgu12, h12, "wgrad_in_3")
    gw_out3, _ = _wgrad(a12, dy12, "wgrad_out_3")
    dx4, st11, dps, gw_pool = _pool_bwd(dx5, x4, pb[1][1], pooled, pool_out, pool_all, ps_full)
    (dx3, dgu10, a10, dy10, st10), (p_in[3],) = _ffn_bwd(
        dx4, x3, pb[1][0], g10, u10, y10, win[2], wout[2], 2, _ScatterRider([gw_in3]))
    gw_in2, (p_out[3],) = _wgrad(dgu10, h10, "wgrad_in_2", _ScatterRider([gw_out3]))
    gw_out2, (p_pool,) = _wgrad(a10, dy10, "wgrad_out_2", _ScatterRider([gw_pool]))
    (dx2, dgu02, a02, dy02, st02), (p_in[2],) = _ffn_bwd(
        dx3, x2, pb[0][2], g02, u02, y02, win[1], wout[1], 1, _ScatterRider([gw_in2]))
    gw_in1, (p_out[2],) = _wgrad(dgu02, h02, "wgrad_in_1", _ScatterRider([gw_out2]))
    gw_out1, _ = _wgrad(a02, dy02, "wgrad_out_1")
    (dx1, dproj, cat, dy01, st01, dnv, dws, dbs, dcw), (p_in[1],) = _ab_bwd(
        dx2, x1, pb[0][1], proj, ab_out, abin_all, about_all, nv, ws, bst, cw8, _ScatterRider([gw_in1]))
    gw_abin, (p_out[1],) = _wgrad(dproj, h01, "wgrad_ab_in", _ScatterRider([gw_out1]))
    gw_about, _ = _wgrad(cat, dy01, "wgrad_ab_out")
    (dx0, dgu00, a00, dy00, st00), (p_abin, p_about) = _ffn_bwd(
        dx1, x0, pb[0][0], g00, u00, y00, win[0], wout[0], 0, _ScatterRider([gw_abin, gw_about]))
    gw_in0, _ = _wgrad(dgu00, h00, "wgrad_in_0")
    gw_out0, (p_in[0],) = _wgrad(a00, dy00, "wgrad_out_0", _ScatterRider([gw_in0]))
    (p_out[0],) = _run_rider(_ScatterRider([gw_out0]), "scatter_out_0")
    grad_x = dx0[None]

    shape_in, shape_out = w_ffn_in.shape, w_ffn_out.shape
    fin = lambda a: a.reshape(4, D, n_in)
    fout = lambda a: a.reshape(4, n_out, D)
    fpool = lambda a: a.reshape(1, 4 * n_pool, POOL_G)
    r_in = _finish(p_in, fin(w_ffn_in), fin(m_w_ffn_in), fin(v_w_ffn_in), True, 256, "finish_ffn_in")
    r_out = _finish(p_out, fout(w_ffn_out), fout(m_w_ffn_out), fout(v_w_ffn_out), False, n_out // 2, "finish_ffn_out")
    r_abin = _finish([p_abin], ab_w_in, m_ab_w_in, v_ab_w_in, True, 256, "finish_ab_in")
    r_about = _finish([p_about], ab_w_out, m_ab_w_out, v_ab_w_out, False, n_about, "finish_ab_out")
    r_pool = _finish([p_pool.reshape(N_DEV, 4 * n_pool, POOL_G)], fpool(pool_w_grp), fpool(m_pool_w_grp), fpool(v_pool_w_grp),
                     False, 4 * n_pool, "finish_pool")
    r_in = [a.reshape(shape_in) for a in r_in]
    r_out = [a.reshape(shape_out) for a in r_out]
    r_pool = [a.reshape(pool_w_grp.shape) for a in r_pool]

    stats = [[st00, st01, st02], [st10, st11, st12]]
    dmod = jnp.stack([jnp.concatenate([stats[l][s][0:3].reshape(-1) for s in range(3)]) for l in range(2)])
    dng = jnp.stack([jnp.stack([stats[l][s][3] for s in range(3)]) for l in range(2)])
    spack, so = _pack([dmod, dng, dnv[0], dws, dbs[:, :, 0], dcw[0:3], dps[0], dfg[0]])
    sgot, ssum = _exchange_small(spack, "reduce_small", True)
    ssum = ssum.reshape(-1)
    take = lambda i, n: lax.dynamic_slice_in_dim(ssum, so[i], n)
    g_bmod = take(0, 2 * 9 * D).reshape(2, 9 * D)
    g_ng = lax.dynamic_slice_in_dim(take(1, 6 * D).reshape(2, 3, D), me * n_ng, n_ng, axis=2)
    g_nv = take(2, D_A).reshape(1, D_A)
    g_ws = take(3, A_HEADS * CHUNK * CHUNK).reshape(1, A_HEADS, CHUNK, CHUNK)
    g_bs = take(4, A_HEADS * CHUNK).reshape(1, A_HEADS, CHUNK)
    g_cw = lax.dynamic_slice_in_dim(take(5, 3 * D_B).reshape(1, 3, D_B), me * n_cw, n_cw, axis=2)
    g_ps = lax.dynamic_slice_in_dim(take(6, D).reshape(1, D), me * n_ps, n_ps, axis=1)
    g_fg = take(7, D)

    dmod_all = sgot.reshape(N_DEV, -1)[:, so[0]:so[0] + 2 * 9 * D].reshape(N_DEV, 2, 9 * D)
    dmod_cols = lax.dynamic_slice_in_dim(dmod_all, me * n_mod, n_mod, axis=2).transpose(1, 0, 2)
    r_wmod = _wmod_finish(act_all.T, dmod_cols, w_mod, m_w_mod, v_w_mod)

    small_w = [b_mod, norm_g, ab_norm_v, ab_w_s, ab_b_s, ab_conv_w, pool_scale, final_g]
    small_g = [g_bmod, g_ng, g_nv, g_ws, g_bs, g_cw, g_ps, g_fg]
    small_m = [m_b_mod, m_norm_g, m_ab_norm_v, m_ab_w_s, m_ab_b_s, m_ab_conv_w, m_pool_scale, m_final_g]
    small_v = [v_b_mod, v_norm_g, v_ab_norm_v, v_ab_w_s, v_ab_b_s, v_ab_conv_w, v_pool_scale, v_final_g]
    pw, po = _pack(small_w)
    pv = jnp.concatenate([jnp.pad(a.reshape(-1), (0, -a.size % 128), constant_values=1.0) for a in small_v]).reshape(-1, 128)
    sd, sm, sv = _adamw(pw, _pack(small_g)[0], _pack(small_m)[0], pv, "adamw_small")
    unpack = lambda packed: [packed.reshape(-1)[po[i]:po[i] + a.size].reshape(a.shape) for i, a in enumerate(small_w)]
    d_s, m_s, v_s = unpack(sd), unpack(sm), unpack(sv)

    def ordered(k, small):
        return [small[1], r_wmod[k], small[0], r_in[k], r_out[k], r_abin[k], small[2], small[3], small[4], small[5],
                r_about[k], r_pool[k], small[6], small[7]]

    grads = ordered(0, small_g)
    deltas = ordered(1, d_s)
    new_m = ordered(2, m_s)
    new_v = ordered(3, v_s)
    return (loss, grad_x, *grads, *deltas, *new_m, *new_v)
```

```python
import functools
import math

import jax
import jax.numpy as jnp
from jax import lax
from jax.experimental import pallas as pl
from jax.experimental.pallas import tpu as pltpu

F32 = jnp.float32
BF16 = jnp.bfloat16

N_DEV = 8
D = 1024
DFF = 2816
HC = 256
NCH = DFF // HC
D_A = 512
D_B = 512
D_AB = 2 * D_A + 3 * D_B
CHUNK = 128
A_HEADS = 4
POOL_G = 256
POOL_HALO = 16
CONV_HALO = 8
EPS = 1e-6
TM = 256
GELU_K = math.sqrt(2.0 / math.pi)
GELU_C = 0.044715

ADAM_LR = 0.001
ADAM_B1 = 0.9
ADAM_B2 = 0.999
ADAM_EPS = 1e-08
ADAM_WD = 0.01
ADAM_STEP = 10

VMEM_LIMIT = 56 * 1024 * 1024
MESH_ID = pl.DeviceIdType.MESH
ANY = pl.BlockSpec(memory_space=pl.ANY)
VMEM_SPEC = pl.BlockSpec(memory_space=pltpu.VMEM)
ARB1 = pltpu.CompilerParams(dimension_semantics=("arbitrary",), vmem_limit_bytes=VMEM_LIMIT)


def _dot_nt(a, b):
    return lax.dot_general(a, b, (((1,), (1,)), ((), ())), preferred_element_type=F32)


def _dot_nn(a, b):
    return lax.dot_general(a, b, (((1,), (0,)), ((), ())), preferred_element_type=F32)


def _dot_tn(a, b):
    return lax.dot_general(a, b, (((0,), (0,)), ((), ())), preferred_element_type=F32)


def _colsum8(v):
    r, n = v.shape
    return jnp.sum(v.reshape(r // 8, 8, n), axis=0)


def _gelu(x):
    return 0.5 * x * (1.0 + jnp.tanh(GELU_K * (x + GELU_C * x * x * x)))


def _gelu_grad(x):
    t = jnp.tanh(GELU_K * (x + GELU_C * x * x * x))
    return 0.5 * (1.0 + t) + 0.5 * x * (1.0 - t * t) * (GELU_K * (1.0 + 3.0 * GELU_C * x * x))


def _mod_rows(p_ref):
    return p_ref[0:1, :], p_ref[1:2, :], p_ref[2:3, :], p_ref[3:4, :]


def _modulate(x, gn, sh, sc):
    r = lax.rsqrt(jnp.mean(x * x, axis=-1, keepdims=True) + EPS)
    return ((x * r) * gn) * (1.0 + sc) + sh


def _modulate_bwd(dh, x, gn, sc, stats):
    r = lax.rsqrt(jnp.mean(x * x, axis=-1, keepdims=True) + EPS)
    xn = x * r
    stats[0] += _colsum8(dh)
    stats[1] += _colsum8(dh * (xn * gn))
    dy0 = dh * (1.0 + sc)
    stats[3] += _colsum8(dy0 * xn)
    dxn = dy0 * gn
    return r * (dxn - xn * jnp.mean(dxn * xn, axis=-1, keepdims=True))


def _stats_out(stats, out_ref):
    rows = [jnp.sum(stats[k], axis=0, keepdims=True) for k in range(4)]
    out_ref[...] = jnp.concatenate(rows + [jnp.zeros((4, stats.shape[-1]), F32)], axis=0)


def _shift_down(v, k, prev):
    n = v.shape[0]
    row = lax.broadcasted_iota(jnp.int32, v.shape, 0)
    out = pltpu.roll(v, k, 0)
    for j in range(k):
        out = jnp.where(row == j, prev[prev.shape[0] - k + j:prev.shape[0] - k + j + 1, :], out)
    return out


def _shift_up(v, k, nxt):
    n = v.shape[0]
    row = lax.broadcasted_iota(jnp.int32, v.shape, 0)
    out = pltpu.roll(v, n - k, 0)
    for j in range(k):
        out = jnp.where(row == n - k + j, nxt[j:j + 1, :], out)
    return out


def _load_rows(w_hbm, sel, dst, sems, base):
    n = dst.shape[0] // N_DEV
    cps = []
    for k in range(N_DEV):
        src = w_hbm.at[k] if sel is None else w_hbm.at[k, sel]
        cps.append(pltpu.make_async_copy(src, dst.at[pl.ds(k * n, n)], sems.at[base + k]))
    return cps


def _my_pos():
    return lax.axis_index("x"), lax.axis_index("y"), lax.axis_index("c")


def _peer(j):
    x, y, c = _my_pos()
    return (1 - x if j & 4 else x, 1 - y if j & 2 else y, 1 - c if j & 1 else c)


def _index(pos):
    return 4 * pos[0] + 2 * pos[1] + pos[2]


def _exchange_small(v, name, with_sum):
    rows = v.shape[0]

    def body(v_ref, *refs):
        if with_sum:
            out_ref, sum_ref, send_sems, recv_sems, local_sem = refs
        else:
            out_ref, send_sems, recv_sems, local_sem = refs
        me = _index(_my_pos())

        def copy(j, slot):
            return pltpu.make_async_remote_copy(
                src_ref=v_ref, dst_ref=out_ref.at[slot], send_sem=send_sems.at[j - 1], recv_sem=recv_sems.at[j - 1],
                device_id=_peer(j), device_id_type=MESH_ID)

        mine = pltpu.make_async_copy(v_ref, out_ref.at[me], local_sem)
        mine.start()
        sends = [copy(j, me) for j in range(1, N_DEV)]
        for cp in sends:
            cp.start()
        for j in range(1, N_DEV):
            copy(j, _index(_peer(j))).wait_recv()
        for cp in sends:
            cp.wait_send()
        mine.wait()
        if with_sum:
            acc = out_ref[0]
            for k in range(1, N_DEV):
                acc = acc + out_ref[k]
            sum_ref[...] = acc

    out_shape = [jax.ShapeDtypeStruct((N_DEV, rows, 128), F32)]
    out_specs = [VMEM_SPEC]
    if with_sum:
        out_shape.append(jax.ShapeDtypeStruct((rows, 128), F32))
        out_specs.append(VMEM_SPEC)
    res = pl.pallas_call(
        body, name=name, out_shape=out_shape, in_specs=[VMEM_SPEC], out_specs=out_specs,
        scratch_shapes=[pltpu.SemaphoreType.DMA((N_DEV - 1,)), pltpu.SemaphoreType.DMA((N_DEV - 1,)),
                        pltpu.SemaphoreType.DMA(())],
    )(v)
    return res if with_sum else res[0]


class _GatherRider:
    has_middle = True

    def __init__(self, shards):
        n = len(shards)
        self.inputs = list(shards)
        self.out_shapes = [jax.ShapeDtypeStruct((N_DEV,) + s.shape, s.dtype) for s in shards]
        self.scratch = [pltpu.SemaphoreType.DMA((7 * n,)), pltpu.SemaphoreType.DMA((7 * n,)), pltpu.SemaphoreType.DMA((n,))]

    def _ctx(self, outs, scr):
        send, recv, _ = scr
        x, y, c = _my_pos()
        chips = [(1 - x, y), (x, 1 - y), (1 - x, 1 - y)]

        def copy(a, k, block, to, src=None):
            slot = outs[a].at[_index(block)]
            return pltpu.make_async_remote_copy(
                src_ref=slot if src is None else src, dst_ref=slot, send_sem=send.at[7 * a + k], recv_sem=recv.at[7 * a + k],
                device_id=to, device_id_type=MESH_ID)

        return (x, y, c), (x, y, 1 - c), chips, copy

    def _sends(self, ins, outs, scr):
        me, sib, chips, copy = self._ctx(outs, scr)
        out = []
        for a in range(len(ins)):
            out.append(copy(a, 0, me, sib, src=ins[a]))
            out += [copy(a, 1 + j, me, (*chips[j], me[2]), src=ins[a]) for j in range(3)]
        return out

    def first(self, ins, outs, scr):
        me = _index(_my_pos())
        for a in range(len(ins)):
            pltpu.make_async_copy(ins[a], outs[a].at[me], scr[2].at[a]).start()
        for cp in self._sends(ins, outs, scr):
            cp.start()

    def middle(self, ins, outs, scr):
        me, sib, chips, copy = self._ctx(outs, scr)
        for j in range(3):
            for a in range(len(ins)):
                copy(a, 1 + j, (*chips[j], me[2]), me).wait_recv()
                copy(a, 4 + j, (*chips[j], me[2]), sib).start()

    def last(self, ins, outs, scr):
        me, sib, chips, copy = self._ctx(outs, scr)
        for a in range(len(ins)):
            copy(a, 0, sib, me).wait_recv()
            for j in range(3):
                copy(a, 4 + j, (*chips[j], sib[2]), me).wait_recv()
        for cp in self._sends(ins, outs, scr):
            cp.wait_send()
        for a in range(len(ins)):
            for j in range(3):
                copy(a, 4 + j, (*chips[j], me[2]), sib).wait_send()
            pltpu.make_async_copy(ins[a], outs[a].at[_index(me)], scr[2].at[a]).wait()


class _ScatterRider:
    has_middle = False

    def __init__(self, grads):
        n = len(grads)
        self.inputs = list(grads)
        self.out_shapes = []
        for g in grads:
            if g.ndim == 3:
                self.out_shapes.append(jax.ShapeDtypeStruct((N_DEV, g.shape[0], g.shape[1] // N_DEV, g.shape[2]), g.dtype))
            else:
                self.out_shapes.append(jax.ShapeDtypeStruct((N_DEV, g.shape[0] // N_DEV, g.shape[1]), g.dtype))
        self.scratch = [pltpu.SemaphoreType.DMA((7 * n,)), pltpu.SemaphoreType.DMA((7 * n,)), pltpu.SemaphoreType.DMA((n,))]

    @staticmethod
    def _part(ref, k):
        if ref.ndim == 3:
            n = ref.shape[1] // N_DEV
            return ref.at[:, pl.ds(pl.multiple_of(k * n, 16), n)]
        n = ref.shape[0] // N_DEV
        return ref.at[pl.ds(pl.multiple_of(k * n, 16), n)]

    def _copy(self, ins, outs, scr, g, j, to, src_dev):
        return pltpu.make_async_remote_copy(
            src_ref=self._part(ins[g], to), dst_ref=outs[g].at[src_dev], send_sem=scr[0].at[7 * g + j - 1],
            recv_sem=scr[1].at[7 * g + j - 1], device_id=_peer(j), device_id_type=MESH_ID)

    def first(self, ins, outs, scr):
        me = _index(_my_pos())
        for g in range(len(ins)):
            pltpu.make_async_copy(self._part(ins[g], me), outs[g].at[me], scr[2].at[g]).start()
        for j in range(1, N_DEV):
            for g in range(len(ins)):
                self._copy(ins, outs, scr, g, j, _index(_peer(j)), me).start()

    def last(self, ins, outs, scr):
        me = _index(_my_pos())
        for j in range(1, N_DEV):
            for g in range(len(ins)):
                self._copy(ins, outs, scr, g, j, me, _index(_peer(j))).wait_recv()
        for j in range(1, N_DEV):
            for g in range(len(ins)):
                self._copy(ins, outs, scr, g, j, _index(_peer(j)), me).wait_send()
        for g in range(len(ins)):
            pltpu.make_async_copy(self._part(ins[g], me), outs[g].at[me], scr[2].at[g]).wait()


def _run(body, *, name, grid, in_specs, out_specs, out_shape, scratch_shapes, args, rider=None, params=None):
    params = ARB1 if params is None else params
    if rider is None:
        outs = pl.pallas_call(body, name=name, grid=grid, in_specs=in_specs, out_specs=out_specs, out_shape=out_shape,
                              scratch_shapes=scratch_shapes, compiler_params=params)(*args)
        return list(outs), []
    ni, no, ns = len(in_specs), len(out_shape), len(scratch_shapes)
    ri, ro = len(rider.inputs), len(rider.out_shapes)
    steps = grid[0]

    def wrapped(*refs):
        cut = [ni, ni + ri, ni + ri + no, ni + ri + no + ro, ni + ri + no + ro + ns]
        a, b, c, d, e, f = (refs[lo:hi] for lo, hi in zip([0] + cut, cut + [len(refs)]))
        i = pl.program_id(0)

        @pl.when(i == 0)
        def _():
            rider.first(b, d, f)

        if rider.has_middle:
            @pl.when(i == steps - 1)
            def _():
                rider.middle(b, d, f)

        body(*a, *c, *e)

        @pl.when(i == steps - 1)
        def _():
            rider.last(b, d, f)

    outs = pl.pallas_call(
        wrapped, name=name, grid=grid, in_specs=list(in_specs) + [ANY] * ri, out_specs=list(out_specs) + [ANY] * ro,
        out_shape=list(out_shape) + rider.out_shapes, scratch_shapes=list(scratch_shapes) + rider.scratch,
        compiler_params=params)(*args, *rider.inputs)
    return list(outs[:no]), list(outs[no:])


def _run_rider(rider, name):
    ri, ro = len(rider.inputs), len(rider.out_shapes)

    def body(*refs):
        b, d, f = refs[:ri], refs[ri:ri + ro], refs[ri + ro:]
        rider.first(b, d, f)
        if rider.has_middle:
            rider.middle(b, d, f)
        rider.last(b, d, f)

    return list(pl.pallas_call(body, name=name, in_specs=[ANY] * ri, out_specs=[ANY] * ro, out_shape=rider.out_shapes,
                               scratch_shapes=rider.scratch)(*rider.inputs))


def _mod_fwd(c_all, w_mod):
    ncol = w_mod.shape[-1]

    def body(c_ref, w_ref, act_ref, out_ref):
        c = c_ref[...]
        act = c * jax.nn.sigmoid(c)
        act_ref[...] = act
        out_ref[0] = _dot_nn(act.astype(BF16), w_ref[0].astype(BF16))

    return pl.pallas_call(
        body, name="mod_fwd", grid=(2,),
        out_shape=[jax.ShapeDtypeStruct((N_DEV, D), F32), jax.ShapeDtypeStruct((2, N_DEV, ncol), F32)],
        in_specs=[pl.BlockSpec((N_DEV, D), lambda l: (0, 0)), pl.BlockSpec((1, D, ncol), lambda l: (l, 0, 0))],
        out_specs=[pl.BlockSpec((N_DEV, D), lambda l: (0, 0)), pl.BlockSpec((1, N_DEV, ncol), lambda l: (l, 0, 0))],
        compiler_params=ARB1,
    )(c_all, w_mod)


def _ffn_fwd(x, p, win_all, wout_all, f, rider=None):
    s = x.shape[0]
    nt = s // TM

    def body(x_ref, p_ref, win_hbm, wout_hbm, xo_ref, h_ref, g_ref, u_ref, y_ref, win, wout, act, sems):
        @pl.when(pl.program_id(0) == 0)
        def _():
            cps = _load_rows(win_hbm, None, win, sems, 0) + _load_rows(wout_hbm, None, wout, sems, N_DEV)
            for cp in cps:
                cp.start()
            for cp in cps:
                cp.wait()

        sh, sc, gate, gn = _mod_rows(p_ref)
        x = x_ref[...]
        hb = _modulate(x, gn, sh, sc).astype(BF16)
        h_ref[...] = hb
        for c in range(NCH):
            g = _dot_nt(hb, win[c * HC:(c + 1) * HC, :])
            u = _dot_nt(hb, win[DFF + c * HC:DFF + (c + 1) * HC, :])
            g_ref[c] = g.astype(BF16)
            u_ref[c] = u.astype(BF16)
            act[:, c * HC:(c + 1) * HC] = ((g * jax.nn.sigmoid(g)) * u).astype(BF16)
        y = _dot_nn(act[...], wout[...])
        y_ref[...] = y
        xo_ref[...] = x + (0.5 * gate) * y

    tile = pl.BlockSpec((TM, D), lambda i: (i, 0))
    chunks = pl.BlockSpec((NCH, TM, HC), lambda i: (0, i, 0))
    return _run(
        body, name=f"ffn_fwd_{f}", grid=(nt,),
        out_shape=[jax.ShapeDtypeStruct((s, D), F32), jax.ShapeDtypeStruct((s, D), BF16),
                   jax.ShapeDtypeStruct((NCH, s, HC), BF16), jax.ShapeDtypeStruct((NCH, s, HC), BF16),
                   jax.ShapeDtypeStruct((s, D), F32)],
        in_specs=[tile, pl.BlockSpec((8, D), lambda i: (0, 0)), ANY, ANY],
        out_specs=[tile, tile, chunks, chunks, tile],
        scratch_shapes=[pltpu.VMEM((2 * DFF, D), BF16), pltpu.VMEM((DFF, D), BF16), pltpu.VMEM((TM, DFF), BF16),
                        pltpu.SemaphoreType.DMA((2 * N_DEV,))],
        args=(x, p, win_all, wout_all), rider=rider)


def _ffn_bwd(dxo, x, p, g3, u3, y, win_all, wout_all, f, rider=None):
    s = x.shape[0]
    nt = s // TM

    def body(dxo_ref, x_ref, p_ref, g_ref, u_ref, y_ref, win_hbm, wout_hbm,
             dx_ref, dgu_ref, a_ref, dy_ref, st_ref, win, wout, dgu, stats, sems):
        i = pl.program_id(0)

        @pl.when(i == 0)
        def _():
            cps = _load_rows(win_hbm, None, win, sems, 0) + _load_rows(wout_hbm, None, wout, sems, N_DEV)
            for cp in cps:
                cp.start()
            stats[...] = jnp.zeros_like(stats)
            for cp in cps:
                cp.wait()

        sh, sc, gate, gn = _mod_rows(p_ref)
        x = x_ref[...]
        dxo = dxo_ref[...]
        dyb = ((0.5 * gate) * dxo).astype(BF16)
        dy_ref[...] = dyb
        stats[2] += _colsum8((0.5 * dxo) * y_ref[...])
        for c in range(NCH):
            da = _dot_nt(dyb, wout[c * HC:(c + 1) * HC, :])
            g = g_ref[c].astype(F32)
            u = u_ref[c].astype(F32)
            sg = jax.nn.sigmoid(g)
            si = g * sg
            dg = ((da * u) * (sg * (1.0 + g * (1.0 - sg)))).astype(BF16)
            du = (da * si).astype(BF16)
            a_ref[c] = (si * u).astype(BF16)
            dgu_ref[c] = dg
            dgu_ref[NCH + c] = du
            dgu[:, c * HC:(c + 1) * HC] = dg
            dgu[:, DFF + c * HC:DFF + (c + 1) * HC] = du
        dh = _dot_nn(dgu[...], win[...])
        dx_ref[...] = dxo + _modulate_bwd(dh, x, gn, sc, stats)

        @pl.when(i == nt - 1)
        def _():
            _stats_out(stats, st_ref)

    tile = pl.BlockSpec((TM, D), lambda i: (i, 0))
    chunks = pl.BlockSpec((NCH, TM, HC), lambda i: (0, i, 0))
    small = pl.BlockSpec((8, D), lambda i: (0, 0))
    return _run(
        body, name=f"ffn_bwd_{f}", grid=(nt,),
        out_shape=[jax.ShapeDtypeStruct((s, D), F32), jax.ShapeDtypeStruct((2 * NCH, s, HC), BF16),
                   jax.ShapeDtypeStruct((NCH, s, HC), BF16), jax.ShapeDtypeStruct((s, D), BF16),
                   jax.ShapeDtypeStruct((8, D), F32)],
        in_specs=[tile, tile, small, chunks, chunks, tile, ANY, ANY],
        out_specs=[tile, pl.BlockSpec((2 * NCH, TM, HC), lambda i: (0, i, 0)), chunks, tile, small],
        scratch_shapes=[pltpu.VMEM((2 * DFF, D), BF16), pltpu.VMEM((DFF, D), BF16), pltpu.VMEM((TM, 2 * DFF), BF16),
                        pltpu.VMEM((4, 8, D), F32), pltpu.SemaphoreType.DMA((2 * N_DEV,))],
        args=(dxo, x, p, g3, u3, y, win_all, wout_all), rider=rider)


def _wgrad(lhs3, rhs, name, rider=None):
    nj, s, _ = lhs3.shape
    n = rhs.shape[1]

    def body(l_ref, r_ref, o_ref):
        o_ref[...] = _dot_tn(l_ref[0], r_ref[...]).astype(BF16)

    outs, rode = _run(
        body, name=name, grid=(nj,),
        out_shape=[jax.ShapeDtypeStruct((nj * HC, n), BF16)],
        in_specs=[pl.BlockSpec((1, s, HC), lambda j: (j, 0, 0)), pl.BlockSpec((s, n), lambda j: (0, 0))],
        out_specs=[pl.BlockSpec((HC, n), lambda j: (j, 0))],
        scratch_shapes=[], args=(lhs3, rhs), rider=rider)
    return outs[0], rode


def _gating(proj, nv, ws_ref, bst):
    u, v = proj[:, 0:D_A], proj[:, D_A:2 * D_A]
    gu, gv = _gelu(u), _gelu(v)
    mu = jnp.mean(gv, axis=-1, keepdims=True)
    dv = gv - mu
    rstd = lax.rsqrt(jnp.mean(dv * dv, axis=-1, keepdims=True) + EPS)
    vhat = dv * rstd
    vn = vhat * nv
    r = lax.broadcasted_iota(jnp.int32, (CHUNK, CHUNK), 0)
    c = lax.broadcasted_iota(jnp.int32, (CHUNK, CHUNK), 1)
    wm = [jnp.where(r >= c, ws_ref[hd], 0.0).astype(BF16) for hd in range(A_HEADS)]
    vnb = vn.astype(BF16)
    rows = []
    for n in range(proj.shape[0] // CHUNK):
        blocks = []
        for hd in range(A_HEADS):
            blk = vnb[n * CHUNK:(n + 1) * CHUNK, hd * CHUNK:(hd + 1) * CHUNK]
            blocks.append(_dot_nn(wm[hd], blk) + bst[:, hd:hd + 1])
        rows.append(jnp.concatenate(blocks, axis=1))
    z = jnp.concatenate(rows, axis=0)
    return u, v, gu, rstd, vhat, vnb, wm, z


def _conv(proj, cw, prev_xp):
    bg = proj[:, 2 * D_A:2 * D_A + D_B]
    cg = proj[:, 2 * D_A + D_B:2 * D_A + 2 * D_B]
    xb = proj[:, 2 * D_A + 2 * D_B:]
    xp = cg * xb
    x1 = _shift_down(xp, 1, prev_xp)
    x2 = _shift_down(xp, 2, prev_xp)
    conv = cw[0:1, :] * x2 + cw[1:2, :] * x1 + cw[2:3, :] * xp
    return bg, cg, xb, xp, x1, x2, conv


def _ab_fwd(x, p, abin_all, about_all, nv, ws, bst, cw, rider=None):
    s = x.shape[0]
    nt = s // TM

    def body(x_ref, p_ref, abin_hbm, about_hbm, nv_ref, ws_ref, bst_ref, cw_ref,
             xo_ref, h_ref, proj_ref, out_ref, abin, about, prev, sems):
        @pl.when(pl.program_id(0) == 0)
        def _():
            cps = _load_rows(abin_hbm, None, abin, sems, 0) + _load_rows(about_hbm, None, about, sems, N_DEV)
            for cp in cps:
                cp.start()
            prev[...] = jnp.zeros_like(prev)
            for cp in cps:
                cp.wait()

        sh, sc, gate, gn = _mod_rows(p_ref)
        x = x_ref[...]
        hb = _modulate(x, gn, sh, sc).astype(BF16)
        h_ref[...] = hb
        proj = _dot_nt(hb, abin[...])
        proj_ref[...] = proj
        _, _, gu, _, _, _, _, z = _gating(proj, nv_ref[...], ws_ref, bst_ref[...])
        bg, _, _, xp, _, _, conv = _conv(proj, cw_ref[...], prev[...])
        prev[...] = xp[TM - CONV_HALO:, :]
        cat = jnp.concatenate([gu * z, bg * conv], axis=1).astype(BF16)
        out = _dot_nn(cat, about[...])
        out_ref[...] = out
        xo_ref[...] = x + gate * out

    tile = pl.BlockSpec((TM, D), lambda i: (i, 0))
    full = lambda a: pl.BlockSpec(a.shape, lambda i: (0,) * a.ndim)
    return _run(
        body, name="ab_fwd", grid=(nt,),
        out_shape=[jax.ShapeDtypeStruct((s, D), F32), jax.ShapeDtypeStruct((s, D), BF16),
                   jax.ShapeDtypeStruct((s, D_AB), F32), jax.ShapeDtypeStruct((s, D), F32)],
        in_specs=[tile, pl.BlockSpec((8, D), lambda i: (0, 0)), ANY, ANY, full(nv), full(ws), full(bst), full(cw)],
        out_specs=[tile, tile, pl.BlockSpec((TM, D_AB), lambda i: (i, 0)), tile],
        scratch_shapes=[pltpu.VMEM((D_AB, D), BF16), pltpu.VMEM((D, D), BF16), pltpu.VMEM((CONV_HALO, D_B), F32),
                        pltpu.SemaphoreType.DMA((2 * N_DEV,))],
        args=(x, p, abin_all, about_all, nv, ws, bst, cw), rider=rider)


def _ab_bwd(dxo, x, p, proj, out, abin_all, about_all, nv, ws, bst, cw, rider=None):
    s = x.shape[0]
    nt = s // TM
    npj = D_AB // HC

    def body(dxo_ref, x_ref, p_ref, proj_ref, halo_ref, out_ref, abin_hbm, about_hbm, nv_ref, ws_ref, bst_ref, cw_ref,
             dx_ref, dproj_ref, cat_ref, dy_ref, st_ref, dnv_ref, dws_ref, dbs_ref, dcw_ref,
             abin, about, nxt, stats, dnv, dws, dbs, dcw, sems):
        i = pl.program_id(0)
        ti = nt - 1 - i

        @pl.when(i == 0)
        def _():
            cps = _load_rows(abin_hbm, None, abin, sems, 0) + _load_rows(about_hbm, None, about, sems, N_DEV)
            for cp in cps:
                cp.start()
            for z in (nxt, stats, dnv, dws, dbs, dcw):
                z[...] = jnp.zeros_like(z)
            for cp in cps:
                cp.wait()

        sh, sc, gate, gn = _mod_rows(p_ref)
        x = x_ref[...]
        dxo = dxo_ref[...]
        dyb = (gate * dxo).astype(BF16)
        dy_ref[...] = dyb
        stats[2] += _colsum8(dxo * out_ref[...])
        dcat = _dot_nt(dyb, about[...])
        dya, dyb2 = dcat[:, 0:D_A], dcat[:, D_A:]

        proj = proj_ref[...]
        nvv = nv_ref[...]
        u, v, gu, rstd, vhat, vnb, wm, z = _gating(proj, nvv, ws_ref, bst_ref[...])
        dgu = dya * z
        dzb = (dya * gu).astype(BF16)
        dz32 = dya * gu
        rows = []
        for n in range(TM // CHUNK):
            blocks = []
            for hd in range(A_HEADS):
                sl = (slice(n * CHUNK, (n + 1) * CHUNK), slice(hd * CHUNK, (hd + 1) * CHUNK))
                dbs[hd] += dz32[sl]
                dws[hd] += _dot_nt(dzb[sl], vnb[sl])
                blocks.append(_dot_tn(wm[hd], dzb[sl]))
            rows.append(jnp.concatenate(blocks, axis=1))
        dvn = jnp.concatenate(rows, axis=0)
        dnv[...] += _colsum8(dvn * vhat)
        dvh = dvn * nvv
        dgv = rstd * (dvh - jnp.mean(dvh, axis=-1, keepdims=True) - vhat * jnp.mean(dvh * vhat, axis=-1, keepdims=True))
        du = dgu * _gelu_grad(u)
        dv = dgv * _gelu_grad(v)

        halo = halo_ref[...]
        prev_xp = jnp.where(ti > 0, halo[:, 2 * D_A + D_B:2 * D_A + 2 * D_B] * halo[:, 2 * D_A + 2 * D_B:], 0.0)
        cwv = cw_ref[...]
        bg, cg, xb, xp, x1, x2, conv = _conv(proj, cwv, prev_xp)
        dbg = dyb2 * conv
        dconv = dyb2 * bg
        dcw[...] += jnp.concatenate(
            [jnp.sum(_colsum8(dconv * t), axis=0, keepdims=True) for t in (x2, x1, xp)] + [jnp.zeros((5, D_B), F32)], axis=0)
        nx = nxt[...]
        dxp = cwv[2:3, :] * dconv + cwv[1:2, :] * _shift_up(dconv, 1, nx) + cwv[0:1, :] * _shift_up(dconv, 2, nx)
        nxt[...] = dconv[0:CONV_HALO, :]
        dcg = dxp * xb
        dxb = dxp * cg

        dproj = jnp.concatenate([du, dv, dbg, dcg, dxb], axis=1).astype(BF16)
        for k in range(npj):
            dproj_ref[k] = dproj[:, k * HC:(k + 1) * HC]
        cat = jnp.concatenate([gu * z, bg * conv], axis=1).astype(BF16)
        for k in range(D // HC):
            cat_ref[k] = cat[:, k * HC:(k + 1) * HC]
        dh = _dot_nn(dproj, abin[...])
        dx_ref[...] = dxo + _modulate_bwd(dh, x, gn, sc, stats)

        @pl.when(i == nt - 1)
        def _():
            _stats_out(stats, st_ref)
            dnv_ref[...] = jnp.concatenate([jnp.sum(dnv[...], axis=0, keepdims=True), jnp.zeros((7, D_A), F32)], axis=0)
            r = lax.broadcasted_iota(jnp.int32, (CHUNK, CHUNK), 0)
            c = lax.broadcasted_iota(jnp.int32, (CHUNK, CHUNK), 1)
            for hd in range(A_HEADS):
                dws_ref[hd] = jnp.where(r >= c, dws[hd], 0.0)
                dbs_ref[hd] = jnp.broadcast_to(jnp.sum(dbs[hd], axis=1, keepdims=True), (CHUNK, CHUNK))
            dcw_ref[...] = dcw[...]

    rev = pl.BlockSpec((TM, D), lambda i: (nt - 1 - i, 0))
    small = pl.BlockSpec((8, D), lambda i: (0, 0))
    full = lambda a: pl.BlockSpec(a.shape, lambda i: (0,) * a.ndim)
    hpt = TM // CONV_HALO
    fixed = lambda shape: pl.BlockSpec(shape, lambda i: (0,) * len(shape))
    return _run(
        body, name="ab_bwd", grid=(nt,),
        out_shape=[jax.ShapeDtypeStruct((s, D), F32), jax.ShapeDtypeStruct((npj, s, HC), BF16),
                   jax.ShapeDtypeStruct((D // HC, s, HC), BF16), jax.ShapeDtypeStruct((s, D), BF16),
                   jax.ShapeDtypeStruct((8, D), F32), jax.ShapeDtypeStruct((8, D_A), F32),
                   jax.ShapeDtypeStruct((A_HEADS, CHUNK, CHUNK), F32), jax.ShapeDtypeStruct((A_HEADS, CHUNK, CHUNK), F32),
                   jax.ShapeDtypeStruct((8, D_B), F32)],
        in_specs=[rev, rev, small,
                  pl.BlockSpec((TM, D_AB), lambda i: (nt - 1 - i, 0)),
                  pl.BlockSpec((CONV_HALO, D_AB), lambda i: (jnp.maximum((nt - 1 - i) * hpt - 1, 0), 0)),
                  rev, ANY, ANY, full(nv), full(ws), full(bst), full(cw)],
        out_specs=[rev, pl.BlockSpec((npj, TM, HC), lambda i: (0, nt - 1 - i, 0)),
                   pl.BlockSpec((D // HC, TM, HC), lambda i: (0, nt - 1 - i, 0)), rev,
                   small, fixed((8, D_A)), fixed((A_HEADS, CHUNK, CHUNK)), fixed((A_HEADS, CHUNK, CHUNK)), fixed((8, D_B))],
        scratch_shapes=[pltpu.VMEM((D_AB, D), BF16), pltpu.VMEM((D, D), BF16), pltpu.VMEM((CONV_HALO, D_B), F32),
                        pltpu.VMEM((4, 8, D), F32), pltpu.VMEM((8, D_A), F32),
                        pltpu.VMEM((A_HEADS, CHUNK, CHUNK), F32), pltpu.VMEM((A_HEADS, CHUNK, CHUNK), F32),
                        pltpu.VMEM((8, D_B), F32), pltpu.SemaphoreType.DMA((2 * N_DEV,))],
        args=(dxo, x, p, proj, proj, out, abin_all, about_all, nv, ws, bst, cw), rider=rider)


def _pool_counts(first_token, rows):
    t = (first_token + lax.broadcasted_iota(jnp.int32, (rows, 1), 0) + 1).astype(F32)
    lane = lax.broadcasted_iota(jnp.int32, (1, D), 1)
    w = jnp.where(lane < POOL_G, 2.0, jnp.where(lane < 2 * POOL_G, 4.0, jnp.where(lane < 3 * POOL_G, 8.0, 16.0)))
    return jnp.minimum(t, w)


def _window_sums(ext, n_keep, lead, back):
    n = ext.shape[0]
    sh = (lambda v, k: pltpu.roll(v, k, 0)) if back else (lambda v, k: pltpu.roll(v, n - k, 0))
    s2 = ext + sh(ext, 1)
    s4 = s2[:, POOL_G:] + sh(s2[:, POOL_G:], 2)
    s8 = s4[:, POOL_G:] + sh(s4[:, POOL_G:], 4)
    s16 = s8[:, POOL_G:] + sh(s8[:, POOL_G:], 8)
    keep = slice(lead, lead + n_keep)
    return jnp.concatenate([s2[keep, 0:POOL_G], s4[keep, 0:POOL_G], s8[keep, 0:POOL_G], s16[keep, :]], axis=1)


def _pool_fwd(x, p, pool_all, pscale):
    s = x.shape[0]
    nt = s // TM
    ng = D // POOL_G

    def body(x_ref, p_ref, wg_ref, ps_ref, xo_ref, pb_ref, op_ref, prev):
        i = pl.program_id(0)

        @pl.when(i == 0)
        def _():
            prev[...] = jnp.zeros_like(prev)

        sh, sc, gate, gn = _mod_rows(p_ref)
        x = x_ref[...]
        h = _modulate(x, gn, sh, sc)
        win = _window_sums(jnp.concatenate([prev[...], h], axis=0), TM, POOL_HALO, True)
        prev[...] = h[TM - POOL_HALO:, :]
        pb = (win / _pool_counts(i * TM, TM) - h).astype(BF16)
        pb_ref[...] = pb
        op = jnp.concatenate(
            [_dot_nn(pb[:, g * POOL_G:(g + 1) * POOL_G], wg_ref[:, g].reshape(POOL_G, POOL_G)) for g in range(ng)], axis=1)
        op_ref[...] = op
        xo_ref[...] = x + gate * (op * ps_ref[...])

    tile = pl.BlockSpec((TM, D), lambda i: (i, 0))
    return pl.pallas_call(
        body, name="pool_fwd", grid=(nt,),
        out_shape=[jax.ShapeDtypeStruct((s, D), F32), jax.ShapeDtypeStruct((s, D), BF16), jax.ShapeDtypeStruct((s, D), F32)],
        in_specs=[tile, pl.BlockSpec((8, D), lambda i: (0, 0)),
                  pl.BlockSpec(pool_all.shape, lambda i: (0, 0, 0, 0)), pl.BlockSpec((1, D), lambda i: (0, 0))],
        out_specs=[tile, tile, tile],
        scratch_shapes=[pltpu.VMEM((POOL_HALO, D), F32)],
        compiler_params=ARB1,
    )(x, p, pool_all, pscale)


def _pool_bwd(dxo, x, p, pb, op, pool_all, pscale):
    s = x.shape[0]
    nt = s // TM
    ng = D // POOL_G

    def body(dxo_ref, x_ref, p_ref, pb_ref, op_ref, wg_ref, ps_ref,
             dx_ref, st_ref, dps_ref, dwg_ref, nxt, stats, dps, dwg):
        i = pl.program_id(0)
        ti = nt - 1 - i

        @pl.when(i == 0)
        def _():
            for z in (nxt, stats, dps, dwg):
                z[...] = jnp.zeros_like(z)

        sh, sc, gate, gn = _mod_rows(p_ref)
        x = x_ref[...]
        dxo = dxo_ref[...]
        ps = ps_ref[...]
        op = op_ref[...]
        dmo = gate * dxo
        stats[2] += _colsum8(dxo * (op * ps))
        dps[...] += _colsum8(dmo * op)
        dopb = (dmo * ps).astype(BF16)
        pbv = pb_ref[...]
        dps_parts = []
        for g in range(ng):
            sl = slice(g * POOL_G, (g + 1) * POOL_G)
            dps_parts.append(_dot_nt(dopb[:, sl], wg_ref[:, g].reshape(POOL_G, POOL_G)))
            dwg[g] += _dot_tn(pbv[:, sl], dopb[:, sl])
        dp = jnp.concatenate(dps_parts, axis=1)
        q = dp / _pool_counts(ti * TM, TM)
        wsum = _window_sums(jnp.concatenate([q, nxt[...]], axis=0), TM, 0, False)
        nxt[...] = q[0:POOL_HALO, :]
        dx_ref[...] = dxo + _modulate_bwd(wsum - dp, x, gn, sc, stats)

        @pl.when(i == nt - 1)
        def _():
            _stats_out(stats, st_ref)
            dps_ref[...] = jnp.concatenate([jnp.sum(dps[...], axis=0, keepdims=True), jnp.zeros((7, D), F32)], axis=0)
            dwg_ref[...] = dwg[...].astype(BF16)

    rev = pl.BlockSpec((TM, D), lambda i: (nt - 1 - i, 0))
    small = pl.BlockSpec((8, D), lambda i: (0, 0))
    return pl.pallas_call(
        body, name="pool_bwd", grid=(nt,),
        out_shape=[jax.ShapeDtypeStruct((s, D), F32), jax.ShapeDtypeStruct((8, D), F32), jax.ShapeDtypeStruct((8, D), F32),
                   jax.ShapeDtypeStruct((ng, POOL_G, POOL_G), BF16)],
        in_specs=[rev, rev, small, rev, rev,
                  pl.BlockSpec(pool_all.shape, lambda i: (0, 0, 0, 0)), pl.BlockSpec((1, D), lambda i: (0, 0))],
        out_specs=[rev, small, small, pl.BlockSpec((ng, POOL_G, POOL_G), lambda i: (0, 0, 0))],
        scratch_shapes=[pltpu.VMEM((POOL_HALO, D), F32), pltpu.VMEM((4, 8, D), F32), pltpu.VMEM((8, D), F32),
                        pltpu.VMEM((ng, POOL_G, POOL_G), F32)],
        compiler_params=ARB1,
    )(dxo, x, p, pb, op, pool_all, pscale)


def _head(x, fg, tgt):
    s = x.shape[0]
    nt = s // TM

    def body(x_ref, fg_ref, t_ref, dx_ref, loss_ref, dfg_ref, sq, dfg):
        i = pl.program_id(0)

        @pl.when(i == 0)
        def _():
            sq[...] = jnp.zeros_like(sq)
            dfg[...] = jnp.zeros_like(dfg)

        x = x_ref[...]
        g = fg_ref[...]
        r = lax.rsqrt(jnp.mean(x * x, axis=-1, keepdims=True) + EPS)
        xn = x * r
        e = xn * g - t_ref[...]
        sq[...] += _colsum8(e * e)
        dy = e * (1.0 / D)
        dfg[...] += _colsum8(dy * xn)
        dxn = dy * g
        dx_ref[...] = r * (dxn - xn * jnp.mean(dxn * xn, axis=-1, keepdims=True))

        @pl.when(i == nt - 1)
        def _():
            total = jnp.sum(jnp.sum(sq[...], axis=0, keepdims=True), axis=1, keepdims=True)
            loss_ref[...] = jnp.broadcast_to(total * (0.5 / D), loss_ref.shape)
            dfg_ref[...] = jnp.concatenate([jnp.sum(dfg[...], axis=0, keepdims=True), jnp.zeros((7, D), F32)], axis=0)

    tile = pl.BlockSpec((TM, D), lambda i: (i, 0))
    return pl.pallas_call(
        body, name="head", grid=(nt,),
        out_shape=[jax.ShapeDtypeStruct((s, D), F32), jax.ShapeDtypeStruct((8, 128), F32), jax.ShapeDtypeStruct((8, D), F32)],
        in_specs=[tile, pl.BlockSpec((1, D), lambda i: (0, 0)), tile],
        out_specs=[tile, pl.BlockSpec((8, 128), lambda i: (0, 0)), pl.BlockSpec((8, D), lambda i: (0, 0))],
        scratch_shapes=[pltpu.VMEM((8, D), F32), pltpu.VMEM((8, D), F32)],
        compiler_params=ARB1,
    )(x, fg, tgt)


def _adamw_math(w, g, m, v):
    m = ADAM_B1 * m + (1.0 - ADAM_B1) * g
    v = ADAM_B2 * v + (1.0 - ADAM_B2) * (g * g)
    m_hat = m / (1.0 - ADAM_B1 ** ADAM_STEP)
    v_hat = v / (1.0 - ADAM_B2 ** ADAM_STEP)
    delta = -ADAM_LR * (m_hat / (jnp.sqrt(v_hat) + ADAM_EPS) + ADAM_WD * w)
    return delta, m, v


def _finish(parts, w, m, v, transposed, rb, name):
    nf, r, c = w.shape

    def body(*refs):
        p_refs = refs[:nf]
        w_ref, m_ref, v_ref, g_ref, d_ref, mo_ref, vo_ref = refs[nf:]
        for f in range(nf):
            @pl.when(pl.program_id(0) == f)
            def _():
                g = p_refs[f][0].astype(F32)
                for k in range(1, N_DEV):
                    g = g + p_refs[f][k].astype(F32)
                if transposed:
                    g = g.T
                g_ref[0] = g
                d_ref[0], mo_ref[0], vo_ref[0] = _adamw_math(w_ref[0], g, m_ref[0], v_ref[0])

    blk = pl.BlockSpec((1, rb, c), lambda f, i: (f, i, 0))

    def pblk(mine):
        if transposed:
            return pl.BlockSpec((N_DEV, c, rb), lambda f, i: (0, 0, jnp.where(f == mine, i, 0)))
        return pl.BlockSpec((N_DEV, rb, c), lambda f, i: (0, jnp.where(f == mine, i, 0), 0))

    return pl.pallas_call(
        body, name=name, grid=(nf, r // rb),
        out_shape=[jax.ShapeDtypeStruct(w.shape, F32)] * 4,
        in_specs=[pblk(f) for f in range(nf)] + [blk, blk, blk], out_specs=[blk] * 4,
        compiler_params=pltpu.CompilerParams(dimension_semantics=("arbitrary", "arbitrary"), vmem_limit_bytes=VMEM_LIMIT),
    )(*parts, w, m, v)


def _adamw(w, g, m, v, name):
    def body(w_ref, g_ref, m_ref, v_ref, d_ref, mo_ref, vo_ref):
        d_ref[...], mo_ref[...], vo_ref[...] = _adamw_math(w_ref[...], g_ref[...], m_ref[...], v_ref[...])

    return pl.pallas_call(
        body, name=name, out_shape=[jax.ShapeDtypeStruct(w.shape, F32)] * 3,
        in_specs=[VMEM_SPEC] * 4, out_specs=[VMEM_SPEC] * 3,
    )(w, g, m, v)


def _wmod_finish(act_t, dmod_cols, w, m, v):
    rb = 256
    ncol = w.shape[-1]

    def body(a_ref, dm_ref, w_ref, m_ref, v_ref, g_ref, d_ref, mo_ref, vo_ref):
        g = a_ref[:, 0:1] * dm_ref[0, 0:1, :]
        for k in range(1, N_DEV):
            g = g + a_ref[:, k:k + 1] * dm_ref[0, k:k + 1, :]
        g_ref[0] = g
        d_ref[0], mo_ref[0], vo_ref[0] = _adamw_math(w_ref[0], g, m_ref[0], v_ref[0])

    blk = pl.BlockSpec((1, rb, ncol), lambda l, i: (l, i, 0))
    return pl.pallas_call(
        body, name="wmod_finish", grid=(2, D // rb),
        out_shape=[jax.ShapeDtypeStruct(w.shape, F32)] * 4,
        in_specs=[pl.BlockSpec((rb, N_DEV), lambda l, i: (i, 0)), pl.BlockSpec((1, N_DEV, ncol), lambda l, i: (l, 0, 0)),
                  blk, blk, blk],
        out_specs=[blk] * 4,
        compiler_params=pltpu.CompilerParams(dimension_semantics=("arbitrary", "arbitrary"), vmem_limit_bytes=VMEM_LIMIT),
    )(act_t, dmod_cols, w, m, v)


def _pack(pieces):
    flat, offs, at = [], [], 0
    for a in pieces:
        a = a.reshape(-1)
        n = -(-a.shape[0] // 128) * 128
        flat.append(jnp.pad(a, (0, n - a.shape[0])))
        offs.append(at)
        at += n
    return jnp.concatenate(flat).reshape(-1, 128), offs


def _param_block(mod_l, sub, gn):
    return jnp.concatenate([mod_l[sub], gn[None, :], jnp.zeros((4, D), F32)], axis=0)


def kernel(x, c, norm_g, w_mod, b_mod, w_ffn_in, w_ffn_out, ab_w_in, ab_norm_v, ab_w_s, ab_b_s, ab_conv_w, ab_w_out, pool_w_grp, pool_scale, final_g, loss_target, m_norm_g, m_w_mod, m_b_mod, m_w_ffn_in, m_w_ffn_out, m_ab_w_in, m_ab_norm_v, m_ab_w_s, m_ab_b_s, m_ab_conv_w, m_ab_w_out, m_pool_w_grp, m_pool_scale, m_final_g, v_norm_g, v_w_mod, v_b_mod, v_w_ffn_in, v_w_ffn_out, v_ab_w_in, v_ab_norm_v, v_ab_w_s, v_ab_b_s, v_ab_conv_w, v_ab_w_out, v_pool_w_grp, v_pool_scale, v_final_g):
    me = 4 * lax.axis_index("x") + 2 * lax.axis_index("y") + lax.axis_index("c")
    x0 = x[0]
    tgt = loss_target[0]
    n_in = w_ffn_in.shape[-1]
    n_out = w_ffn_out.shape[-2]
    n_abin = ab_w_in.shape[-1]
    n_about = ab_w_out.shape[-2]
    n_pool = pool_w_grp.shape[-2]
    n_mod = w_mod.shape[-1]
    n_ng = norm_g.shape[-1]
    n_cw = ab_conv_w.shape[-1]
    n_ps = pool_scale.shape[-1]

    pack, offs = _pack([c, norm_g, ab_conv_w, pool_scale])
    got = _exchange_small(pack, "gather_small", False).reshape(N_DEV, -1)
    c_all = got[:, offs[0]:offs[0] + D]
    ng_full = got[:, offs[1]:offs[1] + 6 * n_ng].reshape(N_DEV, 2, 3, n_ng).transpose(1, 2, 0, 3).reshape(2, 3, D)
    cw_full = got[:, offs[2]:offs[2] + 3 * n_cw].reshape(N_DEV, 3, n_cw).transpose(1, 0, 2).reshape(3, D_B)
    ps_full = got[:, offs[3]:offs[3] + n_ps].reshape(1, D)

    act_all, mod_cols = _mod_fwd(c_all, w_mod)
    mod_got = _exchange_small(mod_cols.reshape(-1, 128), "gather_mod", False).reshape(N_DEV, 2, N_DEV, n_mod)
    mod = lax.dynamic_index_in_dim(mod_got, me, axis=2, keepdims=False).transpose(1, 0, 2).reshape(2, 9 * D) + b_mod
    mod = mod.reshape(2, 3, 3, D)

    win_sh = jnp.swapaxes(w_ffn_in.reshape(4, D, n_in), 1, 2).astype(BF16)
    wout_sh = w_ffn_out.reshape(4, n_out, D).astype(BF16)
    abin_sh = ab_w_in[0].T.astype(BF16)
    about_sh = ab_w_out[0].astype(BF16)
    pool_sh = pool_w_grp[0].astype(BF16)
    nv = ab_norm_v
    ws = ab_w_s[0]
    bst = ab_b_s[0].T
    cw8 = jnp.concatenate([cw_full, jnp.zeros((5, D_B), F32)], axis=0)
    win, wout = [None] * 4, [None] * 4

    pb = [[_param_block(mod[l], s, ng_full[l, s]) for s in range(3)] for l in range(2)]
    win[0], wout[0] = _run_rider(_GatherRider([win_sh[0], wout_sh[0]]), "gather_ffn_0")
    (x1, h00, g00, u00, y00), (abin_all, about_all, win[1]) = _ffn_fwd(
        x0, pb[0][0], win[0], wout[0], 0, _GatherRider([abin_sh, about_sh, win_sh[1]]))
    (x2, h01, proj, ab_out), (wout[1],) = _ab_fwd(
        x1, pb[0][1], abin_all, about_all, nv, ws, bst, cw8, _GatherRider([wout_sh[1]]))
    (x3, h02, g02, u02, y02), (win[2], wout[2]) = _ffn_fwd(
        x2, pb[0][2], win[1], wout[1], 1, _GatherRider([win_sh[2], wout_sh[2]]))
    (x4, h10, g10, u10, y10), (pool_all, win[3], wout[3]) = _ffn_fwd(
        x3, pb[1][0], win[2], wout[2], 2, _GatherRider([pool_sh, win_sh[3], wout_sh[3]]))
    x5, pooled, pool_out = _pool_fwd(x4, pb[1][1], pool_all, ps_full)
    (x6, h12, g12, u12, y12), _ = _ffn_fwd(x5, pb[1][2], win[3], wout[3], 3)
    dx6, loss_blk, dfg = _head(x6, final_g.reshape(1, D), tgt)
    loss = lax.psum(loss_blk[0, 0], ("x", "y", "c"))

    p_in, p_out = [None] * 4, [None] * 4
    (dx5, dgu12, a12, dy12, st12), _ = _ffn_bwd(dx6, x5, pb[1][2], g12, u12, y12, win[3], wout[3], 3)
    gw_in3, _ = _wgrad(dgu12, h12, "wgrad_in_3")
    gw_out3, _ = _wgrad(a12, dy12, "wgrad_out_3")
    dx4, st11, dps, gw_pool = _pool_bwd(dx5, x4, pb[1][1], pooled, pool_out, pool_all, ps_full)
    (dx3, dgu10, a10, dy10, st10), (p_in[3],) = _ffn_bwd(
        dx4, x3, pb[1][0], g10, u10, y10, win[2], wout[2], 2, _ScatterRider([gw_in3]))
    gw_in2, (p_out[3],) = _wgrad(dgu10, h10, "wgrad_in_2", _ScatterRider([gw_out3]))
    gw_out2, (p_pool,) = _wgrad(a10, dy10, "wgrad_out_2", _ScatterRider([gw_pool]))
    (dx2, dgu02, a02, dy02, st02), (p_in[2],) = _ffn_bwd(
        dx3, x2, pb[0][2], g02, u02, y02, win[1], wout[1], 1, _ScatterRider([gw_in2]))
    gw_in1, (p_out[2],) = _wgrad(dgu02, h02, "wgrad_in_1", _ScatterRider([gw_out2]))
    gw_out1, _ = _wgrad(a02, dy02, "wgrad_out_1")
    (dx1, dproj, cat, dy01, st01, dnv, dws, dbs, dcw), (p_in[1],) = _ab_bwd(
        dx2, x1, pb[0][1], proj, ab_out, abin_all, about_all, nv, ws, bst, cw8, _ScatterRider([gw_in1]))
    gw_abin, (p_out[1],) = _wgrad(dproj, h01, "wgrad_ab_in", _ScatterRider([gw_out1]))
    gw_about, _ = _wgrad(cat, dy01, "wgrad_ab_out")
    (dx0, dgu00, a00, dy00, st00), (p_abin, p_about) = _ffn_bwd(
        dx1, x0, pb[0][0], g00, u00, y00, win[0], wout[0], 0, _ScatterRider([gw_abin, gw_about]))
    gw_in0, _ = _wgrad(dgu00, h00, "wgrad_in_0")
    gw_out0, (p_in[0],) = _wgrad(a00, dy00, "wgrad_out_0", _ScatterRider([gw_in0]))
    (p_out[0],) = _run_rider(_ScatterRider([gw_out0]), "scatter_out_0")
    grad_x = dx0[None]

    shape_in, shape_out = w_ffn_in.shape, w_ffn_out.shape
    fin = lambda a: a.reshape(4, D, n_in)
    fout = lambda a: a.reshape(4, n_out, D)
    fpool = lambda a: a.reshape(1, 4 * n_pool, POOL_G)
    r_in = _finish(p_in, fin(w_ffn_in), fin(m_w_ffn_in), fin(v_w_ffn_in), True, 256, "finish_ffn_in")
    r_out = _finish(p_out, fout(w_ffn_out), fout(m_w_ffn_out), fout(v_w_ffn_out), False, n_out // 2, "finish_ffn_out")
    r_abin = _finish([p_abin], ab_w_in, m_ab_w_in, v_ab_w_in, True, 256, "finish_ab_in")
    r_about = _finish([p_about], ab_w_out, m_ab_w_out, v_ab_w_out, False, n_about, "finish_ab_out")
    r_pool = _finish([p_pool.reshape(N_DEV, 4 * n_pool, POOL_G)], fpool(pool_w_grp), fpool(m_pool_w_grp), fpool(v_pool_w_grp),
                     False, 4 * n_pool, "finish_pool")
    r_in = [a.reshape(shape_in) for a in r_in]
    r_out = [a.reshape(shape_out) for a in r_out]
    r_pool = [a.reshape(pool_w_grp.shape) for a in r_pool]

    stats = [[st00, st01, st02], [st10, st11, st12]]
    dmod = jnp.stack([jnp.concatenate([stats[l][s][0:3].reshape(-1) for s in range(3)]) for l in range(2)])
    dng = jnp.stack([jnp.stack([stats[l][s][3] for s in range(3)]) for l in range(2)])
    spack, so = _pack([dmod, dng, dnv[0], dws, dbs[:, :, 0], dcw[0:3], dps[0], dfg[0]])
    sgot, ssum = _exchange_small(spack, "reduce_small", True)
    ssum = ssum.reshape(-1)
    take = lambda i, n: lax.dynamic_slice_in_dim(ssum, so[i], n)
    g_bmod = take(0, 2 * 9 * D).reshape(2, 9 * D)
    g_ng = lax.dynamic_slice_in_dim(take(1, 6 * D).reshape(2, 3, D), me * n_ng, n_ng, axis=2)
    g_nv = take(2, D_A).reshape(1, D_A)
    g_ws = take(3, A_HEADS * CHUNK * CHUNK).reshape(1, A_HEADS, CHUNK, CHUNK)
    g_bs = take(4, A_HEADS * CHUNK).reshape(1, A_HEADS, CHUNK)
    g_cw = lax.dynamic_slice_in_dim(take(5, 3 * D_B).reshape(1, 3, D_B), me * n_cw, n_cw, axis=2)
    g_ps = lax.dynamic_slice_in_dim(take(6, D).reshape(1, D), me * n_ps, n_ps, axis=1)
    g_fg = take(7, D)

    dmod_all = sgot.reshape(N_DEV, -1)[:, so[0]:so[0] + 2 * 9 * D].reshape(N_DEV, 2, 9 * D)
    dmod_cols = lax.dynamic_slice_in_dim(dmod_all, me * n_mod, n_mod, axis=2).transpose(1, 0, 2)
    r_wmod = _wmod_finish(act_all.T, dmod_cols, w_mod, m_w_mod, v_w_mod)

    small_w = [b_mod, norm_g, ab_norm_v, ab_w_s, ab_b_s, ab_conv_w, pool_scale, final_g]
    small_g = [g_bmod, g_ng, g_nv, g_ws, g_bs, g_cw, g_ps, g_fg]
    small_m = [m_b_mod, m_norm_g, m_ab_norm_v, m_ab_w_s, m_ab_b_s, m_ab_conv_w, m_pool_scale, m_final_g]
    small_v = [v_b_mod, v_norm_g, v_ab_norm_v, v_ab_w_s, v_ab_b_s, v_ab_conv_w, v_pool_scale, v_final_g]
    pw, po = _pack(small_w)
    pv = jnp.concatenate([jnp.pad(a.reshape(-1), (0, -a.size % 128), constant_values=1.0) for a in small_v]).reshape(-1, 128)
    sd, sm, sv = _adamw(pw, _pack(small_g)[0], _pack(small_m)[0], pv, "adamw_small")
    unpack = lambda packed: [packed.reshape(-1)[po[i]:po[i] + a.size].reshape(a.shape) for i, a in enumerate(small_w)]
    d_s, m_s, v_s = unpack(sd), unpack(sm), unpack(sv)

    def ordered(k, small):
        return [small[1], r_wmod[k], small[0], r_in[k], r_out[k], r_abin[k], small[2], small[3], small[4], small[5],
                r_about[k], r_pool[k], small[6], small[7]]

    grads = ordered(0, small_g)
    deltas = ordered(1, d_s)
    new_m = ordered(2, m_s)
    new_v = ordered(3, v_s)
    return (loss, grad_x, *grads, *deltas, *new_m, *new_v)
```

```python
import functools
import math

import jax
import jax.numpy as jnp
from jax import lax
from jax.experimental import pallas as pl
from jax.experimental.pallas import tpu as pltpu

F32 = jnp.float32
BF16 = jnp.bfloat16

N_DEV = 8
D = 1024
DFF = 2816
HC = 256
NCH = DFF // HC
D_A = 512
D_B = 512
D_AB = 2 * D_A + 3 * D_B
CHUNK = 128
A_HEADS = 4
POOL_G = 256
POOL_HALO = 16
CONV_HALO = 8
EPS = 1e-6
TM = 256
GELU_K = math.sqrt(2.0 / math.pi)
GELU_C = 0.044715

ADAM_LR = 0.001
ADAM_B1 = 0.9
ADAM_B2 = 0.999
ADAM_EPS = 1e-08
ADAM_WD = 0.01
ADAM_STEP = 10

VMEM_LIMIT = 56 * 1024 * 1024
MESH_ID = pl.DeviceIdType.MESH
ANY = pl.BlockSpec(memory_space=pl.ANY)
VMEM_SPEC = pl.BlockSpec(memory_space=pltpu.VMEM)
ARB1 = pltpu.CompilerParams(dimension_semantics=("arbitrary",), vmem_limit_bytes=VMEM_LIMIT)


def _dot_nt(a, b):
    return lax.dot_general(a, b, (((1,), (1,)), ((), ())), preferred_element_type=F32)


def _dot_nn(a, b):
    return lax.dot_general(a, b, (((1,), (0,)), ((), ())), preferred_element_type=F32)


def _dot_tn(a, b):
    return lax.dot_general(a, b, (((0,), (0,)), ((), ())), preferred_element_type=F32)


def _colsum8(v):
    r, n = v.shape
    return jnp.sum(v.reshape(r // 8, 8, n), axis=0)


def _gelu(x):
    return 0.5 * x * (1.0 + jnp.tanh(GELU_K * (x + GELU_C * x * x * x)))


def _gelu_grad(x):
    t = jnp.tanh(GELU_K * (x + GELU_C * x * x * x))
    return 0.5 * (1.0 + t) + 0.5 * x * (1.0 - t * t) * (GELU_K * (1.0 + 3.0 * GELU_C * x * x))


def _mod_rows(p_ref):
    return p_ref[0:1, :], p_ref[1:2, :], p_ref[2:3, :], p_ref[3:4, :]


def _modulate(x, gn, sh, sc):
    r = lax.rsqrt(jnp.mean(x * x, axis=-1, keepdims=True) + EPS)
    return ((x * r) * gn) * (1.0 + sc) + sh


def _modulate_bwd(dh, x, gn, sc, stats):
    r = lax.rsqrt(jnp.mean(x * x, axis=-1, keepdims=True) + EPS)
    xn = x * r
    stats[0] += _colsum8(dh)
    stats[1] += _colsum8(dh * (xn * gn))
    dy0 = dh * (1.0 + sc)
    stats[3] += _colsum8(dy0 * xn)
    dxn = dy0 * gn
    return r * (dxn - xn * jnp.mean(dxn * xn, axis=-1, keepdims=True))


def _stats_out(stats, out_ref):
    rows = [jnp.sum(stats[k], axis=0, keepdims=True) for k in range(4)]
    out_ref[...] = jnp.concatenate(rows + [jnp.zeros((4, stats.shape[-1]), F32)], axis=0)


def _shift_down(v, k, prev):
    n = v.shape[0]
    row = lax.broadcasted_iota(jnp.int32, v.shape, 0)
    out = pltpu.roll(v, k, 0)
    for j in range(k):
        out = jnp.where(row == j, prev[prev.shape[0] - k + j:prev.shape[0] - k + j + 1, :], out)
    return out


def _shift_up(v, k, nxt):
    n = v.shape[0]
    row = lax.broadcasted_iota(jnp.int32, v.shape, 0)
    out = pltpu.roll(v, n - k, 0)
    for j in range(k):
        out = jnp.where(row == n - k + j, nxt[j:j + 1, :], out)
    return out


def _load_rows(w_hbm, sel, dst, sems, base):
    n = dst.shape[0] // N_DEV
    cps = []
    for k in range(N_DEV):
        src = w_hbm.at[k] if sel is None else w_hbm.at[k, sel]
        cps.append(pltpu.make_async_copy(src, dst.at[pl.ds(k * n, n)], sems.at[base + k]))
    return cps


def _my_pos():
    return lax.axis_index("x"), lax.axis_index("y"), lax.axis_index("c")


def _peer(j):
    x, y, c = _my_pos()
    return (1 - x if j & 4 else x, 1 - y if j & 2 else y, 1 - c if j & 1 else c)


def _index(pos):
    return 4 * pos[0] + 2 * pos[1] + pos[2]


def _exchange_small(v, name, with_sum):
    rows = v.shape[0]

    def body(v_ref, *refs):
        if with_sum:
            out_ref, sum_ref, send_sems, recv_sems, local_sem = refs
        else:
            out_ref, send_sems, recv_sems, local_sem = refs
        me = _index(_my_pos())

        def copy(j, slot):
            return pltpu.make_async_remote_copy(
                src_ref=v_ref, dst_ref=out_ref.at[slot], send_sem=send_sems.at[j - 1], recv_sem=recv_sems.at[j - 1],
                device_id=_peer(j), device_id_type=MESH_ID)

        mine = pltpu.make_async_copy(v_ref, out_ref.at[me], local_sem)
        mine.start()
        sends = [copy(j, me) for j in range(1, N_DEV)]
        for cp in sends:
            cp.start()
        for j in range(1, N_DEV):
            copy(j, _index(_peer(j))).wait_recv()
        for cp in sends:
            cp.wait_send()
        mine.wait()
        if with_sum:
            acc = out_ref[0]
            for k in range(1, N_DEV):
                acc = acc + out_ref[k]
            sum_ref[...] = acc

    out_shape = [jax.ShapeDtypeStruct((N_DEV, rows, 128), F32)]
    out_specs = [VMEM_SPEC]
    if with_sum:
        out_shape.append(jax.ShapeDtypeStruct((rows, 128), F32))
        out_specs.append(VMEM_SPEC)
    res = pl.pallas_call(
        body, name=name, out_shape=out_shape, in_specs=[VMEM_SPEC], out_specs=out_specs,
        scratch_shapes=[pltpu.SemaphoreType.DMA((N_DEV - 1,)), pltpu.SemaphoreType.DMA((N_DEV - 1,)),
                        pltpu.SemaphoreType.DMA(())],
    )(v)
    return res if with_sum else res[0]


class _GatherRider:
    has_middle = True

    def __init__(self, shards):
        n = len(shards)
        self.inputs = list(shards)
        self.out_shapes = [jax.ShapeDtypeStruct((N_DEV,) + s.shape, s.dtype) for s in shards]
        self.scratch = [pltpu.SemaphoreType.DMA((7 * n,)), pltpu.SemaphoreType.DMA((7 * n,)), pltpu.SemaphoreType.DMA((n,))]

    def _ctx(self, outs, scr):
        send, recv, _ = scr
        x, y, c = _my_pos()
        chips = [(1 - x, y), (x, 1 - y), (1 - x, 1 - y)]

        def copy(a, k, block, to, src=None):
            slot = outs[a].at[_index(block)]
            return pltpu.make_async_remote_copy(
                src_ref=slot if src is None else src, dst_ref=slot, send_sem=send.at[7 * a + k], recv_sem=recv.at[7 * a + k],
                device_id=to, device_id_type=MESH_ID)

        return (x, y, c), (x, y, 1 - c), chips, copy

    def _sends(self, ins, outs, scr):
        me, sib, chips, copy = self._ctx(outs, scr)
        out = []
        for a in range(len(ins)):
            out.append(copy(a, 0, me, sib, src=ins[a]))
            out += [copy(a, 1 + j, me, (*chips[j], me[2]), src=ins[a]) for j in range(3)]
        return out

    def first(self, ins, outs, scr):
        me = _index(_my_pos())
        for a in range(len(ins)):
            pltpu.make_async_copy(ins[a], outs[a].at[me], scr[2].at[a]).start()
        for cp in self._sends(ins, outs, scr):
            cp.start()

    def middle(self, ins, outs, scr):
        me, sib, chips, copy = self._ctx(outs, scr)
        for j in range(3):
            for a in range(len(ins)):
                copy(a, 1 + j, (*chips[j], me[2]), me).wait_recv()
                copy(a, 4 + j, (*chips[j], me[2]), sib).start()

    def last(self, ins, outs, scr):
        me, sib, chips, copy = self._ctx(outs, scr)
        for a in range(len(ins)):
            copy(a, 0, sib, me).wait_recv()
            for j in range(3):
                copy(a, 4 + j, (*chips[j], sib[2]), me).wait_recv()
        for cp in self._sends(ins, outs, scr):
            cp.wait_send()
        for a in range(len(ins)):
            for j in range(3):
                copy(a, 4 + j, (*chips[j], me[2]), sib).wait_send()
            pltpu.make_async_copy(ins[a], outs[a].at[_index(me)], scr[2].at[a]).wait()


class _ScatterRider:
    has_middle = False

    def __init__(self, grads):
        n = len(grads)
        self.inputs = list(grads)
        self.out_shapes = []
        for g in grads:
            if g.ndim == 3:
                self.out_shapes.append(jax.ShapeDtypeStruct((N_DEV, g.shape[0], g.shape[1] // N_DEV, g.shape[2]), g.dtype))
            else:
                self.out_shapes.append(jax.ShapeDtypeStruct((N_DEV, g.shape[0] // N_DEV, g.shape[1]), g.dtype))
        self.scratch = [pltpu.SemaphoreType.DMA((7 * n,)), pltpu.SemaphoreType.DMA((7 * n,)), pltpu.SemaphoreType.DMA((n,))]

    @staticmethod
    def _part(ref, k):
        if ref.ndim == 3:
            n = ref.shape[1] // N_DEV
            return ref.at[:, pl.ds(pl.multiple_of(k * n, 16), n)]
        n = ref.shape[0] // N_DEV
        return ref.at[pl.ds(pl.multiple_of(k * n, 16), n)]

    def _copy(self, ins, outs, scr, g, j, to, src_dev):
        return pltpu.make_async_remote_copy(
            src_ref=self._part(ins[g], to), dst_ref=outs[g].at[src_dev], send_sem=scr[0].at[7 * g + j - 1],
            recv_sem=scr[1].at[7 * g + j - 1], device_id=_peer(j), device_id_type=MESH_ID)

    def first(self, ins, outs, scr):
        me = _index(_my_pos())
        for g in range(len(ins)):
            pltpu.make_async_copy(self._part(ins[g], me), outs[g].at[me], scr[2].at[g]).start()
        for j in range(1, N_DEV):
            for g in range(len(ins)):
                self._copy(ins, outs, scr, g, j, _index(_peer(j)), me).start()

    def last(self, ins, outs, scr):
        me = _index(_my_pos())
        for j in range(1, N_DEV):
            for g in range(len(ins)):
                self._copy(ins, outs, scr, g, j, me, _index(_peer(j))).wait_recv()
        for j in range(1, N_DEV):
            for g in range(len(ins)):
                self._copy(ins, outs, scr, g, j, _index(_peer(j)), me).wait_send()
        for g in range(len(ins)):
            pltpu.make_async_copy(self._part(ins[g], me), outs[g].at[me], scr[2].at[g]).wait()


class _SiblingRider:
    has_middle = False

    def __init__(self, items):
        self.inputs = [a for a, _ in items]
        self.counts = [n for _, n in items]
        self.out_shapes = [jax.ShapeDtypeStruct(a.shape if n is None else (n,) + a.shape[1:], a.dtype) for a, n in items]
        self.scratch = [pltpu.SemaphoreType.DMA((len(items),)), pltpu.SemaphoreType.DMA((len(items),))]

    def _copies(self, ins, outs, scr):
        x, y, c = _my_pos()
        out = []
        for i, (ref, n) in enumerate(zip(ins, self.counts)):
            src = ref if n is None else ref.at[pl.ds((1 - c) * n, n)]
            out.append(pltpu.make_async_remote_copy(
                src_ref=src, dst_ref=outs[i], send_sem=scr[0].at[i], recv_sem=scr[1].at[i],
                device_id=(x, y, 1 - c), device_id_type=MESH_ID))
        return out

    def first(self, ins, outs, scr):
        for cp in self._copies(ins, outs, scr):
            cp.start()

    def last(self, ins, outs, scr):
        for cp in self._copies(ins, outs, scr):
            cp.wait()


class _ChipScatterRider:
    has_middle = False

    def __init__(self, rows, cols):
        self.nr, self.nc = len(rows), len(cols)
        self.inputs = list(rows) + list(cols)
        self.out_shapes = [jax.ShapeDtypeStruct((4, a.shape[0] // 4, a.shape[1]), a.dtype) for a in rows]
        self.out_shapes += [jax.ShapeDtypeStruct((4, 2, a.shape[0] // N_DEV, a.shape[1]), a.dtype) for a in cols]
        n = 4 * self.nr + N_DEV * self.nc
        self.scratch = [pltpu.SemaphoreType.DMA((n,)), pltpu.SemaphoreType.DMA((n,)), pltpu.SemaphoreType.DMA((n,))]

    def _pieces(self, ins, outs):
        x, y, c = _my_pos()
        q = 2 * x + y
        out = []
        for a in range(self.nr):
            n = ins[a].shape[0] // 4
            for j in range(4):
                out.append((4 * a + j, ins[a].at[pl.ds(j * n, n)], (c, j >> 1, j & 1), outs[a].at[q], 4 * a + q))
        for a in range(self.nc):
            ref, base = ins[self.nr + a], 4 * self.nr + N_DEV * a
            n = ref.shape[0] // N_DEV
            for k in range(N_DEV):
                out.append((base + k, ref.at[pl.ds(k * n, n)], (k >> 2, (k >> 1) & 1, k & 1),
                            outs[self.nr + a].at[q, c], base + 2 * q + c))
        return out

    def first(self, ins, outs, scr):
        send, recv, local = scr
        me = _index(_my_pos())
        for s, src, to, slot, r in self._pieces(ins, outs):
            mine = _index(to) == me

            @pl.when(mine)
            def _():
                pltpu.make_async_copy(src, slot, local.at[s]).start()

            @pl.when(jnp.logical_not(mine))
            def _():
                pltpu.make_async_remote_copy(src_ref=src, dst_ref=slot, send_sem=send.at[s], recv_sem=recv.at[r],
                                             device_id=to, device_id_type=MESH_ID).start()

    def last(self, ins, outs, scr):
        send, recv, local = scr
        x, y, c = _my_pos()
        me = _index((x, y, c))
        arrivals = []
        for a in range(self.nr):
            n = ins[a].shape[0] // 4
            for q in range(4):
                arrivals.append((4 * a + q, (q >> 1, q & 1, x), ins[a].at[pl.ds(0, n)], outs[a].at[q], 4 * a + 2 * y + c))
        for a in range(self.nc):
            ref, base = ins[self.nr + a], 4 * self.nr + N_DEV * a
            n = ref.shape[0] // N_DEV
            for k in range(N_DEV):
                arrivals.append((base + k, (k >> 2, (k >> 1) & 1, k & 1), ref.at[pl.ds(0, n)],
                                 outs[self.nr + a].at[k >> 1, k & 1], base + me))
        for r, sender, src, slot, s_local in arrivals:
            mine = _index(sender) == me

            @pl.when(mine)
            def _():
                pltpu.make_async_copy(src, slot, local.at[s_local]).wait()

            @pl.when(jnp.logical_not(mine))
            def _():
                pltpu.make_async_remote_copy(src_ref=src, dst_ref=slot, send_sem=send.at[r], recv_sem=recv.at[r],
                                             device_id=sender, device_id_type=MESH_ID).wait_recv()

        for s, src, to, slot, r in self._pieces(ins, outs):
            @pl.when(_index(to) != me)
            def _():
                pltpu.make_async_remote_copy(src_ref=src, dst_ref=slot, send_sem=send.at[s], recv_sem=recv.at[r],
                                             device_id=to, device_id_type=MESH_ID).wait_send()


class _Riders:
    def __init__(self, riders):
        self.riders = list(riders)
        self.inputs = [a for r in self.riders for a in r.inputs]
        self.out_shapes = [s for r in self.riders for s in r.out_shapes]
        self.scratch = [s for r in self.riders for s in r.scratch]
        self.has_middle = any(r.has_middle for r in self.riders)

    def _each(self, ins, outs, scr):
        i = o = s = 0
        for r in self.riders:
            ni, no, ns = len(r.inputs), len(r.out_shapes), len(r.scratch)
            yield r, ins[i:i + ni], outs[o:o + no], scr[s:s + ns]
            i, o, s = i + ni, o + no, s + ns

    def first(self, ins, outs, scr):
        for r, a, b, c in self._each(ins, outs, scr):
            r.first(a, b, c)

    def middle(self, ins, outs, scr):
        for r, a, b, c in self._each(ins, outs, scr):
            if r.has_middle:
                r.middle(a, b, c)

    def last(self, ins, outs, scr):
        for r, a, b, c in self._each(ins, outs, scr):
            r.last(a, b, c)

    def split(self, outs):
        res, o = [], 0
        for r in self.riders:
            res.append(list(outs[o:o + len(r.out_shapes)]))
            o += len(r.out_shapes)
        return res


def _run(body, *, name, grid, in_specs, out_specs, out_shape, scratch_shapes, args, rider=None, params=None, prefetch=()):
    params = ARB1 if params is None else params
    npf = len(prefetch)
    ni, no, ns = len(in_specs), len(out_shape), len(scratch_shapes)
    ri, ro = (len(rider.inputs), len(rider.out_shapes)) if rider is not None else (0, 0)
    steps = grid[0]

    def wrapped(*refs):
        pf, refs = refs[:npf], refs[npf:]
        cut = [ni, ni + ri, ni + ri + no, ni + ri + no + ro, ni + ri + no + ro + ns]
        a, b, c, d, e, f = (refs[lo:hi] for lo, hi in zip([0] + cut, cut + [len(refs)]))
        i = pl.program_id(0)
        if rider is not None:
            @pl.when(i == 0)
            def _():
                rider.first(b, d, f)

            if rider.has_middle:
                @pl.when(i == steps - 1)
                def _():
                    rider.middle(b, d, f)

        body(*pf, *a, *c, *e)

        if rider is not None:
            @pl.when(i == steps - 1)
            def _():
                rider.last(b, d, f)

    extra_shapes = rider.out_shapes if rider is not None else []
    extra_scratch = rider.scratch if rider is not None else []
    extra_inputs = rider.inputs if rider is not None else []
    all_in, all_out = list(in_specs) + [ANY] * ri, list(out_specs) + [ANY] * ro
    all_scratch = list(scratch_shapes) + extra_scratch
    if npf:
        outs = pl.pallas_call(
            wrapped, name=name, out_shape=list(out_shape) + extra_shapes,
            grid_spec=pltpu.PrefetchScalarGridSpec(num_scalar_prefetch=npf, grid=grid, in_specs=all_in, out_specs=all_out,
                                                   scratch_shapes=all_scratch),
            compiler_params=params)(*prefetch, *args, *extra_inputs)
    else:
        outs = pl.pallas_call(
            wrapped, name=name, grid=grid, in_specs=all_in, out_specs=all_out, out_shape=list(out_shape) + extra_shapes,
            scratch_shapes=all_scratch, compiler_params=params)(*args, *extra_inputs)
    return list(outs[:no]), list(outs[no:])


def _run_rider(rider, name):
    ri, ro = len(rider.inputs), len(rider.out_shapes)

    def body(*refs):
        b, d, f = refs[:ri], refs[ri:ri + ro], refs[ri + ro:]
        rider.first(b, d, f)
        if rider.has_middle:
            rider.middle(b, d, f)
        rider.last(b, d, f)

    return list(pl.pallas_call(body, name=name, in_specs=[ANY] * ri, out_specs=[ANY] * ro, out_shape=rider.out_shapes,
                               scratch_shapes=rider.scratch)(*rider.inputs))


def _mod_fwd(c_all, w_mod):
    ncol = w_mod.shape[-1]

    def body(c_ref, w_ref, act_ref, out_ref):
        c = c_ref[...]
        act = c * jax.nn.sigmoid(c)
        act_ref[...] = act
        out_ref[0] = _dot_nn(act.astype(BF16), w_ref[0].astype(BF16))

    return pl.pallas_call(
        body, name="mod_fwd", grid=(2,),
        out_shape=[jax.ShapeDtypeStruct((N_DEV, D), F32), jax.ShapeDtypeStruct((2, N_DEV, ncol), F32)],
        in_specs=[pl.BlockSpec((N_DEV, D), lambda l: (0, 0)), pl.BlockSpec((1, D, ncol), lambda l: (l, 0, 0))],
        out_specs=[pl.BlockSpec((N_DEV, D), lambda l: (0, 0)), pl.BlockSpec((1, N_DEV, ncol), lambda l: (l, 0, 0))],
        compiler_params=ARB1,
    )(c_all, w_mod)


def _ffn_fwd(x, p, win_all, wout_all, f, rider=None):
    s = x.shape[0]
    nt = s // TM

    def body(x_ref, p_ref, win_hbm, wout_hbm, xo_ref, h_ref, g_ref, u_ref, y_ref, win, wout, act, sems):
        @pl.when(pl.program_id(0) == 0)
        def _():
            cps = _load_rows(win_hbm, None, win, sems, 0) + _load_rows(wout_hbm, None, wout, sems, N_DEV)
            for cp in cps:
                cp.start()
            for cp in cps:
                cp.wait()

        sh, sc, gate, gn = _mod_rows(p_ref)
        x = x_ref[...]
        hb = _modulate(x, gn, sh, sc).astype(BF16)
        h_ref[...] = hb
        for c in range(NCH):
            g = _dot_nt(hb, win[c * HC:(c + 1) * HC, :])
            u = _dot_nt(hb, win[DFF + c * HC:DFF + (c + 1) * HC, :])
            g_ref[c] = g.astype(BF16)
            u_ref[c] = u.astype(BF16)
            act[:, c * HC:(c + 1) * HC] = ((g * jax.nn.sigmoid(g)) * u).astype(BF16)
        y = _dot_nn(act[...], wout[...])
        y_ref[...] = y
        xo_ref[...] = x + (0.5 * gate) * y

    tile = pl.BlockSpec((TM, D), lambda i: (i, 0))
    chunks = pl.BlockSpec((NCH, TM, HC), lambda i: (0, i, 0))
    return _run(
        body, name=f"ffn_fwd_{f}", grid=(nt,),
        out_shape=[jax.ShapeDtypeStruct((s, D), F32), jax.ShapeDtypeStruct((s, D), BF16),
                   jax.ShapeDtypeStruct((NCH, s, HC), BF16), jax.ShapeDtypeStruct((NCH, s, HC), BF16),
                   jax.ShapeDtypeStruct((s, D), F32)],
        in_specs=[tile, pl.BlockSpec((8, D), lambda i: (0, 0)), ANY, ANY],
        out_specs=[tile, tile, chunks, chunks, tile],
        scratch_shapes=[pltpu.VMEM((2 * DFF, D), BF16), pltpu.VMEM((DFF, D), BF16), pltpu.VMEM((TM, DFF), BF16),
                        pltpu.SemaphoreType.DMA((2 * N_DEV,))],
        args=(x, p, win_all, wout_all), rider=rider)


def _ffn_bwd(dxo, x, p, g3, u3, y, win_all, wout_all, f, rider=None):
    s = x.shape[0]
    nt = s // TM

    def body(dxo_ref, x_ref, p_ref, g_ref, u_ref, y_ref, win_hbm, wout_hbm,
             dx_ref, dgu_ref, a_ref, dy_ref, st_ref, win, wout, dgu, stats, sems):
        i = pl.program_id(0)

        @pl.when(i == 0)
        def _():
            cps = _load_rows(win_hbm, None, win, sems, 0) + _load_rows(wout_hbm, None, wout, sems, N_DEV)
            for cp in cps:
                cp.start()
            stats[...] = jnp.zeros_like(stats)
            for cp in cps:
                cp.wait()

        sh, sc, gate, gn = _mod_rows(p_ref)
        x = x_ref[...]
        dxo = dxo_ref[...]
        dyb = ((0.5 * gate) * dxo).astype(BF16)
        dy_ref[...] = dyb
        stats[2] += _colsum8((0.5 * dxo) * y_ref[...])
        for c in range(NCH):
            da = _dot_nt(dyb, wout[c * HC:(c + 1) * HC, :])
            g = g_ref[c].astype(F32)
            u = u_ref[c].astype(F32)
            sg = jax.nn.sigmoid(g)
            si = g * sg
            dg = ((da * u) * (sg * (1.0 + g * (1.0 - sg)))).astype(BF16)
            du = (da * si).astype(BF16)
            a_ref[c] = (si * u).astype(BF16)
            dgu_ref[c] = dg
            dgu_ref[NCH + c] = du
            dgu[:, c * HC:(c + 1) * HC] = dg
            dgu[:, DFF + c * HC:DFF + (c + 1) * HC] = du
        dh = _dot_nn(dgu[...], win[...])
        dx_ref[...] = dxo + _modulate_bwd(dh, x, gn, sc, stats)

        @pl.when(i == nt - 1)
        def _():
            _stats_out(stats, st_ref)

    tile = pl.BlockSpec((TM, D), lambda i: (i, 0))
    chunks = pl.BlockSpec((NCH, TM, HC), lambda i: (0, i, 0))
    small = pl.BlockSpec((8, D), lambda i: (0, 0))
    return _run(
        body, name=f"ffn_bwd_{f}", grid=(nt,),
        out_shape=[jax.ShapeDtypeStruct((s, D), F32), jax.ShapeDtypeStruct((2 * NCH, s, HC), BF16),
                   jax.ShapeDtypeStruct((NCH, s, HC), BF16), jax.ShapeDtypeStruct((s, D), BF16),
                   jax.ShapeDtypeStruct((8, D), F32)],
        in_specs=[tile, tile, small, chunks, chunks, tile, ANY, ANY],
        out_specs=[tile, pl.BlockSpec((2 * NCH, TM, HC), lambda i: (0, i, 0)), chunks, tile, small],
        scratch_shapes=[pltpu.VMEM((2 * DFF, D), BF16), pltpu.VMEM((DFF, D), BF16), pltpu.VMEM((TM, 2 * DFF), BF16),
                        pltpu.VMEM((4, 8, D), F32), pltpu.SemaphoreType.DMA((2 * N_DEV,))],
        args=(dxo, x, p, g3, u3, y, win_all, wout_all), rider=rider)


def _wgrad(lhs3, rhs, name, rider=None):
    nj, s, _ = lhs3.shape
    n = rhs.shape[1]

    def body(l_ref, r_ref, o_ref):
        o_ref[...] = _dot_tn(l_ref[0], r_ref[...]).astype(BF16)

    outs, rode = _run(
        body, name=name, grid=(nj,),
        out_shape=[jax.ShapeDtypeStruct((nj * HC, n), BF16)],
        in_specs=[pl.BlockSpec((1, s, HC), lambda j: (j, 0, 0)), pl.BlockSpec((s, n), lambda j: (0, 0))],
        out_specs=[pl.BlockSpec((HC, n), lambda j: (j, 0))],
        scratch_shapes=[], args=(lhs3, rhs), rider=rider)
    return outs[0], rode


def _wgrad_pair(own3, sib3, own_r, sib_r, sel, n, col_split, name, rider=None):
    nj, s, _ = own3.shape
    nw = own_r.shape[1] // 2 if col_split else own_r.shape[1]
    steps = nj if col_split else n

    def body(sel_ref, lo_ref, ls_ref, ro_ref, rs_ref, o_ref):
        o_ref[...] = (_dot_tn(lo_ref[0], ro_ref[...]) + _dot_tn(ls_ref[0], rs_ref[...])).astype(BF16)

    rspec = pl.BlockSpec((s, nw), lambda j, sel_ref: (0, sel_ref[1]))
    outs, rode = _run(
        body, name=name, grid=(steps,),
        out_shape=[jax.ShapeDtypeStruct((steps * HC, nw), BF16)],
        in_specs=[pl.BlockSpec((1, s, HC), lambda j, sel_ref: (sel_ref[0] + j, 0, 0)),
                  pl.BlockSpec((1, s, HC), lambda j, sel_ref: (j, 0, 0)), rspec, rspec],
        out_specs=[pl.BlockSpec((HC, nw), lambda j, sel_ref: (j, 0))],
        scratch_shapes=[], args=(own3, sib3, own_r, sib_r), rider=rider, prefetch=(sel,))
    return outs[0], rode


def _gating(proj, nv, ws_ref, bst):
    u, v = proj[:, 0:D_A], proj[:, D_A:2 * D_A]
    gu, gv = _gelu(u), _gelu(v)
    mu = jnp.mean(gv, axis=-1, keepdims=True)
    dv = gv - mu
    rstd = lax.rsqrt(jnp.mean(dv * dv, axis=-1, keepdims=True) + EPS)
    vhat = dv * rstd
    vn = vhat * nv
    r = lax.broadcasted_iota(jnp.int32, (CHUNK, CHUNK), 0)
    c = lax.broadcasted_iota(jnp.int32, (CHUNK, CHUNK), 1)
    wm = [jnp.where(r >= c, ws_ref[hd], 0.0).astype(BF16) for hd in range(A_HEADS)]
    vnb = vn.astype(BF16)
    rows = []
    for n in range(proj.shape[0] // CHUNK):
        blocks = []
        for hd in range(A_HEADS):
            blk = vnb[n * CHUNK:(n + 1) * CHUNK, hd * CHUNK:(hd + 1) * CHUNK]
            blocks.append(_dot_nn(wm[hd], blk) + bst[:, hd:hd + 1])
        rows.append(jnp.concatenate(blocks, axis=1))
    z = jnp.concatenate(rows, axis=0)
    return u, v, gu, rstd, vhat, vnb, wm, z


def _conv(proj, cw, prev_xp):
    bg = proj[:, 2 * D_A:2 * D_A + D_B]
    cg = proj[:, 2 * D_A + D_B:2 * D_A + 2 * D_B]
    xb = proj[:, 2 * D_A + 2 * D_B:]
    xp = cg * xb
    x1 = _shift_down(xp, 1, prev_xp)
    x2 = _shift_down(xp, 2, prev_xp)
    conv = cw[0:1, :] * x2 + cw[1:2, :] * x1 + cw[2:3, :] * xp
    return bg, cg, xb, xp, x1, x2, conv


def _ab_fwd(x, p, abin_all, about_all, nv, ws, bst, cw, rider=None):
    s = x.shape[0]
    nt = s // TM

    def body(x_ref, p_ref, abin_hbm, about_hbm, nv_ref, ws_ref, bst_ref, cw_ref,
             xo_ref, h_ref, proj_ref, out_ref, abin, about, prev, sems):
        @pl.when(pl.program_id(0) == 0)
        def _():
            cps = _load_rows(abin_hbm, None, abin, sems, 0) + _load_rows(about_hbm, None, about, sems, N_DEV)
            for cp in cps:
                cp.start()
            prev[...] = jnp.zeros_like(prev)
            for cp in cps:
                cp.wait()

        sh, sc, gate, gn = _mod_rows(p_ref)
        x = x_ref[...]
        hb = _modulate(x, gn, sh, sc).astype(BF16)
        h_ref[...] = hb
        proj = _dot_nt(hb, abin[...])
        proj_ref[...] = proj
        _, _, gu, _, _, _, _, z = _gating(proj, nv_ref[...], ws_ref, bst_ref[...])
        bg, _, _, xp, _, _, conv = _conv(proj, cw_ref[...], prev[...])
        prev[...] = xp[TM - CONV_HALO:, :]
        cat = jnp.concatenate([gu * z, bg * conv], axis=1).astype(BF16)
        out = _dot_nn(cat, about[...])
        out_ref[...] = out
        xo_ref[...] = x + gate * out

    tile = pl.BlockSpec((TM, D), lambda i: (i, 0))
    full = lambda a: pl.BlockSpec(a.shape, lambda i: (0,) * a.ndim)
    return _run(
        body, name="ab_fwd", grid=(nt,),
        out_shape=[jax.ShapeDtypeStruct((s, D), F32), jax.ShapeDtypeStruct((s, D), BF16),
                   jax.ShapeDtypeStruct((s, D_AB), F32), jax.ShapeDtypeStruct((s, D), F32)],
        in_specs=[tile, pl.BlockSpec((8, D), lambda i: (0, 0)), ANY, ANY, full(nv), full(ws), full(bst), full(cw)],
        out_specs=[tile, tile, pl.BlockSpec((TM, D_AB), lambda i: (i, 0)), tile],
        scratch_shapes=[pltpu.VMEM((D_AB, D), BF16), pltpu.VMEM((D, D), BF16), pltpu.VMEM((CONV_HALO, D_B), F32),
                        pltpu.SemaphoreType.DMA((2 * N_DEV,))],
        args=(x, p, abin_all, about_all, nv, ws, bst, cw), rider=rider)


def _ab_bwd(dxo, x, p, proj, out, abin_all, about_all, nv, ws, bst, cw, rider=None):
    s = x.shape[0]
    nt = s // TM
    npj = D_AB // HC

    def body(dxo_ref, x_ref, p_ref, proj_ref, halo_ref, out_ref, abin_hbm, about_hbm, nv_ref, ws_ref, bst_ref, cw_ref,
             dx_ref, dproj_ref, cat_ref, dy_ref, st_ref, dnv_ref, dws_ref, dbs_ref, dcw_ref,
             abin, about, nxt, stats, dnv, dws, dbs, dcw, sems):
        i = pl.program_id(0)
        ti = nt - 1 - i

        @pl.when(i == 0)
        def _():
            cps = _load_rows(abin_hbm, None, abin, sems, 0) + _load_rows(about_hbm, None, about, sems, N_DEV)
            for cp in cps:
                cp.start()
            for z in (nxt, stats, dnv, dws, dbs, dcw):
                z[...] = jnp.zeros_like(z)
            for cp in cps:
                cp.wait()

        sh, sc, gate, gn = _mod_rows(p_ref)
        x = x_ref[...]
        dxo = dxo_ref[...]
        dyb = (gate * dxo).astype(BF16)
        dy_ref[...] = dyb
        stats[2] += _colsum8(dxo * out_ref[...])
        dcat = _dot_nt(dyb, about[...])
        dya, dyb2 = dcat[:, 0:D_A], dcat[:, D_A:]

        proj = proj_ref[...]
        nvv = nv_ref[...]
        u, v, gu, rstd, vhat, vnb, wm, z = _gating(proj, nvv, ws_ref, bst_ref[...])
        dgu = dya * z
        dzb = (dya * gu).astype(BF16)
        dz32 = dya * gu
        rows = []
        for n in range(TM // CHUNK):
            blocks = []
            for hd in range(A_HEADS):
                sl = (slice(n * CHUNK, (n + 1) * CHUNK), slice(hd * CHUNK, (hd + 1) * CHUNK))
                dbs[hd] += dz32[sl]
                dws[hd] += _dot_nt(dzb[sl], vnb[sl])
                blocks.append(_dot_tn(wm[hd], dzb[sl]))
            rows.append(jnp.concatenate(blocks, axis=1))
        dvn = jnp.concatenate(rows, axis=0)
        dnv[...] += _colsum8(dvn * vhat)
        dvh = dvn * nvv
        dgv = rstd * (dvh - jnp.mean(dvh, axis=-1, keepdims=True) - vhat * jnp.mean(dvh * vhat, axis=-1, keepdims=True))
        du = dgu * _gelu_grad(u)
        dv = dgv * _gelu_grad(v)

        halo = halo_ref[...]
        prev_xp = jnp.where(ti > 0, halo[:, 2 * D_A + D_B:2 * D_A + 2 * D_B] * halo[:, 2 * D_A + 2 * D_B:], 0.0)
        cwv = cw_ref[...]
        bg, cg, xb, xp, x1, x2, conv = _conv(proj, cwv, prev_xp)
        dbg = dyb2 * conv
        dconv = dyb2 * bg
        dcw[...] += jnp.concatenate(
            [jnp.sum(_colsum8(dconv * t), axis=0, keepdims=True) for t in (x2, x1, xp)] + [jnp.zeros((5, D_B), F32)], axis=0)
        nx = nxt[...]
        dxp = cwv[2:3, :] * dconv + cwv[1:2, :] * _shift_up(dconv, 1, nx) + cwv[0:1, :] * _shift_up(dconv, 2, nx)
        nxt[...] = dconv[0:CONV_HALO, :]
        dcg = dxp * xb
        dxb = dxp * cg

        dproj = jnp.concatenate([du, dv, dbg, dcg, dxb], axis=1).astype(BF16)
        for k in range(npj):
            dproj_ref[k] = dproj[:, k * HC:(k + 1) * HC]
        cat = jnp.concatenate([gu * z, bg * conv], axis=1).astype(BF16)
        for k in range(D // HC):
            cat_ref[k] = cat[:, k * HC:(k + 1) * HC]
        dh = _dot_nn(dproj, abin[...])
        dx_ref[...] = dxo + _modulate_bwd(dh, x, gn, sc, stats)

        @pl.when(i == nt - 1)
        def _():
            _stats_out(stats, st_ref)
            dnv_ref[...] = jnp.concatenate([jnp.sum(dnv[...], axis=0, keepdims=True), jnp.zeros((7, D_A), F32)], axis=0)
            r = lax.broadcasted_iota(jnp.int32, (CHUNK, CHUNK), 0)
            c = lax.broadcasted_iota(jnp.int32, (CHUNK, CHUNK), 1)
            for hd in range(A_HEADS):
                dws_ref[hd] = jnp.where(r >= c, dws[hd], 0.0)
                dbs_ref[hd] = jnp.broadcast_to(jnp.sum(dbs[hd], axis=1, keepdims=True), (CHUNK, CHUNK))
            dcw_ref[...] = dcw[...]

    rev = pl.BlockSpec((TM, D), lambda i: (nt - 1 - i, 0))
    small = pl.BlockSpec((8, D), lambda i: (0, 0))
    full = lambda a: pl.BlockSpec(a.shape, lambda i: (0,) * a.ndim)
    hpt = TM // CONV_HALO
    fixed = lambda shape: pl.BlockSpec(shape, lambda i: (0,) * len(shape))
    return _run(
        body, name="ab_bwd", grid=(nt,),
        out_shape=[jax.ShapeDtypeStruct((s, D), F32), jax.ShapeDtypeStruct((npj, s, HC), BF16),
                   jax.ShapeDtypeStruct((D // HC, s, HC), BF16), jax.ShapeDtypeStruct((s, D), BF16),
                   jax.ShapeDtypeStruct((8, D), F32), jax.ShapeDtypeStruct((8, D_A), F32),
                   jax.ShapeDtypeStruct((A_HEADS, CHUNK, CHUNK), F32), jax.ShapeDtypeStruct((A_HEADS, CHUNK, CHUNK), F32),
                   jax.ShapeDtypeStruct((8, D_B), F32)],
        in_specs=[rev, rev, small,
                  pl.BlockSpec((TM, D_AB), lambda i: (nt - 1 - i, 0)),
                  pl.BlockSpec((CONV_HALO, D_AB), lambda i: (jnp.maximum((nt - 1 - i) * hpt - 1, 0), 0)),
                  rev, ANY, ANY, full(nv), full(ws), full(bst), full(cw)],
        out_specs=[rev, pl.BlockSpec((npj, TM, HC), lambda i: (0, nt - 1 - i, 0)),
                   pl.BlockSpec((D // HC, TM, HC), lambda i: (0, nt - 1 - i, 0)), rev,
                   small, fixed((8, D_A)), fixed((A_HEADS, CHUNK, CHUNK)), fixed((A_HEADS, CHUNK, CHUNK)), fixed((8, D_B))],
        scratch_shapes=[pltpu.VMEM((D_AB, D), BF16), pltpu.VMEM((D, D), BF16), pltpu.VMEM((CONV_HALO, D_B), F32),
                        pltpu.VMEM((4, 8, D), F32), pltpu.VMEM((8, D_A), F32),
                        pltpu.VMEM((A_HEADS, CHUNK, CHUNK), F32), pltpu.VMEM((A_HEADS, CHUNK, CHUNK), F32),
                        pltpu.VMEM((8, D_B), F32), pltpu.SemaphoreType.DMA((2 * N_DEV,))],
        args=(dxo, x, p, proj, proj, out, abin_all, about_all, nv, ws, bst, cw), rider=rider)


def _pool_counts(first_token, rows):
    t = (first_token + lax.broadcasted_iota(jnp.int32, (rows, 1), 0) + 1).astype(F32)
    lane = lax.broadcasted_iota(jnp.int32, (1, D), 1)
    w = jnp.where(lane < POOL_G, 2.0, jnp.where(lane < 2 * POOL_G, 4.0, jnp.where(lane < 3 * POOL_G, 8.0, 16.0)))
    return jnp.minimum(t, w)


def _window_sums(ext, n_keep, lead, back):
    n = ext.shape[0]
    sh = (lambda v, k: pltpu.roll(v, k, 0)) if back else (lambda v, k: pltpu.roll(v, n - k, 0))
    s2 = ext + sh(ext, 1)
    s4 = s2[:, POOL_G:] + sh(s2[:, POOL_G:], 2)
    s8 = s4[:, POOL_G:] + sh(s4[:, POOL_G:], 4)
    s16 = s8[:, POOL_G:] + sh(s8[:, POOL_G:], 8)
    keep = slice(lead, lead + n_keep)
    return jnp.concatenate([s2[keep, 0:POOL_G], s4[keep, 0:POOL_G], s8[keep, 0:POOL_G], s16[keep, :]], axis=1)


def _pool_fwd(x, p, pool_all, pscale):
    s = x.shape[0]
    nt = s // TM
    ng = D // POOL_G

    def body(x_ref, p_ref, wg_ref, ps_ref, xo_ref, pb_ref, op_ref, prev):
        i = pl.program_id(0)

        @pl.when(i == 0)
        def _():
            prev[...] = jnp.zeros_like(prev)

        sh, sc, gate, gn = _mod_rows(p_ref)
        x = x_ref[...]
        h = _modulate(x, gn, sh, sc)
        win = _window_sums(jnp.concatenate([prev[...], h], axis=0), TM, POOL_HALO, True)
        prev[...] = h[TM - POOL_HALO:, :]
        pb = (win / _pool_counts(i * TM, TM) - h).astype(BF16)
        pb_ref[...] = pb
        op = jnp.concatenate(
            [_dot_nn(pb[:, g * POOL_G:(g + 1) * POOL_G], wg_ref[:, g].reshape(POOL_G, POOL_G)) for g in range(ng)], axis=1)
        op_ref[...] = op
        xo_ref[...] = x + gate * (op * ps_ref[...])

    tile = pl.BlockSpec((TM, D), lambda i: (i, 0))
    return pl.pallas_call(
        body, name="pool_fwd", grid=(nt,),
        out_shape=[jax.ShapeDtypeStruct((s, D), F32), jax.ShapeDtypeStruct((s, D), BF16), jax.ShapeDtypeStruct((s, D), F32)],
        in_specs=[tile, pl.BlockSpec((8, D), lambda i: (0, 0)),
                  pl.BlockSpec(pool_all.shape, lambda i: (0, 0, 0, 0)), pl.BlockSpec((1, D), lambda i: (0, 0))],
        out_specs=[tile, tile, tile],
        scratch_shapes=[pltpu.VMEM((POOL_HALO, D), F32)],
        compiler_params=ARB1,
    )(x, p, pool_all, pscale)


def _pool_bwd(dxo, x, p, pb, op, pool_all, pscale):
    s = x.shape[0]
    nt = s // TM
    ng = D // POOL_G

    def body(dxo_ref, x_ref, p_ref, pb_ref, op_ref, wg_ref, ps_ref,
             dx_ref, st_ref, dps_ref, dwg_ref, nxt, stats, dps, dwg):
        i = pl.program_id(0)
        ti = nt - 1 - i

        @pl.when(i == 0)
        def _():
            for z in (nxt, stats, dps, dwg):
                z[...] = jnp.zeros_like(z)

        sh, sc, gate, gn = _mod_rows(p_ref)
        x = x_ref[...]
        dxo = dxo_ref[...]
        ps = ps_ref[...]
        op = op_ref[...]
        dmo = gate * dxo
        stats[2] += _colsum8(dxo * (op * ps))
        dps[...] += _colsum8(dmo * op)
        dopb = (dmo * ps).astype(BF16)
        pbv = pb_ref[...]
        dps_parts = []
        for g in range(ng):
            sl = slice(g * POOL_G, (g + 1) * POOL_G)
            dps_parts.append(_dot_nt(dopb[:, sl], wg_ref[:, g].reshape(POOL_G, POOL_G)))
            dwg[g] += _dot_tn(pbv[:, sl], dopb[:, sl])
        dp = jnp.concatenate(dps_parts, axis=1)
        q = dp / _pool_counts(ti * TM, TM)
        wsum = _window_sums(jnp.concatenate([q, nxt[...]], axis=0), TM, 0, False)
        nxt[...] = q[0:POOL_HALO, :]
        dx_ref[...] = dxo + _modulate_bwd(wsum - dp, x, gn, sc, stats)

        @pl.when(i == nt - 1)
        def _():
            _stats_out(stats, st_ref)
            dps_ref[...] = jnp.concatenate([jnp.sum(dps[...], axis=0, keepdims=True), jnp.zeros((7, D), F32)], axis=0)
            dwg_ref[...] = dwg[...].astype(BF16)

    rev = pl.BlockSpec((TM, D), lambda i: (nt - 1 - i, 0))
    small = pl.BlockSpec((8, D), lambda i: (0, 0))
    return pl.pallas_call(
        body, name="pool_bwd", grid=(nt,),
        out_shape=[jax.ShapeDtypeStruct((s, D), F32), jax.ShapeDtypeStruct((8, D), F32), jax.ShapeDtypeStruct((8, D), F32),
                   jax.ShapeDtypeStruct((ng, POOL_G, POOL_G), BF16)],
        in_specs=[rev, rev, small, rev, rev,
                  pl.BlockSpec(pool_all.shape, lambda i: (0, 0, 0, 0)), pl.BlockSpec((1, D), lambda i: (0, 0))],
        out_specs=[rev, small, small, pl.BlockSpec((ng, POOL_G, POOL_G), lambda i: (0, 0, 0))],
        scratch_shapes=[pltpu.VMEM((POOL_HALO, D), F32), pltpu.VMEM((4, 8, D), F32), pltpu.VMEM((8, D), F32),
                        pltpu.VMEM((ng, POOL_G, POOL_G), F32)],
        compiler_params=ARB1,
    )(dxo, x, p, pb, op, pool_all, pscale)


def _head(x, fg, tgt):
    s = x.shape[0]
    nt = s // TM

    def body(x_ref, fg_ref, t_ref, dx_ref, loss_ref, dfg_ref, sq, dfg):
        i = pl.program_id(0)

        @pl.when(i == 0)
        def _():
            sq[...] = jnp.zeros_like(sq)
            dfg[...] = jnp.zeros_like(dfg)

        x = x_ref[...]
        g = fg_ref[...]
        r = lax.rsqrt(jnp.mean(x * x, axis=-1, keepdims=True) + EPS)
        xn = x * r
        e = xn * g - t_ref[...]
        sq[...] += _colsum8(e * e)
        dy = e * (1.0 / D)
        dfg[...] += _colsum8(dy * xn)
        dxn = dy * g
        dx_ref[...] = r * (dxn - xn * jnp.mean(dxn * xn, axis=-1, keepdims=True))

        @pl.when(i == nt - 1)
        def _():
            total = jnp.sum(jnp.sum(sq[...], axis=0, keepdims=True), axis=1, keepdims=True)
            loss_ref[...] = jnp.broadcast_to(total * (0.5 / D), loss_ref.shape)
            dfg_ref[...] = jnp.concatenate([jnp.sum(dfg[...], axis=0, keepdims=True), jnp.zeros((7, D), F32)], axis=0)

    tile = pl.BlockSpec((TM, D), lambda i: (i, 0))
    return pl.pallas_call(
        body, name="head", grid=(nt,),
        out_shape=[jax.ShapeDtypeStruct((s, D), F32), jax.ShapeDtypeStruct((8, 128), F32), jax.ShapeDtypeStruct((8, D), F32)],
        in_specs=[tile, pl.BlockSpec((1, D), lambda i: (0, 0)), tile],
        out_specs=[tile, pl.BlockSpec((8, 128), lambda i: (0, 0)), pl.BlockSpec((8, D), lambda i: (0, 0))],
        scratch_shapes=[pltpu.VMEM((8, D), F32), pltpu.VMEM((8, D), F32)],
        compiler_params=ARB1,
    )(x, fg, tgt)


def _adamw_math(w, g, m, v):
    m = ADAM_B1 * m + (1.0 - ADAM_B1) * g
    v = ADAM_B2 * v + (1.0 - ADAM_B2) * (g * g)
    m_hat = m / (1.0 - ADAM_B1 ** ADAM_STEP)
    v_hat = v / (1.0 - ADAM_B2 ** ADAM_STEP)
    delta = -ADAM_LR * (m_hat / (jnp.sqrt(v_hat) + ADAM_EPS) + ADAM_WD * w)
    return delta, m, v


def _finish(parts, w, m, v, transposed, rb, name, halves=False):
    nf, r, c = w.shape
    npart = parts[0].shape[0]

    def body(*refs):
        p_refs = refs[:nf]
        w_ref, m_ref, v_ref, g_ref, d_ref, mo_ref, vo_ref = refs[nf:]
        for f in range(nf):
            @pl.when(pl.program_id(0) == f)
            def _():
                g = p_refs[f][0].astype(F32)
                for k in range(1, npart):
                    g = g + p_refs[f][k].astype(F32)
                if halves:
                    g = jnp.concatenate([g[0], g[1]], axis=1)
                if transposed:
                    g = g.T
                g_ref[0] = g
                d_ref[0], mo_ref[0], vo_ref[0] = _adamw_math(w_ref[0], g, m_ref[0], v_ref[0])

    blk = pl.BlockSpec((1, rb, c), lambda f, i: (f, i, 0))

    def pblk(mine):
        if halves:
            return pl.BlockSpec((npart, 2, rb, c // 2), lambda f, i: (0, 0, jnp.where(f == mine, i, 0), 0))
        if transposed:
            return pl.BlockSpec((npart, c, rb), lambda f, i: (0, 0, jnp.where(f == mine, i, 0)))
        return pl.BlockSpec((npart, rb, c), lambda f, i: (0, jnp.where(f == mine, i, 0), 0))

    return pl.pallas_call(
        body, name=name, grid=(nf, r // rb),
        out_shape=[jax.ShapeDtypeStruct(w.shape, F32)] * 4,
        in_specs=[pblk(f) for f in range(nf)] + [blk, blk, blk], out_specs=[blk] * 4,
        compiler_params=pltpu.CompilerParams(dimension_semantics=("arbitrary", "arbitrary"), vmem_limit_bytes=VMEM_LIMIT),
    )(*parts, w, m, v)


def _adamw(w, g, m, v, name):
    def body(w_ref, g_ref, m_ref, v_ref, d_ref, mo_ref, vo_ref):
        d_ref[...], mo_ref[...], vo_ref[...] = _adamw_math(w_ref[...], g_ref[...], m_ref[...], v_ref[...])

    return pl.pallas_call(
        body, name=name, out_shape=[jax.ShapeDtypeStruct(w.shape, F32)] * 3,
        in_specs=[VMEM_SPEC] * 4, out_specs=[VMEM_SPEC] * 3,
    )(w, g, m, v)


def _wmod_finish(act_t, dmod_cols, w, m, v):
    rb = 256
    ncol = w.shape[-1]

    def body(a_ref, dm_ref, w_ref, m_ref, v_ref, g_ref, d_ref, mo_ref, vo_ref):
        g = a_ref[:, 0:1] * dm_ref[0, 0:1, :]
        for k in range(1, N_DEV):
            g = g + a_ref[:, k:k + 1] * dm_ref[0, k:k + 1, :]
        g_ref[0] = g
        d_ref[0], mo_ref[0], vo_ref[0] = _adamw_math(w_ref[0], g, m_ref[0], v_ref[0])

    blk = pl.BlockSpec((1, rb, ncol), lambda l, i: (l, i, 0))
    return pl.pallas_call(
        body, name="wmod_finish", grid=(2, D // rb),
        out_shape=[jax.ShapeDtypeStruct(w.shape, F32)] * 4,
        in_specs=[pl.BlockSpec((rb, N_DEV), lambda l, i: (i, 0)), pl.BlockSpec((1, N_DEV, ncol), lambda l, i: (l, 0, 0)),
                  blk, blk, blk],
        out_specs=[blk] * 4,
        compiler_params=pltpu.CompilerParams(dimension_semantics=("arbitrary", "arbitrary"), vmem_limit_bytes=VMEM_LIMIT),
    )(act_t, dmod_cols, w, m, v)


def _pack(pieces):
    flat, offs, at = [], [], 0
    for a in pieces:
        a = a.reshape(-1)
        n = -(-a.shape[0] // 128) * 128
        flat.append(jnp.pad(a, (0, n - a.shape[0])))
        offs.append(at)
        at += n
    return jnp.concatenate(flat).reshape(-1, 128), offs


def _param_block(mod_l, sub, gn):
    return jnp.concatenate([mod_l[sub], gn[None, :], jnp.zeros((4, D), F32)], axis=0)


def kernel(x, c, norm_g, w_mod, b_mod, w_ffn_in, w_ffn_out, ab_w_in, ab_norm_v, ab_w_s, ab_b_s, ab_conv_w, ab_w_out, pool_w_grp, pool_scale, final_g, loss_target, m_norm_g, m_w_mod, m_b_mod, m_w_ffn_in, m_w_ffn_out, m_ab_w_in, m_ab_norm_v, m_ab_w_s, m_ab_b_s, m_ab_conv_w, m_ab_w_out, m_pool_w_grp, m_pool_scale, m_final_g, v_norm_g, v_w_mod, v_b_mod, v_w_ffn_in, v_w_ffn_out, v_ab_w_in, v_ab_norm_v, v_ab_w_s, v_ab_b_s, v_ab_conv_w, v_ab_w_out, v_pool_w_grp, v_pool_scale, v_final_g):
    me = 4 * lax.axis_index("x") + 2 * lax.axis_index("y") + lax.axis_index("c")
    x0 = x[0]
    tgt = loss_target[0]
    n_in = w_ffn_in.shape[-1]
    n_out = w_ffn_out.shape[-2]
    n_abin = ab_w_in.shape[-1]
    n_about = ab_w_out.shape[-2]
    n_pool = pool_w_grp.shape[-2]
    n_mod = w_mod.shape[-1]
    n_ng = norm_g.shape[-1]
    n_cw = ab_conv_w.shape[-1]
    n_ps = pool_scale.shape[-1]

    pack, offs = _pack([c, norm_g, ab_conv_w, pool_scale])
    got = _exchange_small(pack, "gather_small", False).reshape(N_DEV, -1)
    c_all = got[:, offs[0]:offs[0] + D]
    ng_full = got[:, offs[1]:offs[1] + 6 * n_ng].reshape(N_DEV, 2, 3, n_ng).transpose(1, 2, 0, 3).reshape(2, 3, D)
    cw_full = got[:, offs[2]:offs[2] + 3 * n_cw].reshape(N_DEV, 3, n_cw).transpose(1, 0, 2).reshape(3, D_B)
    ps_full = got[:, offs[3]:offs[3] + n_ps].reshape(1, D)

    act_all, mod_cols = _mod_fwd(c_all, w_mod)
    mod_got = _exchange_small(mod_cols.reshape(-1, 128), "gather_mod", False).reshape(N_DEV, 2, N_DEV, n_mod)
    mod = lax.dynamic_index_in_dim(mod_got, me, axis=2, keepdims=False).transpose(1, 0, 2).reshape(2, 9 * D) + b_mod
    mod = mod.reshape(2, 3, 3, D)

    win_sh = jnp.swapaxes(w_ffn_in.reshape(4, D, n_in), 1, 2).astype(BF16)
    wout_sh = w_ffn_out.reshape(4, n_out, D).astype(BF16)
    abin_sh = ab_w_in[0].T.astype(BF16)
    about_sh = ab_w_out[0].astype(BF16)
    pool_sh = pool_w_grp[0].astype(BF16)
    nv = ab_norm_v
    ws = ab_w_s[0]
    bst = ab_b_s[0].T
    cw8 = jnp.concatenate([cw_full, jnp.zeros((5, D_B), F32)], axis=0)
    win, wout = [None] * 4, [None] * 4

    pb = [[_param_block(mod[l], s, ng_full[l, s]) for s in range(3)] for l in range(2)]
    win[0], wout[0] = _run_rider(_GatherRider([win_sh[0], wout_sh[0]]), "gather_ffn_0")
    (x1, h00, g00, u00, y00), (abin_all, about_all, win[1]) = _ffn_fwd(
        x0, pb[0][0], win[0], wout[0], 0, _GatherRider([abin_sh, about_sh, win_sh[1]]))
    (x2, h01, proj, ab_out), (wout[1],) = _ab_fwd(
        x1, pb[0][1], abin_all, about_all, nv, ws, bst, cw8, _GatherRider([wout_sh[1]]))
    (x3, h02, g02, u02, y02), (win[2], wout[2]) = _ffn_fwd(
        x2, pb[0][2], win[1], wout[1], 1, _GatherRider([win_sh[2], wout_sh[2]]))
    (x4, h10, g10, u10, y10), (pool_all, win[3], wout[3]) = _ffn_fwd(
        x3, pb[1][0], win[2], wout[2], 2, _GatherRider([pool_sh, win_sh[3], wout_sh[3]]))
    x5, pooled, pool_out = _pool_fwd(x4, pb[1][1], pool_all, ps_full)
    (x6, h12, g12, u12, y12), _ = _ffn_fwd(x5, pb[1][2], win[3], wout[3], 3)
    dx6, loss_blk, dfg = _head(x6, final_g.reshape(1, D), tgt)
    loss = lax.psum(loss_blk[0, 0], ("x", "y", "c"))

    core = lax.axis_index("c")
    sel_rows = lambda n: jnp.stack([core * n, 0]).astype(jnp.int32)
    sel_cols = jnp.stack([0, core]).astype(jnp.int32)
    p_in, p_out = [None] * 4, [None] * 4

    def exchange(dgu, a, h, dy):
        return _SiblingRider([(dgu, NCH), (a, None), (h, None), (dy, None)])

    def ffn_wgrads(f, dgu, a, h, dy, got, ride_out=None):
        s_dgu, s_a, s_h, s_dy = got
        g_out, rode = _wgrad_pair(a, s_a, dy, s_dy, sel_cols, NCH, True, f"wgrad_out_{f}", ride_out)
        g_in, (p_out[f],) = _wgrad_pair(dgu, s_dgu, h, s_h, sel_rows(NCH), NCH, False, f"wgrad_in_{f}",
                                        _ChipScatterRider([], [g_out]))
        return g_in, rode

    (dx5, dgu12, a12, dy12, st12), _ = _ffn_bwd(dx6, x5, pb[1][2], g12, u12, y12, win[3], wout[3], 3)
    dx4, st11, dps, gw_pool = _pool_bwd(dx5, x4, pb[1][1], pooled, pool_out, pool_all, ps_full)
    (dx3, dgu10, a10, dy10, st10), got3 = _ffn_bwd(
        dx4, x3, pb[1][0], g10, u10, y10, win[2], wout[2], 2, exchange(dgu12, a12, h12, dy12))
    g_in3, (p_pool,) = ffn_wgrads(3, dgu12, a12, h12, dy12, got3, _ScatterRider([gw_pool]))
    ride = _Riders([exchange(dgu10, a10, h10, dy10), _ChipScatterRider([g_in3], [])])
    (dx2, dgu02, a02, dy02, st02), rode = _ffn_bwd(dx3, x2, pb[0][2], g02, u02, y02, win[1], wout[1], 1, ride)
    got2, (p_in[3],) = ride.split(rode)
    g_in2, _ = ffn_wgrads(2, dgu10, a10, h10, dy10, got2)
    ride = _Riders([exchange(dgu02, a02, h02, dy02), _ChipScatterRider([g_in2], [])])
    (dx1, dproj, cat, dy01, st01, dnv, dws, dbs, dcw), rode = _ab_bwd(
        dx2, x1, pb[0][1], proj, ab_out, abin_all, about_all, nv, ws, bst, cw8, ride)
    got1, (p_in[2],) = ride.split(rode)
    g_in1, _ = ffn_wgrads(1, dgu02, a02, h02, dy02, got1)
    ride = _Riders([_SiblingRider([(dproj, 5), (cat, 2), (h01, None), (dy01, None)]), _ChipScatterRider([g_in1], [])])
    (dx0, dgu00, a00, dy00, st00), rode = _ffn_bwd(dx1, x0, pb[0][0], g00, u00, y00, win[0], wout[0], 0, ride)
    (s_dproj, s_cat, s_h01, s_dy01), (p_in[1],) = ride.split(rode)
    g_about, _ = _wgrad_pair(cat, s_cat, dy01, s_dy01, sel_rows(2), 2, False, "wgrad_ab_out")
    g_abin, (p_about,) = _wgrad_pair(dproj, s_dproj, h01, s_h01, sel_rows(5), 5, False, "wgrad_ab_in",
                                     _ChipScatterRider([g_about], []))
    ride = _Riders([exchange(dgu00, a00, h00, dy00), _ChipScatterRider([g_abin], [])])
    got0, (p_abin,) = ride.split(_run_rider(ride, "exchange_0"))
    g_in0, _ = ffn_wgrads(0, dgu00, a00, h00, dy00, got0)
    (p_in[0],) = _run_rider(_ChipScatterRider([g_in0], []), "scatter_in_0")
    grad_x = dx0[None]

    shape_in, shape_out = w_ffn_in.shape, w_ffn_out.shape
    fin = lambda a: a.reshape(4, D, n_in)
    fout = lambda a: a.reshape(4, n_out, D)
    fpool = lambda a: a.reshape(1, 4 * n_pool, POOL_G)
    r_in = _finish(p_in, fin(w_ffn_in), fin(m_w_ffn_in), fin(v_w_ffn_in), True, 256, "finish_ffn_in")
    r_out = _finish(p_out, fout(w_ffn_out), fout(m_w_ffn_out), fout(v_w_ffn_out), False, n_out // 2, "finish_ffn_out",
                    halves=True)
    r_abin = _finish([p_abin], ab_w_in, m_ab_w_in, v_ab_w_in, True, 256, "finish_ab_in")
    r_about = _finish([p_about], ab_w_out, m_ab_w_out, v_ab_w_out, False, n_about, "finish_ab_out")
    r_pool = _finish([p_pool.reshape(N_DEV, 4 * n_pool, POOL_G)], fpool(pool_w_grp), fpool(m_pool_w_grp), fpool(v_pool_w_grp),
                     False, 4 * n_pool, "finish_pool")
    r_in = [a.reshape(shape_in) for a in r_in]
    r_out = [a.reshape(shape_out) for a in r_out]
    r_pool = [a.reshape(pool_w_grp.shape) for a in r_pool]

    stats = [[st00, st01, st02], [st10, st11, st12]]
    dmod = jnp.stack([jnp.concatenate([stats[l][s][0:3].reshape(-1) for s in range(3)]) for l in range(2)])
    dng = jnp.stack([jnp.stack([stats[l][s][3] for s in range(3)]) for l in range(2)])
    spack, so = _pack([dmod, dng, dnv[0], dws, dbs[:, :, 0], dcw[0:3], dps[0], dfg[0]])
    sgot, ssum = _exchange_small(spack, "reduce_small", True)
    ssum = ssum.reshape(-1)
    take = lambda i, n: lax.dynamic_slice_in_dim(ssum, so[i], n)
    g_bmod = take(0, 2 * 9 * D).reshape(2, 9 * D)
    g_ng = lax.dynamic_slice_in_dim(take(1, 6 * D).reshape(2, 3, D), me * n_ng, n_ng, axis=2)
    g_nv = take(2, D_A).reshape(1, D_A)
    g_ws = take(3, A_HEADS * CHUNK * CHUNK).reshape(1, A_HEADS, CHUNK, CHUNK)
    g_bs = take(4, A_HEADS * CHUNK).reshape(1, A_HEADS, CHUNK)
    g_cw = lax.dynamic_slice_in_dim(take(5, 3 * D_B).reshape(1, 3, D_B), me * n_cw, n_cw, axis=2)
    g_ps = lax.dynamic_slice_in_dim(take(6, D).reshape(1, D), me * n_ps, n_ps, axis=1)
    g_fg = take(7, D)

    dmod_all = sgot.reshape(N_DEV, -1)[:, so[0]:so[0] + 2 * 9 * D].reshape(N_DEV, 2, 9 * D)
    dmod_cols = lax.dynamic_slice_in_dim(dmod_all, me * n_mod, n_mod, axis=2).transpose(1, 0, 2)
    r_wmod = _wmod_finish(act_all.T, dmod_cols, w_mod, m_w_mod, v_w_mod)

    small_w = [b_mod, norm_g, ab_norm_v, ab_w_s, ab_b_s, ab_conv_w, pool_scale, final_g]
    small_g = [g_bmod, g_ng, g_nv, g_ws, g_bs, g_cw, g_ps, g_fg]
    small_m = [m_b_mod, m_norm_g, m_ab_norm_v, m_ab_w_s, m_ab_b_s, m_ab_conv_w, m_pool_scale, m_final_g]
    small_v = [v_b_mod, v_norm_g, v_ab_norm_v, v_ab_w_s, v_ab_b_s, v_ab_conv_w, v_pool_scale, v_final_g]
    pw, po = _pack(small_w)
    pv = jnp.concatenate([jnp.pad(a.reshape(-1), (0, -a.size % 128), constant_values=1.0) for a in small_v]).reshape(-1, 128)
    sd, sm, sv = _adamw(pw, _pack(small_g)[0], _pack(small_m)[0], pv, "adamw_small")
    unpack = lambda packed: [packed.reshape(-1)[po[i]:po[i] + a.size].reshape(a.shape) for i, a in enumerate(small_w)]
    d_s, m_s, v_s = unpack(sd), unpack(sm), unpack(sv)

    def ordered(k, small):
        return [small[1], r_wmod[k], small[0], r_in[k], r_out[k], r_abin[k], small[2], small[3], small[4], small[5],
                r_about[k], r_pool[k], small[6], small[7]]

    grads = ordered(0, small_g)
    deltas = ordered(1, d_s)
    new_m = ordered(2, m_s)
    new_v = ordered(3, v_s)
    return (loss, grad_x, *grads, *deltas, *new_m, *new_v)
```

```python
import functools
import math

import jax
import jax.numpy as jnp
from jax import lax
from jax.experimental import pallas as pl
from jax.experimental.pallas import tpu as pltpu

F32 = jnp.float32
BF16 = jnp.bfloat16

N_DEV = 8
D = 1024
DFF = 2816
HC = 256
NCH = DFF // HC
D_A = 512
D_B = 512
D_AB = 2 * D_A + 3 * D_B
CHUNK = 128
A_HEADS = 4
POOL_G = 256
POOL_HALO = 16
CONV_HALO = 8
EPS = 1e-6
TM = 256
GELU_K = math.sqrt(2.0 / math.pi)
GELU_C = 0.044715

ADAM_LR = 0.001
ADAM_B1 = 0.9
ADAM_B2 = 0.999
ADAM_EPS = 1e-08
ADAM_WD = 0.01
ADAM_STEP = 10

VMEM_LIMIT = 56 * 1024 * 1024
MESH_ID = pl.DeviceIdType.MESH
ANY = pl.BlockSpec(memory_space=pl.ANY)
VMEM_SPEC = pl.BlockSpec(memory_space=pltpu.VMEM)
ARB1 = pltpu.CompilerParams(dimension_semantics=("arbitrary",), vmem_limit_bytes=VMEM_LIMIT)


def _dot_nt(a, b):
    return lax.dot_general(a, b, (((1,), (1,)), ((), ())), preferred_element_type=F32)


def _dot_nn(a, b):
    return lax.dot_general(a, b, (((1,), (0,)), ((), ())), preferred_element_type=F32)


def _dot_tn(a, b):
    return lax.dot_general(a, b, (((0,), (0,)), ((), ())), preferred_element_type=F32)


def _colsum8(v):
    r, n = v.shape
    return jnp.sum(v.reshape(r // 8, 8, n), axis=0)


def _gelu(x):
    return 0.5 * x * (1.0 + jnp.tanh(GELU_K * (x + GELU_C * x * x * x)))


def _gelu_grad(x):
    t = jnp.tanh(GELU_K * (x + GELU_C * x * x * x))
    return 0.5 * (1.0 + t) + 0.5 * x * (1.0 - t * t) * (GELU_K * (1.0 + 3.0 * GELU_C * x * x))


def _mod_rows(p_ref):
    return p_ref[0:1, :], p_ref[1:2, :], p_ref[2:3, :], p_ref[3:4, :]


def _modulate(x, gn, sh, sc):
    r = lax.rsqrt(jnp.mean(x * x, axis=-1, keepdims=True) + EPS)
    return ((x * r) * gn) * (1.0 + sc) + sh


def _modulate_bwd(dh, x, gn, sc, stats):
    r = lax.rsqrt(jnp.mean(x * x, axis=-1, keepdims=True) + EPS)
    xn = x * r
    stats[0] += _colsum8(dh)
    stats[1] += _colsum8(dh * (xn * gn))
    dy0 = dh * (1.0 + sc)
    stats[3] += _colsum8(dy0 * xn)
    dxn = dy0 * gn
    return r * (dxn - xn * jnp.mean(dxn * xn, axis=-1, keepdims=True))


def _stats_out(stats, out_ref):
    rows = [jnp.sum(stats[k], axis=0, keepdims=True) for k in range(4)]
    out_ref[...] = jnp.concatenate(rows + [jnp.zeros((4, stats.shape[-1]), F32)], axis=0)


def _shift_down(v, k, prev):
    n = v.shape[0]
    row = lax.broadcasted_iota(jnp.int32, v.shape, 0)
    out = pltpu.roll(v, k, 0)
    for j in range(k):
        out = jnp.where(row == j, prev[prev.shape[0] - k + j:prev.shape[0] - k + j + 1, :], out)
    return out


def _shift_up(v, k, nxt):
    n = v.shape[0]
    row = lax.broadcasted_iota(jnp.int32, v.shape, 0)
    out = pltpu.roll(v, n - k, 0)
    for j in range(k):
        out = jnp.where(row == n - k + j, nxt[j:j + 1, :], out)
    return out


def _load_rows(w_hbm, sel, dst, sems, base):
    n = dst.shape[0] // N_DEV
    cps = []
    for k in range(N_DEV):
        src = w_hbm.at[k] if sel is None else w_hbm.at[k, sel]
        cps.append(pltpu.make_async_copy(src, dst.at[pl.ds(k * n, n)], sems.at[base + k]))
    return cps


def _my_pos():
    return lax.axis_index("x"), lax.axis_index("y"), lax.axis_index("c")


def _peer(j):
    x, y, c = _my_pos()
    return (1 - x if j & 4 else x, 1 - y if j & 2 else y, 1 - c if j & 1 else c)


def _index(pos):
    return 4 * pos[0] + 2 * pos[1] + pos[2]


def _exchange_small(v, name, with_sum, rider=None):
    rows = v.shape[0]
    ri, ro = (len(rider.inputs), len(rider.out_shapes)) if rider is not None else (0, 0)
    nout = 2 if with_sum else 1

    def body(v_ref, *refs):
        r_in, refs = refs[:ri], refs[ri:]
        own, r_out, refs = refs[:nout], refs[nout:nout + ro], refs[nout + ro:]
        send_sems, recv_sems, local_sem = refs[:3]
        out_ref = own[0]
        if rider is not None:
            rider.first(r_in, r_out, refs[3:])
        me = _index(_my_pos())

        def copy(j, slot):
            return pltpu.make_async_remote_copy(
                src_ref=v_ref, dst_ref=out_ref.at[slot], send_sem=send_sems.at[j - 1], recv_sem=recv_sems.at[j - 1],
                device_id=_peer(j), device_id_type=MESH_ID)

        mine = pltpu.make_async_copy(v_ref, out_ref.at[me], local_sem)
        mine.start()
        sends = [copy(j, me) for j in range(1, N_DEV)]
        for cp in sends:
            cp.start()
        for j in range(1, N_DEV):
            copy(j, _index(_peer(j))).wait_recv()
        for cp in sends:
            cp.wait_send()
        mine.wait()
        if with_sum:
            acc = out_ref[0]
            for k in range(1, N_DEV):
                acc = acc + out_ref[k]
            own[1][...] = acc
        if rider is not None:
            rider.last(r_in, r_out, refs[3:])

    out_shape = [jax.ShapeDtypeStruct((N_DEV, rows, 128), F32)]
    out_specs = [VMEM_SPEC]
    if with_sum:
        out_shape.append(jax.ShapeDtypeStruct((rows, 128), F32))
        out_specs.append(VMEM_SPEC)
    scratch = [pltpu.SemaphoreType.DMA((N_DEV - 1,)), pltpu.SemaphoreType.DMA((N_DEV - 1,)), pltpu.SemaphoreType.DMA(())]
    if rider is None:
        res = pl.pallas_call(body, name=name, out_shape=out_shape, in_specs=[VMEM_SPEC], out_specs=out_specs,
                             scratch_shapes=scratch)(v)
        return res if with_sum else res[0]
    assert not rider.has_middle
    res = pl.pallas_call(
        body, name=name, out_shape=out_shape + rider.out_shapes, in_specs=[VMEM_SPEC] + [ANY] * ri,
        out_specs=out_specs + [ANY] * ro, scratch_shapes=scratch + rider.scratch)(v, *rider.inputs)
    return (res[:nout] if with_sum else res[0]), list(res[nout:])


class _GatherRider:
    has_middle = True

    def __init__(self, shards):
        n = len(shards)
        pairs = [s if isinstance(s, tuple) else (s, None) for s in shards]
        self.inputs = [a for a, _ in pairs]
        self.picks = [i for _, i in pairs]
        shapes = [a.shape if i is None else a.shape[1:] for a, i in pairs]
        self.out_shapes = [jax.ShapeDtypeStruct((N_DEV,) + s, a.dtype) for s, (a, _) in zip(shapes, pairs)]
        self.scratch = [pltpu.SemaphoreType.DMA((7 * n,)), pltpu.SemaphoreType.DMA((7 * n,)), pltpu.SemaphoreType.DMA((n,))]

    def _src(self, ins):
        return [r if i is None else r.at[i] for r, i in zip(ins, self.picks)]

    def _ctx(self, outs, scr):
        send, recv, _ = scr
        x, y, c = _my_pos()
        chips = [(1 - x, y), (x, 1 - y), (1 - x, 1 - y)]

        def copy(a, k, block, to, src=None):
            slot = outs[a].at[_index(block)]
            return pltpu.make_async_remote_copy(
                src_ref=slot if src is None else src, dst_ref=slot, send_sem=send.at[7 * a + k], recv_sem=recv.at[7 * a + k],
                device_id=to, device_id_type=MESH_ID)

        return (x, y, c), (x, y, 1 - c), chips, copy

    def _sends(self, ins, outs, scr):
        me, sib, chips, copy = self._ctx(outs, scr)
        out = []
        for a, src in enumerate(self._src(ins)):
            out.append(copy(a, 0, me, sib, src=src))
            out += [copy(a, 1 + j, me, (*chips[j], me[2]), src=src) for j in range(3)]
        return out

    def first(self, ins, outs, scr):
        me = _index(_my_pos())
        for a, src in enumerate(self._src(ins)):
            pltpu.make_async_copy(src, outs[a].at[me], scr[2].at[a]).start()
        for cp in self._sends(ins, outs, scr):
            cp.start()

    def middle(self, ins, outs, scr):
        me, sib, chips, copy = self._ctx(outs, scr)
        for j in range(3):
            for a in range(len(ins)):
                copy(a, 1 + j, (*chips[j], me[2]), me).wait_recv()
                copy(a, 4 + j, (*chips[j], me[2]), sib).start()

    def last(self, ins, outs, scr):
        me, sib, chips, copy = self._ctx(outs, scr)
        for a in range(len(ins)):
            copy(a, 0, sib, me).wait_recv()
            for j in range(3):
                copy(a, 4 + j, (*chips[j], sib[2]), me).wait_recv()
        for cp in self._sends(ins, outs, scr):
            cp.wait_send()
        for a, src in enumerate(self._src(ins)):
            for j in range(3):
                copy(a, 4 + j, (*chips[j], me[2]), sib).wait_send()
            pltpu.make_async_copy(src, outs[a].at[_index(me)], scr[2].at[a]).wait()


class _ScatterRider:
    has_middle = False

    def __init__(self, grads):
        n = len(grads)
        self.inputs = list(grads)
        self.out_shapes = []
        for g in grads:
            if g.ndim == 3:
                self.out_shapes.append(jax.ShapeDtypeStruct((N_DEV, g.shape[0], g.shape[1] // N_DEV, g.shape[2]), g.dtype))
            else:
                self.out_shapes.append(jax.ShapeDtypeStruct((N_DEV, g.shape[0] // N_DEV, g.shape[1]), g.dtype))
        self.scratch = [pltpu.SemaphoreType.DMA((7 * n,)), pltpu.SemaphoreType.DMA((7 * n,)), pltpu.SemaphoreType.DMA((n,))]

    @staticmethod
    def _part(ref, k):
        if ref.ndim == 3:
            n = ref.shape[1] // N_DEV
            return ref.at[:, pl.ds(pl.multiple_of(k * n, 16), n)]
        n = ref.shape[0] // N_DEV
        return ref.at[pl.ds(pl.multiple_of(k * n, 16), n)]

    def _copy(self, ins, outs, scr, g, j, to, src_dev):
        return pltpu.make_async_remote_copy(
            src_ref=self._part(ins[g], to), dst_ref=outs[g].at[src_dev], send_sem=scr[0].at[7 * g + j - 1],
            recv_sem=scr[1].at[7 * g + j - 1], device_id=_peer(j), device_id_type=MESH_ID)

    def first(self, ins, outs, scr):
        me = _index(_my_pos())
        for g in range(len(ins)):
            pltpu.make_async_copy(self._part(ins[g], me), outs[g].at[me], scr[2].at[g]).start()
        for j in range(1, N_DEV):
            for g in range(len(ins)):
                self._copy(ins, outs, scr, g, j, _index(_peer(j)), me).start()

    def last(self, ins, outs, scr):
        me = _index(_my_pos())
        for j in range(1, N_DEV):
            for g in range(len(ins)):
                self._copy(ins, outs, scr, g, j, me, _index(_peer(j))).wait_recv()
        for j in range(1, N_DEV):
            for g in range(len(ins)):
                self._copy(ins, outs, scr, g, j, _index(_peer(j)), me).wait_send()
        for g in range(len(ins)):
            pltpu.make_async_copy(self._part(ins[g], me), outs[g].at[me], scr[2].at[g]).wait()


class _SiblingRider:
    has_middle = False

    def __init__(self, items):
        self.inputs = [a for a, _ in items]
        self.counts = [n for _, n in items]
        self.out_shapes = [jax.ShapeDtypeStruct(a.shape if n is None else (n,) + a.shape[1:], a.dtype) for a, n in items]
        self.scratch = [pltpu.SemaphoreType.DMA((len(items),)), pltpu.SemaphoreType.DMA((len(items),))]

    def _copies(self, ins, outs, scr):
        x, y, c = _my_pos()
        out = []
        for i, (ref, n) in enumerate(zip(ins, self.counts)):
            src = ref if n is None else ref.at[pl.ds((1 - c) * n, n)]
            out.append(pltpu.make_async_remote_copy(
                src_ref=src, dst_ref=outs[i], send_sem=scr[0].at[i], recv_sem=scr[1].at[i],
                device_id=(x, y, 1 - c), device_id_type=MESH_ID))
        return out

    def first(self, ins, outs, scr):
        for cp in self._copies(ins, outs, scr):
            cp.start()

    def last(self, ins, outs, scr):
        for cp in self._copies(ins, outs, scr):
            cp.wait()


class _ChipScatterRider:
    has_middle = False

    def __init__(self, rows, cols):
        self.nr, self.nc = len(rows), len(cols)
        self.inputs = list(rows) + list(cols)
        self.out_shapes = [jax.ShapeDtypeStruct((4, a.shape[0] // 4, a.shape[1]), a.dtype) for a in rows]
        self.out_shapes += [jax.ShapeDtypeStruct((4, 2, a.shape[0] // N_DEV, a.shape[1]), a.dtype) for a in cols]
        n = 4 * self.nr + N_DEV * self.nc
        self.scratch = [pltpu.SemaphoreType.DMA((n,)), pltpu.SemaphoreType.DMA((n,)), pltpu.SemaphoreType.DMA((n,))]

    def _pieces(self, ins, outs):
        x, y, c = _my_pos()
        q = 2 * x + y
        out = []
        for a in range(self.nr):
            n = ins[a].shape[0] // 4
            for j in range(4):
                out.append((4 * a + j, ins[a].at[pl.ds(j * n, n)], (c, j >> 1, j & 1), outs[a].at[q], 4 * a + q))
        for a in range(self.nc):
            ref, base = ins[self.nr + a], 4 * self.nr + N_DEV * a
            n = ref.shape[0] // N_DEV
            for k in range(N_DEV):
                out.append((base + k, ref.at[pl.ds(k * n, n)], (k >> 2, (k >> 1) & 1, k & 1),
                            outs[self.nr + a].at[q, c], base + 2 * q + c))
        return out

    def first(self, ins, outs, scr):
        send, recv, local = scr
        me = _index(_my_pos())
        for s, src, to, slot, r in self._pieces(ins, outs):
            mine = _index(to) == me

            @pl.when(mine)
            def _():
                pltpu.make_async_copy(src, slot, local.at[s]).start()

            @pl.when(jnp.logical_not(mine))
            def _():
                pltpu.make_async_remote_copy(src_ref=src, dst_ref=slot, send_sem=send.at[s], recv_sem=recv.at[r],
                                             device_id=to, device_id_type=MESH_ID).start()

    def last(self, ins, outs, scr):
        send, recv, local = scr
        x, y, c = _my_pos()
        me = _index((x, y, c))
        arrivals = []
        for a in range(self.nr):
            n = ins[a].shape[0] // 4
            for q in range(4):
                arrivals.append((4 * a + q, (q >> 1, q & 1, x), ins[a].at[pl.ds(0, n)], outs[a].at[q], 4 * a + 2 * y + c))
        for a in range(self.nc):
            ref, base = ins[self.nr + a], 4 * self.nr + N_DEV * a
            n = ref.shape[0] // N_DEV
            for k in range(N_DEV):
                arrivals.append((base + k, (k >> 2, (k >> 1) & 1, k & 1), ref.at[pl.ds(0, n)],
                                 outs[self.nr + a].at[k >> 1, k & 1], base + me))
        for r, sender, src, slot, s_local in arrivals:
            mine = _index(sender) == me

            @pl.when(mine)
            def _():
                pltpu.make_async_copy(src, slot, local.at[s_local]).wait()

            @pl.when(jnp.logical_not(mine))
            def _():
                pltpu.make_async_remote_copy(src_ref=src, dst_ref=slot, send_sem=send.at[r], recv_sem=recv.at[r],
                                             device_id=sender, device_id_type=MESH_ID).wait_recv()

        for s, src, to, slot, r in self._pieces(ins, outs):
            @pl.when(_index(to) != me)
            def _():
                pltpu.make_async_remote_copy(src_ref=src, dst_ref=slot, send_sem=send.at[s], recv_sem=recv.at[r],
                                             device_id=to, device_id_type=MESH_ID).wait_send()


class _Riders:
    def __init__(self, riders):
        self.riders = list(riders)
        self.inputs = [a for r in self.riders for a in r.inputs]
        self.out_shapes = [s for r in self.riders for s in r.out_shapes]
        self.scratch = [s for r in self.riders for s in r.scratch]
        self.has_middle = any(r.has_middle for r in self.riders)

    def _each(self, ins, outs, scr):
        i = o = s = 0
        for r in self.riders:
            ni, no, ns = len(r.inputs), len(r.out_shapes), len(r.scratch)
            yield r, ins[i:i + ni], outs[o:o + no], scr[s:s + ns]
            i, o, s = i + ni, o + no, s + ns

    def first(self, ins, outs, scr):
        for r, a, b, c in self._each(ins, outs, scr):
            r.first(a, b, c)

    def middle(self, ins, outs, scr):
        for r, a, b, c in self._each(ins, outs, scr):
            if r.has_middle:
                r.middle(a, b, c)

    def last(self, ins, outs, scr):
        for r, a, b, c in self._each(ins, outs, scr):
            r.last(a, b, c)

    def split(self, outs):
        res, o = [], 0
        for r in self.riders:
            res.append(list(outs[o:o + len(r.out_shapes)]))
            o += len(r.out_shapes)
        return res


def _run(body, *, name, grid, in_specs, out_specs, out_shape, scratch_shapes, args, rider=None, params=None, prefetch=()):
    params = ARB1 if params is None else params
    npf = len(prefetch)
    ni, no, ns = len(in_specs), len(out_shape), len(scratch_shapes)
    ri, ro = (len(rider.inputs), len(rider.out_shapes)) if rider is not None else (0, 0)
    steps = grid[0]

    def wrapped(*refs):
        pf, refs = refs[:npf], refs[npf:]
        cut = [ni, ni + ri, ni + ri + no, ni + ri + no + ro, ni + ri + no + ro + ns]
        a, b, c, d, e, f = (refs[lo:hi] for lo, hi in zip([0] + cut, cut + [len(refs)]))
        i = pl.program_id(0)
        if rider is not None:
            @pl.when(i == 0)
            def _():
                rider.first(b, d, f)

            if rider.has_middle:
                @pl.when(i == steps - 1)
                def _():
                    rider.middle(b, d, f)

        body(*pf, *a, *c, *e)

        if rider is not None:
            @pl.when(i == steps - 1)
            def _():
                rider.last(b, d, f)

    extra_shapes = rider.out_shapes if rider is not None else []
    extra_scratch = rider.scratch if rider is not None else []
    extra_inputs = rider.inputs if rider is not None else []
    all_in, all_out = list(in_specs) + [ANY] * ri, list(out_specs) + [ANY] * ro
    all_scratch = list(scratch_shapes) + extra_scratch
    if npf:
        outs = pl.pallas_call(
            wrapped, name=name, out_shape=list(out_shape) + extra_shapes,
            grid_spec=pltpu.PrefetchScalarGridSpec(num_scalar_prefetch=npf, grid=grid, in_specs=all_in, out_specs=all_out,
                                                   scratch_shapes=all_scratch),
            compiler_params=params)(*prefetch, *args, *extra_inputs)
    else:
        outs = pl.pallas_call(
            wrapped, name=name, grid=grid, in_specs=all_in, out_specs=all_out, out_shape=list(out_shape) + extra_shapes,
            scratch_shapes=all_scratch, compiler_params=params)(*args, *extra_inputs)
    return list(outs[:no]), list(outs[no:])


def _run_rider(rider, name):
    ri, ro = len(rider.inputs), len(rider.out_shapes)

    def body(*refs):
        b, d, f = refs[:ri], refs[ri:ri + ro], refs[ri + ro:]
        rider.first(b, d, f)
        if rider.has_middle:
            rider.middle(b, d, f)
        rider.last(b, d, f)

    return list(pl.pallas_call(body, name=name, in_specs=[ANY] * ri, out_specs=[ANY] * ro, out_shape=rider.out_shapes,
                               scratch_shapes=rider.scratch)(*rider.inputs))


def _mod_fwd(c_all, w_mod):
    ncol = w_mod.shape[-1]

    def body(c_ref, w_ref, act_ref, out_ref):
        c = c_ref[...]
        act = c * jax.nn.sigmoid(c)
        act_ref[...] = act
        out_ref[0] = _dot_nn(act.astype(BF16), w_ref[0].astype(BF16))

    return pl.pallas_call(
        body, name="mod_fwd", grid=(2,),
        out_shape=[jax.ShapeDtypeStruct((N_DEV, D), F32), jax.ShapeDtypeStruct((2, N_DEV, ncol), F32)],
        in_specs=[pl.BlockSpec((N_DEV, D), lambda l: (0, 0)), pl.BlockSpec((1, D, ncol), lambda l: (l, 0, 0))],
        out_specs=[pl.BlockSpec((N_DEV, D), lambda l: (0, 0)), pl.BlockSpec((1, N_DEV, ncol), lambda l: (l, 0, 0))],
        compiler_params=ARB1,
    )(c_all, w_mod)


def _ffn_fwd(x, p, win_all, wout_all, f, rider=None):
    s = x.shape[0]
    nt = s // TM

    def body(x_ref, p_ref, win_hbm, wout_hbm, xo_ref, h_ref, g_ref, u_ref, y_ref, win, wout, act, sems):
        @pl.when(pl.program_id(0) == 0)
        def _():
            cps = _load_rows(win_hbm, None, win, sems, 0) + _load_rows(wout_hbm, None, wout, sems, N_DEV)
            for cp in cps:
                cp.start()
            for cp in cps:
                cp.wait()

        sh, sc, gate, gn = _mod_rows(p_ref)
        x = x_ref[...]
        hb = _modulate(x, gn, sh, sc).astype(BF16)
        h_ref[...] = hb
        for c in range(NCH):
            g = _dot_nt(hb, win[c * HC:(c + 1) * HC, :])
            u = _dot_nt(hb, win[DFF + c * HC:DFF + (c + 1) * HC, :])
            g_ref[c] = g.astype(BF16)
            u_ref[c] = u.astype(BF16)
            act[:, c * HC:(c + 1) * HC] = ((g * jax.nn.sigmoid(g)) * u).astype(BF16)
        y = _dot_nn(act[...], wout[...])
        y_ref[...] = y
        xo_ref[...] = x + (0.5 * gate) * y

    tile = pl.BlockSpec((TM, D), lambda i: (i, 0))
    chunks = pl.BlockSpec((NCH, TM, HC), lambda i: (0, i, 0))
    return _run(
        body, name=f"ffn_fwd_{f}", grid=(nt,),
        out_shape=[jax.ShapeDtypeStruct((s, D), F32), jax.ShapeDtypeStruct((s, D), BF16),
                   jax.ShapeDtypeStruct((NCH, s, HC), BF16), jax.ShapeDtypeStruct((NCH, s, HC), BF16),
                   jax.ShapeDtypeStruct((s, D), F32)],
        in_specs=[tile, pl.BlockSpec((8, D), lambda i: (0, 0)), ANY, ANY],
        out_specs=[tile, tile, chunks, chunks, tile],
        scratch_shapes=[pltpu.VMEM((2 * DFF, D), BF16), pltpu.VMEM((DFF, D), BF16), pltpu.VMEM((TM, DFF), BF16),
                        pltpu.SemaphoreType.DMA((2 * N_DEV,))],
        args=(x, p, win_all, wout_all), rider=rider)


def _ffn_bwd(dxo, x, p, g3, u3, y, win_all, wout_all, f, rider=None):
    s = x.shape[0]
    nt = s // TM

    def body(dxo_ref, x_ref, p_ref, g_ref, u_ref, y_ref, win_hbm, wout_hbm,
             dx_ref, dgu_ref, a_ref, dy_ref, st_ref, win, wout, dgu, stats, sems):
        i = pl.program_id(0)

        @pl.when(i == 0)
        def _():
            cps = _load_rows(win_hbm, None, win, sems, 0) + _load_rows(wout_hbm, None, wout, sems, N_DEV)
            for cp in cps:
                cp.start()
            stats[...] = jnp.zeros_like(stats)
            for cp in cps:
                cp.wait()

        sh, sc, gate, gn = _mod_rows(p_ref)
        x = x_ref[...]
        dxo = dxo_ref[...]
        dyb = ((0.5 * gate) * dxo).astype(BF16)
        dy_ref[...] = dyb
        stats[2] += _colsum8((0.5 * dxo) * y_ref[...])
        for c in range(NCH):
            da = _dot_nt(dyb, wout[c * HC:(c + 1) * HC, :])
            g = g_ref[c].astype(F32)
            u = u_ref[c].astype(F32)
            sg = jax.nn.sigmoid(g)
            si = g * sg
            dg = ((da * u) * (sg * (1.0 + g * (1.0 - sg)))).astype(BF16)
            du = (da * si).astype(BF16)
            a_ref[c] = (si * u).astype(BF16)
            dgu_ref[c] = dg
            dgu_ref[NCH + c] = du
            dgu[:, c * HC:(c + 1) * HC] = dg
            dgu[:, DFF + c * HC:DFF + (c + 1) * HC] = du
        dh = _dot_nn(dgu[...], win[...])
        dx_ref[...] = dxo + _modulate_bwd(dh, x, gn, sc, stats)

        @pl.when(i == nt - 1)
        def _():
            _stats_out(stats, st_ref)

    tile = pl.BlockSpec((TM, D), lambda i: (i, 0))
    chunks = pl.BlockSpec((NCH, TM, HC), lambda i: (0, i, 0))
    small = pl.BlockSpec((8, D), lambda i: (0, 0))
    return _run(
        body, name=f"ffn_bwd_{f}", grid=(nt,),
        out_shape=[jax.ShapeDtypeStruct((s, D), F32), jax.ShapeDtypeStruct((2 * NCH, s, HC), BF16),
                   jax.ShapeDtypeStruct((NCH, s, HC), BF16), jax.ShapeDtypeStruct((s, D), BF16),
                   jax.ShapeDtypeStruct((8, D), F32)],
        in_specs=[tile, tile, small, chunks, chunks, tile, ANY, ANY],
        out_specs=[tile, pl.BlockSpec((2 * NCH, TM, HC), lambda i: (0, i, 0)), chunks, tile, small],
        scratch_shapes=[pltpu.VMEM((2 * DFF, D), BF16), pltpu.VMEM((DFF, D), BF16), pltpu.VMEM((TM, 2 * DFF), BF16),
                        pltpu.VMEM((4, 8, D), F32), pltpu.SemaphoreType.DMA((2 * N_DEV,))],
        args=(dxo, x, p, g3, u3, y, win_all, wout_all), rider=rider)


def _wgrad_pair(own3, sib3, own_r, sib_r, sel, n, col_split, name, rider=None):
    nj, s, _ = own3.shape
    nw = own_r.shape[1] // 2 if col_split else own_r.shape[1]
    steps = nj if col_split else n

    def body(sel_ref, lo_ref, ls_ref, ro_ref, rs_ref, o_ref):
        o_ref[...] = (_dot_tn(lo_ref[0], ro_ref[...]) + _dot_tn(ls_ref[0], rs_ref[...])).astype(BF16)

    rspec = pl.BlockSpec((s, nw), lambda j, sel_ref: (0, sel_ref[1]))
    outs, rode = _run(
        body, name=name, grid=(steps,),
        out_shape=[jax.ShapeDtypeStruct((steps * HC, nw), BF16)],
        in_specs=[pl.BlockSpec((1, s, HC), lambda j, sel_ref: (sel_ref[0] + j, 0, 0)),
                  pl.BlockSpec((1, s, HC), lambda j, sel_ref: (j, 0, 0)), rspec, rspec],
        out_specs=[pl.BlockSpec((HC, nw), lambda j, sel_ref: (j, 0))],
        scratch_shapes=[], args=(own3, sib3, own_r, sib_r), rider=rider, prefetch=(sel,))
    return outs[0], rode


def _gating(proj, nv, ws_ref, bst):
    u, v = proj[:, 0:D_A], proj[:, D_A:2 * D_A]
    gu, gv = _gelu(u), _gelu(v)
    mu = jnp.mean(gv, axis=-1, keepdims=True)
    dv = gv - mu
    rstd = lax.rsqrt(jnp.mean(dv * dv, axis=-1, keepdims=True) + EPS)
    vhat = dv * rstd
    vn = vhat * nv
    r = lax.broadcasted_iota(jnp.int32, (CHUNK, CHUNK), 0)
    c = lax.broadcasted_iota(jnp.int32, (CHUNK, CHUNK), 1)
    wm = [jnp.where(r >= c, ws_ref[hd], 0.0).astype(BF16) for hd in range(A_HEADS)]
    vnb = vn.astype(BF16)
    rows = []
    for n in range(proj.shape[0] // CHUNK):
        blocks = []
        for hd in range(A_HEADS):
            blk = vnb[n * CHUNK:(n + 1) * CHUNK, hd * CHUNK:(hd + 1) * CHUNK]
            blocks.append(_dot_nn(wm[hd], blk) + bst[:, hd:hd + 1])
        rows.append(jnp.concatenate(blocks, axis=1))
    z = jnp.concatenate(rows, axis=0)
    return u, v, gu, rstd, vhat, vnb, wm, z


def _conv(proj, cw, prev_xp):
    bg = proj[:, 2 * D_A:2 * D_A + D_B]
    cg = proj[:, 2 * D_A + D_B:2 * D_A + 2 * D_B]
    xb = proj[:, 2 * D_A + 2 * D_B:]
    xp = cg * xb
    x1 = _shift_down(xp, 1, prev_xp)
    x2 = _shift_down(xp, 2, prev_xp)
    conv = cw[0:1, :] * x2 + cw[1:2, :] * x1 + cw[2:3, :] * xp
    return bg, cg, xb, xp, x1, x2, conv


def _ab_fwd(x, p, abin_all, about_all, nv, ws, bst, cw, rider=None):
    s = x.shape[0]
    nt = s // TM

    def body(x_ref, p_ref, abin_hbm, about_hbm, nv_ref, ws_ref, bst_ref, cw_ref,
             xo_ref, h_ref, proj_ref, out_ref, abin, about, prev, sems):
        @pl.when(pl.program_id(0) == 0)
        def _():
            cps = _load_rows(abin_hbm, None, abin, sems, 0) + _load_rows(about_hbm, None, about, sems, N_DEV)
            for cp in cps:
                cp.start()
            prev[...] = jnp.zeros_like(prev)
            for cp in cps:
                cp.wait()

        sh, sc, gate, gn = _mod_rows(p_ref)
        x = x_ref[...]
        hb = _modulate(x, gn, sh, sc).astype(BF16)
        h_ref[...] = hb
        proj = _dot_nt(hb, abin[...])
        proj_ref[...] = proj
        _, _, gu, _, _, _, _, z = _gating(proj, nv_ref[...], ws_ref, bst_ref[...])
        bg, _, _, xp, _, _, conv = _conv(proj, cw_ref[...], prev[...])
        prev[...] = xp[TM - CONV_HALO:, :]
        cat = jnp.concatenate([gu * z, bg * conv], axis=1).astype(BF16)
        out = _dot_nn(cat, about[...])
        out_ref[...] = out
        xo_ref[...] = x + gate * out

    tile = pl.BlockSpec((TM, D), lambda i: (i, 0))
    full = lambda a: pl.BlockSpec(a.shape, lambda i: (0,) * a.ndim)
    return _run(
        body, name="ab_fwd", grid=(nt,),
        out_shape=[jax.ShapeDtypeStruct((s, D), F32), jax.ShapeDtypeStruct((s, D), BF16),
                   jax.ShapeDtypeStruct((s, D_AB), F32), jax.ShapeDtypeStruct((s, D), F32)],
        in_specs=[tile, pl.BlockSpec((8, D), lambda i: (0, 0)), ANY, ANY, full(nv), full(ws), full(bst), full(cw)],
        out_specs=[tile, tile, pl.BlockSpec((TM, D_AB), lambda i: (i, 0)), tile],
        scratch_shapes=[pltpu.VMEM((D_AB, D), BF16), pltpu.VMEM((D, D), BF16), pltpu.VMEM((CONV_HALO, D_B), F32),
                        pltpu.SemaphoreType.DMA((2 * N_DEV,))],
        args=(x, p, abin_all, about_all, nv, ws, bst, cw), rider=rider)


def _ab_bwd(dxo, x, p, proj, out, abin_all, about_all, nv, ws, bst, cw, rider=None):
    s = x.shape[0]
    nt = s // TM
    npj = D_AB // HC

    def body(dxo_ref, x_ref, p_ref, proj_ref, halo_ref, out_ref, abin_hbm, about_hbm, nv_ref, ws_ref, bst_ref, cw_ref,
             dx_ref, dproj_ref, cat_ref, dy_ref, st_ref, dnv_ref, dws_ref, dbs_ref, dcw_ref,
             abin, about, nxt, stats, dnv, dws, dbs, dcw, sems):
        i = pl.program_id(0)
        ti = nt - 1 - i

        @pl.when(i == 0)
        def _():
            cps = _load_rows(abin_hbm, None, abin, sems, 0) + _load_rows(about_hbm, None, about, sems, N_DEV)
            for cp in cps:
                cp.start()
            for z in (nxt, stats, dnv, dws, dbs, dcw):
                z[...] = jnp.zeros_like(z)
            for cp in cps:
                cp.wait()

        sh, sc, gate, gn = _mod_rows(p_ref)
        x = x_ref[...]
        dxo = dxo_ref[...]
        dyb = (gate * dxo).astype(BF16)
        dy_ref[...] = dyb
        stats[2] += _colsum8(dxo * out_ref[...])
        dcat = _dot_nt(dyb, about[...])
        dya, dyb2 = dcat[:, 0:D_A], dcat[:, D_A:]

        proj = proj_ref[...]
        nvv = nv_ref[...]
        u, v, gu, rstd, vhat, vnb, wm, z = _gating(proj, nvv, ws_ref, bst_ref[...])
        dgu = dya * z
        dzb = (dya * gu).astype(BF16)
        dz32 = dya * gu
        rows = []
        for n in range(TM // CHUNK):
            blocks = []
            for hd in range(A_HEADS):
                sl = (slice(n * CHUNK, (n + 1) * CHUNK), slice(hd * CHUNK, (hd + 1) * CHUNK))
                dbs[hd] += dz32[sl]
                dws[hd] += _dot_nt(dzb[sl], vnb[sl])
                blocks.append(_dot_tn(wm[hd], dzb[sl]))
            rows.append(jnp.concatenate(blocks, axis=1))
        dvn = jnp.concatenate(rows, axis=0)
        dnv[...] += _colsum8(dvn * vhat)
        dvh = dvn * nvv
        dgv = rstd * (dvh - jnp.mean(dvh, axis=-1, keepdims=True) - vhat * jnp.mean(dvh * vhat, axis=-1, keepdims=True))
        du = dgu * _gelu_grad(u)
        dv = dgv * _gelu_grad(v)

        halo = halo_ref[...]
        prev_xp = jnp.where(ti > 0, halo[:, 2 * D_A + D_B:2 * D_A + 2 * D_B] * halo[:, 2 * D_A + 2 * D_B:], 0.0)
        cwv = cw_ref[...]
        bg, cg, xb, xp, x1, x2, conv = _conv(proj, cwv, prev_xp)
        dbg = dyb2 * conv
        dconv = dyb2 * bg
        dcw[...] += jnp.concatenate(
            [jnp.sum(_colsum8(dconv * t), axis=0, keepdims=True) for t in (x2, x1, xp)] + [jnp.zeros((5, D_B), F32)], axis=0)
        nx = nxt[...]
        dxp = cwv[2:3, :] * dconv + cwv[1:2, :] * _shift_up(dconv, 1, nx) + cwv[0:1, :] * _shift_up(dconv, 2, nx)
        nxt[...] = dconv[0:CONV_HALO, :]
        dcg = dxp * xb
        dxb = dxp * cg

        dproj = jnp.concatenate([du, dv, dbg, dcg, dxb], axis=1).astype(BF16)
        for k in range(npj):
            dproj_ref[k] = dproj[:, k * HC:(k + 1) * HC]
        cat = jnp.concatenate([gu * z, bg * conv], axis=1).astype(BF16)
        for k in range(D // HC):
            cat_ref[k] = cat[:, k * HC:(k + 1) * HC]
        dh = _dot_nn(dproj, abin[...])
        dx_ref[...] = dxo + _modulate_bwd(dh, x, gn, sc, stats)

        @pl.when(i == nt - 1)
        def _():
            _stats_out(stats, st_ref)
            dnv_ref[...] = jnp.concatenate([jnp.sum(dnv[...], axis=0, keepdims=True), jnp.zeros((7, D_A), F32)], axis=0)
            r = lax.broadcasted_iota(jnp.int32, (CHUNK, CHUNK), 0)
            c = lax.broadcasted_iota(jnp.int32, (CHUNK, CHUNK), 1)
            for hd in range(A_HEADS):
                dws_ref[hd] = jnp.where(r >= c, dws[hd], 0.0)
                dbs_ref[hd] = jnp.broadcast_to(jnp.sum(dbs[hd], axis=1, keepdims=True), (CHUNK, CHUNK))
            dcw_ref[...] = dcw[...]

    rev = pl.BlockSpec((TM, D), lambda i: (nt - 1 - i, 0))
    small = pl.BlockSpec((8, D), lambda i: (0, 0))
    full = lambda a: pl.BlockSpec(a.shape, lambda i: (0,) * a.ndim)
    hpt = TM // CONV_HALO
    fixed = lambda shape: pl.BlockSpec(shape, lambda i: (0,) * len(shape))
    return _run(
        body, name="ab_bwd", grid=(nt,),
        out_shape=[jax.ShapeDtypeStruct((s, D), F32), jax.ShapeDtypeStruct((npj, s, HC), BF16),
                   jax.ShapeDtypeStruct((D // HC, s, HC), BF16), jax.ShapeDtypeStruct((s, D), BF16),
                   jax.ShapeDtypeStruct((8, D), F32), jax.ShapeDtypeStruct((8, D_A), F32),
                   jax.ShapeDtypeStruct((A_HEADS, CHUNK, CHUNK), F32), jax.ShapeDtypeStruct((A_HEADS, CHUNK, CHUNK), F32),
                   jax.ShapeDtypeStruct((8, D_B), F32)],
        in_specs=[rev, rev, small,
                  pl.BlockSpec((TM, D_AB), lambda i: (nt - 1 - i, 0)),
                  pl.BlockSpec((CONV_HALO, D_AB), lambda i: (jnp.maximum((nt - 1 - i) * hpt - 1, 0), 0)),
                  rev, ANY, ANY, full(nv), full(ws), full(bst), full(cw)],
        out_specs=[rev, pl.BlockSpec((npj, TM, HC), lambda i: (0, nt - 1 - i, 0)),
                   pl.BlockSpec((D // HC, TM, HC), lambda i: (0, nt - 1 - i, 0)), rev,
                   small, fixed((8, D_A)), fixed((A_HEADS, CHUNK, CHUNK)), fixed((A_HEADS, CHUNK, CHUNK)), fixed((8, D_B))],
        scratch_shapes=[pltpu.VMEM((D_AB, D), BF16), pltpu.VMEM((D, D), BF16), pltpu.VMEM((CONV_HALO, D_B), F32),
                        pltpu.VMEM((4, 8, D), F32), pltpu.VMEM((8, D_A), F32),
                        pltpu.VMEM((A_HEADS, CHUNK, CHUNK), F32), pltpu.VMEM((A_HEADS, CHUNK, CHUNK), F32),
                        pltpu.VMEM((8, D_B), F32), pltpu.SemaphoreType.DMA((2 * N_DEV,))],
        args=(dxo, x, p, proj, proj, out, abin_all, about_all, nv, ws, bst, cw), rider=rider)


def _pool_counts(first_token, rows):
    t = (first_token + lax.broadcasted_iota(jnp.int32, (rows, 1), 0) + 1).astype(F32)
    lane = lax.broadcasted_iota(jnp.int32, (1, D), 1)
    w = jnp.where(lane < POOL_G, 2.0, jnp.where(lane < 2 * POOL_G, 4.0, jnp.where(lane < 3 * POOL_G, 8.0, 16.0)))
    return jnp.minimum(t, w)


def _window_sums(ext, n_keep, lead, back):
    n = ext.shape[0]
    sh = (lambda v, k: pltpu.roll(v, k, 0)) if back else (lambda v, k: pltpu.roll(v, n - k, 0))
    s2 = ext + sh(ext, 1)
    s4 = s2[:, POOL_G:] + sh(s2[:, POOL_G:], 2)
    s8 = s4[:, POOL_G:] + sh(s4[:, POOL_G:], 4)
    s16 = s8[:, POOL_G:] + sh(s8[:, POOL_G:], 8)
    keep = slice(lead, lead + n_keep)
    return jnp.concatenate([s2[keep, 0:POOL_G], s4[keep, 0:POOL_G], s8[keep, 0:POOL_G], s16[keep, :]], axis=1)


def _pool_fwd(x, p, pool_all, pscale):
    s = x.shape[0]
    nt = s // TM
    ng = D // POOL_G

    def body(x_ref, p_ref, wg_ref, ps_ref, xo_ref, pb_ref, op_ref, prev):
        i = pl.program_id(0)

        @pl.when(i == 0)
        def _():
            prev[...] = jnp.zeros_like(prev)

        sh, sc, gate, gn = _mod_rows(p_ref)
        x = x_ref[...]
        h = _modulate(x, gn, sh, sc)
        win = _window_sums(jnp.concatenate([prev[...], h], axis=0), TM, POOL_HALO, True)
        prev[...] = h[TM - POOL_HALO:, :]
        pb = (win / _pool_counts(i * TM, TM) - h).astype(BF16)
        pb_ref[...] = pb
        op = jnp.concatenate(
            [_dot_nn(pb[:, g * POOL_G:(g + 1) * POOL_G], wg_ref[:, g].reshape(POOL_G, POOL_G)) for g in range(ng)], axis=1)
        op_ref[...] = op
        xo_ref[...] = x + gate * (op * ps_ref[...])

    tile = pl.BlockSpec((TM, D), lambda i: (i, 0))
    return pl.pallas_call(
        body, name="pool_fwd", grid=(nt,),
        out_shape=[jax.ShapeDtypeStruct((s, D), F32), jax.ShapeDtypeStruct((s, D), BF16), jax.ShapeDtypeStruct((s, D), F32)],
        in_specs=[tile, pl.BlockSpec((8, D), lambda i: (0, 0)),
                  pl.BlockSpec(pool_all.shape, lambda i: (0, 0, 0, 0)), pl.BlockSpec((1, D), lambda i: (0, 0))],
        out_specs=[tile, tile, tile],
        scratch_shapes=[pltpu.VMEM((POOL_HALO, D), F32)],
        compiler_params=ARB1,
    )(x, p, pool_all, pscale)


def _pool_bwd(dxo, x, p, pb, op, pool_all, pscale):
    s = x.shape[0]
    nt = s // TM
    ng = D // POOL_G

    def body(dxo_ref, x_ref, p_ref, pb_ref, op_ref, wg_ref, ps_ref,
             dx_ref, st_ref, dps_ref, dwg_ref, nxt, stats, dps, dwg):
        i = pl.program_id(0)
        ti = nt - 1 - i

        @pl.when(i == 0)
        def _():
            for z in (nxt, stats, dps, dwg):
                z[...] = jnp.zeros_like(z)

        sh, sc, gate, gn = _mod_rows(p_ref)
        x = x_ref[...]
        dxo = dxo_ref[...]
        ps = ps_ref[...]
        op = op_ref[...]
        dmo = gate * dxo
        stats[2] += _colsum8(dxo * (op * ps))
        dps[...] += _colsum8(dmo * op)
        dopb = (dmo * ps).astype(BF16)
        pbv = pb_ref[...]
        dps_parts = []
        for g in range(ng):
            sl = slice(g * POOL_G, (g + 1) * POOL_G)
            dps_parts.append(_dot_nt(dopb[:, sl], wg_ref[:, g].reshape(POOL_G, POOL_G)))
            dwg[g] += _dot_tn(pbv[:, sl], dopb[:, sl])
        dp = jnp.concatenate(dps_parts, axis=1)
        q = dp / _pool_counts(ti * TM, TM)
        wsum = _window_sums(jnp.concatenate([q, nxt[...]], axis=0), TM, 0, False)
        nxt[...] = q[0:POOL_HALO, :]
        dx_ref[...] = dxo + _modulate_bwd(wsum - dp, x, gn, sc, stats)

        @pl.when(i == nt - 1)
        def _():
            _stats_out(stats, st_ref)
            dps_ref[...] = jnp.concatenate([jnp.sum(dps[...], axis=0, keepdims=True), jnp.zeros((7, D), F32)], axis=0)
            dwg_ref[...] = dwg[...].astype(BF16)

    rev = pl.BlockSpec((TM, D), lambda i: (nt - 1 - i, 0))
    small = pl.BlockSpec((8, D), lambda i: (0, 0))
    return pl.pallas_call(
        body, name="pool_bwd", grid=(nt,),
        out_shape=[jax.ShapeDtypeStruct((s, D), F32), jax.ShapeDtypeStruct((8, D), F32), jax.ShapeDtypeStruct((8, D), F32),
                   jax.ShapeDtypeStruct((ng, POOL_G, POOL_G), BF16)],
        in_specs=[rev, rev, small, rev, rev,
                  pl.BlockSpec(pool_all.shape, lambda i: (0, 0, 0, 0)), pl.BlockSpec((1, D), lambda i: (0, 0))],
        out_specs=[rev, small, small, pl.BlockSpec((ng, POOL_G, POOL_G), lambda i: (0, 0, 0))],
        scratch_shapes=[pltpu.VMEM((POOL_HALO, D), F32), pltpu.VMEM((4, 8, D), F32), pltpu.VMEM((8, D), F32),
                        pltpu.VMEM((ng, POOL_G, POOL_G), F32)],
        compiler_params=ARB1,
    )(dxo, x, p, pb, op, pool_all, pscale)


def _head(x, fg, tgt):
    s = x.shape[0]
    nt = s // TM

    def body(x_ref, fg_ref, t_ref, dx_ref, loss_ref, dfg_ref, sq, dfg):
        i = pl.program_id(0)

        @pl.when(i == 0)
        def _():
            sq[...] = jnp.zeros_like(sq)
            dfg[...] = jnp.zeros_like(dfg)

        x = x_ref[...]
        g = fg_ref[...]
        r = lax.rsqrt(jnp.mean(x * x, axis=-1, keepdims=True) + EPS)
        xn = x * r
        e = xn * g - t_ref[...]
        sq[...] += _colsum8(e * e)
        dy = e * (1.0 / D)
        dfg[...] += _colsum8(dy * xn)
        dxn = dy * g
        dx_ref[...] = r * (dxn - xn * jnp.mean(dxn * xn, axis=-1, keepdims=True))

        @pl.when(i == nt - 1)
        def _():
            total = jnp.sum(jnp.sum(sq[...], axis=0, keepdims=True), axis=1, keepdims=True)
            loss_ref[...] = jnp.broadcast_to(total * (0.5 / D), loss_ref.shape)
            dfg_ref[...] = jnp.concatenate([jnp.sum(dfg[...], axis=0, keepdims=True), jnp.zeros((7, D), F32)], axis=0)

    tile = pl.BlockSpec((TM, D), lambda i: (i, 0))
    return pl.pallas_call(
        body, name="head", grid=(nt,),
        out_shape=[jax.ShapeDtypeStruct((s, D), F32), jax.ShapeDtypeStruct((8, 128), F32), jax.ShapeDtypeStruct((8, D), F32)],
        in_specs=[tile, pl.BlockSpec((1, D), lambda i: (0, 0)), tile],
        out_specs=[tile, pl.BlockSpec((8, 128), lambda i: (0, 0)), pl.BlockSpec((8, D), lambda i: (0, 0))],
        scratch_shapes=[pltpu.VMEM((8, D), F32), pltpu.VMEM((8, D), F32)],
        compiler_params=ARB1,
    )(x, fg, tgt)


def _adamw_math(w, g, m, v):
    m = ADAM_B1 * m + (1.0 - ADAM_B1) * g
    v = ADAM_B2 * v + (1.0 - ADAM_B2) * (g * g)
    m_hat = m / (1.0 - ADAM_B1 ** ADAM_STEP)
    v_hat = v / (1.0 - ADAM_B2 ** ADAM_STEP)
    delta = -ADAM_LR * (m_hat / (jnp.sqrt(v_hat) + ADAM_EPS) + ADAM_WD * w)
    return delta, m, v


def _finish(parts, w, m, v, rb, name, halves=False):
    nf, r, c = w.shape
    npart = parts[0].shape[0]

    def body(*refs):
        p_refs = refs[:nf]
        w_ref, m_ref, v_ref, g_ref, d_ref, mo_ref, vo_ref = refs[nf:]
        for f in range(nf):
            @pl.when(pl.program_id(0) == f)
            def _():
                g = p_refs[f][0].astype(F32)
                for k in range(1, npart):
                    g = g + p_refs[f][k].astype(F32)
                if halves:
                    g = jnp.concatenate([g[0], g[1]], axis=1)
                g_ref[0] = g
                d_ref[0], mo_ref[0], vo_ref[0] = _adamw_math(w_ref[0], g, m_ref[0], v_ref[0])

    blk = pl.BlockSpec((1, rb, c), lambda f, i: (f, i, 0))

    def pblk(mine):
        if halves:
            return pl.BlockSpec((npart, 2, rb, c // 2), lambda f, i: (0, 0, jnp.where(f == mine, i, 0), 0))
        return pl.BlockSpec((npart, rb, c), lambda f, i: (0, jnp.where(f == mine, i, 0), 0))

    return pl.pallas_call(
        body, name=name, grid=(nf, r // rb),
        out_shape=[jax.ShapeDtypeStruct(w.shape, F32)] * 4,
        in_specs=[pblk(f) for f in range(nf)] + [blk, blk, blk], out_specs=[blk] * 4,
        compiler_params=pltpu.CompilerParams(dimension_semantics=("arbitrary", "arbitrary"), vmem_limit_bytes=VMEM_LIMIT),
    )(*parts, w, m, v)


def _adamw(w, g, m, v, name):
    def body(w_ref, g_ref, m_ref, v_ref, d_ref, mo_ref, vo_ref):
        d_ref[...], mo_ref[...], vo_ref[...] = _adamw_math(w_ref[...], g_ref[...], m_ref[...], v_ref[...])

    return pl.pallas_call(
        body, name=name, out_shape=[jax.ShapeDtypeStruct(w.shape, F32)] * 3,
        in_specs=[VMEM_SPEC] * 4, out_specs=[VMEM_SPEC] * 3,
    )(w, g, m, v)


def _wmod_finish(act_t, dmod_cols, w, m, v):
    rb = 256
    ncol = w.shape[-1]

    def body(a_ref, dm_ref, w_ref, m_ref, v_ref, g_ref, d_ref, mo_ref, vo_ref):
        g = a_ref[:, 0:1] * dm_ref[0, 0:1, :]
        for k in range(1, N_DEV):
            g = g + a_ref[:, k:k + 1] * dm_ref[0, k:k + 1, :]
        g_ref[0] = g
        d_ref[0], mo_ref[0], vo_ref[0] = _adamw_math(w_ref[0], g, m_ref[0], v_ref[0])

    blk = pl.BlockSpec((1, rb, ncol), lambda l, i: (l, i, 0))
    return pl.pallas_call(
        body, name="wmod_finish", grid=(2, D // rb),
        out_shape=[jax.ShapeDtypeStruct(w.shape, F32)] * 4,
        in_specs=[pl.BlockSpec((rb, N_DEV), lambda l, i: (i, 0)), pl.BlockSpec((1, N_DEV, ncol), lambda l, i: (l, 0, 0)),
                  blk, blk, blk],
        out_specs=[blk] * 4,
        compiler_params=pltpu.CompilerParams(dimension_semantics=("arbitrary", "arbitrary"), vmem_limit_bytes=VMEM_LIMIT),
    )(act_t, dmod_cols, w, m, v)


def _pack(pieces):
    flat, offs, at = [], [], 0
    for a in pieces:
        a = a.reshape(-1)
        n = -(-a.shape[0] // 128) * 128
        flat.append(jnp.pad(a, (0, n - a.shape[0])))
        offs.append(at)
        at += n
    return jnp.concatenate(flat).reshape(-1, 128), offs


def _param_block(mod_l, sub, gn):
    return jnp.concatenate([mod_l[sub], gn[None, :], jnp.zeros((4, D), F32)], axis=0)


def kernel(x, c, norm_g, w_mod, b_mod, w_ffn_in, w_ffn_out, ab_w_in, ab_norm_v, ab_w_s, ab_b_s, ab_conv_w, ab_w_out, pool_w_grp, pool_scale, final_g, loss_target, m_norm_g, m_w_mod, m_b_mod, m_w_ffn_in, m_w_ffn_out, m_ab_w_in, m_ab_norm_v, m_ab_w_s, m_ab_b_s, m_ab_conv_w, m_ab_w_out, m_pool_w_grp, m_pool_scale, m_final_g, v_norm_g, v_w_mod, v_b_mod, v_w_ffn_in, v_w_ffn_out, v_ab_w_in, v_ab_norm_v, v_ab_w_s, v_ab_b_s, v_ab_conv_w, v_ab_w_out, v_pool_w_grp, v_pool_scale, v_final_g):
    me = 4 * lax.axis_index("x") + 2 * lax.axis_index("y") + lax.axis_index("c")
    x0 = x[0]
    tgt = loss_target[0]
    n_in = w_ffn_in.shape[-1]
    n_out = w_ffn_out.shape[-2]
    n_abin = ab_w_in.shape[-1]
    n_about = ab_w_out.shape[-2]
    n_pool = pool_w_grp.shape[-2]
    n_mod = w_mod.shape[-1]
    n_ng = norm_g.shape[-1]
    n_cw = ab_conv_w.shape[-1]
    n_ps = pool_scale.shape[-1]

    pack, offs = _pack([c, norm_g, ab_conv_w, pool_scale])
    got = _exchange_small(pack, "gather_small", False).reshape(N_DEV, -1)
    c_all = got[:, offs[0]:offs[0] + D]
    ng_full = got[:, offs[1]:offs[1] + 6 * n_ng].reshape(N_DEV, 2, 3, n_ng).transpose(1, 2, 0, 3).reshape(2, 3, D)
    cw_full = got[:, offs[2]:offs[2] + 3 * n_cw].reshape(N_DEV, 3, n_cw).transpose(1, 0, 2).reshape(3, D_B)
    ps_full = got[:, offs[3]:offs[3] + n_ps].reshape(1, D)

    act_all, mod_cols = _mod_fwd(c_all, w_mod)
    mod_got = _exchange_small(mod_cols.reshape(-1, 128), "gather_mod", False).reshape(N_DEV, 2, N_DEV, n_mod)
    mod = lax.dynamic_index_in_dim(mod_got, me, axis=2, keepdims=False).transpose(1, 0, 2).reshape(2, 9 * D) + b_mod
    mod = mod.reshape(2, 3, 3, D)

    win_sh = jnp.swapaxes(w_ffn_in.reshape(4, D, n_in), 1, 2).astype(BF16)
    wout_sh = w_ffn_out.reshape(4, n_out, D).astype(BF16)
    abin_sh = ab_w_in[0].T.astype(BF16)
    about_sh = ab_w_out[0].astype(BF16)
    pool_sh = pool_w_grp[0].astype(BF16)
    nv = ab_norm_v
    ws = ab_w_s[0]
    bst = ab_b_s[0].T
    cw8 = jnp.concatenate([cw_full, jnp.zeros((5, D_B), F32)], axis=0)
    win, wout = [None] * 4, [None] * 4

    pb = [[_param_block(mod[l], s, ng_full[l, s]) for s in range(3)] for l in range(2)]
    win[0], wout[0] = _run_rider(_GatherRider([(win_sh, 0), (wout_sh, 0)]), "gather_ffn_0")
    (x1, h00, g00, u00, y00), (abin_all, about_all, win[1]) = _ffn_fwd(
        x0, pb[0][0], win[0], wout[0], 0, _GatherRider([abin_sh, about_sh, (win_sh, 1)]))
    (x2, h01, proj, ab_out), (wout[1],) = _ab_fwd(
        x1, pb[0][1], abin_all, about_all, nv, ws, bst, cw8, _GatherRider([(wout_sh, 1)]))
    (x3, h02, g02, u02, y02), (win[2], wout[2]) = _ffn_fwd(
        x2, pb[0][2], win[1], wout[1], 1, _GatherRider([(win_sh, 2), (wout_sh, 2)]))
    (x4, h10, g10, u10, y10), (pool_all, win[3], wout[3]) = _ffn_fwd(
        x3, pb[1][0], win[2], wout[2], 2, _GatherRider([pool_sh, (win_sh, 3), (wout_sh, 3)]))
    x5, pooled, pool_out = _pool_fwd(x4, pb[1][1], pool_all, ps_full)
    (x6, h12, g12, u12, y12), _ = _ffn_fwd(x5, pb[1][2], win[3], wout[3], 3)
    dx6, loss_blk, dfg = _head(x6, final_g.reshape(1, D), tgt)

    core = lax.axis_index("c")
    sel_rows = lambda n: jnp.stack([core * n, 0]).astype(jnp.int32)
    sel_cols = jnp.stack([0, core]).astype(jnp.int32)
    p_in, p_out = [None] * 4, [None] * 4

    def exchange(dgu, a, h, dy):
        return _SiblingRider([(dgu, NCH), (a, None), (h, None), (dy, None)])

    def ffn_wgrads(f, dgu, a, h, dy, got, ride_out=None):
        s_dgu, s_a, s_h, s_dy = got
        g_out, rode = _wgrad_pair(a, s_a, dy, s_dy, sel_cols, NCH, True, f"wgrad_out_{f}", ride_out)
        g_in, (p_out[f],) = _wgrad_pair(dgu, s_dgu, h, s_h, sel_rows(NCH), NCH, False, f"wgrad_in_{f}",
                                        _ChipScatterRider([], [g_out]))
        return g_in, rode

    (dx5, dgu12, a12, dy12, st12), _ = _ffn_bwd(dx6, x5, pb[1][2], g12, u12, y12, win[3], wout[3], 3)
    dx4, st11, dps, gw_pool = _pool_bwd(dx5, x4, pb[1][1], pooled, pool_out, pool_all, ps_full)
    (dx3, dgu10, a10, dy10, st10), got3 = _ffn_bwd(
        dx4, x3, pb[1][0], g10, u10, y10, win[2], wout[2], 2, exchange(dgu12, a12, h12, dy12))
    g_in3, (p_pool,) = ffn_wgrads(3, dgu12, a12, h12, dy12, got3, _ScatterRider([gw_pool]))
    ride = _Riders([exchange(dgu10, a10, h10, dy10), _ChipScatterRider([g_in3], [])])
    (dx2, dgu02, a02, dy02, st02), rode = _ffn_bwd(dx3, x2, pb[0][2], g02, u02, y02, win[1], wout[1], 1, ride)
    got2, (p_in[3],) = ride.split(rode)
    g_in2, _ = ffn_wgrads(2, dgu10, a10, h10, dy10, got2)
    ride = _Riders([exchange(dgu02, a02, h02, dy02), _ChipScatterRider([g_in2], [])])
    (dx1, dproj, cat, dy01, st01, dnv, dws, dbs, dcw), rode = _ab_bwd(
        dx2, x1, pb[0][1], proj, ab_out, abin_all, about_all, nv, ws, bst, cw8, ride)
    got1, (p_in[2],) = ride.split(rode)
    g_in1, _ = ffn_wgrads(1, dgu02, a02, h02, dy02, got1)
    ride = _Riders([_SiblingRider([(dproj, 5), (cat, 2), (h01, None), (dy01, None)]), _ChipScatterRider([g_in1], [])])
    (dx0, dgu00, a00, dy00, st00), rode = _ffn_bwd(dx1, x0, pb[0][0], g00, u00, y00, win[0], wout[0], 0, ride)
    (s_dproj, s_cat, s_h01, s_dy01), (p_in[1],) = ride.split(rode)
    g_about, _ = _wgrad_pair(cat, s_cat, dy01, s_dy01, sel_rows(2), 2, False, "wgrad_ab_out")
    g_abin, (p_about,) = _wgrad_pair(dproj, s_dproj, h01, s_h01, sel_rows(5), 5, False, "wgrad_ab_in",
                                     _ChipScatterRider([g_about], []))
    stats = [[st00, st01, st02], [st10, st11, st12]]
    dmod = jnp.stack([jnp.concatenate([stats[l][s][0:3].reshape(-1) for s in range(3)]) for l in range(2)])
    dng = jnp.stack([jnp.stack([stats[l][s][3] for s in range(3)]) for l in range(2)])
    spack, so = _pack([dmod, dng, dnv[0], dws, dbs[:, :, 0], dcw[0:3], dps[0], dfg[0], loss_blk[0]])
    ride = _Riders([exchange(dgu00, a00, h00, dy00), _ChipScatterRider([g_abin], [])])
    (sgot, ssum), rode = _exchange_small(spack, "reduce_small", True, ride)
    got0, (p_abin,) = ride.split(rode)
    g_in0, _ = ffn_wgrads(0, dgu00, a00, h00, dy00, got0)
    (p_in[0],) = _run_rider(_ChipScatterRider([g_in0], []), "scatter_in_0")
    grad_x = dx0[None]

    shape_in, shape_out = w_ffn_in.shape, w_ffn_out.shape
    fin = lambda a: jnp.swapaxes(a.reshape(4, D, n_in), 1, 2)
    fabin = lambda a: jnp.swapaxes(a, 1, 2)
    fout = lambda a: a.reshape(4, n_out, D)
    fpool = lambda a: a.reshape(1, 4 * n_pool, POOL_G)
    r_in = _finish(p_in, fin(w_ffn_in), fin(m_w_ffn_in), fin(v_w_ffn_in), n_in // 4, "finish_ffn_in")
    r_out = _finish(p_out, fout(w_ffn_out), fout(m_w_ffn_out), fout(v_w_ffn_out), n_out // 2, "finish_ffn_out", halves=True)
    r_abin = _finish([p_abin], fabin(ab_w_in), fabin(m_ab_w_in), fabin(v_ab_w_in), n_abin, "finish_ab_in")
    r_about = _finish([p_about], ab_w_out, m_ab_w_out, v_ab_w_out, n_about, "finish_ab_out")
    r_pool = _finish([p_pool.reshape(N_DEV, 4 * n_pool, POOL_G)], fpool(pool_w_grp), fpool(m_pool_w_grp), fpool(v_pool_w_grp),
                     4 * n_pool, "finish_pool")
    r_in = [jnp.swapaxes(a, 1, 2).reshape(shape_in) for a in r_in]
    r_abin = [jnp.swapaxes(a, 1, 2) for a in r_abin]
    r_out = [a.reshape(shape_out) for a in r_out]
    r_pool = [a.reshape(pool_w_grp.shape) for a in r_pool]

    ssum = ssum.reshape(-1)
    loss = ssum[so[8]]
    take = lambda i, n: lax.dynamic_slice_in_dim(ssum, so[i], n)
    g_bmod = take(0, 2 * 9 * D).reshape(2, 9 * D)
    g_ng = lax.dynamic_slice_in_dim(take(1, 6 * D).reshape(2, 3, D), me * n_ng, n_ng, axis=2)
    g_nv = take(2, D_A).reshape(1, D_A)
    g_ws = take(3, A_HEADS * CHUNK * CHUNK).reshape(1, A_HEADS, CHUNK, CHUNK)
    g_bs = take(4, A_HEADS * CHUNK).reshape(1, A_HEADS, CHUNK)
    g_cw = lax.dynamic_slice_in_dim(take(5, 3 * D_B).reshape(1, 3, D_B), me * n_cw, n_cw, axis=2)
    g_ps = lax.dynamic_slice_in_dim(take(6, D).reshape(1, D), me * n_ps, n_ps, axis=1)
    g_fg = take(7, D)

    dmod_all = sgot.reshape(N_DEV, -1)[:, so[0]:so[0] + 2 * 9 * D].reshape(N_DEV, 2, 9 * D)
    dmod_cols = lax.dynamic_slice_in_dim(dmod_all, me * n_mod, n_mod, axis=2).transpose(1, 0, 2)
    r_wmod = _wmod_finish(act_all.T, dmod_cols, w_mod, m_w_mod, v_w_mod)

    small_w = [b_mod, norm_g, ab_norm_v, ab_w_s, ab_b_s, ab_conv_w, pool_scale, final_g]
    small_g = [g_bmod, g_ng, g_nv, g_ws, g_bs, g_cw, g_ps, g_fg]
    small_m = [m_b_mod, m_norm_g, m_ab_norm_v, m_ab_w_s, m_ab_b_s, m_ab_conv_w, m_pool_scale, m_final_g]
    small_v = [v_b_mod, v_norm_g, v_ab_norm_v, v_ab_w_s, v_ab_b_s, v_ab_conv_w, v_pool_scale, v_final_g]
    pw, po = _pack(small_w)
    pv = jnp.concatenate([jnp.pad(a.reshape(-1), (0, -a.size % 128), constant_values=1.0) for a in small_v]).reshape(-1, 128)
    sd, sm, sv = _adamw(pw, _pack(small_g)[0], _pack(small_m)[0], pv, "adamw_small")
    unpack = lambda packed: [packed.reshape(-1)[po[i]:po[i] + a.size].reshape(a.shape) for i, a in enumerate(small_w)]
    d_s, m_s, v_s = unpack(sd), unpack(sm), unpack(sv)

    def ordered(k, small):
        return [small[1], r_wmod[k], small[0], r_in[k], r_out[k], r_abin[k], small[2], small[3], small[4], small[5],
                r_about[k], r_pool[k], small[6], small[7]]

    grads = ordered(0, small_g)
    deltas = ordered(1, d_s)
    new_m = ordered(2, m_s)
    new_v = ordered(3, v_s)
    return (loss, grad_x, *grads, *deltas, *new_m, *new_v)
```

```python
import functools
import math

import jax
import jax.numpy as jnp
from jax import lax
from jax.experimental import pallas as pl
from jax.experimental.pallas import tpu as pltpu

F32 = jnp.float32
BF16 = jnp.bfloat16

N_DEV = 8
D = 1024
DFF = 2816
HC = 256
NCH = DFF // HC
D_A = 512
D_B = 512
D_AB = 2 * D_A + 3 * D_B
CHUNK = 128
A_HEADS = 4
POOL_G = 256
POOL_HALO = 16
CONV_HALO = 8
EPS = 1e-6
TM = 256
GELU_K = math.sqrt(2.0 / math.pi)
GELU_C = 0.044715

ADAM_LR = 0.001
ADAM_B1 = 0.9
ADAM_B2 = 0.999
ADAM_EPS = 1e-08
ADAM_WD = 0.01
ADAM_STEP = 10

VMEM_LIMIT = 56 * 1024 * 1024
MESH_ID = pl.DeviceIdType.MESH
ANY = pl.BlockSpec(memory_space=pl.ANY)
VMEM_SPEC = pl.BlockSpec(memory_space=pltpu.VMEM)
ARB1 = pltpu.CompilerParams(dimension_semantics=("arbitrary",), vmem_limit_bytes=VMEM_LIMIT)


def _dot_nt(a, b):
    return lax.dot_general(a, b, (((1,), (1,)), ((), ())), preferred_element_type=F32)


def _dot_nn(a, b):
    return lax.dot_general(a, b, (((1,), (0,)), ((), ())), preferred_element_type=F32)


def _dot_tn(a, b):
    return lax.dot_general(a, b, (((0,), (0,)), ((), ())), preferred_element_type=F32)


def _colsum8(v):
    r, n = v.shape
    return jnp.sum(v.reshape(r // 8, 8, n), axis=0)


def _gelu(x):
    return 0.5 * x * (1.0 + jnp.tanh(GELU_K * (x + GELU_C * x * x * x)))


def _gelu_grad(x):
    t = jnp.tanh(GELU_K * (x + GELU_C * x * x * x))
    return 0.5 * (1.0 + t) + 0.5 * x * (1.0 - t * t) * (GELU_K * (1.0 + 3.0 * GELU_C * x * x))


def _mod_rows(p_ref):
    return p_ref[0:1, :], p_ref[1:2, :], p_ref[2:3, :], p_ref[3:4, :]


def _modulate(x, gn, sh, sc):
    r = lax.rsqrt(jnp.mean(x * x, axis=-1, keepdims=True) + EPS)
    return ((x * r) * gn) * (1.0 + sc) + sh


def _modulate_bwd(dh, x, gn, sc, stats):
    r = lax.rsqrt(jnp.mean(x * x, axis=-1, keepdims=True) + EPS)
    xn = x * r
    stats[0] += _colsum8(dh)
    stats[1] += _colsum8(dh * (xn * gn))
    dy0 = dh * (1.0 + sc)
    stats[3] += _colsum8(dy0 * xn)
    dxn = dy0 * gn
    return r * (dxn - xn * jnp.mean(dxn * xn, axis=-1, keepdims=True))


def _stats_out(stats, out_ref):
    rows = [jnp.sum(stats[k], axis=0, keepdims=True) for k in range(4)]
    out_ref[...] = jnp.concatenate(rows + [jnp.zeros((4, stats.shape[-1]), F32)], axis=0)


def _shift_down(v, k, prev):
    n = v.shape[0]
    row = lax.broadcasted_iota(jnp.int32, v.shape, 0)
    out = pltpu.roll(v, k, 0)
    for j in range(k):
        out = jnp.where(row == j, prev[prev.shape[0] - k + j:prev.shape[0] - k + j + 1, :], out)
    return out


def _shift_up(v, k, nxt):
    n = v.shape[0]
    row = lax.broadcasted_iota(jnp.int32, v.shape, 0)
    out = pltpu.roll(v, n - k, 0)
    for j in range(k):
        out = jnp.where(row == n - k + j, nxt[j:j + 1, :], out)
    return out


def _load_rows(w_hbm, sel, dst, sems, base):
    n = dst.shape[0] // N_DEV
    cps = []
    for k in range(N_DEV):
        src = w_hbm.at[k] if sel is None else w_hbm.at[k, sel]
        cps.append(pltpu.make_async_copy(src, dst.at[pl.ds(k * n, n)], sems.at[base + k]))
    return cps


def _my_pos():
    return lax.axis_index("x"), lax.axis_index("y"), lax.axis_index("c")


def _peer(j):
    x, y, c = _my_pos()
    return (1 - x if j & 4 else x, 1 - y if j & 2 else y, 1 - c if j & 1 else c)


def _index(pos):
    return 4 * pos[0] + 2 * pos[1] + pos[2]


def _exchange_small(v, name, rider):
    rows = v.shape[0]
    ri, ro = len(rider.inputs), len(rider.out_shapes)

    def body(v_ref, *refs):
        r_in, out_ref, r_out, refs = refs[:ri], refs[ri], refs[ri + 1:ri + 1 + ro], refs[ri + 1 + ro:]
        send_sems, recv_sems, local_sem = refs[:3]
        rider.first(r_in, r_out, refs[3:])
        me = _index(_my_pos())

        def copy(j, slot):
            return pltpu.make_async_remote_copy(
                src_ref=v_ref, dst_ref=out_ref.at[slot], send_sem=send_sems.at[j - 1], recv_sem=recv_sems.at[j - 1],
                device_id=_peer(j), device_id_type=MESH_ID)

        mine = pltpu.make_async_copy(v_ref, out_ref.at[me], local_sem)
        mine.start()
        sends = [copy(j, me) for j in range(1, N_DEV)]
        for cp in sends:
            cp.start()
        for j in range(1, N_DEV):
            copy(j, _index(_peer(j))).wait_recv()
        for cp in sends:
            cp.wait_send()
        mine.wait()
        if rider.has_middle:
            rider.middle(r_in, r_out, refs[3:])
        rider.last(r_in, r_out, refs[3:])

    scratch = [pltpu.SemaphoreType.DMA((N_DEV - 1,)), pltpu.SemaphoreType.DMA((N_DEV - 1,)), pltpu.SemaphoreType.DMA(())]
    res = pl.pallas_call(
        body, name=name, out_shape=[jax.ShapeDtypeStruct((N_DEV, rows, 128), F32)] + rider.out_shapes,
        in_specs=[VMEM_SPEC] + [ANY] * ri, out_specs=[VMEM_SPEC] + [ANY] * ro,
        scratch_shapes=scratch + rider.scratch)(v, *rider.inputs)
    return res[0], list(res[1:])


class _SmallGatherRider:
    has_middle = False

    def __init__(self, v):
        self.inputs = [v]
        self.out_shapes = [jax.ShapeDtypeStruct((N_DEV,) + v.shape, v.dtype)]
        self.scratch = [pltpu.SemaphoreType.DMA((N_DEV - 1,)), pltpu.SemaphoreType.DMA((N_DEV - 1,)), pltpu.SemaphoreType.DMA(())]

    def _copy(self, ins, outs, scr, j, slot):
        return pltpu.make_async_remote_copy(
            src_ref=ins[0], dst_ref=outs[0].at[slot], send_sem=scr[0].at[j - 1], recv_sem=scr[1].at[j - 1],
            device_id=_peer(j), device_id_type=MESH_ID)

    def first(self, ins, outs, scr):
        me = _index(_my_pos())
        pltpu.make_async_copy(ins[0], outs[0].at[me], scr[2]).start()
        for j in range(1, N_DEV):
            self._copy(ins, outs, scr, j, me).start()

    def last(self, ins, outs, scr):
        me = _index(_my_pos())
        for j in range(1, N_DEV):
            self._copy(ins, outs, scr, j, _index(_peer(j))).wait_recv()
        for j in range(1, N_DEV):
            self._copy(ins, outs, scr, j, me).wait_send()
        pltpu.make_async_copy(ins[0], outs[0].at[me], scr[2]).wait()


class _GatherRider:
    has_middle = True

    def __init__(self, shards):
        n = len(shards)
        pairs = [s if isinstance(s, tuple) else (s, None) for s in shards]
        self.inputs = [a for a, _ in pairs]
        self.picks = [i for _, i in pairs]
        shapes = [a.shape if i is None else a.shape[1:] for a, i in pairs]
        self.out_shapes = [jax.ShapeDtypeStruct((N_DEV,) + s, a.dtype) for s, (a, _) in zip(shapes, pairs)]
        self.scratch = [pltpu.SemaphoreType.DMA((7 * n,)), pltpu.SemaphoreType.DMA((7 * n,)), pltpu.SemaphoreType.DMA((n,))]

    def _src(self, ins):
        return [r if i is None else r.at[i] for r, i in zip(ins, self.picks)]

    def _ctx(self, outs, scr):
        send, recv, _ = scr
        x, y, c = _my_pos()
        chips = [(1 - x, y), (x, 1 - y), (1 - x, 1 - y)]

        def copy(a, k, block, to, src=None):
            slot = outs[a].at[_index(block)]
            return pltpu.make_async_remote_copy(
                src_ref=slot if src is None else src, dst_ref=slot, send_sem=send.at[7 * a + k], recv_sem=recv.at[7 * a + k],
                device_id=to, device_id_type=MESH_ID)

        return (x, y, c), (x, y, 1 - c), chips, copy

    def _sends(self, ins, outs, scr):
        me, sib, chips, copy = self._ctx(outs, scr)
        out = []
        for a, src in enumerate(self._src(ins)):
            out.append(copy(a, 0, me, sib, src=src))
            out += [copy(a, 1 + j, me, (*chips[j], me[2]), src=src) for j in range(3)]
        return out

    def first(self, ins, outs, scr):
        me = _index(_my_pos())
        for a, src in enumerate(self._src(ins)):
            pltpu.make_async_copy(src, outs[a].at[me], scr[2].at[a]).start()
        for cp in self._sends(ins, outs, scr):
            cp.start()

    def middle(self, ins, outs, scr):
        me, sib, chips, copy = self._ctx(outs, scr)
        for j in range(3):
            for a in range(len(ins)):
                copy(a, 1 + j, (*chips[j], me[2]), me).wait_recv()
                copy(a, 4 + j, (*chips[j], me[2]), sib).start()

    def last(self, ins, outs, scr):
        me, sib, chips, copy = self._ctx(outs, scr)
        for a in range(len(ins)):
            copy(a, 0, sib, me).wait_recv()
            for j in range(3):
                copy(a, 4 + j, (*chips[j], sib[2]), me).wait_recv()
        for cp in self._sends(ins, outs, scr):
            cp.wait_send()
        for a, src in enumerate(self._src(ins)):
            for j in range(3):
                copy(a, 4 + j, (*chips[j], me[2]), sib).wait_send()
            pltpu.make_async_copy(src, outs[a].at[_index(me)], scr[2].at[a]).wait()


class _ScatterRider:
    has_middle = False

    def __init__(self, grads):
        n = len(grads)
        self.inputs = list(grads)
        self.out_shapes = []
        for g in grads:
            if g.ndim == 3:
                self.out_shapes.append(jax.ShapeDtypeStruct((N_DEV, g.shape[0], g.shape[1] // N_DEV, g.shape[2]), g.dtype))
            else:
                self.out_shapes.append(jax.ShapeDtypeStruct((N_DEV, g.shape[0] // N_DEV, g.shape[1]), g.dtype))
        self.scratch = [pltpu.SemaphoreType.DMA((7 * n,)), pltpu.SemaphoreType.DMA((7 * n,)), pltpu.SemaphoreType.DMA((n,))]

    @staticmethod
    def _part(ref, k):
        if ref.ndim == 3:
            n = ref.shape[1] // N_DEV
            return ref.at[:, pl.ds(pl.multiple_of(k * n, 16), n)]
        n = ref.shape[0] // N_DEV
        return ref.at[pl.ds(pl.multiple_of(k * n, 16), n)]

    def _copy(self, ins, outs, scr, g, j, to, src_dev):
        return pltpu.make_async_remote_copy(
            src_ref=self._part(ins[g], to), dst_ref=outs[g].at[src_dev], send_sem=scr[0].at[7 * g + j - 1],
            recv_sem=scr[1].at[7 * g + j - 1], device_id=_peer(j), device_id_type=MESH_ID)

    def first(self, ins, outs, scr):
        me = _index(_my_pos())
        for g in range(len(ins)):
            pltpu.make_async_copy(self._part(ins[g], me), outs[g].at[me], scr[2].at[g]).start()
        for j in range(1, N_DEV):
            for g in range(len(ins)):
                self._copy(ins, outs, scr, g, j, _index(_peer(j)), me).start()

    def last(self, ins, outs, scr):
        me = _index(_my_pos())
        for j in range(1, N_DEV):
            for g in range(len(ins)):
                self._copy(ins, outs, scr, g, j, me, _index(_peer(j))).wait_recv()
        for j in range(1, N_DEV):
            for g in range(len(ins)):
                self._copy(ins, outs, scr, g, j, _index(_peer(j)), me).wait_send()
        for g in range(len(ins)):
            pltpu.make_async_copy(self._part(ins[g], me), outs[g].at[me], scr[2].at[g]).wait()


class _SiblingRider:
    has_middle = False

    def __init__(self, items):
        self.inputs = [a for a, _ in items]
        self.counts = [n for _, n in items]
        self.out_shapes = [jax.ShapeDtypeStruct(a.shape if n is None else (n,) + a.shape[1:], a.dtype) for a, n in items]
        self.scratch = [pltpu.SemaphoreType.DMA((len(items),)), pltpu.SemaphoreType.DMA((len(items),))]

    def _copies(self, ins, outs, scr):
        x, y, c = _my_pos()
        out = []
        for i, (ref, n) in enumerate(zip(ins, self.counts)):
            src = ref if n is None else ref.at[pl.ds((1 - c) * n, n)]
            out.append(pltpu.make_async_remote_copy(
                src_ref=src, dst_ref=outs[i], send_sem=scr[0].at[i], recv_sem=scr[1].at[i],
                device_id=(x, y, 1 - c), device_id_type=MESH_ID))
        return out

    def first(self, ins, outs, scr):
        for cp in self._copies(ins, outs, scr):
            cp.start()

    def last(self, ins, outs, scr):
        for cp in self._copies(ins, outs, scr):
            cp.wait()


class _ChipScatterRider:
    has_middle = False

    def __init__(self, rows, cols):
        self.nr, self.nc = len(rows), len(cols)
        self.inputs = list(rows) + list(cols)
        self.out_shapes = [jax.ShapeDtypeStruct((4, a.shape[0] // 4, a.shape[1]), a.dtype) for a in rows]
        self.out_shapes += [jax.ShapeDtypeStruct((4, 2, a.shape[0] // N_DEV, a.shape[1]), a.dtype) for a in cols]
        n = 4 * self.nr + N_DEV * self.nc
        self.scratch = [pltpu.SemaphoreType.DMA((n,)), pltpu.SemaphoreType.DMA((n,)), pltpu.SemaphoreType.DMA((n,))]

    def _pieces(self, ins, outs):
        x, y, c = _my_pos()
        q = 2 * x + y
        out = []
        for a in range(self.nr):
            n = ins[a].shape[0] // 4
            for j in range(4):
                out.append((4 * a + j, ins[a].at[pl.ds(j * n, n)], (c, j >> 1, j & 1), outs[a].at[q], 4 * a + q))
        for a in range(self.nc):
            ref, base = ins[self.nr + a], 4 * self.nr + N_DEV * a
            n = ref.shape[0] // N_DEV
            for k in range(N_DEV):
                out.append((base + k, ref.at[pl.ds(k * n, n)], (k >> 2, (k >> 1) & 1, k & 1),
                            outs[self.nr + a].at[q, c], base + 2 * q + c))
        return out

    def first(self, ins, outs, scr):
        send, recv, local = scr
        me = _index(_my_pos())
        for s, src, to, slot, r in self._pieces(ins, outs):
            mine = _index(to) == me

            @pl.when(mine)
            def _():
                pltpu.make_async_copy(src, slot, local.at[s]).start()

            @pl.when(jnp.logical_not(mine))
            def _():
                pltpu.make_async_remote_copy(src_ref=src, dst_ref=slot, send_sem=send.at[s], recv_sem=recv.at[r],
                                             device_id=to, device_id_type=MESH_ID).start()

    def last(self, ins, outs, scr):
        send, recv, local = scr
        x, y, c = _my_pos()
        me = _index((x, y, c))
        arrivals = []
        for a in range(self.nr):
            n = ins[a].shape[0] // 4
            for q in range(4):
                arrivals.append((4 * a + q, (q >> 1, q & 1, x), ins[a].at[pl.ds(0, n)], outs[a].at[q], 4 * a + 2 * y + c))
        for a in range(self.nc):
            ref, base = ins[self.nr + a], 4 * self.nr + N_DEV * a
            n = ref.shape[0] // N_DEV
            for k in range(N_DEV):
                arrivals.append((base + k, (k >> 2, (k >> 1) & 1, k & 1), ref.at[pl.ds(0, n)],
                                 outs[self.nr + a].at[k >> 1, k & 1], base + me))
        for r, sender, src, slot, s_local in arrivals:
            mine = _index(sender) == me

            @pl.when(mine)
            def _():
                pltpu.make_async_copy(src, slot, local.at[s_local]).wait()

            @pl.when(jnp.logical_not(mine))
            def _():
                pltpu.make_async_remote_copy(src_ref=src, dst_ref=slot, send_sem=send.at[r], recv_sem=recv.at[r],
                                             device_id=sender, device_id_type=MESH_ID).wait_recv()

        for s, src, to, slot, r in self._pieces(ins, outs):
            @pl.when(_index(to) != me)
            def _():
                pltpu.make_async_remote_copy(src_ref=src, dst_ref=slot, send_sem=send.at[s], recv_sem=recv.at[r],
                                             device_id=to, device_id_type=MESH_ID).wait_send()


class _Riders:
    def __init__(self, riders):
        self.riders = list(riders)
        self.inputs = [a for r in self.riders for a in r.inputs]
        self.out_shapes = [s for r in self.riders for s in r.out_shapes]
        self.scratch = [s for r in self.riders for s in r.scratch]
        self.has_middle = any(r.has_middle for r in self.riders)

    def _each(self, ins, outs, scr):
        i = o = s = 0
        for r in self.riders:
            ni, no, ns = len(r.inputs), len(r.out_shapes), len(r.scratch)
            yield r, ins[i:i + ni], outs[o:o + no], scr[s:s + ns]
            i, o, s = i + ni, o + no, s + ns

    def first(self, ins, outs, scr):
        for r, a, b, c in self._each(ins, outs, scr):
            r.first(a, b, c)

    def middle(self, ins, outs, scr):
        for r, a, b, c in self._each(ins, outs, scr):
            if r.has_middle:
                r.middle(a, b, c)

    def last(self, ins, outs, scr):
        for r, a, b, c in self._each(ins, outs, scr):
            r.last(a, b, c)

    def split(self, outs):
        res, o = [], 0
        for r in self.riders:
            res.append(list(outs[o:o + len(r.out_shapes)]))
            o += len(r.out_shapes)
        return res


def _run(body, *, name, grid, in_specs, out_specs, out_shape, scratch_shapes, args, rider=None, params=None, prefetch=()):
    params = ARB1 if params is None else params
    npf = len(prefetch)
    ni, no, ns = len(in_specs), len(out_shape), len(scratch_shapes)
    ri, ro = (len(rider.inputs), len(rider.out_shapes)) if rider is not None else (0, 0)

    def wrapped(*refs):
        pf, refs = refs[:npf], refs[npf:]
        cut = [ni, ni + ri, ni + ri + no, ni + ri + no + ro, ni + ri + no + ro + ns]
        a, b, c, d, e, f = (refs[lo:hi] for lo, hi in zip([0] + cut, cut + [len(refs)]))
        if rider is not None:
            ids = [pl.program_id(k) for k in range(len(grid))]
            at_first = functools.reduce(jnp.logical_and, [i == 0 for i in ids])
            at_last = functools.reduce(jnp.logical_and, [i == n - 1 for i, n in zip(ids, grid)])

            @pl.when(at_first)
            def _():
                rider.first(b, d, f)

            if rider.has_middle:
                @pl.when(at_last)
                def _():
                    rider.middle(b, d, f)

        body(*pf, *a, *c, *e)

        if rider is not None:
            @pl.when(at_last)
            def _():
                rider.last(b, d, f)

    extra_shapes = rider.out_shapes if rider is not None else []
    extra_scratch = rider.scratch if rider is not None else []
    extra_inputs = rider.inputs if rider is not None else []
    all_in, all_out = list(in_specs) + [ANY] * ri, list(out_specs) + [ANY] * ro
    all_scratch = list(scratch_shapes) + extra_scratch
    if npf:
        outs = pl.pallas_call(
            wrapped, name=name, out_shape=list(out_shape) + extra_shapes,
            grid_spec=pltpu.PrefetchScalarGridSpec(num_scalar_prefetch=npf, grid=grid, in_specs=all_in, out_specs=all_out,
                                                   scratch_shapes=all_scratch),
            compiler_params=params)(*prefetch, *args, *extra_inputs)
    else:
        outs = pl.pallas_call(
            wrapped, name=name, grid=grid, in_specs=all_in, out_specs=all_out, out_shape=list(out_shape) + extra_shapes,
            scratch_shapes=all_scratch, compiler_params=params)(*args, *extra_inputs)
    return list(outs[:no]), list(outs[no:])


def _mod_fwd(c_all, w_mod):
    ncol = w_mod.shape[-1]

    def body(c_ref, w_ref, act_ref, out_ref):
        c = c_ref[...]
        act = c * jax.nn.sigmoid(c)
        act_ref[...] = act
        out_ref[0] = _dot_nn(act.astype(BF16), w_ref[0].astype(BF16))

    return pl.pallas_call(
        body, name="mod_fwd", grid=(2,),
        out_shape=[jax.ShapeDtypeStruct((N_DEV, D), F32), jax.ShapeDtypeStruct((2, N_DEV, ncol), F32)],
        in_specs=[pl.BlockSpec((N_DEV, D), lambda l: (0, 0)), pl.BlockSpec((1, D, ncol), lambda l: (l, 0, 0))],
        out_specs=[pl.BlockSpec((N_DEV, D), lambda l: (0, 0)), pl.BlockSpec((1, N_DEV, ncol), lambda l: (l, 0, 0))],
        compiler_params=ARB1,
    )(c_all, w_mod)


def _ffn_fwd(x, p, win_all, wout_all, f, rider=None):
    s = x.shape[0]
    nt = s // TM

    def body(x_ref, p_ref, win_hbm, wout_hbm, xo_ref, h_ref, g_ref, u_ref, y_ref, win, wout, act, sems):
        @pl.when(pl.program_id(0) == 0)
        def _():
            cps = _load_rows(win_hbm, None, win, sems, 0) + _load_rows(wout_hbm, None, wout, sems, N_DEV)
            for cp in cps:
                cp.start()
            for cp in cps:
                cp.wait()

        sh, sc, gate, gn = _mod_rows(p_ref)
        x = x_ref[...]
        hb = _modulate(x, gn, sh, sc).astype(BF16)
        h_ref[...] = hb
        for c in range(NCH):
            g = _dot_nt(hb, win[c * HC:(c + 1) * HC, :])
            u = _dot_nt(hb, win[DFF + c * HC:DFF + (c + 1) * HC, :])
            g_ref[c] = g.astype(BF16)
            u_ref[c] = u.astype(BF16)
            act[:, c * HC:(c + 1) * HC] = ((g * jax.nn.sigmoid(g)) * u).astype(BF16)
        y = _dot_nn(act[...], wout[...])
        y_ref[...] = y
        xo_ref[...] = x + (0.5 * gate) * y

    tile = pl.BlockSpec((TM, D), lambda i: (i, 0))
    chunks = pl.BlockSpec((NCH, TM, HC), lambda i: (0, i, 0))
    return _run(
        body, name=f"ffn_fwd_{f}", grid=(nt,),
        out_shape=[jax.ShapeDtypeStruct((s, D), F32), jax.ShapeDtypeStruct((s, D), BF16),
                   jax.ShapeDtypeStruct((NCH, s, HC), BF16), jax.ShapeDtypeStruct((NCH, s, HC), BF16),
                   jax.ShapeDtypeStruct((s, D), F32)],
        in_specs=[tile, pl.BlockSpec((8, D), lambda i: (0, 0)), ANY, ANY],
        out_specs=[tile, tile, chunks, chunks, tile],
        scratch_shapes=[pltpu.VMEM((2 * DFF, D), BF16), pltpu.VMEM((DFF, D), BF16), pltpu.VMEM((TM, DFF), BF16),
                        pltpu.SemaphoreType.DMA((2 * N_DEV,))],
        args=(x, p, win_all, wout_all), rider=rider)


def _swiglu_bwd_acts(dyb, g_ref, u_ref, wout, a_ref, dgu_ref, dgu):
    for c in range(NCH):
        da = _dot_nt(dyb, wout[c * HC:(c + 1) * HC, :])
        g = g_ref[c].astype(F32)
        u = u_ref[c].astype(F32)
        sg = jax.nn.sigmoid(g)
        si = g * sg
        dg = ((da * u) * (sg * (1.0 + g * (1.0 - sg)))).astype(BF16)
        du = (da * si).astype(BF16)
        a_ref[c] = (si * u).astype(BF16)
        dgu_ref[c] = dg
        dgu_ref[NCH + c] = du
        if dgu is not None:
            dgu[:, c * HC:(c + 1) * HC] = dg
            dgu[:, DFF + c * HC:DFF + (c + 1) * HC] = du


def _ffn_bwd(dxo, x, p, g3, u3, y, win_all, wout_all, f, rider=None):
    s = x.shape[0]
    nt = s // TM

    def body(dxo_ref, x_ref, p_ref, g_ref, u_ref, y_ref, win_hbm, wout_hbm,
             dx_ref, dgu_ref, a_ref, dy_ref, st_ref, win, wout, dgu, stats, sems):
        i = pl.program_id(0)

        @pl.when(i == 0)
        def _():
            cps = _load_rows(win_hbm, None, win, sems, 0) + _load_rows(wout_hbm, None, wout, sems, N_DEV)
            for cp in cps:
                cp.start()
            stats[...] = jnp.zeros_like(stats)
            for cp in cps:
                cp.wait()

        sh, sc, gate, gn = _mod_rows(p_ref)
        x = x_ref[...]
        dxo = dxo_ref[...]
        dyb = ((0.5 * gate) * dxo).astype(BF16)
        dy_ref[...] = dyb
        stats[2] += _colsum8((0.5 * dxo) * y_ref[...])
        _swiglu_bwd_acts(dyb, g_ref, u_ref, wout, a_ref, dgu_ref, dgu)
        dh = _dot_nn(dgu[...], win[...])
        dx_ref[...] = dxo + _modulate_bwd(dh, x, gn, sc, stats)

        @pl.when(i == nt - 1)
        def _():
            _stats_out(stats, st_ref)

    tile = pl.BlockSpec((TM, D), lambda i: (i, 0))
    chunks = pl.BlockSpec((NCH, TM, HC), lambda i: (0, i, 0))
    small = pl.BlockSpec((8, D), lambda i: (0, 0))
    return _run(
        body, name=f"ffn_bwd_{f}", grid=(nt,),
        out_shape=[jax.ShapeDtypeStruct((s, D), F32), jax.ShapeDtypeStruct((2 * NCH, s, HC), BF16),
                   jax.ShapeDtypeStruct((NCH, s, HC), BF16), jax.ShapeDtypeStruct((s, D), BF16),
                   jax.ShapeDtypeStruct((8, D), F32)],
        in_specs=[tile, tile, small, chunks, chunks, tile, ANY, ANY],
        out_specs=[tile, pl.BlockSpec((2 * NCH, TM, HC), lambda i: (0, i, 0)), chunks, tile, small],
        scratch_shapes=[pltpu.VMEM((2 * DFF, D), BF16), pltpu.VMEM((DFF, D), BF16), pltpu.VMEM((TM, 2 * DFF), BF16),
                        pltpu.VMEM((4, 8, D), F32), pltpu.SemaphoreType.DMA((2 * N_DEV,))],
        args=(dxo, x, p, g3, u3, y, win_all, wout_all), rider=rider)


def _ffn_bwd_acts(dxo, p, g3, u3, y, wout_all, f, rider=None):
    s = dxo.shape[0]
    nt = s // TM

    def body(dxo_ref, p_ref, g_ref, u_ref, y_ref, wout_hbm, dgu_ref, a_ref, dy_ref, st_ref, wout, stat, sems):
        i = pl.program_id(0)

        @pl.when(i == 0)
        def _():
            cps = _load_rows(wout_hbm, None, wout, sems, 0)
            for cp in cps:
                cp.start()
            stat[...] = jnp.zeros_like(stat)
            for cp in cps:
                cp.wait()

        gate = p_ref[2:3, :]
        dxo = dxo_ref[...]
        dyb = ((0.5 * gate) * dxo).astype(BF16)
        dy_ref[...] = dyb
        stat[...] += _colsum8((0.5 * dxo) * y_ref[...])
        _swiglu_bwd_acts(dyb, g_ref, u_ref, wout, a_ref, dgu_ref, None)

        @pl.when(i == nt - 1)
        def _():
            row = jnp.sum(stat[...], axis=0, keepdims=True)
            st_ref[...] = jnp.concatenate([jnp.zeros((2, D), F32), row, jnp.zeros((5, D), F32)], axis=0)

    tile = pl.BlockSpec((TM, D), lambda i: (i, 0))
    chunks = pl.BlockSpec((NCH, TM, HC), lambda i: (0, i, 0))
    small = pl.BlockSpec((8, D), lambda i: (0, 0))
    return _run(
        body, name=f"ffn_bwd_acts_{f}", grid=(nt,),
        out_shape=[jax.ShapeDtypeStruct((2 * NCH, s, HC), BF16), jax.ShapeDtypeStruct((NCH, s, HC), BF16),
                   jax.ShapeDtypeStruct((s, D), BF16), jax.ShapeDtypeStruct((8, D), F32)],
        in_specs=[tile, small, chunks, chunks, tile, ANY],
        out_specs=[pl.BlockSpec((2 * NCH, TM, HC), lambda i: (0, i, 0)), chunks, tile, small],
        scratch_shapes=[pltpu.VMEM((DFF, D), BF16), pltpu.VMEM((8, D), F32), pltpu.SemaphoreType.DMA((N_DEV,))],
        args=(dxo, p, g3, u3, y, wout_all), rider=rider)


def _ffn_bwd_dx(dxo, x, p, dgu3, gate_stats, win_all, f, rider=None):
    s = x.shape[0]
    nt = s // TM

    def body(dxo_ref, x_ref, p_ref, dgu_ref, gs_ref, win_hbm, dx_ref, st_ref, win, dgu, stats, sems):
        i = pl.program_id(0)

        @pl.when(i == 0)
        def _():
            cps = _load_rows(win_hbm, None, win, sems, 0)
            for cp in cps:
                cp.start()
            stats[...] = jnp.zeros_like(stats)
            for cp in cps:
                cp.wait()

        _, sc, _, gn = _mod_rows(p_ref)
        for c in range(2 * NCH):
            dgu[:, c * HC:(c + 1) * HC] = dgu_ref[c]
        dh = _dot_nn(dgu[...], win[...])
        dx_ref[...] = dxo_ref[...] + _modulate_bwd(dh, x_ref[...], gn, sc, stats)

        @pl.when(i == nt - 1)
        def _():
            _stats_out(stats, st_ref)
            st_ref[2:3, :] = gs_ref[2:3, :]

    tile = pl.BlockSpec((TM, D), lambda i: (i, 0))
    small = pl.BlockSpec((8, D), lambda i: (0, 0))
    return _run(
        body, name=f"ffn_bwd_dx_{f}", grid=(nt,),
        out_shape=[jax.ShapeDtypeStruct((s, D), F32), jax.ShapeDtypeStruct((8, D), F32)],
        in_specs=[tile, tile, small, pl.BlockSpec((2 * NCH, TM, HC), lambda i: (0, i, 0)), small, ANY],
        out_specs=[tile, small],
        scratch_shapes=[pltpu.VMEM((2 * DFF, D), BF16), pltpu.VMEM((TM, 2 * DFF), BF16), pltpu.VMEM((4, 8, D), F32),
                        pltpu.SemaphoreType.DMA((N_DEV,))],
        args=(dxo, x, p, dgu3, gate_stats, win_all), rider=rider)


def _wgrad_pair(own3, sib3, own_r, sib_r, sel, n, col_split, name, rider=None):
    nj, s, _ = own3.shape
    nw = own_r.shape[1] // 2 if col_split else own_r.shape[1]
    steps = nj if col_split else n

    def body(sel_ref, lo_ref, ls_ref, ro_ref, rs_ref, o_ref):
        o_ref[...] = (_dot_tn(lo_ref[0], ro_ref[...]) + _dot_tn(ls_ref[0], rs_ref[...])).astype(BF16)

    rspec = pl.BlockSpec((s, nw), lambda j, sel_ref: (0, sel_ref[1]))
    outs, rode = _run(
        body, name=name, grid=(steps,),
        out_shape=[jax.ShapeDtypeStruct((steps * HC, nw), BF16)],
        in_specs=[pl.BlockSpec((1, s, HC), lambda j, sel_ref: (sel_ref[0] + j, 0, 0)),
                  pl.BlockSpec((1, s, HC), lambda j, sel_ref: (j, 0, 0)), rspec, rspec],
        out_specs=[pl.BlockSpec((HC, nw), lambda j, sel_ref: (j, 0))],
        scratch_shapes=[], args=(own3, sib3, own_r, sib_r), rider=rider, prefetch=(sel,))
    return outs[0], rode


def _gating(proj, nv, ws_ref, bst):
    u, v = proj[:, 0:D_A], proj[:, D_A:2 * D_A]
    gu, gv = _gelu(u), _gelu(v)
    mu = jnp.mean(gv, axis=-1, keepdims=True)
    dv = gv - mu
    rstd = lax.rsqrt(jnp.mean(dv * dv, axis=-1, keepdims=True) + EPS)
    vhat = dv * rstd
    vn = vhat * nv
    r = lax.broadcasted_iota(jnp.int32, (CHUNK, CHUNK), 0)
    c = lax.broadcasted_iota(jnp.int32, (CHUNK, CHUNK), 1)
    wm = [jnp.where(r >= c, ws_ref[hd], 0.0).astype(BF16) for hd in range(A_HEADS)]
    vnb = vn.astype(BF16)
    rows = []
    for n in range(proj.shape[0] // CHUNK):
        blocks = []
        for hd in range(A_HEADS):
            blk = vnb[n * CHUNK:(n + 1) * CHUNK, hd * CHUNK:(hd + 1) * CHUNK]
            blocks.append(_dot_nn(wm[hd], blk) + bst[:, hd:hd + 1])
        rows.append(jnp.concatenate(blocks, axis=1))
    z = jnp.concatenate(rows, axis=0)
    return u, v, gu, rstd, vhat, vnb, wm, z


def _conv(proj, cw, prev_xp):
    bg = proj[:, 2 * D_A:2 * D_A + D_B]
    cg = proj[:, 2 * D_A + D_B:2 * D_A + 2 * D_B]
    xb = proj[:, 2 * D_A + 2 * D_B:]
    xp = cg * xb
    x1 = _shift_down(xp, 1, prev_xp)
    x2 = _shift_down(xp, 2, prev_xp)
    conv = cw[0:1, :] * x2 + cw[1:2, :] * x1 + cw[2:3, :] * xp
    return bg, cg, xb, xp, x1, x2, conv


def _ab_fwd(x, p, abin_all, about_all, nv, ws, bst, cw, rider=None):
    s = x.shape[0]
    nt = s // TM

    def body(x_ref, p_ref, abin_hbm, about_hbm, nv_ref, ws_ref, bst_ref, cw_ref,
             xo_ref, h_ref, proj_ref, out_ref, abin, about, prev, sems):
        @pl.when(pl.program_id(0) == 0)
        def _():
            cps = _load_rows(abin_hbm, None, abin, sems, 0) + _load_rows(about_hbm, None, about, sems, N_DEV)
            for cp in cps:
                cp.start()
            prev[...] = jnp.zeros_like(prev)
            for cp in cps:
                cp.wait()

        sh, sc, gate, gn = _mod_rows(p_ref)
        x = x_ref[...]
        hb = _modulate(x, gn, sh, sc).astype(BF16)
        h_ref[...] = hb
        proj = _dot_nt(hb, abin[...])
        proj_ref[...] = proj
        _, _, gu, _, _, _, _, z = _gating(proj, nv_ref[...], ws_ref, bst_ref[...])
        bg, _, _, xp, _, _, conv = _conv(proj, cw_ref[...], prev[...])
        prev[...] = xp[TM - CONV_HALO:, :]
        cat = jnp.concatenate([gu * z, bg * conv], axis=1).astype(BF16)
        out = _dot_nn(cat, about[...])
        out_ref[...] = out
        xo_ref[...] = x + gate * out

    tile = pl.BlockSpec((TM, D), lambda i: (i, 0))
    full = lambda a: pl.BlockSpec(a.shape, lambda i: (0,) * a.ndim)
    return _run(
        body, name="ab_fwd", grid=(nt,),
        out_shape=[jax.ShapeDtypeStruct((s, D), F32), jax.ShapeDtypeStruct((s, D), BF16),
                   jax.ShapeDtypeStruct((s, D_AB), F32), jax.ShapeDtypeStruct((s, D), F32)],
        in_specs=[tile, pl.BlockSpec((8, D), lambda i: (0, 0)), ANY, ANY, full(nv), full(ws), full(bst), full(cw)],
        out_specs=[tile, tile, pl.BlockSpec((TM, D_AB), lambda i: (i, 0)), tile],
        scratch_shapes=[pltpu.VMEM((D_AB, D), BF16), pltpu.VMEM((D, D), BF16), pltpu.VMEM((CONV_HALO, D_B), F32),
                        pltpu.SemaphoreType.DMA((2 * N_DEV,))],
        args=(x, p, abin_all, about_all, nv, ws, bst, cw), rider=rider)


def _ab_bwd(dxo, x, p, proj, out, abin_all, about_all, nv, ws, bst, cw, rider=None):
    s = x.shape[0]
    nt = s // TM
    npj = D_AB // HC

    def body(dxo_ref, x_ref, p_ref, proj_ref, halo_ref, out_ref, abin_hbm, about_hbm, nv_ref, ws_ref, bst_ref, cw_ref,
             dx_ref, dproj_ref, cat_ref, dy_ref, st_ref, dnv_ref, dws_ref, dbs_ref, dcw_ref,
             abin, about, nxt, stats, dnv, dws, dbs, dcw, sems):
        i = pl.program_id(0)
        ti = nt - 1 - i

        @pl.when(i == 0)
        def _():
            cps = _load_rows(abin_hbm, None, abin, sems, 0) + _load_rows(about_hbm, None, about, sems, N_DEV)
            for cp in cps:
                cp.start()
            for z in (nxt, stats, dnv, dws, dbs, dcw):
                z[...] = jnp.zeros_like(z)
            for cp in cps:
                cp.wait()

        sh, sc, gate, gn = _mod_rows(p_ref)
        x = x_ref[...]
        dxo = dxo_ref[...]
        dyb = (gate * dxo).astype(BF16)
        dy_ref[...] = dyb
        stats[2] += _colsum8(dxo * out_ref[...])
        dcat = _dot_nt(dyb, about[...])
        dya, dyb2 = dcat[:, 0:D_A], dcat[:, D_A:]

        proj = proj_ref[...]
        nvv = nv_ref[...]
        u, v, gu, rstd, vhat, vnb, wm, z = _gating(proj, nvv, ws_ref, bst_ref[...])
        dgu = dya * z
        dzb = (dya * gu).astype(BF16)
        dz32 = dya * gu
        rows = []
        for n in range(TM // CHUNK):
            blocks = []
            for hd in range(A_HEADS):
                sl = (slice(n * CHUNK, (n + 1) * CHUNK), slice(hd * CHUNK, (hd + 1) * CHUNK))
                dbs[hd] += dz32[sl]
                dws[hd] += _dot_nt(dzb[sl], vnb[sl])
                blocks.append(_dot_tn(wm[hd], dzb[sl]))
            rows.append(jnp.concatenate(blocks, axis=1))
        dvn = jnp.concatenate(rows, axis=0)
        dnv[...] += _colsum8(dvn * vhat)
        dvh = dvn * nvv
        dgv = rstd * (dvh - jnp.mean(dvh, axis=-1, keepdims=True) - vhat * jnp.mean(dvh * vhat, axis=-1, keepdims=True))
        du = dgu * _gelu_grad(u)
        dv = dgv * _gelu_grad(v)

        halo = halo_ref[...]
        prev_xp = jnp.where(ti > 0, halo[:, 2 * D_A + D_B:2 * D_A + 2 * D_B] * halo[:, 2 * D_A + 2 * D_B:], 0.0)
        cwv = cw_ref[...]
        bg, cg, xb, xp, x1, x2, conv = _conv(proj, cwv, prev_xp)
        dbg = dyb2 * conv
        dconv = dyb2 * bg
        dcw[...] += jnp.concatenate(
            [jnp.sum(_colsum8(dconv * t), axis=0, keepdims=True) for t in (x2, x1, xp)] + [jnp.zeros((5, D_B), F32)], axis=0)
        nx = nxt[...]
        dxp = cwv[2:3, :] * dconv + cwv[1:2, :] * _shift_up(dconv, 1, nx) + cwv[0:1, :] * _shift_up(dconv, 2, nx)
        nxt[...] = dconv[0:CONV_HALO, :]
        dcg = dxp * xb
        dxb = dxp * cg

        dproj = jnp.concatenate([du, dv, dbg, dcg, dxb], axis=1).astype(BF16)
        for k in range(npj):
            dproj_ref[k] = dproj[:, k * HC:(k + 1) * HC]
        cat = jnp.concatenate([gu * z, bg * conv], axis=1).astype(BF16)
        for k in range(D // HC):
            cat_ref[k] = cat[:, k * HC:(k + 1) * HC]
        dh = _dot_nn(dproj, abin[...])
        dx_ref[...] = dxo + _modulate_bwd(dh, x, gn, sc, stats)

        @pl.when(i == nt - 1)
        def _():
            _stats_out(stats, st_ref)
            dnv_ref[...] = jnp.concatenate([jnp.sum(dnv[...], axis=0, keepdims=True), jnp.zeros((7, D_A), F32)], axis=0)
            r = lax.broadcasted_iota(jnp.int32, (CHUNK, CHUNK), 0)
            c = lax.broadcasted_iota(jnp.int32, (CHUNK, CHUNK), 1)
            for hd in range(A_HEADS):
                dws_ref[hd] = jnp.where(r >= c, dws[hd], 0.0)
                dbs_ref[hd] = jnp.broadcast_to(jnp.sum(dbs[hd], axis=1, keepdims=True), (CHUNK, CHUNK))
            dcw_ref[...] = dcw[...]

    rev = pl.BlockSpec((TM, D), lambda i: (nt - 1 - i, 0))
    small = pl.BlockSpec((8, D), lambda i: (0, 0))
    full = lambda a: pl.BlockSpec(a.shape, lambda i: (0,) * a.ndim)
    hpt = TM // CONV_HALO
    fixed = lambda shape: pl.BlockSpec(shape, lambda i: (0,) * len(shape))
    return _run(
        body, name="ab_bwd", grid=(nt,),
        out_shape=[jax.ShapeDtypeStruct((s, D), F32), jax.ShapeDtypeStruct((npj, s, HC), BF16),
                   jax.ShapeDtypeStruct((D // HC, s, HC), BF16), jax.ShapeDtypeStruct((s, D), BF16),
                   jax.ShapeDtypeStruct((8, D), F32), jax.ShapeDtypeStruct((8, D_A), F32),
                   jax.ShapeDtypeStruct((A_HEADS, CHUNK, CHUNK), F32), jax.ShapeDtypeStruct((A_HEADS, CHUNK, CHUNK), F32),
                   jax.ShapeDtypeStruct((8, D_B), F32)],
        in_specs=[rev, rev, small,
                  pl.BlockSpec((TM, D_AB), lambda i: (nt - 1 - i, 0)),
                  pl.BlockSpec((CONV_HALO, D_AB), lambda i: (jnp.maximum((nt - 1 - i) * hpt - 1, 0), 0)),
                  rev, ANY, ANY, full(nv), full(ws), full(bst), full(cw)],
        out_specs=[rev, pl.BlockSpec((npj, TM, HC), lambda i: (0, nt - 1 - i, 0)),
                   pl.BlockSpec((D // HC, TM, HC), lambda i: (0, nt - 1 - i, 0)), rev,
                   small, fixed((8, D_A)), fixed((A_HEADS, CHUNK, CHUNK)), fixed((A_HEADS, CHUNK, CHUNK)), fixed((8, D_B))],
        scratch_shapes=[pltpu.VMEM((D_AB, D), BF16), pltpu.VMEM((D, D), BF16), pltpu.VMEM((CONV_HALO, D_B), F32),
                        pltpu.VMEM((4, 8, D), F32), pltpu.VMEM((8, D_A), F32),
                        pltpu.VMEM((A_HEADS, CHUNK, CHUNK), F32), pltpu.VMEM((A_HEADS, CHUNK, CHUNK), F32),
                        pltpu.VMEM((8, D_B), F32), pltpu.SemaphoreType.DMA((2 * N_DEV,))],
        args=(dxo, x, p, proj, proj, out, abin_all, about_all, nv, ws, bst, cw), rider=rider)


def _pool_counts(first_token, rows):
    t = (first_token + lax.broadcasted_iota(jnp.int32, (rows, 1), 0) + 1).astype(F32)
    lane = lax.broadcasted_iota(jnp.int32, (1, D), 1)
    w = jnp.where(lane < POOL_G, 2.0, jnp.where(lane < 2 * POOL_G, 4.0, jnp.where(lane < 3 * POOL_G, 8.0, 16.0)))
    return jnp.minimum(t, w)


def _window_sums(ext, n_keep, lead, back):
    n = ext.shape[0]
    sh = (lambda v, k: pltpu.roll(v, k, 0)) if back else (lambda v, k: pltpu.roll(v, n - k, 0))
    s2 = ext + sh(ext, 1)
    s4 = s2[:, POOL_G:] + sh(s2[:, POOL_G:], 2)
    s8 = s4[:, POOL_G:] + sh(s4[:, POOL_G:], 4)
    s16 = s8[:, POOL_G:] + sh(s8[:, POOL_G:], 8)
    keep = slice(lead, lead + n_keep)
    return jnp.concatenate([s2[keep, 0:POOL_G], s4[keep, 0:POOL_G], s8[keep, 0:POOL_G], s16[keep, :]], axis=1)


def _pool_fwd(x, p, pool_all, pscale):
    s = x.shape[0]
    nt = s // TM
    ng = D // POOL_G

    def body(x_ref, p_ref, wg_ref, ps_ref, xo_ref, pb_ref, op_ref, prev):
        i = pl.program_id(0)

        @pl.when(i == 0)
        def _():
            prev[...] = jnp.zeros_like(prev)

        sh, sc, gate, gn = _mod_rows(p_ref)
        x = x_ref[...]
        h = _modulate(x, gn, sh, sc)
        win = _window_sums(jnp.concatenate([prev[...], h], axis=0), TM, POOL_HALO, True)
        prev[...] = h[TM - POOL_HALO:, :]
        pb = (win / _pool_counts(i * TM, TM) - h).astype(BF16)
        pb_ref[...] = pb
        op = jnp.concatenate(
            [_dot_nn(pb[:, g * POOL_G:(g + 1) * POOL_G], wg_ref[:, g].reshape(POOL_G, POOL_G)) for g in range(ng)], axis=1)
        op_ref[...] = op
        xo_ref[...] = x + gate * (op * ps_ref[...])

    tile = pl.BlockSpec((TM, D), lambda i: (i, 0))
    return pl.pallas_call(
        body, name="pool_fwd", grid=(nt,),
        out_shape=[jax.ShapeDtypeStruct((s, D), F32), jax.ShapeDtypeStruct((s, D), BF16), jax.ShapeDtypeStruct((s, D), F32)],
        in_specs=[tile, pl.BlockSpec((8, D), lambda i: (0, 0)),
                  pl.BlockSpec(pool_all.shape, lambda i: (0, 0, 0, 0)), pl.BlockSpec((1, D), lambda i: (0, 0))],
        out_specs=[tile, tile, tile],
        scratch_shapes=[pltpu.VMEM((POOL_HALO, D), F32)],
        compiler_params=ARB1,
    )(x, p, pool_all, pscale)


def _pool_bwd(dxo, x, p, pb, op, pool_all, pscale):
    s = x.shape[0]
    nt = s // TM
    ng = D // POOL_G

    def body(dxo_ref, x_ref, p_ref, pb_ref, op_ref, wg_ref, ps_ref,
             dx_ref, st_ref, dps_ref, dwg_ref, nxt, stats, dps, dwg):
        i = pl.program_id(0)
        ti = nt - 1 - i

        @pl.when(i == 0)
        def _():
            for z in (nxt, stats, dps, dwg):
                z[...] = jnp.zeros_like(z)

        sh, sc, gate, gn = _mod_rows(p_ref)
        x = x_ref[...]
        dxo = dxo_ref[...]
        ps = ps_ref[...]
        op = op_ref[...]
        dmo = gate * dxo
        stats[2] += _colsum8(dxo * (op * ps))
        dps[...] += _colsum8(dmo * op)
        dopb = (dmo * ps).astype(BF16)
        pbv = pb_ref[...]
        dps_parts = []
        for g in range(ng):
            sl = slice(g * POOL_G, (g + 1) * POOL_G)
            dps_parts.append(_dot_nt(dopb[:, sl], wg_ref[:, g].reshape(POOL_G, POOL_G)))
            dwg[g] += _dot_tn(pbv[:, sl], dopb[:, sl])
        dp = jnp.concatenate(dps_parts, axis=1)
        q = dp / _pool_counts(ti * TM, TM)
        wsum = _window_sums(jnp.concatenate([q, nxt[...]], axis=0), TM, 0, False)
        nxt[...] = q[0:POOL_HALO, :]
        dx_ref[...] = dxo + _modulate_bwd(wsum - dp, x, gn, sc, stats)

        @pl.when(i == nt - 1)
        def _():
            _stats_out(stats, st_ref)
            dps_ref[...] = jnp.concatenate([jnp.sum(dps[...], axis=0, keepdims=True), jnp.zeros((7, D), F32)], axis=0)
            dwg_ref[...] = dwg[...].astype(BF16)

    rev = pl.BlockSpec((TM, D), lambda i: (nt - 1 - i, 0))
    small = pl.BlockSpec((8, D), lambda i: (0, 0))
    return pl.pallas_call(
        body, name="pool_bwd", grid=(nt,),
        out_shape=[jax.ShapeDtypeStruct((s, D), F32), jax.ShapeDtypeStruct((8, D), F32), jax.ShapeDtypeStruct((8, D), F32),
                   jax.ShapeDtypeStruct((ng, POOL_G, POOL_G), BF16)],
        in_specs=[rev, rev, small, rev, rev,
                  pl.BlockSpec(pool_all.shape, lambda i: (0, 0, 0, 0)), pl.BlockSpec((1, D), lambda i: (0, 0))],
        out_specs=[rev, small, small, pl.BlockSpec((ng, POOL_G, POOL_G), lambda i: (0, 0, 0))],
        scratch_shapes=[pltpu.VMEM((POOL_HALO, D), F32), pltpu.VMEM((4, 8, D), F32), pltpu.VMEM((8, D), F32),
                        pltpu.VMEM((ng, POOL_G, POOL_G), F32)],
        compiler_params=ARB1,
    )(dxo, x, p, pb, op, pool_all, pscale)


def _head(x, fg, tgt):
    s = x.shape[0]
    nt = s // TM

    def body(x_ref, fg_ref, t_ref, dx_ref, loss_ref, dfg_ref, sq, dfg):
        i = pl.program_id(0)

        @pl.when(i == 0)
        def _():
            sq[...] = jnp.zeros_like(sq)
            dfg[...] = jnp.zeros_like(dfg)

        x = x_ref[...]
        g = fg_ref[...]
        r = lax.rsqrt(jnp.mean(x * x, axis=-1, keepdims=True) + EPS)
        xn = x * r
        e = xn * g - t_ref[...]
        sq[...] += _colsum8(e * e)
        dy = e * (1.0 / D)
        dfg[...] += _colsum8(dy * xn)
        dxn = dy * g
        dx_ref[...] = r * (dxn - xn * jnp.mean(dxn * xn, axis=-1, keepdims=True))

        @pl.when(i == nt - 1)
        def _():
            total = jnp.sum(jnp.sum(sq[...], axis=0, keepdims=True), axis=1, keepdims=True)
            loss_ref[...] = jnp.broadcast_to(total * (0.5 / D), loss_ref.shape)
            dfg_ref[...] = jnp.concatenate([jnp.sum(dfg[...], axis=0, keepdims=True), jnp.zeros((7, D), F32)], axis=0)

    tile = pl.BlockSpec((TM, D), lambda i: (i, 0))
    return pl.pallas_call(
        body, name="head", grid=(nt,),
        out_shape=[jax.ShapeDtypeStruct((s, D), F32), jax.ShapeDtypeStruct((8, 128), F32), jax.ShapeDtypeStruct((8, D), F32)],
        in_specs=[tile, pl.BlockSpec((1, D), lambda i: (0, 0)), tile],
        out_specs=[tile, pl.BlockSpec((8, 128), lambda i: (0, 0)), pl.BlockSpec((8, D), lambda i: (0, 0))],
        scratch_shapes=[pltpu.VMEM((8, D), F32), pltpu.VMEM((8, D), F32)],
        compiler_params=ARB1,
    )(x, fg, tgt)


def _adamw_math(w, g, m, v):
    m = ADAM_B1 * m + (1.0 - ADAM_B1) * g
    v = ADAM_B2 * v + (1.0 - ADAM_B2) * (g * g)
    m_hat = m / (1.0 - ADAM_B1 ** ADAM_STEP)
    v_hat = v / (1.0 - ADAM_B2 ** ADAM_STEP)
    delta = -ADAM_LR * (m_hat / (jnp.sqrt(v_hat) + ADAM_EPS) + ADAM_WD * w)
    return delta, m, v


def _finish(parts, w, m, v, rb, name, halves=False, rider=None):
    nf, r, c = w.shape
    npart = parts[0].shape[0]

    def body(*refs):
        p_refs = refs[:nf]
        w_ref, m_ref, v_ref, g_ref, d_ref, mo_ref, vo_ref = refs[nf:]
        for f in range(nf):
            @pl.when(pl.program_id(0) == f)
            def _():
                g = p_refs[f][0].astype(F32)
                for k in range(1, npart):
                    g = g + p_refs[f][k].astype(F32)
                if halves:
                    g = jnp.concatenate([g[0], g[1]], axis=1)
                g_ref[0] = g
                d_ref[0], mo_ref[0], vo_ref[0] = _adamw_math(w_ref[0], g, m_ref[0], v_ref[0])

    blk = pl.BlockSpec((1, rb, c), lambda f, i: (f, i, 0))

    def pblk(mine):
        if halves:
            return pl.BlockSpec((npart, 2, rb, c // 2), lambda f, i: (0, 0, jnp.where(f == mine, i, 0), 0))
        return pl.BlockSpec((npart, rb, c), lambda f, i: (0, jnp.where(f == mine, i, 0), 0))

    return _run(
        body, name=name, grid=(nf, r // rb),
        out_shape=[jax.ShapeDtypeStruct(w.shape, F32)] * 4,
        in_specs=[pblk(f) for f in range(nf)] + [blk, blk, blk], out_specs=[blk] * 4, scratch_shapes=[],
        args=(*parts, w, m, v), rider=rider,
        params=pltpu.CompilerParams(dimension_semantics=("arbitrary", "arbitrary"), vmem_limit_bytes=VMEM_LIMIT))


def _sum_small(parts):
    def body(p_ref, o_ref):
        acc = p_ref[0]
        for k in range(1, N_DEV):
            acc = acc + p_ref[k]
        o_ref[...] = acc

    return pl.pallas_call(body, name="sum_small", out_shape=jax.ShapeDtypeStruct(parts.shape[1:], F32),
                          in_specs=[VMEM_SPEC], out_specs=VMEM_SPEC)(parts)


def _adamw(w, g, m, v, name):
    def body(w_ref, g_ref, m_ref, v_ref, d_ref, mo_ref, vo_ref):
        d_ref[...], mo_ref[...], vo_ref[...] = _adamw_math(w_ref[...], g_ref[...], m_ref[...], v_ref[...])

    return pl.pallas_call(
        body, name=name, out_shape=[jax.ShapeDtypeStruct(w.shape, F32)] * 3,
        in_specs=[VMEM_SPEC] * 4, out_specs=[VMEM_SPEC] * 3,
    )(w, g, m, v)


def _wmod_finish(act_t, dmod_cols, w, m, v):
    rb = 256
    ncol = w.shape[-1]

    def body(a_ref, dm_ref, w_ref, m_ref, v_ref, g_ref, d_ref, mo_ref, vo_ref):
        g = a_ref[:, 0:1] * dm_ref[0, 0:1, :]
        for k in range(1, N_DEV):
            g = g + a_ref[:, k:k + 1] * dm_ref[0, k:k + 1, :]
        g_ref[0] = g
        d_ref[0], mo_ref[0], vo_ref[0] = _adamw_math(w_ref[0], g, m_ref[0], v_ref[0])

    blk = pl.BlockSpec((1, rb, ncol), lambda l, i: (l, i, 0))
    return pl.pallas_call(
        body, name="wmod_finish", grid=(2, D // rb),
        out_shape=[jax.ShapeDtypeStruct(w.shape, F32)] * 4,
        in_specs=[pl.BlockSpec((rb, N_DEV), lambda l, i: (i, 0)), pl.BlockSpec((1, N_DEV, ncol), lambda l, i: (l, 0, 0)),
                  blk, blk, blk],
        out_specs=[blk] * 4,
        compiler_params=pltpu.CompilerParams(dimension_semantics=("arbitrary", "arbitrary"), vmem_limit_bytes=VMEM_LIMIT),
    )(act_t, dmod_cols, w, m, v)


def _pack(pieces):
    flat, offs, at = [], [], 0
    for a in pieces:
        a = a.reshape(-1)
        n = -(-a.shape[0] // 128) * 128
        flat.append(jnp.pad(a, (0, n - a.shape[0])))
        offs.append(at)
        at += n
    return jnp.concatenate(flat).reshape(-1, 128), offs


def _param_block(mod_l, sub, gn):
    return jnp.concatenate([mod_l[sub], gn[None, :], jnp.zeros((4, D), F32)], axis=0)


def kernel(x, c, norm_g, w_mod, b_mod, w_ffn_in, w_ffn_out, ab_w_in, ab_norm_v, ab_w_s, ab_b_s, ab_conv_w, ab_w_out, pool_w_grp, pool_scale, final_g, loss_target, m_norm_g, m_w_mod, m_b_mod, m_w_ffn_in, m_w_ffn_out, m_ab_w_in, m_ab_norm_v, m_ab_w_s, m_ab_b_s, m_ab_conv_w, m_ab_w_out, m_pool_w_grp, m_pool_scale, m_final_g, v_norm_g, v_w_mod, v_b_mod, v_w_ffn_in, v_w_ffn_out, v_ab_w_in, v_ab_norm_v, v_ab_w_s, v_ab_b_s, v_ab_conv_w, v_ab_w_out, v_pool_w_grp, v_pool_scale, v_final_g):
    me = 4 * lax.axis_index("x") + 2 * lax.axis_index("y") + lax.axis_index("c")
    x0 = x[0]
    tgt = loss_target[0]
    n_in = w_ffn_in.shape[-1]
    n_out = w_ffn_out.shape[-2]
    n_abin = ab_w_in.shape[-1]
    n_about = ab_w_out.shape[-2]
    n_pool = pool_w_grp.shape[-2]
    n_mod = w_mod.shape[-1]
    n_ng = norm_g.shape[-1]
    n_cw = ab_conv_w.shape[-1]
    n_ps = pool_scale.shape[-1]

    win_sh = jnp.swapaxes(w_ffn_in.reshape(4, D, n_in), 1, 2).astype(BF16)
    wout_sh = w_ffn_out.reshape(4, n_out, D).astype(BF16)
    abin_sh = ab_w_in[0].T.astype(BF16)
    about_sh = ab_w_out[0].astype(BF16)
    pool_sh = pool_w_grp[0].astype(BF16)
    win, wout = [None] * 4, [None] * 4

    pack, offs = _pack([c, norm_g, ab_conv_w, pool_scale])
    got, (win[0],) = _exchange_small(pack, "gather_small", _GatherRider([(win_sh, 0)]))
    got = got.reshape(N_DEV, -1)
    c_all = got[:, offs[0]:offs[0] + D]
    ng_full = got[:, offs[1]:offs[1] + 6 * n_ng].reshape(N_DEV, 2, 3, n_ng).transpose(1, 2, 0, 3).reshape(2, 3, D)
    cw_full = got[:, offs[2]:offs[2] + 3 * n_cw].reshape(N_DEV, 3, n_cw).transpose(1, 0, 2).reshape(3, D_B)
    ps_full = got[:, offs[3]:offs[3] + n_ps].reshape(1, D)

    act_all, mod_cols = _mod_fwd(c_all, w_mod)
    mod_got, (wout[0],) = _exchange_small(mod_cols.reshape(-1, 128), "gather_mod", _GatherRider([(wout_sh, 0)]))
    mod_got = mod_got.reshape(N_DEV, 2, N_DEV, n_mod)
    mod = lax.dynamic_index_in_dim(mod_got, me, axis=2, keepdims=False).transpose(1, 0, 2).reshape(2, 9 * D) + b_mod
    mod = mod.reshape(2, 3, 3, D)
    nv = ab_norm_v
    ws = ab_w_s[0]
    bst = ab_b_s[0].T
    cw8 = jnp.concatenate([cw_full, jnp.zeros((5, D_B), F32)], axis=0)

    pb = [[_param_block(mod[l], s, ng_full[l, s]) for s in range(3)] for l in range(2)]
    (x1, h00, g00, u00, y00), (abin_all, about_all, win[1]) = _ffn_fwd(
        x0, pb[0][0], win[0], wout[0], 0, _GatherRider([abin_sh, about_sh, (win_sh, 1)]))
    (x2, h01, proj, ab_out), (wout[1],) = _ab_fwd(
        x1, pb[0][1], abin_all, about_all, nv, ws, bst, cw8, _GatherRider([(wout_sh, 1)]))
    (x3, h02, g02, u02, y02), (win[2], wout[2]) = _ffn_fwd(
        x2, pb[0][2], win[1], wout[1], 1, _GatherRider([(win_sh, 2), (wout_sh, 2)]))
    (x4, h10, g10, u10, y10), (pool_all, win[3], wout[3]) = _ffn_fwd(
        x3, pb[1][0], win[2], wout[2], 2, _GatherRider([pool_sh, (win_sh, 3), (wout_sh, 3)]))
    x5, pooled, pool_out = _pool_fwd(x4, pb[1][1], pool_all, ps_full)
    (x6, h12, g12, u12, y12), _ = _ffn_fwd(x5, pb[1][2], win[3], wout[3], 3)
    dx6, loss_blk, dfg = _head(x6, final_g.reshape(1, D), tgt)

    core = lax.axis_index("c")
    sel_rows = lambda n: jnp.stack([core * n, 0]).astype(jnp.int32)
    sel_cols = jnp.stack([0, core]).astype(jnp.int32)
    p_in, p_out = [None] * 4, [None] * 4

    def exchange(dgu, a, h, dy):
        return _SiblingRider([(dgu, NCH), (a, None), (h, None), (dy, None)])

    def ffn_wgrads(f, dgu, a, h, dy, got, ride_out=None):
        s_dgu, s_a, s_h, s_dy = got
        g_out, rode = _wgrad_pair(a, s_a, dy, s_dy, sel_cols, NCH, True, f"wgrad_out_{f}", ride_out)
        g_in, (p_out[f],) = _wgrad_pair(dgu, s_dgu, h, s_h, sel_rows(NCH), NCH, False, f"wgrad_in_{f}",
                                        _ChipScatterRider([], [g_out]))
        return g_in, rode

    (dx5, dgu12, a12, dy12, st12), _ = _ffn_bwd(dx6, x5, pb[1][2], g12, u12, y12, win[3], wout[3], 3)
    dx4, st11, dps, gw_pool = _pool_bwd(dx5, x4, pb[1][1], pooled, pool_out, pool_all, ps_full)
    (dx3, dgu10, a10, dy10, st10), got3 = _ffn_bwd(
        dx4, x3, pb[1][0], g10, u10, y10, win[2], wout[2], 2, exchange(dgu12, a12, h12, dy12))
    g_in3, (p_pool,) = ffn_wgrads(3, dgu12, a12, h12, dy12, got3, _ScatterRider([gw_pool]))
    ride = _Riders([exchange(dgu10, a10, h10, dy10), _ChipScatterRider([g_in3], [])])
    (dx2, dgu02, a02, dy02, st02), rode = _ffn_bwd(dx3, x2, pb[0][2], g02, u02, y02, win[1], wout[1], 1, ride)
    got2, (p_in[3],) = ride.split(rode)
    g_in2, _ = ffn_wgrads(2, dgu10, a10, h10, dy10, got2)
    ride = _Riders([exchange(dgu02, a02, h02, dy02), _ChipScatterRider([g_in2], [])])
    (dx1, dproj, cat, dy01, st01, dnv, dws, dbs, dcw), rode = _ab_bwd(
        dx2, x1, pb[0][1], proj, ab_out, abin_all, about_all, nv, ws, bst, cw8, ride)
    got1, (p_in[2],) = ride.split(rode)
    g_in1, _ = ffn_wgrads(1, dgu02, a02, h02, dy02, got1)
    ride = _Riders([_SiblingRider([(dproj, 5), (cat, 2), (h01, None), (dy01, None)]), _ChipScatterRider([g_in1], [])])
    (dgu00, a00, dy00, gst00), rode = _ffn_bwd_acts(dx1, pb[0][0], g00, u00, y00, wout[0], 0, ride)
    (s_dproj, s_cat, s_h01, s_dy01), (p_in[1],) = ride.split(rode)
    g_about, _ = _wgrad_pair(cat, s_cat, dy01, s_dy01, sel_rows(2), 2, False, "wgrad_ab_out")
    g_abin, (p_about,) = _wgrad_pair(dproj, s_dproj, h01, s_h01, sel_rows(5), 5, False, "wgrad_ab_in",
                                     _ChipScatterRider([g_about], []))
    ride = _Riders([exchange(dgu00, a00, h00, dy00), _ChipScatterRider([g_abin], [])])
    (dx0, st00), rode = _ffn_bwd_dx(dx1, x0, pb[0][0], dgu00, gst00, win[0], 0, ride)
    got0, (p_abin,) = ride.split(rode)
    grad_x = dx0[None]

    stats = [[st00, st01, st02], [st10, st11, st12]]
    dmod = jnp.stack([jnp.concatenate([stats[l][s][0:3].reshape(-1) for s in range(3)]) for l in range(2)])
    dng = jnp.stack([jnp.stack([stats[l][s][3] for s in range(3)]) for l in range(2)])
    spack, so = _pack([dmod, dng, dnv[0], dws, dbs[:, :, 0], dcw[0:3], dps[0], dfg[0], loss_blk[0]])
    g_in0, (sgot,) = ffn_wgrads(0, dgu00, a00, h00, dy00, got0, _SmallGatherRider(spack))

    shape_in, shape_out = w_ffn_in.shape, w_ffn_out.shape
    fin = lambda a: jnp.swapaxes(a.reshape(4, D, n_in), 1, 2)
    fabin = lambda a: jnp.swapaxes(a, 1, 2)
    fout = lambda a: a.reshape(4, n_out, D)
    fpool = lambda a: a.reshape(1, 4 * n_pool, POOL_G)
    r_out, (p_in[0],) = _finish(p_out, fout(w_ffn_out), fout(m_w_ffn_out), fout(v_w_ffn_out), n_out // 2, "finish_ffn_out",
                                halves=True, rider=_ChipScatterRider([g_in0], []))
    r_in, _ = _finish(p_in, fin(w_ffn_in), fin(m_w_ffn_in), fin(v_w_ffn_in), n_in // 4, "finish_ffn_in")
    r_abin, _ = _finish([p_abin], fabin(ab_w_in), fabin(m_ab_w_in), fabin(v_ab_w_in), n_abin, "finish_ab_in")
    r_about, _ = _finish([p_about], ab_w_out, m_ab_w_out, v_ab_w_out, n_about, "finish_ab_out")
    r_pool, _ = _finish([p_pool.reshape(N_DEV, 4 * n_pool, POOL_G)], fpool(pool_w_grp), fpool(m_pool_w_grp),
                        fpool(v_pool_w_grp), 4 * n_pool, "finish_pool")
    r_in = [jnp.swapaxes(a, 1, 2).reshape(shape_in) for a in r_in]
    r_abin = [jnp.swapaxes(a, 1, 2) for a in r_abin]
    r_out = [a.reshape(shape_out) for a in r_out]
    r_pool = [a.reshape(pool_w_grp.shape) for a in r_pool]

    ssum = _sum_small(sgot).reshape(-1)
    loss = ssum[so[8]]
    take = lambda i, n: lax.dynamic_slice_in_dim(ssum, so[i], n)
    g_bmod = take(0, 2 * 9 * D).reshape(2, 9 * D)
    g_ng = lax.dynamic_slice_in_dim(take(1, 6 * D).reshape(2, 3, D), me * n_ng, n_ng, axis=2)
    g_nv = take(2, D_A).reshape(1, D_A)
    g_ws = take(3, A_HEADS * CHUNK * CHUNK).reshape(1, A_HEADS, CHUNK, CHUNK)
    g_bs = take(4, A_HEADS * CHUNK).reshape(1, A_HEADS, CHUNK)
    g_cw = lax.dynamic_slice_in_dim(take(5, 3 * D_B).reshape(1, 3, D_B), me * n_cw, n_cw, axis=2)
    g_ps = lax.dynamic_slice_in_dim(take(6, D).reshape(1, D), me * n_ps, n_ps, axis=1)
    g_fg = take(7, D)

    dmod_all = sgot.reshape(N_DEV, -1)[:, so[0]:so[0] + 2 * 9 * D].reshape(N_DEV, 2, 9 * D)
    dmod_cols = lax.dynamic_slice_in_dim(dmod_all, me * n_mod, n_mod, axis=2).transpose(1, 0, 2)
    r_wmod = _wmod_finish(act_all.T, dmod_cols, w_mod, m_w_mod, v_w_mod)

    small_w = [b_mod, norm_g, ab_norm_v, ab_w_s, ab_b_s, ab_conv_w, pool_scale, final_g]
    small_g = [g_bmod, g_ng, g_nv, g_ws, g_bs, g_cw, g_ps, g_fg]
    small_m = [m_b_mod, m_norm_g, m_ab_norm_v, m_ab_w_s, m_ab_b_s, m_ab_conv_w, m_pool_scale, m_final_g]
    small_v = [v_b_mod, v_norm_g, v_ab_norm_v, v_ab_w_s, v_ab_b_s, v_ab_conv_w, v_pool_scale, v_final_g]
    pw, po = _pack(small_w)
    pv = jnp.concatenate([jnp.pad(a.reshape(-1), (0, -a.size % 128), constant_values=1.0) for a in small_v]).reshape(-1, 128)
    sd, sm, sv = _adamw(pw, _pack(small_g)[0], _pack(small_m)[0], pv, "adamw_small")
    unpack = lambda packed: [packed.reshape(-1)[po[i]:po[i] + a.size].reshape(a.shape) for i, a in enumerate(small_w)]
    d_s, m_s, v_s = unpack(sd), unpack(sm), unpack(sv)

    def ordered(k, small):
        return [small[1], r_wmod[k], small[0], r_in[k], r_out[k], r_abin[k], small[2], small[3], small[4], small[5],
                r_about[k], r_pool[k], small[6], small[7]]

    grads = ordered(0, small_g)
    deltas = ordered(1, d_s)
    new_m = ordered(2, m_s)
    new_v = ordered(3, v_s)
    return (loss, grad_x, *grads, *deltas, *new_m, *new_v)
```

```python
import functools
import math

import jax
import jax.numpy as jnp
from jax import lax
from jax.experimental import pallas as pl
from jax.experimental.pallas import tpu as pltpu

F32 = jnp.float32
BF16 = jnp.bfloat16

N_DEV = 8
D = 1024
DFF = 2816
HC = 256
NCH = DFF // HC
D_A = 512
D_B = 512
D_AB = 2 * D_A + 3 * D_B
CHUNK = 128
A_HEADS = 4
POOL_G = 256
POOL_HALO = 16
CONV_HALO = 8
EPS = 1e-6
TM = 256
GELU_K = math.sqrt(2.0 / math.pi)
GELU_C = 0.044715

ADAM_LR = 0.001
ADAM_B1 = 0.9
ADAM_B2 = 0.999
ADAM_EPS = 1e-08
ADAM_WD = 0.01
ADAM_STEP = 10

VMEM_LIMIT = 56 * 1024 * 1024
MESH_ID = pl.DeviceIdType.MESH
ANY = pl.BlockSpec(memory_space=pl.ANY)
VMEM_SPEC = pl.BlockSpec(memory_space=pltpu.VMEM)
ARB1 = pltpu.CompilerParams(dimension_semantics=("arbitrary",), vmem_limit_bytes=VMEM_LIMIT)


def _dot_nt(a, b):
    return lax.dot_general(a, b, (((1,), (1,)), ((), ())), preferred_element_type=F32)


def _dot_nn(a, b):
    return lax.dot_general(a, b, (((1,), (0,)), ((), ())), preferred_element_type=F32)


def _dot_tn(a, b):
    return lax.dot_general(a, b, (((0,), (0,)), ((), ())), preferred_element_type=F32)


def _colsum8(v):
    r, n = v.shape
    return jnp.sum(v.reshape(r // 8, 8, n), axis=0)


def _gelu(x):
    return 0.5 * x * (1.0 + jnp.tanh(GELU_K * (x + GELU_C * x * x * x)))


def _gelu_grad(x):
    t = jnp.tanh(GELU_K * (x + GELU_C * x * x * x))
    return 0.5 * (1.0 + t) + 0.5 * x * (1.0 - t * t) * (GELU_K * (1.0 + 3.0 * GELU_C * x * x))


def _mod_rows(p_ref):
    return p_ref[0:1, :], p_ref[1:2, :], p_ref[2:3, :], p_ref[3:4, :]


def _modulate(x, gn, sh, sc):
    r = lax.rsqrt(jnp.mean(x * x, axis=-1, keepdims=True) + EPS)
    return ((x * r) * gn) * (1.0 + sc) + sh


def _modulate_bwd(dh, x, gn, sc, stats):
    r = lax.rsqrt(jnp.mean(x * x, axis=-1, keepdims=True) + EPS)
    xn = x * r
    stats[0] += _colsum8(dh)
    stats[1] += _colsum8(dh * (xn * gn))
    dy0 = dh * (1.0 + sc)
    stats[3] += _colsum8(dy0 * xn)
    dxn = dy0 * gn
    return r * (dxn - xn * jnp.mean(dxn * xn, axis=-1, keepdims=True))


def _stats_out(stats, out_ref):
    rows = [jnp.sum(stats[k], axis=0, keepdims=True) for k in range(4)]
    out_ref[...] = jnp.concatenate(rows + [jnp.zeros((4, stats.shape[-1]), F32)], axis=0)


def _shift_down(v, k, prev):
    n = v.shape[0]
    row = lax.broadcasted_iota(jnp.int32, v.shape, 0)
    out = pltpu.roll(v, k, 0)
    for j in range(k):
        out = jnp.where(row == j, prev[prev.shape[0] - k + j:prev.shape[0] - k + j + 1, :], out)
    return out


def _shift_up(v, k, nxt):
    n = v.shape[0]
    row = lax.broadcasted_iota(jnp.int32, v.shape, 0)
    out = pltpu.roll(v, n - k, 0)
    for j in range(k):
        out = jnp.where(row == n - k + j, nxt[j:j + 1, :], out)
    return out


def _load_rows(w_hbm, sel, dst, sems, base):
    n = dst.shape[0] // N_DEV
    cps = []
    for k in range(N_DEV):
        src = w_hbm.at[k] if sel is None else w_hbm.at[k, sel]
        cps.append(pltpu.make_async_copy(src, dst.at[pl.ds(k * n, n)], sems.at[base + k]))
    return cps


def _my_pos():
    return lax.axis_index("x"), lax.axis_index("y"), lax.axis_index("c")


def _peer(j):
    x, y, c = _my_pos()
    return (1 - x if j & 4 else x, 1 - y if j & 2 else y, 1 - c if j & 1 else c)


def _index(pos):
    return 4 * pos[0] + 2 * pos[1] + pos[2]


def _exchange_small(v, name, rider):
    rows = v.shape[0]
    ri, ro = len(rider.inputs), len(rider.out_shapes)

    def body(v_ref, *refs):
        r_in, out_ref, r_out, refs = refs[:ri], refs[ri], refs[ri + 1:ri + 1 + ro], refs[ri + 1 + ro:]
        send_sems, recv_sems, local_sem = refs[:3]
        rider.first(r_in, r_out, refs[3:])
        me = _index(_my_pos())

        def copy(j, slot):
            return pltpu.make_async_remote_copy(
                src_ref=v_ref, dst_ref=out_ref.at[slot], send_sem=send_sems.at[j - 1], recv_sem=recv_sems.at[j - 1],
                device_id=_peer(j), device_id_type=MESH_ID)

        mine = pltpu.make_async_copy(v_ref, out_ref.at[me], local_sem)
        mine.start()
        sends = [copy(j, me) for j in range(1, N_DEV)]
        for cp in sends:
            cp.start()
        for j in range(1, N_DEV):
            copy(j, _index(_peer(j))).wait_recv()
        for cp in sends:
            cp.wait_send()
        mine.wait()
        if rider.has_middle:
            rider.middle(r_in, r_out, refs[3:])
        rider.last(r_in, r_out, refs[3:])

    scratch = [pltpu.SemaphoreType.DMA((N_DEV - 1,)), pltpu.SemaphoreType.DMA((N_DEV - 1,)), pltpu.SemaphoreType.DMA(())]
    res = pl.pallas_call(
        body, name=name, out_shape=[jax.ShapeDtypeStruct((N_DEV, rows, 128), F32)] + rider.out_shapes,
        in_specs=[VMEM_SPEC] + [ANY] * ri, out_specs=[VMEM_SPEC] + [ANY] * ro,
        scratch_shapes=scratch + rider.scratch)(v, *rider.inputs)
    return res[0], list(res[1:])


class _SmallGatherRider:
    has_middle = False

    def __init__(self, v):
        self.inputs = [v]
        self.out_shapes = [jax.ShapeDtypeStruct((N_DEV,) + v.shape, v.dtype)]
        self.scratch = [pltpu.SemaphoreType.DMA((N_DEV - 1,)), pltpu.SemaphoreType.DMA((N_DEV - 1,)), pltpu.SemaphoreType.DMA(())]

    def _copy(self, ins, outs, scr, j, slot):
        return pltpu.make_async_remote_copy(
            src_ref=ins[0], dst_ref=outs[0].at[slot], send_sem=scr[0].at[j - 1], recv_sem=scr[1].at[j - 1],
            device_id=_peer(j), device_id_type=MESH_ID)

    def first(self, ins, outs, scr):
        me = _index(_my_pos())
        pltpu.make_async_copy(ins[0], outs[0].at[me], scr[2]).start()
        for j in range(1, N_DEV):
            self._copy(ins, outs, scr, j, me).start()

    def last(self, ins, outs, scr):
        me = _index(_my_pos())
        for j in range(1, N_DEV):
            self._copy(ins, outs, scr, j, _index(_peer(j))).wait_recv()
        for j in range(1, N_DEV):
            self._copy(ins, outs, scr, j, me).wait_send()
        pltpu.make_async_copy(ins[0], outs[0].at[me], scr[2]).wait()


class _GatherRider:
    has_middle = True

    def __init__(self, shards):
        pairs = [s if isinstance(s, tuple) else (s, None) for s in shards]
        self.inputs = [a for a, _ in pairs]
        self.picks = [i for _, i in pairs]
        shapes = [a.shape if i is None else a.shape[1:] for a, i in pairs]
        self.out_shapes = [jax.ShapeDtypeStruct((N_DEV,) + s, a.dtype) for s, (a, _) in zip(shapes, pairs)]
        self.pieces = [4 if (len(s) > 2 or s[0] % 64 == 0) else 2 for s in shapes]
        self.base = [sum(1 + 6 * n for n in self.pieces[:a]) for a in range(len(pairs))]
        total = sum(1 + 6 * n for n in self.pieces)
        self.scratch = [pltpu.SemaphoreType.DMA((total,)), pltpu.SemaphoreType.DMA((total,)),
                        pltpu.SemaphoreType.DMA((len(pairs),))]

    def _src(self, ins):
        return [r if i is None else r.at[i] for r, i in zip(ins, self.picks)]

    def _ctx(self, outs, scr):
        send, recv, _ = scr
        x, y, c = _my_pos()
        chips = [(1 - x, y), (x, 1 - y), (1 - x, 1 - y)]

        def copy(a, k, block, to, src=None, piece=None):
            slot = outs[a].at[_index(block)]
            src = slot if src is None else src
            if piece is not None:
                rows = slot.shape[0] // self.pieces[a]
                slot, src = slot.at[pl.ds(piece * rows, rows)], src.at[pl.ds(piece * rows, rows)]
            return pltpu.make_async_remote_copy(
                src_ref=src, dst_ref=slot, send_sem=send.at[self.base[a] + k], recv_sem=recv.at[self.base[a] + k],
                device_id=to, device_id_type=MESH_ID)

        return (x, y, c), (x, y, 1 - c), chips, copy

    def _sends(self, ins, outs, scr):
        me, sib, chips, copy = self._ctx(outs, scr)
        srcs = self._src(ins)
        out = [copy(a, 0, me, sib, src=srcs[a]) for a in range(len(ins))]
        for s in range(4):
            for a in range(len(ins)):
                if s < self.pieces[a]:
                    out += [copy(a, 1 + j * self.pieces[a] + s, me, (*chips[j], me[2]), src=srcs[a], piece=s) for j in range(3)]
        return out

    def first(self, ins, outs, scr):
        me = _index(_my_pos())
        for a, src in enumerate(self._src(ins)):
            pltpu.make_async_copy(src, outs[a].at[me], scr[2].at[a]).start()
        for cp in self._sends(ins, outs, scr):
            cp.start()

    def _forwards(self, ins, outs, scr, core):
        me, sib, chips, copy = self._ctx(outs, scr)
        out = []
        for s in range(4):
            for a in range(len(ins)):
                n = self.pieces[a]
                if s < n:
                    for j in range(3):
                        out.append((copy(a, 1 + j * n + s, (*chips[j], core), me, piece=s),
                                    copy(a, 1 + 3 * n + j * n + s, (*chips[j], core), sib, piece=s)))
        return out

    def middle(self, ins, outs, scr):
        me = _my_pos()
        for arrival, forward in self._forwards(ins, outs, scr, me[2]):
            arrival.wait_recv()
            forward.start()

    def last(self, ins, outs, scr):
        me, sib, chips, copy = self._ctx(outs, scr)
        for a in range(len(ins)):
            copy(a, 0, sib, me).wait_recv()
        for _, forward in self._forwards(ins, outs, scr, sib[2]):
            forward.wait_recv()
        for cp in self._sends(ins, outs, scr):
            cp.wait_send()
        for _, forward in self._forwards(ins, outs, scr, me[2]):
            forward.wait_send()
        for a, src in enumerate(self._src(ins)):
            pltpu.make_async_copy(src, outs[a].at[_index(me)], scr[2].at[a]).wait()


class _ScatterRider:
    has_middle = False

    def __init__(self, grads):
        n = len(grads)
        self.inputs = list(grads)
        self.out_shapes = []
        for g in grads:
            if g.ndim == 3:
                self.out_shapes.append(jax.ShapeDtypeStruct((N_DEV, g.shape[0], g.shape[1] // N_DEV, g.shape[2]), g.dtype))
            else:
                self.out_shapes.append(jax.ShapeDtypeStruct((N_DEV, g.shape[0] // N_DEV, g.shape[1]), g.dtype))
        self.scratch = [pltpu.SemaphoreType.DMA((7 * n,)), pltpu.SemaphoreType.DMA((7 * n,)), pltpu.SemaphoreType.DMA((n,))]

    @staticmethod
    def _part(ref, k):
        if ref.ndim == 3:
            n = ref.shape[1] // N_DEV
            return ref.at[:, pl.ds(pl.multiple_of(k * n, 16), n)]
        n = ref.shape[0] // N_DEV
        return ref.at[pl.ds(pl.multiple_of(k * n, 16), n)]

    def _copy(self, ins, outs, scr, g, j, to, src_dev):
        return pltpu.make_async_remote_copy(
            src_ref=self._part(ins[g], to), dst_ref=outs[g].at[src_dev], send_sem=scr[0].at[7 * g + j - 1],
            recv_sem=scr[1].at[7 * g + j - 1], device_id=_peer(j), device_id_type=MESH_ID)

    def first(self, ins, outs, scr):
        me = _index(_my_pos())
        for g in range(len(ins)):
            pltpu.make_async_copy(self._part(ins[g], me), outs[g].at[me], scr[2].at[g]).start()
        for j in range(1, N_DEV):
            for g in range(len(ins)):
                self._copy(ins, outs, scr, g, j, _index(_peer(j)), me).start()

    def last(self, ins, outs, scr):
        me = _index(_my_pos())
        for j in range(1, N_DEV):
            for g in range(len(ins)):
                self._copy(ins, outs, scr, g, j, me, _index(_peer(j))).wait_recv()
        for j in range(1, N_DEV):
            for g in range(len(ins)):
                self._copy(ins, outs, scr, g, j, _index(_peer(j)), me).wait_send()
        for g in range(len(ins)):
            pltpu.make_async_copy(self._part(ins[g], me), outs[g].at[me], scr[2].at[g]).wait()


class _SiblingRider:
    has_middle = False

    def __init__(self, items):
        self.inputs = [a for a, _ in items]
        self.counts = [n for _, n in items]
        self.out_shapes = [jax.ShapeDtypeStruct(a.shape if n is None else (n,) + a.shape[1:], a.dtype) for a, n in items]
        self.scratch = [pltpu.SemaphoreType.DMA((len(items),)), pltpu.SemaphoreType.DMA((len(items),))]

    def _copies(self, ins, outs, scr):
        x, y, c = _my_pos()
        out = []
        for i, (ref, n) in enumerate(zip(ins, self.counts)):
            src = ref if n is None else ref.at[pl.ds((1 - c) * n, n)]
            out.append(pltpu.make_async_remote_copy(
                src_ref=src, dst_ref=outs[i], send_sem=scr[0].at[i], recv_sem=scr[1].at[i],
                device_id=(x, y, 1 - c), device_id_type=MESH_ID))
        return out

    def first(self, ins, outs, scr):
        for cp in self._copies(ins, outs, scr):
            cp.start()

    def last(self, ins, outs, scr):
        for cp in self._copies(ins, outs, scr):
            cp.wait()


class _ChipScatterRider:
    has_middle = False

    def __init__(self, rows, cols):
        self.nr, self.nc = len(rows), len(cols)
        self.inputs = list(rows) + list(cols)
        self.out_shapes = [jax.ShapeDtypeStruct((4, a.shape[0] // 4, a.shape[1]), a.dtype) for a in rows]
        self.out_shapes += [jax.ShapeDtypeStruct((4, 2, a.shape[0] // N_DEV, a.shape[1]), a.dtype) for a in cols]
        n = 4 * self.nr + N_DEV * self.nc
        self.scratch = [pltpu.SemaphoreType.DMA((n,)), pltpu.SemaphoreType.DMA((n,)), pltpu.SemaphoreType.DMA((n,))]

    def _pieces(self, ins, outs):
        x, y, c = _my_pos()
        q = 2 * x + y
        out = []
        for a in range(self.nr):
            n = ins[a].shape[0] // 4
            for j in range(4):
                out.append((4 * a + j, ins[a].at[pl.ds(j * n, n)], (c, j >> 1, j & 1), outs[a].at[q], 4 * a + q))
        for a in range(self.nc):
            ref, base = ins[self.nr + a], 4 * self.nr + N_DEV * a
            n = ref.shape[0] // N_DEV
            for k in range(N_DEV):
                out.append((base + k, ref.at[pl.ds(k * n, n)], (k >> 2, (k >> 1) & 1, k & 1),
                            outs[self.nr + a].at[q, c], base + 2 * q + c))
        return out

    def first(self, ins, outs, scr):
        send, recv, local = scr
        me = _index(_my_pos())
        for s, src, to, slot, r in self._pieces(ins, outs):
            mine = _index(to) == me

            @pl.when(mine)
            def _():
                pltpu.make_async_copy(src, slot, local.at[s]).start()

            @pl.when(jnp.logical_not(mine))
            def _():
                pltpu.make_async_remote_copy(src_ref=src, dst_ref=slot, send_sem=send.at[s], recv_sem=recv.at[r],
                                             device_id=to, device_id_type=MESH_ID).start()

    def last(self, ins, outs, scr):
        send, recv, local = scr
        x, y, c = _my_pos()
        me = _index((x, y, c))
        arrivals = []
        for a in range(self.nr):
            n = ins[a].shape[0] // 4
            for q in range(4):
                arrivals.append((4 * a + q, (q >> 1, q & 1, x), ins[a].at[pl.ds(0, n)], outs[a].at[q], 4 * a + 2 * y + c))
        for a in range(self.nc):
            ref, base = ins[self.nr + a], 4 * self.nr + N_DEV * a
            n = ref.shape[0] // N_DEV
            for k in range(N_DEV):
                arrivals.append((base + k, (k >> 2, (k >> 1) & 1, k & 1), ref.at[pl.ds(0, n)],
                                 outs[self.nr + a].at[k >> 1, k & 1], base + me))
        for r, sender, src, slot, s_local in arrivals:
            mine = _index(sender) == me

            @pl.when(mine)
            def _():
                pltpu.make_async_copy(src, slot, local.at[s_local]).wait()

            @pl.when(jnp.logical_not(mine))
            def _():
                pltpu.make_async_remote_copy(src_ref=src, dst_ref=slot, send_sem=send.at[r], recv_sem=recv.at[r],
                                             device_id=sender, device_id_type=MESH_ID).wait_recv()

        for s, src, to, slot, r in self._pieces(ins, outs):
            @pl.when(_index(to) != me)
            def _():
                pltpu.make_async_remote_copy(src_ref=src, dst_ref=slot, send_sem=send.at[s], recv_sem=recv.at[r],
                                             device_id=to, device_id_type=MESH_ID).wait_send()


class _Riders:
    def __init__(self, riders):
        self.riders = list(riders)
        self.inputs = [a for r in self.riders for a in r.inputs]
        self.out_shapes = [s for r in self.riders for s in r.out_shapes]
        self.scratch = [s for r in self.riders for s in r.scratch]
        self.has_middle = any(r.has_middle for r in self.riders)

    def _each(self, ins, outs, scr):
        i = o = s = 0
        for r in self.riders:
            ni, no, ns = len(r.inputs), len(r.out_shapes), len(r.scratch)
            yield r, ins[i:i + ni], outs[o:o + no], scr[s:s + ns]
            i, o, s = i + ni, o + no, s + ns

    def first(self, ins, outs, scr):
        for r, a, b, c in self._each(ins, outs, scr):
            r.first(a, b, c)

    def middle(self, ins, outs, scr):
        for r, a, b, c in self._each(ins, outs, scr):
            if r.has_middle:
                r.middle(a, b, c)

    def last(self, ins, outs, scr):
        for r, a, b, c in self._each(ins, outs, scr):
            r.last(a, b, c)

    def split(self, outs):
        res, o = [], 0
        for r in self.riders:
            res.append(list(outs[o:o + len(r.out_shapes)]))
            o += len(r.out_shapes)
        return res


def _run(body, *, name, grid, in_specs, out_specs, out_shape, scratch_shapes, args, rider=None, params=None, prefetch=()):
    params = ARB1 if params is None else params
    npf = len(prefetch)
    ni, no, ns = len(in_specs), len(out_shape), len(scratch_shapes)
    ri, ro = (len(rider.inputs), len(rider.out_shapes)) if rider is not None else (0, 0)

    def wrapped(*refs):
        pf, refs = refs[:npf], refs[npf:]
        cut = [ni, ni + ri, ni + ri + no, ni + ri + no + ro, ni + ri + no + ro + ns]
        a, b, c, d, e, f = (refs[lo:hi] for lo, hi in zip([0] + cut, cut + [len(refs)]))
        if rider is not None:
            ids = [pl.program_id(k) for k in range(len(grid))]
            at_first = functools.reduce(jnp.logical_and, [i == 0 for i in ids])
            at_last = functools.reduce(jnp.logical_and, [i == n - 1 for i, n in zip(ids, grid)])

            @pl.when(at_first)
            def _():
                rider.first(b, d, f)

            if rider.has_middle:
                @pl.when(at_last)
                def _():
                    rider.middle(b, d, f)

        body(*pf, *a, *c, *e)

        if rider is not None:
            @pl.when(at_last)
            def _():
                rider.last(b, d, f)

    extra_shapes = rider.out_shapes if rider is not None else []
    extra_scratch = rider.scratch if rider is not None else []
    extra_inputs = rider.inputs if rider is not None else []
    all_in, all_out = list(in_specs) + [ANY] * ri, list(out_specs) + [ANY] * ro
    all_scratch = list(scratch_shapes) + extra_scratch
    if npf:
        outs = pl.pallas_call(
            wrapped, name=name, out_shape=list(out_shape) + extra_shapes,
            grid_spec=pltpu.PrefetchScalarGridSpec(num_scalar_prefetch=npf, grid=grid, in_specs=all_in, out_specs=all_out,
                                                   scratch_shapes=all_scratch),
            compiler_params=params)(*prefetch, *args, *extra_inputs)
    else:
        outs = pl.pallas_call(
            wrapped, name=name, grid=grid, in_specs=all_in, out_specs=all_out, out_shape=list(out_shape) + extra_shapes,
            scratch_shapes=all_scratch, compiler_params=params)(*args, *extra_inputs)
    return list(outs[:no]), list(outs[no:])


def _mod_fwd(c_all, w_mod):
    ncol = w_mod.shape[-1]

    def body(c_ref, w_ref, act_ref, out_ref):
        c = c_ref[...]
        act = c * jax.nn.sigmoid(c)
        act_ref[...] = act
        out_ref[0] = _dot_nn(act.astype(BF16), w_ref[0].astype(BF16))

    return pl.pallas_call(
        body, name="mod_fwd", grid=(2,),
        out_shape=[jax.ShapeDtypeStruct((N_DEV, D), F32), jax.ShapeDtypeStruct((2, N_DEV, ncol), F32)],
        in_specs=[pl.BlockSpec((N_DEV, D), lambda l: (0, 0)), pl.BlockSpec((1, D, ncol), lambda l: (l, 0, 0))],
        out_specs=[pl.BlockSpec((N_DEV, D), lambda l: (0, 0)), pl.BlockSpec((1, N_DEV, ncol), lambda l: (l, 0, 0))],
        compiler_params=ARB1,
    )(c_all, w_mod)


def _ffn_fwd(x, p, win_all, wout_all, f, rider=None):
    s = x.shape[0]
    nt = s // TM

    def body(x_ref, p_ref, win_hbm, wout_hbm, xo_ref, h_ref, g_ref, u_ref, y_ref, win, wout, act, sems):
        @pl.when(pl.program_id(0) == 0)
        def _():
            cps = _load_rows(win_hbm, None, win, sems, 0) + _load_rows(wout_hbm, None, wout, sems, N_DEV)
            for cp in cps:
                cp.start()
            for cp in cps:
                cp.wait()

        sh, sc, gate, gn = _mod_rows(p_ref)
        x = x_ref[...]
        hb = _modulate(x, gn, sh, sc).astype(BF16)
        h_ref[...] = hb
        for c in range(NCH):
            g = _dot_nt(hb, win[c * HC:(c + 1) * HC, :])
            u = _dot_nt(hb, win[DFF + c * HC:DFF + (c + 1) * HC, :])
            g_ref[c] = g.astype(BF16)
            u_ref[c] = u.astype(BF16)
            act[:, c * HC:(c + 1) * HC] = ((g * jax.nn.sigmoid(g)) * u).astype(BF16)
        y = _dot_nn(act[...], wout[...])
        y_ref[...] = y
        xo_ref[...] = x + (0.5 * gate) * y

    tile = pl.BlockSpec((TM, D), lambda i: (i, 0))
    chunks = pl.BlockSpec((NCH, TM, HC), lambda i: (0, i, 0))
    return _run(
        body, name=f"ffn_fwd_{f}", grid=(nt,),
        out_shape=[jax.ShapeDtypeStruct((s, D), F32), jax.ShapeDtypeStruct((s, D), BF16),
                   jax.ShapeDtypeStruct((NCH, s, HC), BF16), jax.ShapeDtypeStruct((NCH, s, HC), BF16),
                   jax.ShapeDtypeStruct((s, D), F32)],
        in_specs=[tile, pl.BlockSpec((8, D), lambda i: (0, 0)), ANY, ANY],
        out_specs=[tile, tile, chunks, chunks, tile],
        scratch_shapes=[pltpu.VMEM((2 * DFF, D), BF16), pltpu.VMEM((DFF, D), BF16), pltpu.VMEM((TM, DFF), BF16),
                        pltpu.SemaphoreType.DMA((2 * N_DEV,))],
        args=(x, p, win_all, wout_all), rider=rider)


def _swiglu_bwd_acts(dyb, g_ref, u_ref, wout, a_ref, dgu_ref, dgu):
    for c in range(NCH):
        da = _dot_nt(dyb, wout[c * HC:(c + 1) * HC, :])
        g = g_ref[c].astype(F32)
        u = u_ref[c].astype(F32)
        sg = jax.nn.sigmoid(g)
        si = g * sg
        dg = ((da * u) * (sg * (1.0 + g * (1.0 - sg)))).astype(BF16)
        du = (da * si).astype(BF16)
        a_ref[c] = (si * u).astype(BF16)
        dgu_ref[c] = dg
        dgu_ref[NCH + c] = du
        if dgu is not None:
            dgu[:, c * HC:(c + 1) * HC] = dg
            dgu[:, DFF + c * HC:DFF + (c + 1) * HC] = du


def _ffn_bwd(dxo, x, p, g3, u3, y, win_all, wout_all, f, rider=None):
    s = x.shape[0]
    nt = s // TM

    def body(dxo_ref, x_ref, p_ref, g_ref, u_ref, y_ref, win_hbm, wout_hbm,
             dx_ref, dgu_ref, a_ref, dy_ref, st_ref, win, wout, dgu, stats, sems):
        i = pl.program_id(0)

        @pl.when(i == 0)
        def _():
            cps = _load_rows(win_hbm, None, win, sems, 0) + _load_rows(wout_hbm, None, wout, sems, N_DEV)
            for cp in cps:
                cp.start()
            stats[...] = jnp.zeros_like(stats)
            for cp in cps:
                cp.wait()

        sh, sc, gate, gn = _mod_rows(p_ref)
        x = x_ref[...]
        dxo = dxo_ref[...]
        dyb = ((0.5 * gate) * dxo).astype(BF16)
        dy_ref[...] = dyb
        stats[2] += _colsum8((0.5 * dxo) * y_ref[...])
        _swiglu_bwd_acts(dyb, g_ref, u_ref, wout, a_ref, dgu_ref, dgu)
        dh = _dot_nn(dgu[...], win[...])
        dx_ref[...] = dxo + _modulate_bwd(dh, x, gn, sc, stats)

        @pl.when(i == nt - 1)
        def _():
            _stats_out(stats, st_ref)

    tile = pl.BlockSpec((TM, D), lambda i: (i, 0))
    chunks = pl.BlockSpec((NCH, TM, HC), lambda i: (0, i, 0))
    small = pl.BlockSpec((8, D), lambda i: (0, 0))
    return _run(
        body, name=f"ffn_bwd_{f}", grid=(nt,),
        out_shape=[jax.ShapeDtypeStruct((s, D), F32), jax.ShapeDtypeStruct((2 * NCH, s, HC), BF16),
                   jax.ShapeDtypeStruct((NCH, s, HC), BF16), jax.ShapeDtypeStruct((s, D), BF16),
                   jax.ShapeDtypeStruct((8, D), F32)],
        in_specs=[tile, tile, small, chunks, chunks, tile, ANY, ANY],
        out_specs=[tile, pl.BlockSpec((2 * NCH, TM, HC), lambda i: (0, i, 0)), chunks, tile, small],
        scratch_shapes=[pltpu.VMEM((2 * DFF, D), BF16), pltpu.VMEM((DFF, D), BF16), pltpu.VMEM((TM, 2 * DFF), BF16),
                        pltpu.VMEM((4, 8, D), F32), pltpu.SemaphoreType.DMA((2 * N_DEV,))],
        args=(dxo, x, p, g3, u3, y, win_all, wout_all), rider=rider)


def _ffn_bwd_acts(dxo, p, g3, u3, y, wout_all, f, rider=None):
    s = dxo.shape[0]
    nt = s // TM

    def body(dxo_ref, p_ref, g_ref, u_ref, y_ref, wout_hbm, dgu_ref, a_ref, dy_ref, st_ref, wout, stat, sems):
        i = pl.program_id(0)

        @pl.when(i == 0)
        def _():
            cps = _load_rows(wout_hbm, None, wout, sems, 0)
            for cp in cps:
                cp.start()
            stat[...] = jnp.zeros_like(stat)
            for cp in cps:
                cp.wait()

        gate = p_ref[2:3, :]
        dxo = dxo_ref[...]
        dyb = ((0.5 * gate) * dxo).astype(BF16)
        dy_ref[...] = dyb
        stat[...] += _colsum8((0.5 * dxo) * y_ref[...])
        _swiglu_bwd_acts(dyb, g_ref, u_ref, wout, a_ref, dgu_ref, None)

        @pl.when(i == nt - 1)
        def _():
            row = jnp.sum(stat[...], axis=0, keepdims=True)
            st_ref[...] = jnp.concatenate([jnp.zeros((2, D), F32), row, jnp.zeros((5, D), F32)], axis=0)

    tile = pl.BlockSpec((TM, D), lambda i: (i, 0))
    chunks = pl.BlockSpec((NCH, TM, HC), lambda i: (0, i, 0))
    small = pl.BlockSpec((8, D), lambda i: (0, 0))
    return _run(
        body, name=f"ffn_bwd_acts_{f}", grid=(nt,),
        out_shape=[jax.ShapeDtypeStruct((2 * NCH, s, HC), BF16), jax.ShapeDtypeStruct((NCH, s, HC), BF16),
                   jax.ShapeDtypeStruct((s, D), BF16), jax.ShapeDtypeStruct((8, D), F32)],
        in_specs=[tile, small, chunks, chunks, tile, ANY],
        out_specs=[pl.BlockSpec((2 * NCH, TM, HC), lambda i: (0, i, 0)), chunks, tile, small],
        scratch_shapes=[pltpu.VMEM((DFF, D), BF16), pltpu.VMEM((8, D), F32), pltpu.SemaphoreType.DMA((N_DEV,))],
        args=(dxo, p, g3, u3, y, wout_all), rider=rider)


def _ffn_bwd_dx(dxo, x, p, dgu3, gate_stats, win_all, f, rider=None):
    s = x.shape[0]
    nt = s // TM

    def body(dxo_ref, x_ref, p_ref, dgu_ref, gs_ref, win_hbm, dx_ref, st_ref, win, dgu, stats, sems):
        i = pl.program_id(0)

        @pl.when(i == 0)
        def _():
            cps = _load_rows(win_hbm, None, win, sems, 0)
            for cp in cps:
                cp.start()
            stats[...] = jnp.zeros_like(stats)
            for cp in cps:
                cp.wait()

        _, sc, _, gn = _mod_rows(p_ref)
        for c in range(2 * NCH):
            dgu[:, c * HC:(c + 1) * HC] = dgu_ref[c]
        dh = _dot_nn(dgu[...], win[...])
        dx_ref[...] = dxo_ref[...] + _modulate_bwd(dh, x_ref[...], gn, sc, stats)

        @pl.when(i == nt - 1)
        def _():
            _stats_out(stats, st_ref)
            st_ref[2:3, :] = gs_ref[2:3, :]

    tile = pl.BlockSpec((TM, D), lambda i: (i, 0))
    small = pl.BlockSpec((8, D), lambda i: (0, 0))
    return _run(
        body, name=f"ffn_bwd_dx_{f}", grid=(nt,),
        out_shape=[jax.ShapeDtypeStruct((s, D), F32), jax.ShapeDtypeStruct((8, D), F32)],
        in_specs=[tile, tile, small, pl.BlockSpec((2 * NCH, TM, HC), lambda i: (0, i, 0)), small, ANY],
        out_specs=[tile, small],
        scratch_shapes=[pltpu.VMEM((2 * DFF, D), BF16), pltpu.VMEM((TM, 2 * DFF), BF16), pltpu.VMEM((4, 8, D), F32),
                        pltpu.SemaphoreType.DMA((N_DEV,))],
        args=(dxo, x, p, dgu3, gate_stats, win_all), rider=rider)


def _wgrad_pair(own3, sib3, own_r, sib_r, sel, n, col_split, name, rider=None):
    nj, s, _ = own3.shape
    nw = own_r.shape[1] // 2 if col_split else own_r.shape[1]
    steps = nj if col_split else n

    def body(sel_ref, lo_ref, ls_ref, ro_ref, rs_ref, o_ref):
        o_ref[...] = (_dot_tn(lo_ref[0], ro_ref[...]) + _dot_tn(ls_ref[0], rs_ref[...])).astype(BF16)

    rspec = pl.BlockSpec((s, nw), lambda j, sel_ref: (0, sel_ref[1]))
    outs, rode = _run(
        body, name=name, grid=(steps,),
        out_shape=[jax.ShapeDtypeStruct((steps * HC, nw), BF16)],
        in_specs=[pl.BlockSpec((1, s, HC), lambda j, sel_ref: (sel_ref[0] + j, 0, 0)),
                  pl.BlockSpec((1, s, HC), lambda j, sel_ref: (j, 0, 0)), rspec, rspec],
        out_specs=[pl.BlockSpec((HC, nw), lambda j, sel_ref: (j, 0))],
        scratch_shapes=[], args=(own3, sib3, own_r, sib_r), rider=rider, prefetch=(sel,))
    return outs[0], rode


def _gating(proj, nv, ws_ref, bst):
    u, v = proj[:, 0:D_A], proj[:, D_A:2 * D_A]
    gu, gv = _gelu(u), _gelu(v)
    mu = jnp.mean(gv, axis=-1, keepdims=True)
    dv = gv - mu
    rstd = lax.rsqrt(jnp.mean(dv * dv, axis=-1, keepdims=True) + EPS)
    vhat = dv * rstd
    vn = vhat * nv
    r = lax.broadcasted_iota(jnp.int32, (CHUNK, CHUNK), 0)
    c = lax.broadcasted_iota(jnp.int32, (CHUNK, CHUNK), 1)
    wm = [jnp.where(r >= c, ws_ref[hd], 0.0).astype(BF16) for hd in range(A_HEADS)]
    vnb = vn.astype(BF16)
    rows = []
    for n in range(proj.shape[0] // CHUNK):
        blocks = []
        for hd in range(A_HEADS):
            blk = vnb[n * CHUNK:(n + 1) * CHUNK, hd * CHUNK:(hd + 1) * CHUNK]
            blocks.append(_dot_nn(wm[hd], blk) + bst[:, hd:hd + 1])
        rows.append(jnp.concatenate(blocks, axis=1))
    z = jnp.concatenate(rows, axis=0)
    return u, v, gu, rstd, vhat, vnb, wm, z


def _conv(proj, cw, prev_xp):
    bg = proj[:, 2 * D_A:2 * D_A + D_B]
    cg = proj[:, 2 * D_A + D_B:2 * D_A + 2 * D_B]
    xb = proj[:, 2 * D_A + 2 * D_B:]
    xp = cg * xb
    x1 = _shift_down(xp, 1, prev_xp)
    x2 = _shift_down(xp, 2, prev_xp)
    conv = cw[0:1, :] * x2 + cw[1:2, :] * x1 + cw[2:3, :] * xp
    return bg, cg, xb, xp, x1, x2, conv


def _ab_fwd(x, p, abin_all, about_all, nv, ws, bst, cw, rider=None):
    s = x.shape[0]
    nt = s // TM

    def body(x_ref, p_ref, abin_hbm, about_hbm, nv_ref, ws_ref, bst_ref, cw_ref,
             xo_ref, h_ref, proj_ref, out_ref, abin, about, prev, sems):
        @pl.when(pl.program_id(0) == 0)
        def _():
            cps = _load_rows(abin_hbm, None, abin, sems, 0) + _load_rows(about_hbm, None, about, sems, N_DEV)
            for cp in cps:
                cp.start()
            prev[...] = jnp.zeros_like(prev)
            for cp in cps:
                cp.wait()

        sh, sc, gate, gn = _mod_rows(p_ref)
        x = x_ref[...]
        hb = _modulate(x, gn, sh, sc).astype(BF16)
        h_ref[...] = hb
        proj = _dot_nt(hb, abin[...])
        proj_ref[...] = proj
        _, _, gu, _, _, _, _, z = _gating(proj, nv_ref[...], ws_ref, bst_ref[...])
        bg, _, _, xp, _, _, conv = _conv(proj, cw_ref[...], prev[...])
        prev[...] = xp[TM - CONV_HALO:, :]
        cat = jnp.concatenate([gu * z, bg * conv], axis=1).astype(BF16)
        out = _dot_nn(cat, about[...])
        out_ref[...] = out
        xo_ref[...] = x + gate * out

    tile = pl.BlockSpec((TM, D), lambda i: (i, 0))
    full = lambda a: pl.BlockSpec(a.shape, lambda i: (0,) * a.ndim)
    return _run(
        body, name="ab_fwd", grid=(nt,),
        out_shape=[jax.ShapeDtypeStruct((s, D), F32), jax.ShapeDtypeStruct((s, D), BF16),
                   jax.ShapeDtypeStruct((s, D_AB), F32), jax.ShapeDtypeStruct((s, D), F32)],
        in_specs=[tile, pl.BlockSpec((8, D), lambda i: (0, 0)), ANY, ANY, full(nv), full(ws), full(bst), full(cw)],
        out_specs=[tile, tile, pl.BlockSpec((TM, D_AB), lambda i: (i, 0)), tile],
        scratch_shapes=[pltpu.VMEM((D_AB, D), BF16), pltpu.VMEM((D, D), BF16), pltpu.VMEM((CONV_HALO, D_B), F32),
                        pltpu.SemaphoreType.DMA((2 * N_DEV,))],
        args=(x, p, abin_all, about_all, nv, ws, bst, cw), rider=rider)


def _ab_bwd(dxo, x, p, proj, out, abin_all, about_all, nv, ws, bst, cw, rider=None):
    s = x.shape[0]
    nt = s // TM
    npj = D_AB // HC

    def body(dxo_ref, x_ref, p_ref, proj_ref, halo_ref, out_ref, abin_hbm, about_hbm, nv_ref, ws_ref, bst_ref, cw_ref,
             dx_ref, dproj_ref, cat_ref, dy_ref, st_ref, dnv_ref, dws_ref, dbs_ref, dcw_ref,
             abin, about, nxt, stats, dnv, dws, dbs, dcw, sems):
        i = pl.program_id(0)
        ti = nt - 1 - i

        @pl.when(i == 0)
        def _():
            cps = _load_rows(abin_hbm, None, abin, sems, 0) + _load_rows(about_hbm, None, about, sems, N_DEV)
            for cp in cps:
                cp.start()
            for z in (nxt, stats, dnv, dws, dbs, dcw):
                z[...] = jnp.zeros_like(z)
            for cp in cps:
                cp.wait()

        sh, sc, gate, gn = _mod_rows(p_ref)
        x = x_ref[...]
        dxo = dxo_ref[...]
        dyb = (gate * dxo).astype(BF16)
        dy_ref[...] = dyb
        stats[2] += _colsum8(dxo * out_ref[...])
        dcat = _dot_nt(dyb, about[...])
        dya, dyb2 = dcat[:, 0:D_A], dcat[:, D_A:]

        proj = proj_ref[...]
        nvv = nv_ref[...]
        u, v, gu, rstd, vhat, vnb, wm, z = _gating(proj, nvv, ws_ref, bst_ref[...])
        dgu = dya * z
        dzb = (dya * gu).astype(BF16)
        dz32 = dya * gu
        rows = []
        for n in range(TM // CHUNK):
            blocks = []
            for hd in range(A_HEADS):
                sl = (slice(n * CHUNK, (n + 1) * CHUNK), slice(hd * CHUNK, (hd + 1) * CHUNK))
                dbs[hd] += dz32[sl]
                dws[hd] += _dot_nt(dzb[sl], vnb[sl])
                blocks.append(_dot_tn(wm[hd], dzb[sl]))
            rows.append(jnp.concatenate(blocks, axis=1))
        dvn = jnp.concatenate(rows, axis=0)
        dnv[...] += _colsum8(dvn * vhat)
        dvh = dvn * nvv
        dgv = rstd * (dvh - jnp.mean(dvh, axis=-1, keepdims=True) - vhat * jnp.mean(dvh * vhat, axis=-1, keepdims=True))
        du = dgu * _gelu_grad(u)
        dv = dgv * _gelu_grad(v)

        halo = halo_ref[...]
        prev_xp = jnp.where(ti > 0, halo[:, 2 * D_A + D_B:2 * D_A + 2 * D_B] * halo[:, 2 * D_A + 2 * D_B:], 0.0)
        cwv = cw_ref[...]
        bg, cg, xb, xp, x1, x2, conv = _conv(proj, cwv, prev_xp)
        dbg = dyb2 * conv
        dconv = dyb2 * bg
        dcw[...] += jnp.concatenate(
            [jnp.sum(_colsum8(dconv * t), axis=0, keepdims=True) for t in (x2, x1, xp)] + [jnp.zeros((5, D_B), F32)], axis=0)
        nx = nxt[...]
        dxp = cwv[2:3, :] * dconv + cwv[1:2, :] * _shift_up(dconv, 1, nx) + cwv[0:1, :] * _shift_up(dconv, 2, nx)
        nxt[...] = dconv[0:CONV_HALO, :]
        dcg = dxp * xb
        dxb = dxp * cg

        dproj = jnp.concatenate([du, dv, dbg, dcg, dxb], axis=1).astype(BF16)
        for k in range(npj):
            dproj_ref[k] = dproj[:, k * HC:(k + 1) * HC]
        cat = jnp.concatenate([gu * z, bg * conv], axis=1).astype(BF16)
        for k in range(D // HC):
            cat_ref[k] = cat[:, k * HC:(k + 1) * HC]
        dh = _dot_nn(dproj, abin[...])
        dx_ref[...] = dxo + _modulate_bwd(dh, x, gn, sc, stats)

        @pl.when(i == nt - 1)
        def _():
            _stats_out(stats, st_ref)
            dnv_ref[...] = jnp.concatenate([jnp.sum(dnv[...], axis=0, keepdims=True), jnp.zeros((7, D_A), F32)], axis=0)
            r = lax.broadcasted_iota(jnp.int32, (CHUNK, CHUNK), 0)
            c = lax.broadcasted_iota(jnp.int32, (CHUNK, CHUNK), 1)
            for hd in range(A_HEADS):
                dws_ref[hd] = jnp.where(r >= c, dws[hd], 0.0)
                dbs_ref[hd] = jnp.broadcast_to(jnp.sum(dbs[hd], axis=1, keepdims=True), (CHUNK, CHUNK))
            dcw_ref[...] = dcw[...]

    rev = pl.BlockSpec((TM, D), lambda i: (nt - 1 - i, 0))
    small = pl.BlockSpec((8, D), lambda i: (0, 0))
    full = lambda a: pl.BlockSpec(a.shape, lambda i: (0,) * a.ndim)
    hpt = TM // CONV_HALO
    fixed = lambda shape: pl.BlockSpec(shape, lambda i: (0,) * len(shape))
    return _run(
        body, name="ab_bwd", grid=(nt,),
        out_shape=[jax.ShapeDtypeStruct((s, D), F32), jax.ShapeDtypeStruct((npj, s, HC), BF16),
                   jax.ShapeDtypeStruct((D // HC, s, HC), BF16), jax.ShapeDtypeStruct((s, D), BF16),
                   jax.ShapeDtypeStruct((8, D), F32), jax.ShapeDtypeStruct((8, D_A), F32),
                   jax.ShapeDtypeStruct((A_HEADS, CHUNK, CHUNK), F32), jax.ShapeDtypeStruct((A_HEADS, CHUNK, CHUNK), F32),
                   jax.ShapeDtypeStruct((8, D_B), F32)],
        in_specs=[rev, rev, small,
                  pl.BlockSpec((TM, D_AB), lambda i: (nt - 1 - i, 0)),
                  pl.BlockSpec((CONV_HALO, D_AB), lambda i: (jnp.maximum((nt - 1 - i) * hpt - 1, 0), 0)),
                  rev, ANY, ANY, full(nv), full(ws), full(bst), full(cw)],
        out_specs=[rev, pl.BlockSpec((npj, TM, HC), lambda i: (0, nt - 1 - i, 0)),
                   pl.BlockSpec((D // HC, TM, HC), lambda i: (0, nt - 1 - i, 0)), rev,
                   small, fixed((8, D_A)), fixed((A_HEADS, CHUNK, CHUNK)), fixed((A_HEADS, CHUNK, CHUNK)), fixed((8, D_B))],
        scratch_shapes=[pltpu.VMEM((D_AB, D), BF16), pltpu.VMEM((D, D), BF16), pltpu.VMEM((CONV_HALO, D_B), F32),
                        pltpu.VMEM((4, 8, D), F32), pltpu.VMEM((8, D_A), F32),
                        pltpu.VMEM((A_HEADS, CHUNK, CHUNK), F32), pltpu.VMEM((A_HEADS, CHUNK, CHUNK), F32),
                        pltpu.VMEM((8, D_B), F32), pltpu.SemaphoreType.DMA((2 * N_DEV,))],
        args=(dxo, x, p, proj, proj, out, abin_all, about_all, nv, ws, bst, cw), rider=rider)


def _pool_counts(first_token, rows):
    t = (first_token + lax.broadcasted_iota(jnp.int32, (rows, 1), 0) + 1).astype(F32)
    lane = lax.broadcasted_iota(jnp.int32, (1, D), 1)
    w = jnp.where(lane < POOL_G, 2.0, jnp.where(lane < 2 * POOL_G, 4.0, jnp.where(lane < 3 * POOL_G, 8.0, 16.0)))
    return jnp.minimum(t, w)


def _window_sums(ext, n_keep, lead, back):
    n = ext.shape[0]
    sh = (lambda v, k: pltpu.roll(v, k, 0)) if back else (lambda v, k: pltpu.roll(v, n - k, 0))
    s2 = ext + sh(ext, 1)
    s4 = s2[:, POOL_G:] + sh(s2[:, POOL_G:], 2)
    s8 = s4[:, POOL_G:] + sh(s4[:, POOL_G:], 4)
    s16 = s8[:, POOL_G:] + sh(s8[:, POOL_G:], 8)
    keep = slice(lead, lead + n_keep)
    return jnp.concatenate([s2[keep, 0:POOL_G], s4[keep, 0:POOL_G], s8[keep, 0:POOL_G], s16[keep, :]], axis=1)


def _pool_fwd(x, p, pool_all, pscale):
    s = x.shape[0]
    nt = s // TM
    ng = D // POOL_G

    def body(x_ref, p_ref, wg_ref, ps_ref, xo_ref, pb_ref, op_ref, prev):
        i = pl.program_id(0)

        @pl.when(i == 0)
        def _():
            prev[...] = jnp.zeros_like(prev)

        sh, sc, gate, gn = _mod_rows(p_ref)
        x = x_ref[...]
        h = _modulate(x, gn, sh, sc)
        win = _window_sums(jnp.concatenate([prev[...], h], axis=0), TM, POOL_HALO, True)
        prev[...] = h[TM - POOL_HALO:, :]
        pb = (win / _pool_counts(i * TM, TM) - h).astype(BF16)
        pb_ref[...] = pb
        op = jnp.concatenate(
            [_dot_nn(pb[:, g * POOL_G:(g + 1) * POOL_G], wg_ref[:, g].reshape(POOL_G, POOL_G)) for g in range(ng)], axis=1)
        op_ref[...] = op
        xo_ref[...] = x + gate * (op * ps_ref[...])

    tile = pl.BlockSpec((TM, D), lambda i: (i, 0))
    return pl.pallas_call(
        body, name="pool_fwd", grid=(nt,),
        out_shape=[jax.ShapeDtypeStruct((s, D), F32), jax.ShapeDtypeStruct((s, D), BF16), jax.ShapeDtypeStruct((s, D), F32)],
        in_specs=[tile, pl.BlockSpec((8, D), lambda i: (0, 0)),
                  pl.BlockSpec(pool_all.shape, lambda i: (0, 0, 0, 0)), pl.BlockSpec((1, D), lambda i: (0, 0))],
        out_specs=[tile, tile, tile],
        scratch_shapes=[pltpu.VMEM((POOL_HALO, D), F32)],
        compiler_params=ARB1,
    )(x, p, pool_all, pscale)


def _pool_bwd(dxo, x, p, pb, op, pool_all, pscale):
    s = x.shape[0]
    nt = s // TM
    ng = D // POOL_G

    def body(dxo_ref, x_ref, p_ref, pb_ref, op_ref, wg_ref, ps_ref,
             dx_ref, st_ref, dps_ref, dwg_ref, nxt, stats, dps, dwg):
        i = pl.program_id(0)
        ti = nt - 1 - i

        @pl.when(i == 0)
        def _():
            for z in (nxt, stats, dps, dwg):
                z[...] = jnp.zeros_like(z)

        sh, sc, gate, gn = _mod_rows(p_ref)
        x = x_ref[...]
        dxo = dxo_ref[...]
        ps = ps_ref[...]
        op = op_ref[...]
        dmo = gate * dxo
        stats[2] += _colsum8(dxo * (op * ps))
        dps[...] += _colsum8(dmo * op)
        dopb = (dmo * ps).astype(BF16)
        pbv = pb_ref[...]
        dps_parts = []
        for g in range(ng):
            sl = slice(g * POOL_G, (g + 1) * POOL_G)
            dps_parts.append(_dot_nt(dopb[:, sl], wg_ref[:, g].reshape(POOL_G, POOL_G)))
            dwg[g] += _dot_tn(pbv[:, sl], dopb[:, sl])
        dp = jnp.concatenate(dps_parts, axis=1)
        q = dp / _pool_counts(ti * TM, TM)
        wsum = _window_sums(jnp.concatenate([q, nxt[...]], axis=0), TM, 0, False)
        nxt[...] = q[0:POOL_HALO, :]
        dx_ref[...] = dxo + _modulate_bwd(wsum - dp, x, gn, sc, stats)

        @pl.when(i == nt - 1)
        def _():
            _stats_out(stats, st_ref)
            dps_ref[...] = jnp.concatenate([jnp.sum(dps[...], axis=0, keepdims=True), jnp.zeros((7, D), F32)], axis=0)
            dwg_ref[...] = dwg[...].astype(BF16)

    rev = pl.BlockSpec((TM, D), lambda i: (nt - 1 - i, 0))
    small = pl.BlockSpec((8, D), lambda i: (0, 0))
    return pl.pallas_call(
        body, name="pool_bwd", grid=(nt,),
        out_shape=[jax.ShapeDtypeStruct((s, D), F32), jax.ShapeDtypeStruct((8, D), F32), jax.ShapeDtypeStruct((8, D), F32),
                   jax.ShapeDtypeStruct((ng, POOL_G, POOL_G), BF16)],
        in_specs=[rev, rev, small, rev, rev,
                  pl.BlockSpec(pool_all.shape, lambda i: (0, 0, 0, 0)), pl.BlockSpec((1, D), lambda i: (0, 0))],
        out_specs=[rev, small, small, pl.BlockSpec((ng, POOL_G, POOL_G), lambda i: (0, 0, 0))],
        scratch_shapes=[pltpu.VMEM((POOL_HALO, D), F32), pltpu.VMEM((4, 8, D), F32), pltpu.VMEM((8, D), F32),
                        pltpu.VMEM((ng, POOL_G, POOL_G), F32)],
        compiler_params=ARB1,
    )(dxo, x, p, pb, op, pool_all, pscale)


def _head(x, fg, tgt):
    s = x.shape[0]
    nt = s // TM

    def body(x_ref, fg_ref, t_ref, dx_ref, loss_ref, dfg_ref, sq, dfg):
        i = pl.program_id(0)

        @pl.when(i == 0)
        def _():
            sq[...] = jnp.zeros_like(sq)
            dfg[...] = jnp.zeros_like(dfg)

        x = x_ref[...]
        g = fg_ref[...]
        r = lax.rsqrt(jnp.mean(x * x, axis=-1, keepdims=True) + EPS)
        xn = x * r
        e = xn * g - t_ref[...]
        sq[...] += _colsum8(e * e)
        dy = e * (1.0 / D)
        dfg[...] += _colsum8(dy * xn)
        dxn = dy * g
        dx_ref[...] = r * (dxn - xn * jnp.mean(dxn * xn, axis=-1, keepdims=True))

        @pl.when(i == nt - 1)
        def _():
            total = jnp.sum(jnp.sum(sq[...], axis=0, keepdims=True), axis=1, keepdims=True)
            loss_ref[...] = jnp.broadcast_to(total * (0.5 / D), loss_ref.shape)
            dfg_ref[...] = jnp.concatenate([jnp.sum(dfg[...], axis=0, keepdims=True), jnp.zeros((7, D), F32)], axis=0)

    tile = pl.BlockSpec((TM, D), lambda i: (i, 0))
    return pl.pallas_call(
        body, name="head", grid=(nt,),
        out_shape=[jax.ShapeDtypeStruct((s, D), F32), jax.ShapeDtypeStruct((8, 128), F32), jax.ShapeDtypeStruct((8, D), F32)],
        in_specs=[tile, pl.BlockSpec((1, D), lambda i: (0, 0)), tile],
        out_specs=[tile, pl.BlockSpec((8, 128), lambda i: (0, 0)), pl.BlockSpec((8, D), lambda i: (0, 0))],
        scratch_shapes=[pltpu.VMEM((8, D), F32), pltpu.VMEM((8, D), F32)],
        compiler_params=ARB1,
    )(x, fg, tgt)


def _adamw_math(w, g, m, v):
    m = ADAM_B1 * m + (1.0 - ADAM_B1) * g
    v = ADAM_B2 * v + (1.0 - ADAM_B2) * (g * g)
    m_hat = m / (1.0 - ADAM_B1 ** ADAM_STEP)
    v_hat = v / (1.0 - ADAM_B2 ** ADAM_STEP)
    delta = -ADAM_LR * (m_hat / (jnp.sqrt(v_hat) + ADAM_EPS) + ADAM_WD * w)
    return delta, m, v


def _finish(parts, w, m, v, rb, name, halves=False, rider=None):
    nf, r, c = w.shape
    npart = parts[0].shape[0]

    def body(*refs):
        p_refs = refs[:nf]
        w_ref, m_ref, v_ref, g_ref, d_ref, mo_ref, vo_ref = refs[nf:]
        for f in range(nf):
            @pl.when(pl.program_id(0) == f)
            def _():
                g = p_refs[f][0].astype(F32)
                for k in range(1, npart):
                    g = g + p_refs[f][k].astype(F32)
                if halves:
                    g = jnp.concatenate([g[0], g[1]], axis=1)
                g_ref[0] = g
                d_ref[0], mo_ref[0], vo_ref[0] = _adamw_math(w_ref[0], g, m_ref[0], v_ref[0])

    blk = pl.BlockSpec((1, rb, c), lambda f, i: (f, i, 0))

    def pblk(mine):
        if halves:
            return pl.BlockSpec((npart, 2, rb, c // 2), lambda f, i: (0, 0, jnp.where(f == mine, i, 0), 0))
        return pl.BlockSpec((npart, rb, c), lambda f, i: (0, jnp.where(f == mine, i, 0), 0))

    return _run(
        body, name=name, grid=(nf, r // rb),
        out_shape=[jax.ShapeDtypeStruct(w.shape, F32)] * 4,
        in_specs=[pblk(f) for f in range(nf)] + [blk, blk, blk], out_specs=[blk] * 4, scratch_shapes=[],
        args=(*parts, w, m, v), rider=rider,
        params=pltpu.CompilerParams(dimension_semantics=("arbitrary", "arbitrary"), vmem_limit_bytes=VMEM_LIMIT))


def _sum_small(parts, late, late_rows):
    def body(p_ref, l_ref, o_ref):
        acc, acc_l = p_ref[0], l_ref[0]
        for k in range(1, N_DEV):
            acc, acc_l = acc + p_ref[k], acc_l + l_ref[k]
        o_ref[...] = acc
        for a, b, r in late_rows:
            o_ref[r:r + b - a, :] += acc_l[a:b, :]

    return pl.pallas_call(body, name="sum_small", out_shape=jax.ShapeDtypeStruct(parts.shape[1:], F32),
                          in_specs=[VMEM_SPEC, VMEM_SPEC], out_specs=VMEM_SPEC)(parts, late)


def _adamw(w, g, m, v, name):
    def body(w_ref, g_ref, m_ref, v_ref, d_ref, mo_ref, vo_ref):
        d_ref[...], mo_ref[...], vo_ref[...] = _adamw_math(w_ref[...], g_ref[...], m_ref[...], v_ref[...])

    return pl.pallas_call(
        body, name=name, out_shape=[jax.ShapeDtypeStruct(w.shape, F32)] * 3,
        in_specs=[VMEM_SPEC] * 4, out_specs=[VMEM_SPEC] * 3,
    )(w, g, m, v)


def _wmod_finish(act_t, dmod_cols, w, m, v):
    rb = 256
    ncol = w.shape[-1]

    def body(a_ref, dm_ref, w_ref, m_ref, v_ref, g_ref, d_ref, mo_ref, vo_ref):
        g = a_ref[:, 0:1] * dm_ref[0, 0:1, :]
        for k in range(1, N_DEV):
            g = g + a_ref[:, k:k + 1] * dm_ref[0, k:k + 1, :]
        g_ref[0] = g
        d_ref[0], mo_ref[0], vo_ref[0] = _adamw_math(w_ref[0], g, m_ref[0], v_ref[0])

    blk = pl.BlockSpec((1, rb, ncol), lambda l, i: (l, i, 0))
    return pl.pallas_call(
        body, name="wmod_finish", grid=(2, D // rb),
        out_shape=[jax.ShapeDtypeStruct(w.shape, F32)] * 4,
        in_specs=[pl.BlockSpec((rb, N_DEV), lambda l, i: (i, 0)), pl.BlockSpec((1, N_DEV, ncol), lambda l, i: (l, 0, 0)),
                  blk, blk, blk],
        out_specs=[blk] * 4,
        compiler_params=pltpu.CompilerParams(dimension_semantics=("arbitrary", "arbitrary"), vmem_limit_bytes=VMEM_LIMIT),
    )(act_t, dmod_cols, w, m, v)


def _pack(pieces):
    flat, offs, at = [], [], 0
    for a in pieces:
        a = a.reshape(-1)
        n = -(-a.shape[0] // 128) * 128
        flat.append(jnp.pad(a, (0, n - a.shape[0])))
        offs.append(at)
        at += n
    return jnp.concatenate(flat).reshape(-1, 128), offs


def _param_block(mod_l, sub, gn):
    return jnp.concatenate([mod_l[sub], gn[None, :], jnp.zeros((4, D), F32)], axis=0)


def kernel(x, c, norm_g, w_mod, b_mod, w_ffn_in, w_ffn_out, ab_w_in, ab_norm_v, ab_w_s, ab_b_s, ab_conv_w, ab_w_out, pool_w_grp, pool_scale, final_g, loss_target, m_norm_g, m_w_mod, m_b_mod, m_w_ffn_in, m_w_ffn_out, m_ab_w_in, m_ab_norm_v, m_ab_w_s, m_ab_b_s, m_ab_conv_w, m_ab_w_out, m_pool_w_grp, m_pool_scale, m_final_g, v_norm_g, v_w_mod, v_b_mod, v_w_ffn_in, v_w_ffn_out, v_ab_w_in, v_ab_norm_v, v_ab_w_s, v_ab_b_s, v_ab_conv_w, v_ab_w_out, v_pool_w_grp, v_pool_scale, v_final_g):
    me = 4 * lax.axis_index("x") + 2 * lax.axis_index("y") + lax.axis_index("c")
    x0 = x[0]
    tgt = loss_target[0]
    n_in = w_ffn_in.shape[-1]
    n_out = w_ffn_out.shape[-2]
    n_abin = ab_w_in.shape[-1]
    n_about = ab_w_out.shape[-2]
    n_pool = pool_w_grp.shape[-2]
    n_mod = w_mod.shape[-1]
    n_ng = norm_g.shape[-1]
    n_cw = ab_conv_w.shape[-1]
    n_ps = pool_scale.shape[-1]

    win_sh = jnp.swapaxes(w_ffn_in.reshape(4, D, n_in), 1, 2).astype(BF16)
    wout_sh = w_ffn_out.reshape(4, n_out, D).astype(BF16)
    abin_sh = ab_w_in[0].T.astype(BF16)
    about_sh = ab_w_out[0].astype(BF16)
    pool_sh = pool_w_grp[0].astype(BF16)
    win, wout = [None] * 4, [None] * 4

    pack, offs = _pack([c, norm_g, ab_conv_w, pool_scale])
    got, (win[0],) = _exchange_small(pack, "gather_small", _GatherRider([(win_sh, 0)]))
    got = got.reshape(N_DEV, -1)
    c_all = got[:, offs[0]:offs[0] + D]
    ng_full = got[:, offs[1]:offs[1] + 6 * n_ng].reshape(N_DEV, 2, 3, n_ng).transpose(1, 2, 0, 3).reshape(2, 3, D)
    cw_full = got[:, offs[2]:offs[2] + 3 * n_cw].reshape(N_DEV, 3, n_cw).transpose(1, 0, 2).reshape(3, D_B)
    ps_full = got[:, offs[3]:offs[3] + n_ps].reshape(1, D)

    act_all, mod_cols = _mod_fwd(c_all, w_mod)
    mod_got, (wout[0],) = _exchange_small(mod_cols.reshape(-1, 128), "gather_mod", _GatherRider([(wout_sh, 0)]))
    mod_got = mod_got.reshape(N_DEV, 2, N_DEV, n_mod)
    mod = lax.dynamic_index_in_dim(mod_got, me, axis=2, keepdims=False).transpose(1, 0, 2).reshape(2, 9 * D) + b_mod
    mod = mod.reshape(2, 3, 3, D)
    nv = ab_norm_v
    ws = ab_w_s[0]
    bst = ab_b_s[0].T
    cw8 = jnp.concatenate([cw_full, jnp.zeros((5, D_B), F32)], axis=0)

    pb = [[_param_block(mod[l], s, ng_full[l, s]) for s in range(3)] for l in range(2)]
    (x1, h00, g00, u00, y00), (abin_all, about_all, win[1]) = _ffn_fwd(
        x0, pb[0][0], win[0], wout[0], 0, _GatherRider([abin_sh, about_sh, (win_sh, 1)]))
    (x2, h01, proj, ab_out), (wout[1],) = _ab_fwd(
        x1, pb[0][1], abin_all, about_all, nv, ws, bst, cw8, _GatherRider([(wout_sh, 1)]))
    (x3, h02, g02, u02, y02), (win[2], wout[2]) = _ffn_fwd(
        x2, pb[0][2], win[1], wout[1], 1, _GatherRider([(win_sh, 2), (wout_sh, 2)]))
    (x4, h10, g10, u10, y10), (pool_all, win[3], wout[3]) = _ffn_fwd(
        x3, pb[1][0], win[2], wout[2], 2, _GatherRider([pool_sh, (win_sh, 3), (wout_sh, 3)]))
    x5, pooled, pool_out = _pool_fwd(x4, pb[1][1], pool_all, ps_full)
    (x6, h12, g12, u12, y12), _ = _ffn_fwd(x5, pb[1][2], win[3], wout[3], 3)
    dx6, loss_blk, dfg = _head(x6, final_g.reshape(1, D), tgt)

    core = lax.axis_index("c")
    sel_rows = lambda n: jnp.stack([core * n, 0]).astype(jnp.int32)
    sel_cols = jnp.stack([0, core]).astype(jnp.int32)
    p_in, p_out = [None] * 4, [None] * 4

    def exchange(dgu, a, h, dy):
        return _SiblingRider([(dgu, NCH), (a, None), (h, None), (dy, None)])

    def ffn_wgrads(f, dgu, a, h, dy, got, ride_out=None):
        s_dgu, s_a, s_h, s_dy = got
        g_out, rode = _wgrad_pair(a, s_a, dy, s_dy, sel_cols, NCH, True, f"wgrad_out_{f}", ride_out)
        g_in, (p_out[f],) = _wgrad_pair(dgu, s_dgu, h, s_h, sel_rows(NCH), NCH, False, f"wgrad_in_{f}",
                                        _ChipScatterRider([], [g_out]))
        return g_in, rode

    (dx5, dgu12, a12, dy12, st12), _ = _ffn_bwd(dx6, x5, pb[1][2], g12, u12, y12, win[3], wout[3], 3)
    dx4, st11, dps, gw_pool = _pool_bwd(dx5, x4, pb[1][1], pooled, pool_out, pool_all, ps_full)
    (dx3, dgu10, a10, dy10, st10), got3 = _ffn_bwd(
        dx4, x3, pb[1][0], g10, u10, y10, win[2], wout[2], 2, exchange(dgu12, a12, h12, dy12))
    g_in3, (p_pool,) = ffn_wgrads(3, dgu12, a12, h12, dy12, got3, _ScatterRider([gw_pool]))
    ride = _Riders([exchange(dgu10, a10, h10, dy10), _ChipScatterRider([g_in3], [])])
    (dx2, dgu02, a02, dy02, st02), rode = _ffn_bwd(dx3, x2, pb[0][2], g02, u02, y02, win[1], wout[1], 1, ride)
    got2, (p_in[3],) = ride.split(rode)
    g_in2, _ = ffn_wgrads(2, dgu10, a10, h10, dy10, got2)
    ride = _Riders([exchange(dgu02, a02, h02, dy02), _ChipScatterRider([g_in2], [])])
    (dx1, dproj, cat, dy01, st01, dnv, dws, dbs, dcw), rode = _ab_bwd(
        dx2, x1, pb[0][1], proj, ab_out, abin_all, about_all, nv, ws, bst, cw8, ride)
    got1, (p_in[2],) = ride.split(rode)
    g_in1, _ = ffn_wgrads(1, dgu02, a02, h02, dy02, got1)
    zero_stats = jnp.zeros((8, D), F32)
    stats = [[zero_stats, st01, st02], [st10, st11, st12]]
    dmod = jnp.stack([jnp.concatenate([stats[l][s][0:3].reshape(-1) for s in range(3)]) for l in range(2)])
    dng = jnp.stack([jnp.stack([stats[l][s][3] for s in range(3)]) for l in range(2)])
    spack, so = _pack([dmod, dng, dnv[0], dws, dbs[:, :, 0], dcw[0:3], dps[0], dfg[0], loss_blk[0]])
    ride = _Riders([_SiblingRider([(dproj, 5), (cat, 2), (h01, None), (dy01, None)]), _SmallGatherRider(spack)])
    (dgu00, a00, dy00, gst00), rode = _ffn_bwd_acts(dx1, pb[0][0], g00, u00, y00, wout[0], 0, ride)
    (s_dproj, s_cat, s_h01, s_dy01), (sgot,) = ride.split(rode)
    g_about, _ = _wgrad_pair(cat, s_cat, dy01, s_dy01, sel_rows(2), 2, False, "wgrad_ab_out")
    g_abin, (p_about,) = _wgrad_pair(dproj, s_dproj, h01, s_h01, sel_rows(5), 5, False, "wgrad_ab_in",
                                     _ChipScatterRider([g_about], []))
    ride = _Riders([exchange(dgu00, a00, h00, dy00), _ChipScatterRider([g_in1], [])])
    (dx0, st00), rode = _ffn_bwd_dx(dx1, x0, pb[0][0], dgu00, gst00, win[0], 0, ride)
    got0, (p_in[1],) = ride.split(rode)
    grad_x = dx0[None]
    ride = _Riders([_ChipScatterRider([g_abin], []), _SmallGatherRider(st00[0:4].reshape(-1, 128))])
    g_in0, rode = ffn_wgrads(0, dgu00, a00, h00, dy00, got0, ride)
    (p_abin,), (sgot0,) = ride.split(rode)

    shape_in, shape_out = w_ffn_in.shape, w_ffn_out.shape
    fin = lambda a: jnp.swapaxes(a.reshape(4, D, n_in), 1, 2)
    fabin = lambda a: jnp.swapaxes(a, 1, 2)
    fout = lambda a: a.reshape(4, n_out, D)
    fpool = lambda a: a.reshape(1, 4 * n_pool, POOL_G)
    r_out, (p_in[0],) = _finish(p_out, fout(w_ffn_out), fout(m_w_ffn_out), fout(v_w_ffn_out), n_out // 2, "finish_ffn_out",
                                halves=True, rider=_ChipScatterRider([g_in0], []))
    r_in, _ = _finish(p_in, fin(w_ffn_in), fin(m_w_ffn_in), fin(v_w_ffn_in), n_in // 4, "finish_ffn_in")
    r_abin, _ = _finish([p_abin], fabin(ab_w_in), fabin(m_ab_w_in), fabin(v_ab_w_in), n_abin, "finish_ab_in")
    r_about, _ = _finish([p_about], ab_w_out, m_ab_w_out, v_ab_w_out, n_about, "finish_ab_out")
    r_pool, _ = _finish([p_pool.reshape(N_DEV, 4 * n_pool, POOL_G)], fpool(pool_w_grp), fpool(m_pool_w_grp),
                        fpool(v_pool_w_grp), 4 * n_pool, "finish_pool")
    r_in = [jnp.swapaxes(a, 1, 2).reshape(shape_in) for a in r_in]
    r_abin = [jnp.swapaxes(a, 1, 2) for a in r_abin]
    r_out = [a.reshape(shape_out) for a in r_out]
    r_pool = [a.reshape(pool_w_grp.shape) for a in r_pool]

    rows_mod, rows_gn = 3 * D // 128, D // 128
    ssum = _sum_small(sgot, sgot0, [(0, rows_mod, so[0] // 128), (rows_mod, rows_mod + rows_gn, so[1] // 128)]).reshape(-1)
    loss = ssum[so[8]]
    take = lambda i, n: lax.dynamic_slice_in_dim(ssum, so[i], n)
    g_bmod = take(0, 2 * 9 * D).reshape(2, 9 * D)
    g_ng = lax.dynamic_slice_in_dim(take(1, 6 * D).reshape(2, 3, D), me * n_ng, n_ng, axis=2)
    g_nv = take(2, D_A).reshape(1, D_A)
    g_ws = take(3, A_HEADS * CHUNK * CHUNK).reshape(1, A_HEADS, CHUNK, CHUNK)
    g_bs = take(4, A_HEADS * CHUNK).reshape(1, A_HEADS, CHUNK)
    g_cw = lax.dynamic_slice_in_dim(take(5, 3 * D_B).reshape(1, 3, D_B), me * n_cw, n_cw, axis=2)
    g_ps = lax.dynamic_slice_in_dim(take(6, D).reshape(1, D), me * n_ps, n_ps, axis=1)
    g_fg = take(7, D)

    dmod_all = sgot.reshape(N_DEV, -1)[:, so[0]:so[0] + 2 * 9 * D].reshape(N_DEV, 2, 9 * D)
    dmod_all = dmod_all.at[:, 0, 0:3 * D].set(sgot0[:, 0:rows_mod].reshape(N_DEV, 3 * D))
    dmod_cols = lax.dynamic_slice_in_dim(dmod_all, me * n_mod, n_mod, axis=2).transpose(1, 0, 2)
    r_wmod = _wmod_finish(act_all.T, dmod_cols, w_mod, m_w_mod, v_w_mod)

    small_w = [b_mod, norm_g, ab_norm_v, ab_w_s, ab_b_s, ab_conv_w, pool_scale, final_g]
    small_g = [g_bmod, g_ng, g_nv, g_ws, g_bs, g_cw, g_ps, g_fg]
    small_m = [m_b_mod, m_norm_g, m_ab_norm_v, m_ab_w_s, m_ab_b_s, m_ab_conv_w, m_pool_scale, m_final_g]
    small_v = [v_b_mod, v_norm_g, v_ab_norm_v, v_ab_w_s, v_ab_b_s, v_ab_conv_w, v_pool_scale, v_final_g]
    pw, po = _pack(small_w)
    pv = jnp.concatenate([jnp.pad(a.reshape(-1), (0, -a.size % 128), constant_values=1.0) for a in small_v]).reshape(-1, 128)
    sd, sm, sv = _adamw(pw, _pack(small_g)[0], _pack(small_m)[0], pv, "adamw_small")
    unpack = lambda packed: [packed.reshape(-1)[po[i]:po[i] + a.size].reshape(a.shape) for i, a in enumerate(small_w)]
    d_s, m_s, v_s = unpack(sd), unpack(sm), unpack(sv)

    def ordered(k, small):
        return [small[1], r_wmod[k], small[0], r_in[k], r_out[k], r_abin[k], small[2], small[3], small[4], small[5],
                r_about[k], r_pool[k], small[6], small[7]]

    grads = ordered(0, small_g)
    deltas = ordered(1, d_s)
    new_m = ordered(2, m_s)
    new_v = ordered(3, v_s)
    return (loss, grad_x, *grads, *deltas, *new_m, *new_v)
```

```python
import functools
import math

import jax
import jax.numpy as jnp
from jax import lax
from jax.experimental import pallas as pl
from jax.experimental.pallas import tpu as pltpu

F32 = jnp.float32
BF16 = jnp.bfloat16

N_DEV = 8
D = 1024
DFF = 2816
HC = 256
NCH = DFF // HC
D_A = 512
D_B = 512
D_AB = 2 * D_A + 3 * D_B
CHUNK = 128
A_HEADS = 4
POOL_G = 256
POOL_HALO = 16
CONV_HALO = 8
EPS = 1e-6
TM = 256
GELU_K = math.sqrt(2.0 / math.pi)
GELU_C = 0.044715

ADAM_LR = 0.001
ADAM_B1 = 0.9
ADAM_B2 = 0.999
ADAM_EPS = 1e-08
ADAM_WD = 0.01
ADAM_STEP = 10

VMEM_LIMIT = 56 * 1024 * 1024
MESH_ID = pl.DeviceIdType.MESH
ANY = pl.BlockSpec(memory_space=pl.ANY)
VMEM_SPEC = pl.BlockSpec(memory_space=pltpu.VMEM)
ARB1 = pltpu.CompilerParams(dimension_semantics=("arbitrary",), vmem_limit_bytes=VMEM_LIMIT)


def _dot_nt(a, b):
    return lax.dot_general(a, b, (((1,), (1,)), ((), ())), preferred_element_type=F32)


def _dot_nn(a, b):
    return lax.dot_general(a, b, (((1,), (0,)), ((), ())), preferred_element_type=F32)


def _dot_tn(a, b):
    return lax.dot_general(a, b, (((0,), (0,)), ((), ())), preferred_element_type=F32)


def _colsum8(v):
    r, n = v.shape
    return jnp.sum(v.reshape(r // 8, 8, n), axis=0)


def _gelu(x):
    return 0.5 * x * (1.0 + jnp.tanh(GELU_K * (x + GELU_C * x * x * x)))


def _gelu_grad(x):
    t = jnp.tanh(GELU_K * (x + GELU_C * x * x * x))
    return 0.5 * (1.0 + t) + 0.5 * x * (1.0 - t * t) * (GELU_K * (1.0 + 3.0 * GELU_C * x * x))


def _mod_rows(p_ref):
    return p_ref[0:1, :], p_ref[1:2, :], p_ref[2:3, :], p_ref[3:4, :]


def _modulate(x, gn, sh, sc):
    r = lax.rsqrt(jnp.mean(x * x, axis=-1, keepdims=True) + EPS)
    return ((x * r) * gn) * (1.0 + sc) + sh


def _modulate_bwd(dh, x, gn, sc, stats):
    r = lax.rsqrt(jnp.mean(x * x, axis=-1, keepdims=True) + EPS)
    xn = x * r
    stats[0] += _colsum8(dh)
    stats[1] += _colsum8(dh * (xn * gn))
    dy0 = dh * (1.0 + sc)
    stats[3] += _colsum8(dy0 * xn)
    dxn = dy0 * gn
    return r * (dxn - xn * jnp.mean(dxn * xn, axis=-1, keepdims=True))


def _stats_out(stats, out_ref):
    rows = [jnp.sum(stats[k], axis=0, keepdims=True) for k in range(4)]
    out_ref[...] = jnp.concatenate(rows + [jnp.zeros((4, stats.shape[-1]), F32)], axis=0)


def _shift_down(v, k, prev):
    n = v.shape[0]
    row = lax.broadcasted_iota(jnp.int32, v.shape, 0)
    out = pltpu.roll(v, k, 0)
    for j in range(k):
        out = jnp.where(row == j, prev[prev.shape[0] - k + j:prev.shape[0] - k + j + 1, :], out)
    return out


def _shift_up(v, k, nxt):
    n = v.shape[0]
    row = lax.broadcasted_iota(jnp.int32, v.shape, 0)
    out = pltpu.roll(v, n - k, 0)
    for j in range(k):
        out = jnp.where(row == n - k + j, nxt[j:j + 1, :], out)
    return out


def _load_rows(w_hbm, sel, dst, sems, base):
    n = dst.shape[0] // N_DEV
    cps = []
    for k in range(N_DEV):
        src = w_hbm.at[k] if sel is None else w_hbm.at[k, sel]
        cps.append(pltpu.make_async_copy(src, dst.at[pl.ds(k * n, n)], sems.at[base + k]))
    return cps


def _my_pos():
    return lax.axis_index("x"), lax.axis_index("y"), lax.axis_index("c")


def _peer(j):
    x, y, c = _my_pos()
    return (1 - x if j & 4 else x, 1 - y if j & 2 else y, 1 - c if j & 1 else c)


def _index(pos):
    return 4 * pos[0] + 2 * pos[1] + pos[2]


def _exchange_small(v, name, rider):
    rows = v.shape[0]
    ri, ro = len(rider.inputs), len(rider.out_shapes)

    def body(v_ref, *refs):
        r_in, out_ref, r_out, refs = refs[:ri], refs[ri], refs[ri + 1:ri + 1 + ro], refs[ri + 1 + ro:]
        send_sems, recv_sems, local_sem = refs[:3]
        rider.first(r_in, r_out, refs[3:])
        me = _index(_my_pos())

        def copy(j, slot):
            return pltpu.make_async_remote_copy(
                src_ref=v_ref, dst_ref=out_ref.at[slot], send_sem=send_sems.at[j - 1], recv_sem=recv_sems.at[j - 1],
                device_id=_peer(j), device_id_type=MESH_ID)

        mine = pltpu.make_async_copy(v_ref, out_ref.at[me], local_sem)
        mine.start()
        sends = [copy(j, me) for j in range(1, N_DEV)]
        for cp in sends:
            cp.start()
        for j in range(1, N_DEV):
            copy(j, _index(_peer(j))).wait_recv()
        for cp in sends:
            cp.wait_send()
        mine.wait()
        if rider.has_middle:
            rider.middle(r_in, r_out, refs[3:])
        rider.last(r_in, r_out, refs[3:])

    scratch = [pltpu.SemaphoreType.DMA((N_DEV - 1,)), pltpu.SemaphoreType.DMA((N_DEV - 1,)), pltpu.SemaphoreType.DMA(())]
    res = pl.pallas_call(
        body, name=name, out_shape=[jax.ShapeDtypeStruct((N_DEV, rows, 128), F32)] + rider.out_shapes,
        in_specs=[VMEM_SPEC] + [ANY] * ri, out_specs=[VMEM_SPEC] + [ANY] * ro,
        scratch_shapes=scratch + rider.scratch)(v, *rider.inputs)
    return res[0], list(res[1:])


class _SmallGatherRider:
    has_middle = False

    def __init__(self, v):
        self.inputs = [v]
        self.out_shapes = [jax.ShapeDtypeStruct((N_DEV,) + v.shape, v.dtype)]
        self.scratch = [pltpu.SemaphoreType.DMA((N_DEV - 1,)), pltpu.SemaphoreType.DMA((N_DEV - 1,)), pltpu.SemaphoreType.DMA(())]

    def _copy(self, ins, outs, scr, j, slot):
        return pltpu.make_async_remote_copy(
            src_ref=ins[0], dst_ref=outs[0].at[slot], send_sem=scr[0].at[j - 1], recv_sem=scr[1].at[j - 1],
            device_id=_peer(j), device_id_type=MESH_ID)

    def first(self, ins, outs, scr):
        me = _index(_my_pos())
        pltpu.make_async_copy(ins[0], outs[0].at[me], scr[2]).start()
        for j in range(1, N_DEV):
            self._copy(ins, outs, scr, j, me).start()

    def last(self, ins, outs, scr):
        me = _index(_my_pos())
        for j in range(1, N_DEV):
            self._copy(ins, outs, scr, j, _index(_peer(j))).wait_recv()
        for j in range(1, N_DEV):
            self._copy(ins, outs, scr, j, me).wait_send()
        pltpu.make_async_copy(ins[0], outs[0].at[me], scr[2]).wait()


class _GatherRider:
    has_middle = True

    def __init__(self, shards):
        pairs = [s if isinstance(s, tuple) else (s, None) for s in shards]
        self.inputs = [a for a, _ in pairs]
        self.picks = [i for _, i in pairs]
        shapes = [a.shape if i is None else a.shape[1:] for a, i in pairs]
        self.out_shapes = [jax.ShapeDtypeStruct((N_DEV,) + s, a.dtype) for s, (a, _) in zip(shapes, pairs)]
        self.pieces = [4 if (len(s) > 2 or s[0] % 64 == 0) else 2 for s in shapes]
        self.base = [sum(1 + 6 * n for n in self.pieces[:a]) for a in range(len(pairs))]
        total = sum(1 + 6 * n for n in self.pieces)
        self.scratch = [pltpu.SemaphoreType.DMA((total,)), pltpu.SemaphoreType.DMA((total,)),
                        pltpu.SemaphoreType.DMA((len(pairs),))]

    def _src(self, ins):
        return [r if i is None else r.at[i] for r, i in zip(ins, self.picks)]

    def _ctx(self, outs, scr):
        send, recv, _ = scr
        x, y, c = _my_pos()
        chips = [(1 - x, y), (x, 1 - y), (1 - x, 1 - y)]

        def copy(a, k, block, to, src=None, piece=None):
            slot = outs[a].at[_index(block)]
            src = slot if src is None else src
            if piece is not None:
                rows = slot.shape[0] // self.pieces[a]
                slot, src = slot.at[pl.ds(piece * rows, rows)], src.at[pl.ds(piece * rows, rows)]
            return pltpu.make_async_remote_copy(
                src_ref=src, dst_ref=slot, send_sem=send.at[self.base[a] + k], recv_sem=recv.at[self.base[a] + k],
                device_id=to, device_id_type=MESH_ID)

        return (x, y, c), (x, y, 1 - c), chips, copy

    def _sends(self, ins, outs, scr):
        me, sib, chips, copy = self._ctx(outs, scr)
        srcs = self._src(ins)
        out = [copy(a, 0, me, sib, src=srcs[a]) for a in range(len(ins))]
        for s in range(4):
            for a in range(len(ins)):
                if s < self.pieces[a]:
                    out += [copy(a, 1 + j * self.pieces[a] + s, me, (*chips[j], me[2]), src=srcs[a], piece=s) for j in range(3)]
        return out

    def first(self, ins, outs, scr):
        me = _index(_my_pos())
        for a, src in enumerate(self._src(ins)):
            pltpu.make_async_copy(src, outs[a].at[me], scr[2].at[a]).start()
        for cp in self._sends(ins, outs, scr):
            cp.start()

    def _forwards(self, ins, outs, scr, core):
        me, sib, chips, copy = self._ctx(outs, scr)
        out = []
        for s in range(4):
            for a in range(len(ins)):
                n = self.pieces[a]
                if s < n:
                    for j in range(3):
                        out.append((copy(a, 1 + j * n + s, (*chips[j], core), me, piece=s),
                                    copy(a, 1 + 3 * n + j * n + s, (*chips[j], core), sib, piece=s)))
        return out

    def middle(self, ins, outs, scr):
        me = _my_pos()
        for arrival, forward in self._forwards(ins, outs, scr, me[2]):
            arrival.wait_recv()
            forward.start()

    def last(self, ins, outs, scr):
        me, sib, chips, copy = self._ctx(outs, scr)
        for a in range(len(ins)):
            copy(a, 0, sib, me).wait_recv()
        for _, forward in self._forwards(ins, outs, scr, sib[2]):
            forward.wait_recv()
        for cp in self._sends(ins, outs, scr):
            cp.wait_send()
        for _, forward in self._forwards(ins, outs, scr, me[2]):
            forward.wait_send()
        for a, src in enumerate(self._src(ins)):
            pltpu.make_async_copy(src, outs[a].at[_index(me)], scr[2].at[a]).wait()


class _ScatterRider:
    has_middle = False

    def __init__(self, grads):
        n = len(grads)
        self.inputs = list(grads)
        self.out_shapes = []
        for g in grads:
            if g.ndim == 3:
                self.out_shapes.append(jax.ShapeDtypeStruct((N_DEV, g.shape[0], g.shape[1] // N_DEV, g.shape[2]), g.dtype))
            else:
                self.out_shapes.append(jax.ShapeDtypeStruct((N_DEV, g.shape[0] // N_DEV, g.shape[1]), g.dtype))
        self.scratch = [pltpu.SemaphoreType.DMA((7 * n,)), pltpu.SemaphoreType.DMA((7 * n,)), pltpu.SemaphoreType.DMA((n,))]

    @staticmethod
    def _part(ref, k):
        if ref.ndim == 3:
            n = ref.shape[1] // N_DEV
            return ref.at[:, pl.ds(pl.multiple_of(k * n, 16), n)]
        n = ref.shape[0] // N_DEV
        return ref.at[pl.ds(pl.multiple_of(k * n, 16), n)]

    def _copy(self, ins, outs, scr, g, j, to, src_dev):
        return pltpu.make_async_remote_copy(
            src_ref=self._part(ins[g], to), dst_ref=outs[g].at[src_dev], send_sem=scr[0].at[7 * g + j - 1],
            recv_sem=scr[1].at[7 * g + j - 1], device_id=_peer(j), device_id_type=MESH_ID)

    def first(self, ins, outs, scr):
        me = _index(_my_pos())
        for g in range(len(ins)):
            pltpu.make_async_copy(self._part(ins[g], me), outs[g].at[me], scr[2].at[g]).start()
        for j in range(1, N_DEV):
            for g in range(len(ins)):
                self._copy(ins, outs, scr, g, j, _index(_peer(j)), me).start()

    def last(self, ins, outs, scr):
        me = _index(_my_pos())
        for j in range(1, N_DEV):
            for g in range(len(ins)):
                self._copy(ins, outs, scr, g, j, me, _index(_peer(j))).wait_recv()
        for j in range(1, N_DEV):
            for g in range(len(ins)):
                self._copy(ins, outs, scr, g, j, _index(_peer(j)), me).wait_send()
        for g in range(len(ins)):
            pltpu.make_async_copy(self._part(ins[g], me), outs[g].at[me], scr[2].at[g]).wait()


class _SiblingRider:
    has_middle = False

    def __init__(self, items):
        self.inputs = [a for a, _ in items]
        self.counts = [n for _, n in items]
        self.out_shapes = [jax.ShapeDtypeStruct(a.shape if n is None else (n,) + a.shape[1:], a.dtype) for a, n in items]
        self.scratch = [pltpu.SemaphoreType.DMA((len(items),)), pltpu.SemaphoreType.DMA((len(items),))]

    def _copies(self, ins, outs, scr):
        x, y, c = _my_pos()
        out = []
        for i, (ref, n) in enumerate(zip(ins, self.counts)):
            src = ref if n is None else ref.at[pl.ds((1 - c) * n, n)]
            out.append(pltpu.make_async_remote_copy(
                src_ref=src, dst_ref=outs[i], send_sem=scr[0].at[i], recv_sem=scr[1].at[i],
                device_id=(x, y, 1 - c), device_id_type=MESH_ID))
        return out

    def first(self, ins, outs, scr):
        for cp in self._copies(ins, outs, scr):
            cp.start()

    def last(self, ins, outs, scr):
        for cp in self._copies(ins, outs, scr):
            cp.wait()


class _ChipScatterRider:
    has_middle = False

    def __init__(self, rows, cols):
        self.nr, self.nc = len(rows), len(cols)
        self.inputs = list(rows) + list(cols)
        self.out_shapes = [jax.ShapeDtypeStruct((4, a.shape[0] // 4, a.shape[1]), a.dtype) for a in rows]
        self.out_shapes += [jax.ShapeDtypeStruct((4, 2, a.shape[0] // N_DEV, a.shape[1]), a.dtype) for a in cols]
        n = self.nsem = 4 * self.nr + N_DEV * self.nc
        self.scratch = [pltpu.SemaphoreType.DMA((n,)), pltpu.SemaphoreType.DMA((n,)), pltpu.SemaphoreType.DMA((n,))]

    def _pieces(self, ins, outs):
        x, y, c = _my_pos()
        q = 2 * x + y
        out = []
        for a in range(self.nr):
            n = ins[a].shape[0] // 4
            for j in range(4):
                out.append((4 * a + j, ins[a].at[pl.ds(j * n, n)], (c, j >> 1, j & 1), outs[a].at[q], 4 * a + q))
        for a in range(self.nc):
            ref, base = ins[self.nr + a], 4 * self.nr + N_DEV * a
            n = ref.shape[0] // N_DEV
            for k in range(N_DEV):
                out.append((base + k, ref.at[pl.ds(k * n, n)], (k >> 2, (k >> 1) & 1, k & 1),
                            outs[self.nr + a].at[q, c], base + 2 * q + c))
        return out

    def first(self, ins, outs, scr):
        send, recv, local = scr
        me = _index(_my_pos())
        for s, src, to, slot, r in self._pieces(ins, outs):
            mine = _index(to) == me

            @pl.when(mine)
            def _():
                pltpu.make_async_copy(src, slot, local.at[s]).start()

            @pl.when(jnp.logical_not(mine))
            def _():
                pltpu.make_async_remote_copy(src_ref=src, dst_ref=slot, send_sem=send.at[s], recv_sem=recv.at[r],
                                             device_id=to, device_id_type=MESH_ID).start()

    def last(self, ins, outs, scr):
        self.wait(ins, outs, scr, True, True)

    def wait(self, ins, outs, scr, local_ones, remote_ones):
        send, recv, local = scr
        x, y, c = _my_pos()
        me = _index((x, y, c))
        arrivals = []
        for a in range(self.nr):
            n = ins[a].shape[0] // 4
            for q in range(4):
                arrivals.append((4 * a + q, (q >> 1, q & 1, x), ins[a].at[pl.ds(0, n)], outs[a].at[q], 4 * a + 2 * y + c))
        for a in range(self.nc):
            ref, base = ins[self.nr + a], 4 * self.nr + N_DEV * a
            n = ref.shape[0] // N_DEV
            for k in range(N_DEV):
                arrivals.append((base + k, (k >> 2, (k >> 1) & 1, k & 1), ref.at[pl.ds(0, n)],
                                 outs[self.nr + a].at[k >> 1, k & 1], base + me))
        for r, sender, src, slot, s_local in arrivals:
            mine = _index(sender) == me

            if local_ones:
                @pl.when(mine)
                def _():
                    pltpu.make_async_copy(src, slot, local.at[s_local]).wait()

            if remote_ones:
                @pl.when(jnp.logical_not(mine))
                def _():
                    pltpu.make_async_remote_copy(src_ref=src, dst_ref=slot, send_sem=send.at[r], recv_sem=recv.at[r],
                                                 device_id=sender, device_id_type=MESH_ID).wait_recv()

        if remote_ones:
            for s, src, to, slot, r in self._pieces(ins, outs):
                @pl.when(_index(to) != me)
                def _():
                    pltpu.make_async_remote_copy(src_ref=src, dst_ref=slot, send_sem=send.at[s], recv_sem=recv.at[r],
                                                 device_id=to, device_id_type=MESH_ID).wait_send()


HBM_SPEC = pl.BlockSpec(memory_space=pltpu.HBM)
SEM_SPEC = pl.BlockSpec(memory_space=pltpu.SEMAPHORE)


def _scatter_start(rows, cols, name):
    rider = _ChipScatterRider(rows, cols)
    ni = len(rider.inputs)
    lands = [pltpu.with_memory_space_constraint(lax.empty(s.shape, s.dtype), pltpu.HBM) for s in rider.out_shapes]
    srcs = [pltpu.with_memory_space_constraint(a, pltpu.HBM) for a in rider.inputs]

    def body(*refs):
        ins, land = refs[:ni], refs[ni:2 * ni]
        send, recv = refs[2 * ni], refs[2 * ni + 1]
        token, local = refs[-2], refs[-1]
        rider.first(ins, land, (send, recv, local))
        rider.wait(ins, land, (send, recv, local), True, False)
        token[...] = jnp.zeros_like(token)

    nsem = (rider.nsem,)
    outs = pl.pallas_call(
        body, name=name,
        out_shape=(pltpu.SemaphoreType.DMA(nsem), pltpu.SemaphoreType.DMA(nsem),
                   *[pltpu.HBM(a.shape, a.dtype) for a in rider.inputs],
                   *[pltpu.HBM(s.shape, s.dtype) for s in rider.out_shapes],
                   jax.ShapeDtypeStruct((8, 128), F32)),
        in_specs=[HBM_SPEC] * (2 * ni),
        out_specs=(SEM_SPEC, SEM_SPEC, *[HBM_SPEC] * (2 * ni), VMEM_SPEC),
        input_output_aliases={i: 2 + i for i in range(2 * ni)},
        scratch_shapes=[rider.scratch[2]],
        compiler_params=pltpu.CompilerParams(has_side_effects=pltpu.SideEffectType.DATAFLOW_SIDE_EFFECTING),
    )(*srcs, *lands)
    return rider, outs[:-1], outs[-1]


def _scatter_wait(rider, state, after, name):
    ni = len(rider.inputs)
    send, recv, thru = state[0], state[1], state[2:]

    def body(*refs):
        ins, land = refs[:ni], refs[ni:2 * ni]
        send_ref, recv_ref = refs[2 * ni], refs[2 * ni + 1]
        rider.wait(ins, land, (send_ref, recv_ref, None), False, True)

    outs = pl.pallas_call(
        body, name=name,
        out_shape=tuple(pltpu.HBM(a.shape, a.dtype) for a in thru),
        in_specs=[HBM_SPEC] * (2 * ni) + [SEM_SPEC, SEM_SPEC, ANY],
        out_specs=tuple([HBM_SPEC] * (2 * ni)),
        input_output_aliases={i: i for i in range(2 * ni)},
        compiler_params=pltpu.CompilerParams(has_side_effects=pltpu.SideEffectType.DATAFLOW_SIDE_EFFECTING),
    )(*thru, send, recv, after)
    return list(outs[ni:])


class _Riders:
    def __init__(self, riders):
        self.riders = list(riders)
        self.inputs = [a for r in self.riders for a in r.inputs]
        self.out_shapes = [s for r in self.riders for s in r.out_shapes]
        self.scratch = [s for r in self.riders for s in r.scratch]
        self.has_middle = any(r.has_middle for r in self.riders)

    def _each(self, ins, outs, scr):
        i = o = s = 0
        for r in self.riders:
            ni, no, ns = len(r.inputs), len(r.out_shapes), len(r.scratch)
            yield r, ins[i:i + ni], outs[o:o + no], scr[s:s + ns]
            i, o, s = i + ni, o + no, s + ns

    def first(self, ins, outs, scr):
        for r, a, b, c in self._each(ins, outs, scr):
            r.first(a, b, c)

    def middle(self, ins, outs, scr):
        for r, a, b, c in self._each(ins, outs, scr):
            if r.has_middle:
                r.middle(a, b, c)

    def last(self, ins, outs, scr):
        for r, a, b, c in self._each(ins, outs, scr):
            r.last(a, b, c)

    def split(self, outs):
        res, o = [], 0
        for r in self.riders:
            res.append(list(outs[o:o + len(r.out_shapes)]))
            o += len(r.out_shapes)
        return res


def _run(body, *, name, grid, in_specs, out_specs, out_shape, scratch_shapes, args, rider=None, params=None, prefetch=()):
    params = ARB1 if params is None else params
    npf = len(prefetch)
    ni, no, ns = len(in_specs), len(out_shape), len(scratch_shapes)
    ri, ro = (len(rider.inputs), len(rider.out_shapes)) if rider is not None else (0, 0)

    def wrapped(*refs):
        pf, refs = refs[:npf], refs[npf:]
        cut = [ni, ni + ri, ni + ri + no, ni + ri + no + ro, ni + ri + no + ro + ns]
        a, b, c, d, e, f = (refs[lo:hi] for lo, hi in zip([0] + cut, cut + [len(refs)]))
        if rider is not None:
            ids = [pl.program_id(k) for k in range(len(grid))]
            at_first = functools.reduce(jnp.logical_and, [i == 0 for i in ids])
            at_last = functools.reduce(jnp.logical_and, [i == n - 1 for i, n in zip(ids, grid)])

            @pl.when(at_first)
            def _():
                rider.first(b, d, f)

            if rider.has_middle:
                @pl.when(at_last)
                def _():
                    rider.middle(b, d, f)

        body(*pf, *a, *c, *e)

        if rider is not None:
            @pl.when(at_last)
            def _():
                rider.last(b, d, f)

    extra_shapes = rider.out_shapes if rider is not None else []
    extra_scratch = rider.scratch if rider is not None else []
    extra_inputs = rider.inputs if rider is not None else []
    all_in, all_out = list(in_specs) + [ANY] * ri, list(out_specs) + [ANY] * ro
    all_scratch = list(scratch_shapes) + extra_scratch
    if npf:
        outs = pl.pallas_call(
            wrapped, name=name, out_shape=list(out_shape) + extra_shapes,
            grid_spec=pltpu.PrefetchScalarGridSpec(num_scalar_prefetch=npf, grid=grid, in_specs=all_in, out_specs=all_out,
                                                   scratch_shapes=all_scratch),
            compiler_params=params)(*prefetch, *args, *extra_inputs)
    else:
        outs = pl.pallas_call(
            wrapped, name=name, grid=grid, in_specs=all_in, out_specs=all_out, out_shape=list(out_shape) + extra_shapes,
            scratch_shapes=all_scratch, compiler_params=params)(*args, *extra_inputs)
    return list(outs[:no]), list(outs[no:])


def _mod_fwd(c_all, w_mod):
    ncol = w_mod.shape[-1]

    def body(c_ref, w_ref, act_ref, out_ref):
        c = c_ref[...]
        act = c * jax.nn.sigmoid(c)
        act_ref[...] = act
        out_ref[0] = _dot_nn(act.astype(BF16), w_ref[0].astype(BF16))

    return pl.pallas_call(
        body, name="mod_fwd", grid=(2,),
        out_shape=[jax.ShapeDtypeStruct((N_DEV, D), F32), jax.ShapeDtypeStruct((2, N_DEV, ncol), F32)],
        in_specs=[pl.BlockSpec((N_DEV, D), lambda l: (0, 0)), pl.BlockSpec((1, D, ncol), lambda l: (l, 0, 0))],
        out_specs=[pl.BlockSpec((N_DEV, D), lambda l: (0, 0)), pl.BlockSpec((1, N_DEV, ncol), lambda l: (l, 0, 0))],
        compiler_params=ARB1,
    )(c_all, w_mod)


def _ffn_fwd(x, p, win_all, wout_all, f, rider=None):
    s = x.shape[0]
    nt = s // TM

    def body(x_ref, p_ref, win_hbm, wout_hbm, xo_ref, h_ref, g_ref, u_ref, y_ref, win, wout, act, sems):
        @pl.when(pl.program_id(0) == 0)
        def _():
            cps = _load_rows(win_hbm, None, win, sems, 0) + _load_rows(wout_hbm, None, wout, sems, N_DEV)
            for cp in cps:
                cp.start()
            for cp in cps:
                cp.wait()

        sh, sc, gate, gn = _mod_rows(p_ref)
        x = x_ref[...]
        hb = _modulate(x, gn, sh, sc).astype(BF16)
        h_ref[...] = hb
        for c in range(NCH):
            g = _dot_nt(hb, win[c * HC:(c + 1) * HC, :])
            u = _dot_nt(hb, win[DFF + c * HC:DFF + (c + 1) * HC, :])
            g_ref[c] = g.astype(BF16)
            u_ref[c] = u.astype(BF16)
            act[:, c * HC:(c + 1) * HC] = ((g * jax.nn.sigmoid(g)) * u).astype(BF16)
        y = _dot_nn(act[...], wout[...])
        y_ref[...] = y
        xo_ref[...] = x + (0.5 * gate) * y

    tile = pl.BlockSpec((TM, D), lambda i: (i, 0))
    chunks = pl.BlockSpec((NCH, TM, HC), lambda i: (0, i, 0))
    return _run(
        body, name=f"ffn_fwd_{f}", grid=(nt,),
        out_shape=[jax.ShapeDtypeStruct((s, D), F32), jax.ShapeDtypeStruct((s, D), BF16),
                   jax.ShapeDtypeStruct((NCH, s, HC), BF16), jax.ShapeDtypeStruct((NCH, s, HC), BF16),
                   jax.ShapeDtypeStruct((s, D), F32)],
        in_specs=[tile, pl.BlockSpec((8, D), lambda i: (0, 0)), ANY, ANY],
        out_specs=[tile, tile, chunks, chunks, tile],
        scratch_shapes=[pltpu.VMEM((2 * DFF, D), BF16), pltpu.VMEM((DFF, D), BF16), pltpu.VMEM((TM, DFF), BF16),
                        pltpu.SemaphoreType.DMA((2 * N_DEV,))],
        args=(x, p, win_all, wout_all), rider=rider)


def _swiglu_bwd_acts(dyb, g_ref, u_ref, wout, a_ref, dgu_ref, dgu):
    for c in range(NCH):
        da = _dot_nt(dyb, wout[c * HC:(c + 1) * HC, :])
        g = g_ref[c].astype(F32)
        u = u_ref[c].astype(F32)
        sg = jax.nn.sigmoid(g)
        si = g * sg
        dg = ((da * u) * (sg * (1.0 + g * (1.0 - sg)))).astype(BF16)
        du = (da * si).astype(BF16)
        a_ref[c] = (si * u).astype(BF16)
        dgu_ref[c] = dg
        dgu_ref[NCH + c] = du
        if dgu is not None:
            dgu[:, c * HC:(c + 1) * HC] = dg
            dgu[:, DFF + c * HC:DFF + (c + 1) * HC] = du


def _ffn_bwd(dxo, x, p, g3, u3, y, win_all, wout_all, f, rider=None):
    s = x.shape[0]
    nt = s // TM

    def body(dxo_ref, x_ref, p_ref, g_ref, u_ref, y_ref, win_hbm, wout_hbm,
             dx_ref, dgu_ref, a_ref, dy_ref, st_ref, win, wout, dgu, stats, sems):
        i = pl.program_id(0)

        @pl.when(i == 0)
        def _():
            cps = _load_rows(win_hbm, None, win, sems, 0) + _load_rows(wout_hbm, None, wout, sems, N_DEV)
            for cp in cps:
                cp.start()
            stats[...] = jnp.zeros_like(stats)
            for cp in cps:
                cp.wait()

        sh, sc, gate, gn = _mod_rows(p_ref)
        x = x_ref[...]
        dxo = dxo_ref[...]
        dyb = ((0.5 * gate) * dxo).astype(BF16)
        dy_ref[...] = dyb
        stats[2] += _colsum8((0.5 * dxo) * y_ref[...])
        _swiglu_bwd_acts(dyb, g_ref, u_ref, wout, a_ref, dgu_ref, dgu)
        dh = _dot_nn(dgu[...], win[...])
        dx_ref[...] = dxo + _modulate_bwd(dh, x, gn, sc, stats)

        @pl.when(i == nt - 1)
        def _():
            _stats_out(stats, st_ref)

    tile = pl.BlockSpec((TM, D), lambda i: (i, 0))
    chunks = pl.BlockSpec((NCH, TM, HC), lambda i: (0, i, 0))
    small = pl.BlockSpec((8, D), lambda i: (0, 0))
    return _run(
        body, name=f"ffn_bwd_{f}", grid=(nt,),
        out_shape=[jax.ShapeDtypeStruct((s, D), F32), jax.ShapeDtypeStruct((2 * NCH, s, HC), BF16),
                   jax.ShapeDtypeStruct((NCH, s, HC), BF16), jax.ShapeDtypeStruct((s, D), BF16),
                   jax.ShapeDtypeStruct((8, D), F32)],
        in_specs=[tile, tile, small, chunks, chunks, tile, ANY, ANY],
        out_specs=[tile, pl.BlockSpec((2 * NCH, TM, HC), lambda i: (0, i, 0)), chunks, tile, small],
        scratch_shapes=[pltpu.VMEM((2 * DFF, D), BF16), pltpu.VMEM((DFF, D), BF16), pltpu.VMEM((TM, 2 * DFF), BF16),
                        pltpu.VMEM((4, 8, D), F32), pltpu.SemaphoreType.DMA((2 * N_DEV,))],
        args=(dxo, x, p, g3, u3, y, win_all, wout_all), rider=rider)


def _ffn_bwd_acts(dxo, p, g3, u3, y, wout_all, f, rider=None):
    s = dxo.shape[0]
    nt = s // TM

    def body(dxo_ref, p_ref, g_ref, u_ref, y_ref, wout_hbm, dgu_ref, a_ref, dy_ref, st_ref, wout, stat, sems):
        i = pl.program_id(0)

        @pl.when(i == 0)
        def _():
            cps = _load_rows(wout_hbm, None, wout, sems, 0)
            for cp in cps:
                cp.start()
            stat[...] = jnp.zeros_like(stat)
            for cp in cps:
                cp.wait()

        gate = p_ref[2:3, :]
        dxo = dxo_ref[...]
        dyb = ((0.5 * gate) * dxo).astype(BF16)
        dy_ref[...] = dyb
        stat[...] += _colsum8((0.5 * dxo) * y_ref[...])
        _swiglu_bwd_acts(dyb, g_ref, u_ref, wout, a_ref, dgu_ref, None)

        @pl.when(i == nt - 1)
        def _():
            row = jnp.sum(stat[...], axis=0, keepdims=True)
            st_ref[...] = jnp.concatenate([jnp.zeros((2, D), F32), row, jnp.zeros((5, D), F32)], axis=0)

    tile = pl.BlockSpec((TM, D), lambda i: (i, 0))
    chunks = pl.BlockSpec((NCH, TM, HC), lambda i: (0, i, 0))
    small = pl.BlockSpec((8, D), lambda i: (0, 0))
    return _run(
        body, name=f"ffn_bwd_acts_{f}", grid=(nt,),
        out_shape=[jax.ShapeDtypeStruct((2 * NCH, s, HC), BF16), jax.ShapeDtypeStruct((NCH, s, HC), BF16),
                   jax.ShapeDtypeStruct((s, D), BF16), jax.ShapeDtypeStruct((8, D), F32)],
        in_specs=[tile, small, chunks, chunks, tile, ANY],
        out_specs=[pl.BlockSpec((2 * NCH, TM, HC), lambda i: (0, i, 0)), chunks, tile, small],
        scratch_shapes=[pltpu.VMEM((DFF, D), BF16), pltpu.VMEM((8, D), F32), pltpu.SemaphoreType.DMA((N_DEV,))],
        args=(dxo, p, g3, u3, y, wout_all), rider=rider)


def _ffn_bwd_dx(dxo, x, p, dgu3, gate_stats, win_all, f, rider=None):
    s = x.shape[0]
    nt = s // TM

    def body(dxo_ref, x_ref, p_ref, dgu_ref, gs_ref, win_hbm, dx_ref, st_ref, win, dgu, stats, sems):
        i = pl.program_id(0)

        @pl.when(i == 0)
        def _():
            cps = _load_rows(win_hbm, None, win, sems, 0)
            for cp in cps:
                cp.start()
            stats[...] = jnp.zeros_like(stats)
            for cp in cps:
                cp.wait()

        _, sc, _, gn = _mod_rows(p_ref)
        for c in range(2 * NCH):
            dgu[:, c * HC:(c + 1) * HC] = dgu_ref[c]
        dh = _dot_nn(dgu[...], win[...])
        dx_ref[...] = dxo_ref[...] + _modulate_bwd(dh, x_ref[...], gn, sc, stats)

        @pl.when(i == nt - 1)
        def _():
            _stats_out(stats, st_ref)
            st_ref[2:3, :] = gs_ref[2:3, :]

    tile = pl.BlockSpec((TM, D), lambda i: (i, 0))
    small = pl.BlockSpec((8, D), lambda i: (0, 0))
    return _run(
        body, name=f"ffn_bwd_dx_{f}", grid=(nt,),
        out_shape=[jax.ShapeDtypeStruct((s, D), F32), jax.ShapeDtypeStruct((8, D), F32)],
        in_specs=[tile, tile, small, pl.BlockSpec((2 * NCH, TM, HC), lambda i: (0, i, 0)), small, ANY],
        out_specs=[tile, small],
        scratch_shapes=[pltpu.VMEM((2 * DFF, D), BF16), pltpu.VMEM((TM, 2 * DFF), BF16), pltpu.VMEM((4, 8, D), F32),
                        pltpu.SemaphoreType.DMA((N_DEV,))],
        args=(dxo, x, p, dgu3, gate_stats, win_all), rider=rider)


def _wgrad_pair(own3, sib3, own_r, sib_r, sel, n, col_split, name, rider=None):
    nj, s, _ = own3.shape
    nw = own_r.shape[1] // 2 if col_split else own_r.shape[1]
    steps = nj if col_split else n

    def body(sel_ref, lo_ref, ls_ref, ro_ref, rs_ref, o_ref):
        o_ref[...] = (_dot_tn(lo_ref[0], ro_ref[...]) + _dot_tn(ls_ref[0], rs_ref[...])).astype(BF16)

    rspec = pl.BlockSpec((s, nw), lambda j, sel_ref: (0, sel_ref[1]))
    outs, rode = _run(
        body, name=name, grid=(steps,),
        out_shape=[jax.ShapeDtypeStruct((steps * HC, nw), BF16)],
        in_specs=[pl.BlockSpec((1, s, HC), lambda j, sel_ref: (sel_ref[0] + j, 0, 0)),
                  pl.BlockSpec((1, s, HC), lambda j, sel_ref: (j, 0, 0)), rspec, rspec],
        out_specs=[pl.BlockSpec((HC, nw), lambda j, sel_ref: (j, 0))],
        scratch_shapes=[], args=(own3, sib3, own_r, sib_r), rider=rider, prefetch=(sel,))
    return outs[0], rode


def _gating(proj, nv, ws_ref, bst):
    u, v = proj[:, 0:D_A], proj[:, D_A:2 * D_A]
    gu, gv = _gelu(u), _gelu(v)
    mu = jnp.mean(gv, axis=-1, keepdims=True)
    dv = gv - mu
    rstd = lax.rsqrt(jnp.mean(dv * dv, axis=-1, keepdims=True) + EPS)
    vhat = dv * rstd
    vn = vhat * nv
    r = lax.broadcasted_iota(jnp.int32, (CHUNK, CHUNK), 0)
    c = lax.broadcasted_iota(jnp.int32, (CHUNK, CHUNK), 1)
    wm = [jnp.where(r >= c, ws_ref[hd], 0.0).astype(BF16) for hd in range(A_HEADS)]
    vnb = vn.astype(BF16)
    rows = []
    for n in range(proj.shape[0] // CHUNK):
        blocks = []
        for hd in range(A_HEADS):
            blk = vnb[n * CHUNK:(n + 1) * CHUNK, hd * CHUNK:(hd + 1) * CHUNK]
            blocks.append(_dot_nn(wm[hd], blk) + bst[:, hd:hd + 1])
        rows.append(jnp.concatenate(blocks, axis=1))
    z = jnp.concatenate(rows, axis=0)
    return u, v, gu, rstd, vhat, vnb, wm, z


def _conv(proj, cw, prev_xp):
    bg = proj[:, 2 * D_A:2 * D_A + D_B]
    cg = proj[:, 2 * D_A + D_B:2 * D_A + 2 * D_B]
    xb = proj[:, 2 * D_A + 2 * D_B:]
    xp = cg * xb
    x1 = _shift_down(xp, 1, prev_xp)
    x2 = _shift_down(xp, 2, prev_xp)
    conv = cw[0:1, :] * x2 + cw[1:2, :] * x1 + cw[2:3, :] * xp
    return bg, cg, xb, xp, x1, x2, conv


def _ab_fwd(x, p, abin_all, about_all, nv, ws, bst, cw, rider=None):
    s = x.shape[0]
    nt = s // TM

    def body(x_ref, p_ref, abin_hbm, about_hbm, nv_ref, ws_ref, bst_ref, cw_ref,
             xo_ref, h_ref, proj_ref, out_ref, abin, about, prev, sems):
        @pl.when(pl.program_id(0) == 0)
        def _():
            cps = _load_rows(abin_hbm, None, abin, sems, 0) + _load_rows(about_hbm, None, about, sems, N_DEV)
            for cp in cps:
                cp.start()
            prev[...] = jnp.zeros_like(prev)
            for cp in cps:
                cp.wait()

        sh, sc, gate, gn = _mod_rows(p_ref)
        x = x_ref[...]
        hb = _modulate(x, gn, sh, sc).astype(BF16)
        h_ref[...] = hb
        proj = _dot_nt(hb, abin[...])
        proj_ref[...] = proj
        _, _, gu, _, _, _, _, z = _gating(proj, nv_ref[...], ws_ref, bst_ref[...])
        bg, _, _, xp, _, _, conv = _conv(proj, cw_ref[...], prev[...])
        prev[...] = xp[TM - CONV_HALO:, :]
        cat = jnp.concatenate([gu * z, bg * conv], axis=1).astype(BF16)
        out = _dot_nn(cat, about[...])
        out_ref[...] = out
        xo_ref[...] = x + gate * out

    tile = pl.BlockSpec((TM, D), lambda i: (i, 0))
    full = lambda a: pl.BlockSpec(a.shape, lambda i: (0,) * a.ndim)
    return _run(
        body, name="ab_fwd", grid=(nt,),
        out_shape=[jax.ShapeDtypeStruct((s, D), F32), jax.ShapeDtypeStruct((s, D), BF16),
                   jax.ShapeDtypeStruct((s, D_AB), F32), jax.ShapeDtypeStruct((s, D), F32)],
        in_specs=[tile, pl.BlockSpec((8, D), lambda i: (0, 0)), ANY, ANY, full(nv), full(ws), full(bst), full(cw)],
        out_specs=[tile, tile, pl.BlockSpec((TM, D_AB), lambda i: (i, 0)), tile],
        scratch_shapes=[pltpu.VMEM((D_AB, D), BF16), pltpu.VMEM((D, D), BF16), pltpu.VMEM((CONV_HALO, D_B), F32),
                        pltpu.SemaphoreType.DMA((2 * N_DEV,))],
        args=(x, p, abin_all, about_all, nv, ws, bst, cw), rider=rider)


def _ab_bwd(dxo, x, p, proj, out, abin_all, about_all, nv, ws, bst, cw, rider=None):
    s = x.shape[0]
    nt = s // TM
    npj = D_AB // HC

    def body(dxo_ref, x_ref, p_ref, proj_ref, halo_ref, out_ref, abin_hbm, about_hbm, nv_ref, ws_ref, bst_ref, cw_ref,
             dx_ref, dproj_ref, cat_ref, dy_ref, st_ref, dnv_ref, dws_ref, dbs_ref, dcw_ref,
             abin, about, nxt, stats, dnv, dws, dbs, dcw, sems):
        i = pl.program_id(0)
        ti = nt - 1 - i

        @pl.when(i == 0)
        def _():
            cps = _load_rows(abin_hbm, None, abin, sems, 0) + _load_rows(about_hbm, None, about, sems, N_DEV)
            for cp in cps:
                cp.start()
            for z in (nxt, stats, dnv, dws, dbs, dcw):
                z[...] = jnp.zeros_like(z)
            for cp in cps:
                cp.wait()

        sh, sc, gate, gn = _mod_rows(p_ref)
        x = x_ref[...]
        dxo = dxo_ref[...]
        dyb = (gate * dxo).astype(BF16)
        dy_ref[...] = dyb
        stats[2] += _colsum8(dxo * out_ref[...])
        dcat = _dot_nt(dyb, about[...])
        dya, dyb2 = dcat[:, 0:D_A], dcat[:, D_A:]

        proj = proj_ref[...]
        nvv = nv_ref[...]
        u, v, gu, rstd, vhat, vnb, wm, z = _gating(proj, nvv, ws_ref, bst_ref[...])
        dgu = dya * z
        dzb = (dya * gu).astype(BF16)
        dz32 = dya * gu
        rows = []
        for n in range(TM // CHUNK):
            blocks = []
            for hd in range(A_HEADS):
                sl = (slice(n * CHUNK, (n + 1) * CHUNK), slice(hd * CHUNK, (hd + 1) * CHUNK))
                dbs[hd] += dz32[sl]
                dws[hd] += _dot_nt(dzb[sl], vnb[sl])
                blocks.append(_dot_tn(wm[hd], dzb[sl]))
            rows.append(jnp.concatenate(blocks, axis=1))
        dvn = jnp.concatenate(rows, axis=0)
        dnv[...] += _colsum8(dvn * vhat)
        dvh = dvn * nvv
        dgv = rstd * (dvh - jnp.mean(dvh, axis=-1, keepdims=True) - vhat * jnp.mean(dvh * vhat, axis=-1, keepdims=True))
        du = dgu * _gelu_grad(u)
        dv = dgv * _gelu_grad(v)

        halo = halo_ref[...]
        prev_xp = jnp.where(ti > 0, halo[:, 2 * D_A + D_B:2 * D_A + 2 * D_B] * halo[:, 2 * D_A + 2 * D_B:], 0.0)
        cwv = cw_ref[...]
        bg, cg, xb, xp, x1, x2, conv = _conv(proj, cwv, prev_xp)
        dbg = dyb2 * conv
        dconv = dyb2 * bg
        dcw[...] += jnp.concatenate(
            [jnp.sum(_colsum8(dconv * t), axis=0, keepdims=True) for t in (x2, x1, xp)] + [jnp.zeros((5, D_B), F32)], axis=0)
        nx = nxt[...]
        dxp = cwv[2:3, :] * dconv + cwv[1:2, :] * _shift_up(dconv, 1, nx) + cwv[0:1, :] * _shift_up(dconv, 2, nx)
        nxt[...] = dconv[0:CONV_HALO, :]
        dcg = dxp * xb
        dxb = dxp * cg

        dproj = jnp.concatenate([du, dv, dbg, dcg, dxb], axis=1).astype(BF16)
        for k in range(npj):
            dproj_ref[k] = dproj[:, k * HC:(k + 1) * HC]
        cat = jnp.concatenate([gu * z, bg * conv], axis=1).astype(BF16)
        for k in range(D // HC):
            cat_ref[k] = cat[:, k * HC:(k + 1) * HC]
        dh = _dot_nn(dproj, abin[...])
        dx_ref[...] = dxo + _modulate_bwd(dh, x, gn, sc, stats)

        @pl.when(i == nt - 1)
        def _():
            _stats_out(stats, st_ref)
            dnv_ref[...] = jnp.concatenate([jnp.sum(dnv[...], axis=0, keepdims=True), jnp.zeros((7, D_A), F32)], axis=0)
            r = lax.broadcasted_iota(jnp.int32, (CHUNK, CHUNK), 0)
            c = lax.broadcasted_iota(jnp.int32, (CHUNK, CHUNK), 1)
            for hd in range(A_HEADS):
                dws_ref[hd] = jnp.where(r >= c, dws[hd], 0.0)
                dbs_ref[hd] = jnp.broadcast_to(jnp.sum(dbs[hd], axis=1, keepdims=True), (CHUNK, CHUNK))
            dcw_ref[...] = dcw[...]

    rev = pl.BlockSpec((TM, D), lambda i: (nt - 1 - i, 0))
    small = pl.BlockSpec((8, D), lambda i: (0, 0))
    full = lambda a: pl.BlockSpec(a.shape, lambda i: (0,) * a.ndim)
    hpt = TM // CONV_HALO
    fixed = lambda shape: pl.BlockSpec(shape, lambda i: (0,) * len(shape))
    return _run(
        body, name="ab_bwd", grid=(nt,),
        out_shape=[jax.ShapeDtypeStruct((s, D), F32), jax.ShapeDtypeStruct((npj, s, HC), BF16),
                   jax.ShapeDtypeStruct((D // HC, s, HC), BF16), jax.ShapeDtypeStruct((s, D), BF16),
                   jax.ShapeDtypeStruct((8, D), F32), jax.ShapeDtypeStruct((8, D_A), F32),
                   jax.ShapeDtypeStruct((A_HEADS, CHUNK, CHUNK), F32), jax.ShapeDtypeStruct((A_HEADS, CHUNK, CHUNK), F32),
                   jax.ShapeDtypeStruct((8, D_B), F32)],
        in_specs=[rev, rev, small,
                  pl.BlockSpec((TM, D_AB), lambda i: (nt - 1 - i, 0)),
                  pl.BlockSpec((CONV_HALO, D_AB), lambda i: (jnp.maximum((nt - 1 - i) * hpt - 1, 0), 0)),
                  rev, ANY, ANY, full(nv), full(ws), full(bst), full(cw)],
        out_specs=[rev, pl.BlockSpec((npj, TM, HC), lambda i: (0, nt - 1 - i, 0)),
                   pl.BlockSpec((D // HC, TM, HC), lambda i: (0, nt - 1 - i, 0)), rev,
                   small, fixed((8, D_A)), fixed((A_HEADS, CHUNK, CHUNK)), fixed((A_HEADS, CHUNK, CHUNK)), fixed((8, D_B))],
        scratch_shapes=[pltpu.VMEM((D_AB, D), BF16), pltpu.VMEM((D, D), BF16), pltpu.VMEM((CONV_HALO, D_B), F32),
                        pltpu.VMEM((4, 8, D), F32), pltpu.VMEM((8, D_A), F32),
                        pltpu.VMEM((A_HEADS, CHUNK, CHUNK), F32), pltpu.VMEM((A_HEADS, CHUNK, CHUNK), F32),
                        pltpu.VMEM((8, D_B), F32), pltpu.SemaphoreType.DMA((2 * N_DEV,))],
        args=(dxo, x, p, proj, proj, out, abin_all, about_all, nv, ws, bst, cw), rider=rider)


def _pool_counts(first_token, rows):
    t = (first_token + lax.broadcasted_iota(jnp.int32, (rows, 1), 0) + 1).astype(F32)
    lane = lax.broadcasted_iota(jnp.int32, (1, D), 1)
    w = jnp.where(lane < POOL_G, 2.0, jnp.where(lane < 2 * POOL_G, 4.0, jnp.where(lane < 3 * POOL_G, 8.0, 16.0)))
    return jnp.minimum(t, w)


def _window_sums(ext, n_keep, lead, back):
    n = ext.shape[0]
    sh = (lambda v, k: pltpu.roll(v, k, 0)) if back else (lambda v, k: pltpu.roll(v, n - k, 0))
    s2 = ext + sh(ext, 1)
    s4 = s2[:, POOL_G:] + sh(s2[:, POOL_G:], 2)
    s8 = s4[:, POOL_G:] + sh(s4[:, POOL_G:], 4)
    s16 = s8[:, POOL_G:] + sh(s8[:, POOL_G:], 8)
    keep = slice(lead, lead + n_keep)
    return jnp.concatenate([s2[keep, 0:POOL_G], s4[keep, 0:POOL_G], s8[keep, 0:POOL_G], s16[keep, :]], axis=1)


def _pool_fwd(x, p, pool_all, pscale):
    s = x.shape[0]
    nt = s // TM
    ng = D // POOL_G

    def body(x_ref, p_ref, wg_ref, ps_ref, xo_ref, pb_ref, op_ref, prev):
        i = pl.program_id(0)

        @pl.when(i == 0)
        def _():
            prev[...] = jnp.zeros_like(prev)

        sh, sc, gate, gn = _mod_rows(p_ref)
        x = x_ref[...]
        h = _modulate(x, gn, sh, sc)
        win = _window_sums(jnp.concatenate([prev[...], h], axis=0), TM, POOL_HALO, True)
        prev[...] = h[TM - POOL_HALO:, :]
        pb = (win / _pool_counts(i * TM, TM) - h).astype(BF16)
        pb_ref[...] = pb
        op = jnp.concatenate(
            [_dot_nn(pb[:, g * POOL_G:(g + 1) * POOL_G], wg_ref[:, g].reshape(POOL_G, POOL_G)) for g in range(ng)], axis=1)
        op_ref[...] = op
        xo_ref[...] = x + gate * (op * ps_ref[...])

    tile = pl.BlockSpec((TM, D), lambda i: (i, 0))
    return pl.pallas_call(
        body, name="pool_fwd", grid=(nt,),
        out_shape=[jax.ShapeDtypeStruct((s, D), F32), jax.ShapeDtypeStruct((s, D), BF16), jax.ShapeDtypeStruct((s, D), F32)],
        in_specs=[tile, pl.BlockSpec((8, D), lambda i: (0, 0)),
                  pl.BlockSpec(pool_all.shape, lambda i: (0, 0, 0, 0)), pl.BlockSpec((1, D), lambda i: (0, 0))],
        out_specs=[tile, tile, tile],
        scratch_shapes=[pltpu.VMEM((POOL_HALO, D), F32)],
        compiler_params=ARB1,
    )(x, p, pool_all, pscale)


def _pool_bwd(dxo, x, p, pb, op, pool_all, pscale):
    s = x.shape[0]
    nt = s // TM
    ng = D // POOL_G

    def body(dxo_ref, x_ref, p_ref, pb_ref, op_ref, wg_ref, ps_ref,
             dx_ref, st_ref, dps_ref, dwg_ref, nxt, stats, dps, dwg):
        i = pl.program_id(0)
        ti = nt - 1 - i

        @pl.when(i == 0)
        def _():
            for z in (nxt, stats, dps, dwg):
                z[...] = jnp.zeros_like(z)

        sh, sc, gate, gn = _mod_rows(p_ref)
        x = x_ref[...]
        dxo = dxo_ref[...]
        ps = ps_ref[...]
        op = op_ref[...]
        dmo = gate * dxo
        stats[2] += _colsum8(dxo * (op * ps))
        dps[...] += _colsum8(dmo * op)
        dopb = (dmo * ps).astype(BF16)
        pbv = pb_ref[...]
        dps_parts = []
        for g in range(ng):
            sl = slice(g * POOL_G, (g + 1) * POOL_G)
            dps_parts.append(_dot_nt(dopb[:, sl], wg_ref[:, g].reshape(POOL_G, POOL_G)))
            dwg[g] += _dot_tn(pbv[:, sl], dopb[:, sl])
        dp = jnp.concatenate(dps_parts, axis=1)
        q = dp / _pool_counts(ti * TM, TM)
        wsum = _window_sums(jnp.concatenate([q, nxt[...]], axis=0), TM, 0, False)
        nxt[...] = q[0:POOL_HALO, :]
        dx_ref[...] = dxo + _modulate_bwd(wsum - dp, x, gn, sc, stats)

        @pl.when(i == nt - 1)
        def _():
            _stats_out(stats, st_ref)
            dps_ref[...] = jnp.concatenate([jnp.sum(dps[...], axis=0, keepdims=True), jnp.zeros((7, D), F32)], axis=0)
            dwg_ref[...] = dwg[...].astype(BF16)

    rev = pl.BlockSpec((TM, D), lambda i: (nt - 1 - i, 0))
    small = pl.BlockSpec((8, D), lambda i: (0, 0))
    return pl.pallas_call(
        body, name="pool_bwd", grid=(nt,),
        out_shape=[jax.ShapeDtypeStruct((s, D), F32), jax.ShapeDtypeStruct((8, D), F32), jax.ShapeDtypeStruct((8, D), F32),
                   jax.ShapeDtypeStruct((ng, POOL_G, POOL_G), BF16)],
        in_specs=[rev, rev, small, rev, rev,
                  pl.BlockSpec(pool_all.shape, lambda i: (0, 0, 0, 0)), pl.BlockSpec((1, D), lambda i: (0, 0))],
        out_specs=[rev, small, small, pl.BlockSpec((ng, POOL_G, POOL_G), lambda i: (0, 0, 0))],
        scratch_shapes=[pltpu.VMEM((POOL_HALO, D), F32), pltpu.VMEM((4, 8, D), F32), pltpu.VMEM((8, D), F32),
                        pltpu.VMEM((ng, POOL_G, POOL_G), F32)],
        compiler_params=ARB1,
    )(dxo, x, p, pb, op, pool_all, pscale)


def _head(x, fg, tgt):
    s = x.shape[0]
    nt = s // TM

    def body(x_ref, fg_ref, t_ref, dx_ref, loss_ref, dfg_ref, sq, dfg):
        i = pl.program_id(0)

        @pl.when(i == 0)
        def _():
            sq[...] = jnp.zeros_like(sq)
            dfg[...] = jnp.zeros_like(dfg)

        x = x_ref[...]
        g = fg_ref[...]
        r = lax.rsqrt(jnp.mean(x * x, axis=-1, keepdims=True) + EPS)
        xn = x * r
        e = xn * g - t_ref[...]
        sq[...] += _colsum8(e * e)
        dy = e * (1.0 / D)
        dfg[...] += _colsum8(dy * xn)
        dxn = dy * g
        dx_ref[...] = r * (dxn - xn * jnp.mean(dxn * xn, axis=-1, keepdims=True))

        @pl.when(i == nt - 1)
        def _():
            total = jnp.sum(jnp.sum(sq[...], axis=0, keepdims=True), axis=1, keepdims=True)
            loss_ref[...] = jnp.broadcast_to(total * (0.5 / D), loss_ref.shape)
            dfg_ref[...] = jnp.concatenate([jnp.sum(dfg[...], axis=0, keepdims=True), jnp.zeros((7, D), F32)], axis=0)

    tile = pl.BlockSpec((TM, D), lambda i: (i, 0))
    return pl.pallas_call(
        body, name="head", grid=(nt,),
        out_shape=[jax.ShapeDtypeStruct((s, D), F32), jax.ShapeDtypeStruct((8, 128), F32), jax.ShapeDtypeStruct((8, D), F32)],
        in_specs=[tile, pl.BlockSpec((1, D), lambda i: (0, 0)), tile],
        out_specs=[tile, pl.BlockSpec((8, 128), lambda i: (0, 0)), pl.BlockSpec((8, D), lambda i: (0, 0))],
        scratch_shapes=[pltpu.VMEM((8, D), F32), pltpu.VMEM((8, D), F32)],
        compiler_params=ARB1,
    )(x, fg, tgt)


def _adamw_math(w, g, m, v):
    m = ADAM_B1 * m + (1.0 - ADAM_B1) * g
    v = ADAM_B2 * v + (1.0 - ADAM_B2) * (g * g)
    m_hat = m / (1.0 - ADAM_B1 ** ADAM_STEP)
    v_hat = v / (1.0 - ADAM_B2 ** ADAM_STEP)
    delta = -ADAM_LR * (m_hat / (jnp.sqrt(v_hat) + ADAM_EPS) + ADAM_WD * w)
    return delta, m, v


def _finish(parts, w, m, v, rb, name, halves=False, rider=None):
    nf, r, c = w.shape
    npart = parts[0].shape[0]

    def body(*refs):
        p_refs = refs[:nf]
        w_ref, m_ref, v_ref, g_ref, d_ref, mo_ref, vo_ref = refs[nf:]
        for f in range(nf):
            @pl.when(pl.program_id(0) == f)
            def _():
                g = p_refs[f][0].astype(F32)
                for k in range(1, npart):
                    g = g + p_refs[f][k].astype(F32)
                if halves:
                    g = jnp.concatenate([g[0], g[1]], axis=1)
                g_ref[0] = g
                d_ref[0], mo_ref[0], vo_ref[0] = _adamw_math(w_ref[0], g, m_ref[0], v_ref[0])

    blk = pl.BlockSpec((1, rb, c), lambda f, i: (f, i, 0))

    def pblk(mine):
        if halves:
            return pl.BlockSpec((npart, 2, rb, c // 2), lambda f, i: (0, 0, jnp.where(f == mine, i, 0), 0))
        return pl.BlockSpec((npart, rb, c), lambda f, i: (0, jnp.where(f == mine, i, 0), 0))

    return _run(
        body, name=name, grid=(nf, r // rb),
        out_shape=[jax.ShapeDtypeStruct(w.shape, F32)] * 4,
        in_specs=[pblk(f) for f in range(nf)] + [blk, blk, blk], out_specs=[blk] * 4, scratch_shapes=[],
        args=(*parts, w, m, v), rider=rider,
        params=pltpu.CompilerParams(dimension_semantics=("arbitrary", "arbitrary"), vmem_limit_bytes=VMEM_LIMIT))


def _sum_small(parts, late, late_rows):
    def body(p_ref, l_ref, o_ref):
        acc, acc_l = p_ref[0], l_ref[0]
        for k in range(1, N_DEV):
            acc, acc_l = acc + p_ref[k], acc_l + l_ref[k]
        o_ref[...] = acc
        for a, b, r in late_rows:
            o_ref[r:r + b - a, :] += acc_l[a:b, :]

    return pl.pallas_call(body, name="sum_small", out_shape=jax.ShapeDtypeStruct(parts.shape[1:], F32),
                          in_specs=[VMEM_SPEC, VMEM_SPEC], out_specs=VMEM_SPEC)(parts, late)


def _adamw(w, g, m, v, name):
    def body(w_ref, g_ref, m_ref, v_ref, d_ref, mo_ref, vo_ref):
        d_ref[...], mo_ref[...], vo_ref[...] = _adamw_math(w_ref[...], g_ref[...], m_ref[...], v_ref[...])

    return pl.pallas_call(
        body, name=name, out_shape=[jax.ShapeDtypeStruct(w.shape, F32)] * 3,
        in_specs=[VMEM_SPEC] * 4, out_specs=[VMEM_SPEC] * 3,
    )(w, g, m, v)


def _wmod_finish(act_t, dmod_cols, w, m, v):
    rb = 256
    ncol = w.shape[-1]

    def body(a_ref, dm_ref, w_ref, m_ref, v_ref, g_ref, d_ref, mo_ref, vo_ref):
        g = a_ref[:, 0:1] * dm_ref[0, 0:1, :]
        for k in range(1, N_DEV):
            g = g + a_ref[:, k:k + 1] * dm_ref[0, k:k + 1, :]
        g_ref[0] = g
        d_ref[0], mo_ref[0], vo_ref[0] = _adamw_math(w_ref[0], g, m_ref[0], v_ref[0])

    blk = pl.BlockSpec((1, rb, ncol), lambda l, i: (l, i, 0))
    return pl.pallas_call(
        body, name="wmod_finish", grid=(2, D // rb),
        out_shape=[jax.ShapeDtypeStruct(w.shape, F32)] * 4,
        in_specs=[pl.BlockSpec((rb, N_DEV), lambda l, i: (i, 0)), pl.BlockSpec((1, N_DEV, ncol), lambda l, i: (l, 0, 0)),
                  blk, blk, blk],
        out_specs=[blk] * 4,
        compiler_params=pltpu.CompilerParams(dimension_semantics=("arbitrary", "arbitrary"), vmem_limit_bytes=VMEM_LIMIT),
    )(act_t, dmod_cols, w, m, v)


def _pack(pieces):
    flat, offs, at = [], [], 0
    for a in pieces:
        a = a.reshape(-1)
        n = -(-a.shape[0] // 128) * 128
        flat.append(jnp.pad(a, (0, n - a.shape[0])))
        offs.append(at)
        at += n
    return jnp.concatenate(flat).reshape(-1, 128), offs


def _param_block(mod_l, sub, gn):
    return jnp.concatenate([mod_l[sub], gn[None, :], jnp.zeros((4, D), F32)], axis=0)


def kernel(x, c, norm_g, w_mod, b_mod, w_ffn_in, w_ffn_out, ab_w_in, ab_norm_v, ab_w_s, ab_b_s, ab_conv_w, ab_w_out, pool_w_grp, pool_scale, final_g, loss_target, m_norm_g, m_w_mod, m_b_mod, m_w_ffn_in, m_w_ffn_out, m_ab_w_in, m_ab_norm_v, m_ab_w_s, m_ab_b_s, m_ab_conv_w, m_ab_w_out, m_pool_w_grp, m_pool_scale, m_final_g, v_norm_g, v_w_mod, v_b_mod, v_w_ffn_in, v_w_ffn_out, v_ab_w_in, v_ab_norm_v, v_ab_w_s, v_ab_b_s, v_ab_conv_w, v_ab_w_out, v_pool_w_grp, v_pool_scale, v_final_g):
    me = 4 * lax.axis_index("x") + 2 * lax.axis_index("y") + lax.axis_index("c")
    x0 = x[0]
    tgt = loss_target[0]
    n_in = w_ffn_in.shape[-1]
    n_out = w_ffn_out.shape[-2]
    n_abin = ab_w_in.shape[-1]
    n_about = ab_w_out.shape[-2]
    n_pool = pool_w_grp.shape[-2]
    n_mod = w_mod.shape[-1]
    n_ng = norm_g.shape[-1]
    n_cw = ab_conv_w.shape[-1]
    n_ps = pool_scale.shape[-1]

    win_sh = jnp.swapaxes(w_ffn_in.reshape(4, D, n_in), 1, 2).astype(BF16)
    wout_sh = w_ffn_out.reshape(4, n_out, D).astype(BF16)
    abin_sh = ab_w_in[0].T.astype(BF16)
    about_sh = ab_w_out[0].astype(BF16)
    pool_sh = pool_w_grp[0].astype(BF16)
    win, wout = [None] * 4, [None] * 4

    pack, offs = _pack([c, norm_g, ab_conv_w, pool_scale])
    got, (win[0],) = _exchange_small(pack, "gather_small", _GatherRider([(win_sh, 0)]))
    got = got.reshape(N_DEV, -1)
    c_all = got[:, offs[0]:offs[0] + D]
    ng_full = got[:, offs[1]:offs[1] + 6 * n_ng].reshape(N_DEV, 2, 3, n_ng).transpose(1, 2, 0, 3).reshape(2, 3, D)
    cw_full = got[:, offs[2]:offs[2] + 3 * n_cw].reshape(N_DEV, 3, n_cw).transpose(1, 0, 2).reshape(3, D_B)
    ps_full = got[:, offs[3]:offs[3] + n_ps].reshape(1, D)

    act_all, mod_cols = _mod_fwd(c_all, w_mod)
    mod_got, (wout[0],) = _exchange_small(mod_cols.reshape(-1, 128), "gather_mod", _GatherRider([(wout_sh, 0)]))
    mod_got = mod_got.reshape(N_DEV, 2, N_DEV, n_mod)
    mod = lax.dynamic_index_in_dim(mod_got, me, axis=2, keepdims=False).transpose(1, 0, 2).reshape(2, 9 * D) + b_mod
    mod = mod.reshape(2, 3, 3, D)
    nv = ab_norm_v
    ws = ab_w_s[0]
    bst = ab_b_s[0].T
    cw8 = jnp.concatenate([cw_full, jnp.zeros((5, D_B), F32)], axis=0)

    pb = [[_param_block(mod[l], s, ng_full[l, s]) for s in range(3)] for l in range(2)]
    (x1, h00, g00, u00, y00), (abin_all, about_all, win[1]) = _ffn_fwd(
        x0, pb[0][0], win[0], wout[0], 0, _GatherRider([abin_sh, about_sh, (win_sh, 1)]))
    (x2, h01, proj, ab_out), (wout[1],) = _ab_fwd(
        x1, pb[0][1], abin_all, about_all, nv, ws, bst, cw8, _GatherRider([(wout_sh, 1)]))
    (x3, h02, g02, u02, y02), (win[2], wout[2]) = _ffn_fwd(
        x2, pb[0][2], win[1], wout[1], 1, _GatherRider([(win_sh, 2), (wout_sh, 2)]))
    (x4, h10, g10, u10, y10), (pool_all, win[3], wout[3]) = _ffn_fwd(
        x3, pb[1][0], win[2], wout[2], 2, _GatherRider([pool_sh, (win_sh, 3), (wout_sh, 3)]))
    x5, pooled, pool_out = _pool_fwd(x4, pb[1][1], pool_all, ps_full)
    (x6, h12, g12, u12, y12), _ = _ffn_fwd(x5, pb[1][2], win[3], wout[3], 3)
    dx6, loss_blk, dfg = _head(x6, final_g.reshape(1, D), tgt)

    core = lax.axis_index("c")
    sel_rows = lambda n: jnp.stack([core * n, 0]).astype(jnp.int32)
    sel_cols = jnp.stack([0, core]).astype(jnp.int32)
    p_in, p_out = [None] * 4, [None] * 4

    def exchange(dgu, a, h, dy):
        return _SiblingRider([(dgu, NCH), (a, None), (h, None), (dy, None)])

    pending = [None] * 4

    def ffn_wgrads(f, dgu, a, h, dy, got, ride_out=None):
        s_dgu, s_a, s_h, s_dy = got
        g_out, rode = _wgrad_pair(a, s_a, dy, s_dy, sel_cols, NCH, True, f"wgrad_out_{f}", ride_out)
        g_in, _ = _wgrad_pair(dgu, s_dgu, h, s_h, sel_rows(NCH), NCH, False, f"wgrad_in_{f}")
        rider, state, token = _scatter_start([g_in], [g_out], f"scatter_start_{f}")
        pending[f] = (rider, state)
        return token[0, 0], rode

    (dx5, dgu12, a12, dy12, st12), _ = _ffn_bwd(dx6, x5, pb[1][2], g12, u12, y12, win[3], wout[3], 3)
    dx4, st11, dps, gw_pool = _pool_bwd(dx5, x4, pb[1][1], pooled, pool_out, pool_all, ps_full)
    (dx3, dgu10, a10, dy10, st10), got3 = _ffn_bwd(
        dx4, x3, pb[1][0], g10, u10, y10, win[2], wout[2], 2, exchange(dgu12, a12, h12, dy12))
    tok, (p_pool,) = ffn_wgrads(3, dgu12, a12, h12, dy12, got3, _ScatterRider([gw_pool]))
    (dx2, dgu02, a02, dy02, st02), got2 = _ffn_bwd(
        dx3, x2, pb[0][2] + tok, g02, u02, y02, win[1], wout[1], 1, exchange(dgu10, a10, h10, dy10))
    tok, _ = ffn_wgrads(2, dgu10, a10, h10, dy10, got2)
    (dx1, dproj, cat, dy01, st01, dnv, dws, dbs, dcw), got1 = _ab_bwd(
        dx2, x1, pb[0][1] + tok, proj, ab_out, abin_all, about_all, nv, ws, bst, cw8, exchange(dgu02, a02, h02, dy02))
    tok, _ = ffn_wgrads(1, dgu02, a02, h02, dy02, got1)
    zero_stats = jnp.zeros((8, D), F32)
    stats = [[zero_stats, st01, st02], [st10, st11, st12]]
    dmod = jnp.stack([jnp.concatenate([stats[l][s][0:3].reshape(-1) for s in range(3)]) for l in range(2)])
    dng = jnp.stack([jnp.stack([stats[l][s][3] for s in range(3)]) for l in range(2)])
    spack, so = _pack([dmod, dng, dnv[0], dws, dbs[:, :, 0], dcw[0:3], dps[0], dfg[0], loss_blk[0]])
    ride = _Riders([_SiblingRider([(dproj, 5), (cat, 2), (h01, None), (dy01, None)]), _SmallGatherRider(spack)])
    (dgu00, a00, dy00, gst00), rode = _ffn_bwd_acts(dx1, pb[0][0] + tok, g00, u00, y00, wout[0], 0, ride)
    (s_dproj, s_cat, s_h01, s_dy01), (sgot,) = ride.split(rode)
    g_about, _ = _wgrad_pair(cat, s_cat, dy01, s_dy01, sel_rows(2), 2, False, "wgrad_ab_out")
    g_abin, (p_about,) = _wgrad_pair(dproj, s_dproj, h01, s_h01, sel_rows(5), 5, False, "wgrad_ab_in",
                                     _ChipScatterRider([g_about], []))
    (dx0, st00), got0 = _ffn_bwd_dx(dx1, x0, pb[0][0], dgu00, gst00, win[0], 0, exchange(dgu00, a00, h00, dy00))
    grad_x = dx0[None]
    ride = _Riders([_ChipScatterRider([g_abin], []), _SmallGatherRider(st00[0:4].reshape(-1, 128))])
    tok, rode = ffn_wgrads(0, dgu00, a00, h00, dy00, got0, ride)
    (p_abin,), (sgot0,) = ride.split(rode)

    shape_in, shape_out = w_ffn_in.shape, w_ffn_out.shape
    fin = lambda a: jnp.swapaxes(a.reshape(4, D, n_in), 1, 2)
    fabin = lambda a: jnp.swapaxes(a, 1, 2)
    fout = lambda a: a.reshape(4, n_out, D)
    fpool = lambda a: a.reshape(1, 4 * n_pool, POOL_G)
    r_abin, _ = _finish([p_abin], fabin(ab_w_in), fabin(m_ab_w_in), fabin(v_ab_w_in), n_abin, "finish_ab_in")
    r_about, _ = _finish([p_about + tok.astype(BF16)], ab_w_out, m_ab_w_out, v_ab_w_out, n_about, "finish_ab_out")
    r_pool, _ = _finish([p_pool.reshape(N_DEV, 4 * n_pool, POOL_G)], fpool(pool_w_grp), fpool(m_pool_w_grp),
                        fpool(v_pool_w_grp), 4 * n_pool, "finish_pool")

    rows_mod, rows_gn = 3 * D // 128, D // 128
    ssum = _sum_small(sgot, sgot0, [(0, rows_mod, so[0] // 128), (rows_mod, rows_mod + rows_gn, so[1] // 128)]).reshape(-1)
    loss = ssum[so[8]]
    take = lambda i, n: lax.dynamic_slice_in_dim(ssum, so[i], n)
    g_bmod = take(0, 2 * 9 * D).reshape(2, 9 * D)
    g_ng = lax.dynamic_slice_in_dim(take(1, 6 * D).reshape(2, 3, D), me * n_ng, n_ng, axis=2)
    g_nv = take(2, D_A).reshape(1, D_A)
    g_ws = take(3, A_HEADS * CHUNK * CHUNK).reshape(1, A_HEADS, CHUNK, CHUNK)
    g_bs = take(4, A_HEADS * CHUNK).reshape(1, A_HEADS, CHUNK)
    g_cw = lax.dynamic_slice_in_dim(take(5, 3 * D_B).reshape(1, 3, D_B), me * n_cw, n_cw, axis=2)
    g_ps = lax.dynamic_slice_in_dim(take(6, D).reshape(1, D), me * n_ps, n_ps, axis=1)
    g_fg = take(7, D)

    dmod_all = sgot.reshape(N_DEV, -1)[:, so[0]:so[0] + 2 * 9 * D].reshape(N_DEV, 2, 9 * D)
    dmod_all = dmod_all.at[:, 0, 0:3 * D].set(sgot0[:, 0:rows_mod].reshape(N_DEV, 3 * D))
    dmod_cols = lax.dynamic_slice_in_dim(dmod_all, me * n_mod, n_mod, axis=2).transpose(1, 0, 2)
    r_wmod = _wmod_finish(act_all.T, dmod_cols, w_mod, m_w_mod, v_w_mod)

    for f in (3, 2, 1, 0):
        p_in[f], p_out[f] = _scatter_wait(*pending[f], r_wmod[1], f"scatter_wait_{f}")
    r_out, _ = _finish(p_out, fout(w_ffn_out), fout(m_w_ffn_out), fout(v_w_ffn_out), n_out // 2, "finish_ffn_out", halves=True)
    r_in, _ = _finish(p_in, fin(w_ffn_in), fin(m_w_ffn_in), fin(v_w_ffn_in), n_in // 4, "finish_ffn_in")
    r_in = [jnp.swapaxes(a, 1, 2).reshape(shape_in) for a in r_in]
    r_abin = [jnp.swapaxes(a, 1, 2) for a in r_abin]
    r_out = [a.reshape(shape_out) for a in r_out]
    r_pool = [a.reshape(pool_w_grp.shape) for a in r_pool]

    small_w = [b_mod, norm_g, ab_norm_v, ab_w_s, ab_b_s, ab_conv_w, pool_scale, final_g]
    small_g = [g_bmod, g_ng, g_nv, g_ws, g_bs, g_cw, g_ps, g_fg]
    small_m = [m_b_mod, m_norm_g, m_ab_norm_v, m_ab_w_s, m_ab_b_s, m_ab_conv_w, m_pool_scale, m_final_g]
    small_v = [v_b_mod, v_norm_g, v_ab_norm_v, v_ab_w_s, v_ab_b_s, v_ab_conv_w, v_pool_scale, v_final_g]
    pw, po = _pack(small_w)
    pv = jnp.concatenate([jnp.pad(a.reshape(-1), (0, -a.size % 128), constant_values=1.0) for a in small_v]).reshape(-1, 128)
    sd, sm, sv = _adamw(pw, _pack(small_g)[0], _pack(small_m)[0], pv, "adamw_small")
    unpack = lambda packed: [packed.reshape(-1)[po[i]:po[i] + a.size].reshape(a.shape) for i, a in enumerate(small_w)]
    d_s, m_s, v_s = unpack(sd), unpack(sm), unpack(sv)

    def ordered(k, small):
        return [small[1], r_wmod[k], small[0], r_in[k], r_out[k], r_abin[k], small[2], small[3], small[4], small[5],
                r_about[k], r_pool[k], small[6], small[7]]

    grads = ordered(0, small_g)
    deltas = ordered(1, d_s)
    new_m = ordered(2, m_s)
    new_v = ordered(3, v_s)
    return (loss, grad_x, *grads, *deltas, *new_m, *new_v)
```

```python
import functools
import math

import jax
import jax.numpy as jnp
from jax import lax
from jax.experimental import pallas as pl
from jax.experimental.pallas import tpu as pltpu

F32 = jnp.float32
BF16 = jnp.bfloat16

N_DEV = 8
D = 1024
DFF = 2816
HC = 256
NCH = DFF // HC
D_A = 512
D_B = 512
D_AB = 2 * D_A + 3 * D_B
CHUNK = 128
A_HEADS = 4
POOL_G = 256
POOL_HALO = 16
CONV_HALO = 8
EPS = 1e-6
TM = 256
GELU_K = math.sqrt(2.0 / math.pi)
GELU_C = 0.044715

ADAM_LR = 0.001
ADAM_B1 = 0.9
ADAM_B2 = 0.999
ADAM_EPS = 1e-08
ADAM_WD = 0.01
ADAM_STEP = 10

VMEM_LIMIT = 56 * 1024 * 1024
MESH_ID = pl.DeviceIdType.MESH
ANY = pl.BlockSpec(memory_space=pl.ANY)
VMEM_SPEC = pl.BlockSpec(memory_space=pltpu.VMEM)
ARB1 = pltpu.CompilerParams(dimension_semantics=("arbitrary",), vmem_limit_bytes=VMEM_LIMIT)


def _dot_nt(a, b):
    return lax.dot_general(a, b, (((1,), (1,)), ((), ())), preferred_element_type=F32)


def _dot_nn(a, b):
    return lax.dot_general(a, b, (((1,), (0,)), ((), ())), preferred_element_type=F32)


def _dot_tn(a, b):
    return lax.dot_general(a, b, (((0,), (0,)), ((), ())), preferred_element_type=F32)


def _colsum8(v):
    r, n = v.shape
    return jnp.sum(v.reshape(r // 8, 8, n), axis=0)


def _gelu(x):
    return 0.5 * x * (1.0 + jnp.tanh(GELU_K * (x + GELU_C * x * x * x)))


def _gelu_grad(x):
    t = jnp.tanh(GELU_K * (x + GELU_C * x * x * x))
    return 0.5 * (1.0 + t) + 0.5 * x * (1.0 - t * t) * (GELU_K * (1.0 + 3.0 * GELU_C * x * x))


def _mod_rows(p_ref):
    return p_ref[0:1, :], p_ref[1:2, :], p_ref[2:3, :], p_ref[3:4, :]


def _modulate(x, gn, sh, sc):
    r = lax.rsqrt(jnp.mean(x * x, axis=-1, keepdims=True) + EPS)
    return ((x * r) * gn) * (1.0 + sc) + sh


def _modulate_bwd(dh, x, gn, sc, stats):
    r = lax.rsqrt(jnp.mean(x * x, axis=-1, keepdims=True) + EPS)
    xn = x * r
    stats[0] += _colsum8(dh)
    stats[1] += _colsum8(dh * (xn * gn))
    dy0 = dh * (1.0 + sc)
    stats[3] += _colsum8(dy0 * xn)
    dxn = dy0 * gn
    return r * (dxn - xn * jnp.mean(dxn * xn, axis=-1, keepdims=True))


def _stats_out(stats, out_ref):
    rows = [jnp.sum(stats[k], axis=0, keepdims=True) for k in range(4)]
    out_ref[...] = jnp.concatenate(rows + [jnp.zeros((4, stats.shape[-1]), F32)], axis=0)


def _shift_down(v, k, prev):
    n = v.shape[0]
    row = lax.broadcasted_iota(jnp.int32, v.shape, 0)
    out = pltpu.roll(v, k, 0)
    for j in range(k):
        out = jnp.where(row == j, prev[prev.shape[0] - k + j:prev.shape[0] - k + j + 1, :], out)
    return out


def _shift_up(v, k, nxt):
    n = v.shape[0]
    row = lax.broadcasted_iota(jnp.int32, v.shape, 0)
    out = pltpu.roll(v, n - k, 0)
    for j in range(k):
        out = jnp.where(row == n - k + j, nxt[j:j + 1, :], out)
    return out


def _load_rows(w_hbm, sel, dst, sems, base):
    n = dst.shape[0] // N_DEV
    cps = []
    for k in range(N_DEV):
        src = w_hbm.at[k] if sel is None else w_hbm.at[k, sel]
        cps.append(pltpu.make_async_copy(src, dst.at[pl.ds(k * n, n)], sems.at[base + k]))
    return cps


def _my_pos():
    return lax.axis_index("x"), lax.axis_index("y"), lax.axis_index("c")


def _peer(j):
    x, y, c = _my_pos()
    return (1 - x if j & 4 else x, 1 - y if j & 2 else y, 1 - c if j & 1 else c)


def _index(pos):
    return 4 * pos[0] + 2 * pos[1] + pos[2]


def _exchange_small(v, name, rider):
    rows = v.shape[0]
    ri, ro = len(rider.inputs), len(rider.out_shapes)

    def body(v_ref, *refs):
        r_in, out_ref, r_out, refs = refs[:ri], refs[ri], refs[ri + 1:ri + 1 + ro], refs[ri + 1 + ro:]
        send_sems, recv_sems, local_sem = refs[:3]
        rider.first(r_in, r_out, refs[3:])
        me = _index(_my_pos())

        def copy(j, slot):
            return pltpu.make_async_remote_copy(
                src_ref=v_ref, dst_ref=out_ref.at[slot], send_sem=send_sems.at[j - 1], recv_sem=recv_sems.at[j - 1],
                device_id=_peer(j), device_id_type=MESH_ID)

        mine = pltpu.make_async_copy(v_ref, out_ref.at[me], local_sem)
        mine.start()
        sends = [copy(j, me) for j in range(1, N_DEV)]
        for cp in sends:
            cp.start()
        for j in range(1, N_DEV):
            copy(j, _index(_peer(j))).wait_recv()
        for cp in sends:
            cp.wait_send()
        mine.wait()
        if rider.has_middle:
            rider.middle(r_in, r_out, refs[3:])
        rider.last(r_in, r_out, refs[3:])

    scratch = [pltpu.SemaphoreType.DMA((N_DEV - 1,)), pltpu.SemaphoreType.DMA((N_DEV - 1,)), pltpu.SemaphoreType.DMA(())]
    res = pl.pallas_call(
        body, name=name, out_shape=[jax.ShapeDtypeStruct((N_DEV, rows, 128), F32)] + rider.out_shapes,
        in_specs=[VMEM_SPEC] + [ANY] * ri, out_specs=[VMEM_SPEC] + [ANY] * ro,
        scratch_shapes=scratch + rider.scratch)(v, *rider.inputs)
    return res[0], list(res[1:])


class _SmallGatherRider:
    has_middle = False

    def __init__(self, v):
        self.inputs = [v]
        self.out_shapes = [jax.ShapeDtypeStruct((N_DEV,) + v.shape, v.dtype)]
        self.scratch = [pltpu.SemaphoreType.DMA((N_DEV - 1,)), pltpu.SemaphoreType.DMA((N_DEV - 1,)), pltpu.SemaphoreType.DMA(())]

    def _copy(self, ins, outs, scr, j, slot):
        return pltpu.make_async_remote_copy(
            src_ref=ins[0], dst_ref=outs[0].at[slot], send_sem=scr[0].at[j - 1], recv_sem=scr[1].at[j - 1],
            device_id=_peer(j), device_id_type=MESH_ID)

    def first(self, ins, outs, scr):
        me = _index(_my_pos())
        pltpu.make_async_copy(ins[0], outs[0].at[me], scr[2]).start()
        for j in range(1, N_DEV):
            self._copy(ins, outs, scr, j, me).start()

    def last(self, ins, outs, scr):
        me = _index(_my_pos())
        for j in range(1, N_DEV):
            self._copy(ins, outs, scr, j, _index(_peer(j))).wait_recv()
        for j in range(1, N_DEV):
            self._copy(ins, outs, scr, j, me).wait_send()
        pltpu.make_async_copy(ins[0], outs[0].at[me], scr[2]).wait()


class _GatherRider:
    has_middle = True

    def __init__(self, shards):
        pairs = [s if isinstance(s, tuple) else (s, None) for s in shards]
        self.inputs = [a for a, _ in pairs]
        self.picks = [i for _, i in pairs]
        shapes = [a.shape if i is None else a.shape[1:] for a, i in pairs]
        self.out_shapes = [jax.ShapeDtypeStruct((N_DEV,) + s, a.dtype) for s, (a, _) in zip(shapes, pairs)]
        self.pieces = [4 if (len(s) > 2 or s[0] % 64 == 0) else 2 for s in shapes]
        self.base = [sum(1 + 6 * n for n in self.pieces[:a]) for a in range(len(pairs))]
        total = sum(1 + 6 * n for n in self.pieces)
        self.scratch = [pltpu.SemaphoreType.DMA((total,)), pltpu.SemaphoreType.DMA((total,)),
                        pltpu.SemaphoreType.DMA((len(pairs),))]

    def _src(self, ins):
        return [r if i is None else r.at[i] for r, i in zip(ins, self.picks)]

    def _ctx(self, outs, scr):
        send, recv, _ = scr
        x, y, c = _my_pos()
        chips = [(1 - x, y), (x, 1 - y), (1 - x, 1 - y)]

        def copy(a, k, block, to, src=None, piece=None):
            slot = outs[a].at[_index(block)]
            src = slot if src is None else src
            if piece is not None:
                rows = slot.shape[0] // self.pieces[a]
                slot, src = slot.at[pl.ds(piece * rows, rows)], src.at[pl.ds(piece * rows, rows)]
            return pltpu.make_async_remote_copy(
                src_ref=src, dst_ref=slot, send_sem=send.at[self.base[a] + k], recv_sem=recv.at[self.base[a] + k],
                device_id=to, device_id_type=MESH_ID)

        return (x, y, c), (x, y, 1 - c), chips, copy

    def _sends(self, ins, outs, scr):
        me, sib, chips, copy = self._ctx(outs, scr)
        srcs = self._src(ins)
        out = [copy(a, 0, me, sib, src=srcs[a]) for a in range(len(ins))]
        for s in range(4):
            for a in range(len(ins)):
                if s < self.pieces[a]:
                    out += [copy(a, 1 + j * self.pieces[a] + s, me, (*chips[j], me[2]), src=srcs[a], piece=s) for j in range(3)]
        return out

    def first(self, ins, outs, scr):
        me = _index(_my_pos())
        for a, src in enumerate(self._src(ins)):
            pltpu.make_async_copy(src, outs[a].at[me], scr[2].at[a]).start()
        for cp in self._sends(ins, outs, scr):
            cp.start()

    def _forwards(self, ins, outs, scr, core):
        me, sib, chips, copy = self._ctx(outs, scr)
        out = []
        for s in range(4):
            for a in range(len(ins)):
                n = self.pieces[a]
                if s < n:
                    for j in range(3):
                        out.append((copy(a, 1 + j * n + s, (*chips[j], core), me, piece=s),
                                    copy(a, 1 + 3 * n + j * n + s, (*chips[j], core), sib, piece=s)))
        return out

    def middle(self, ins, outs, scr):
        me = _my_pos()
        for arrival, forward in self._forwards(ins, outs, scr, me[2]):
            arrival.wait_recv()
            forward.start()

    def last(self, ins, outs, scr):
        me, sib, chips, copy = self._ctx(outs, scr)
        for a in range(len(ins)):
            copy(a, 0, sib, me).wait_recv()
        for _, forward in self._forwards(ins, outs, scr, sib[2]):
            forward.wait_recv()
        for cp in self._sends(ins, outs, scr):
            cp.wait_send()
        for _, forward in self._forwards(ins, outs, scr, me[2]):
            forward.wait_send()
        for a, src in enumerate(self._src(ins)):
            pltpu.make_async_copy(src, outs[a].at[_index(me)], scr[2].at[a]).wait()


class _ScatterRider:
    has_middle = False

    def __init__(self, grads):
        n = len(grads)
        self.inputs = list(grads)
        self.out_shapes = []
        for g in grads:
            if g.ndim == 3:
                self.out_shapes.append(jax.ShapeDtypeStruct((N_DEV, g.shape[0], g.shape[1] // N_DEV, g.shape[2]), g.dtype))
            else:
                self.out_shapes.append(jax.ShapeDtypeStruct((N_DEV, g.shape[0] // N_DEV, g.shape[1]), g.dtype))
        self.scratch = [pltpu.SemaphoreType.DMA((7 * n,)), pltpu.SemaphoreType.DMA((7 * n,)), pltpu.SemaphoreType.DMA((n,))]

    @staticmethod
    def _part(ref, k):
        if ref.ndim == 3:
            n = ref.shape[1] // N_DEV
            return ref.at[:, pl.ds(pl.multiple_of(k * n, 16), n)]
        n = ref.shape[0] // N_DEV
        return ref.at[pl.ds(pl.multiple_of(k * n, 16), n)]

    def _copy(self, ins, outs, scr, g, j, to, src_dev):
        return pltpu.make_async_remote_copy(
            src_ref=self._part(ins[g], to), dst_ref=outs[g].at[src_dev], send_sem=scr[0].at[7 * g + j - 1],
            recv_sem=scr[1].at[7 * g + j - 1], device_id=_peer(j), device_id_type=MESH_ID)

    def first(self, ins, outs, scr):
        me = _index(_my_pos())
        for g in range(len(ins)):
            pltpu.make_async_copy(self._part(ins[g], me), outs[g].at[me], scr[2].at[g]).start()
        for j in range(1, N_DEV):
            for g in range(len(ins)):
                self._copy(ins, outs, scr, g, j, _index(_peer(j)), me).start()

    def last(self, ins, outs, scr):
        me = _index(_my_pos())
        for j in range(1, N_DEV):
            for g in range(len(ins)):
                self._copy(ins, outs, scr, g, j, me, _index(_peer(j))).wait_recv()
        for j in range(1, N_DEV):
            for g in range(len(ins)):
                self._copy(ins, outs, scr, g, j, _index(_peer(j)), me).wait_send()
        for g in range(len(ins)):
            pltpu.make_async_copy(self._part(ins[g], me), outs[g].at[me], scr[2].at[g]).wait()


class _SiblingRider:
    has_middle = False

    def __init__(self, items):
        self.inputs = [a for a, _ in items]
        self.counts = [n for _, n in items]
        self.out_shapes = [jax.ShapeDtypeStruct(a.shape if n is None else (n,) + a.shape[1:], a.dtype) for a, n in items]
        self.scratch = [pltpu.SemaphoreType.DMA((len(items),)), pltpu.SemaphoreType.DMA((len(items),))]

    def _copies(self, ins, outs, scr):
        x, y, c = _my_pos()
        out = []
        for i, (ref, n) in enumerate(zip(ins, self.counts)):
            src = ref if n is None else ref.at[pl.ds((1 - c) * n, n)]
            out.append(pltpu.make_async_remote_copy(
                src_ref=src, dst_ref=outs[i], send_sem=scr[0].at[i], recv_sem=scr[1].at[i],
                device_id=(x, y, 1 - c), device_id_type=MESH_ID))
        return out

    def first(self, ins, outs, scr):
        for cp in self._copies(ins, outs, scr):
            cp.start()

    def last(self, ins, outs, scr):
        for cp in self._copies(ins, outs, scr):
            cp.wait()


class _ChipScatterRider:
    has_middle = False

    def __init__(self, rows, cols):
        self.nr, self.nc = len(rows), len(cols)
        self.inputs = list(rows) + list(cols)
        self.out_shapes = [jax.ShapeDtypeStruct((4, a.shape[0] // 4, a.shape[1]), a.dtype) for a in rows]
        self.out_shapes += [jax.ShapeDtypeStruct((4, 2, a.shape[0] // N_DEV, a.shape[1]), a.dtype) for a in cols]
        n = 4 * self.nr + N_DEV * self.nc
        self.scratch = [pltpu.SemaphoreType.DMA((n,)), pltpu.SemaphoreType.DMA((n,)), pltpu.SemaphoreType.DMA((n,))]

    def _pieces(self, ins, outs):
        x, y, c = _my_pos()
        q = 2 * x + y
        out = []
        for a in range(self.nr):
            n = ins[a].shape[0] // 4
            for j in range(4):
                out.append((4 * a + j, ins[a].at[pl.ds(j * n, n)], (c, j >> 1, j & 1), outs[a].at[q], 4 * a + q))
        for a in range(self.nc):
            ref, base = ins[self.nr + a], 4 * self.nr + N_DEV * a
            n = ref.shape[0] // N_DEV
            for k in range(N_DEV):
                out.append((base + k, ref.at[pl.ds(k * n, n)], (k >> 2, (k >> 1) & 1, k & 1),
                            outs[self.nr + a].at[q, c], base + 2 * q + c))
        return out

    def first(self, ins, outs, scr):
        send, recv, local = scr
        me = _index(_my_pos())
        for s, src, to, slot, r in self._pieces(ins, outs):
            mine = _index(to) == me

            @pl.when(mine)
            def _():
                pltpu.make_async_copy(src, slot, local.at[s]).start()

            @pl.when(jnp.logical_not(mine))
            def _():
                pltpu.make_async_remote_copy(src_ref=src, dst_ref=slot, send_sem=send.at[s], recv_sem=recv.at[r],
                                             device_id=to, device_id_type=MESH_ID).start()

    def last(self, ins, outs, scr):
        send, recv, local = scr
        x, y, c = _my_pos()
        me = _index((x, y, c))
        arrivals = []
        for a in range(self.nr):
            n = ins[a].shape[0] // 4
            for q in range(4):
                arrivals.append((4 * a + q, (q >> 1, q & 1, x), ins[a].at[pl.ds(0, n)], outs[a].at[q], 4 * a + 2 * y + c))
        for a in range(self.nc):
            ref, base = ins[self.nr + a], 4 * self.nr + N_DEV * a
            n = ref.shape[0] // N_DEV
            for k in range(N_DEV):
                arrivals.append((base + k, (k >> 2, (k >> 1) & 1, k & 1), ref.at[pl.ds(0, n)],
                                 outs[self.nr + a].at[k >> 1, k & 1], base + me))
        for r, sender, src, slot, s_local in arrivals:
            mine = _index(sender) == me

            @pl.when(mine)
            def _():
                pltpu.make_async_copy(src, slot, local.at[s_local]).wait()

            @pl.when(jnp.logical_not(mine))
            def _():
                pltpu.make_async_remote_copy(src_ref=src, dst_ref=slot, send_sem=send.at[r], recv_sem=recv.at[r],
                                             device_id=sender, device_id_type=MESH_ID).wait_recv()

        for s, src, to, slot, r in self._pieces(ins, outs):
            @pl.when(_index(to) != me)
            def _():
                pltpu.make_async_remote_copy(src_ref=src, dst_ref=slot, send_sem=send.at[s], recv_sem=recv.at[r],
                                             device_id=to, device_id_type=MESH_ID).wait_send()


class _Riders:
    def __init__(self, riders):
        self.riders = list(riders)
        self.inputs = [a for r in self.riders for a in r.inputs]
        self.out_shapes = [s for r in self.riders for s in r.out_shapes]
        self.scratch = [s for r in self.riders for s in r.scratch]
        self.has_middle = any(r.has_middle for r in self.riders)

    def _each(self, ins, outs, scr):
        i = o = s = 0
        for r in self.riders:
            ni, no, ns = len(r.inputs), len(r.out_shapes), len(r.scratch)
            yield r, ins[i:i + ni], outs[o:o + no], scr[s:s + ns]
            i, o, s = i + ni, o + no, s + ns

    def first(self, ins, outs, scr):
        for r, a, b, c in self._each(ins, outs, scr):
            r.first(a, b, c)

    def middle(self, ins, outs, scr):
        for r, a, b, c in self._each(ins, outs, scr):
            if r.has_middle:
                r.middle(a, b, c)

    def last(self, ins, outs, scr):
        for r, a, b, c in self._each(ins, outs, scr):
            r.last(a, b, c)

    def split(self, outs):
        res, o = [], 0
        for r in self.riders:
            res.append(list(outs[o:o + len(r.out_shapes)]))
            o += len(r.out_shapes)
        return res


def _run(body, *, name, grid, in_specs, out_specs, out_shape, scratch_shapes, args, rider=None, params=None, prefetch=()):
    params = ARB1 if params is None else params
    npf = len(prefetch)
    ni, no, ns = len(in_specs), len(out_shape), len(scratch_shapes)
    ri, ro = (len(rider.inputs), len(rider.out_shapes)) if rider is not None else (0, 0)

    def wrapped(*refs):
        pf, refs = refs[:npf], refs[npf:]
        cut = [ni, ni + ri, ni + ri + no, ni + ri + no + ro, ni + ri + no + ro + ns]
        a, b, c, d, e, f = (refs[lo:hi] for lo, hi in zip([0] + cut, cut + [len(refs)]))
        if rider is not None:
            ids = [pl.program_id(k) for k in range(len(grid))]
            at_first = functools.reduce(jnp.logical_and, [i == 0 for i in ids])
            at_last = functools.reduce(jnp.logical_and, [i == n - 1 for i, n in zip(ids, grid)])

            @pl.when(at_first)
            def _():
                rider.first(b, d, f)

            if rider.has_middle:
                @pl.when(at_last)
                def _():
                    rider.middle(b, d, f)

        body(*pf, *a, *c, *e)

        if rider is not None:
            @pl.when(at_last)
            def _():
                rider.last(b, d, f)

    extra_shapes = rider.out_shapes if rider is not None else []
    extra_scratch = rider.scratch if rider is not None else []
    extra_inputs = rider.inputs if rider is not None else []
    all_in, all_out = list(in_specs) + [ANY] * ri, list(out_specs) + [ANY] * ro
    all_scratch = list(scratch_shapes) + extra_scratch
    if npf:
        outs = pl.pallas_call(
            wrapped, name=name, out_shape=list(out_shape) + extra_shapes,
            grid_spec=pltpu.PrefetchScalarGridSpec(num_scalar_prefetch=npf, grid=grid, in_specs=all_in, out_specs=all_out,
                                                   scratch_shapes=all_scratch),
            compiler_params=params)(*prefetch, *args, *extra_inputs)
    else:
        outs = pl.pallas_call(
            wrapped, name=name, grid=grid, in_specs=all_in, out_specs=all_out, out_shape=list(out_shape) + extra_shapes,
            scratch_shapes=all_scratch, compiler_params=params)(*args, *extra_inputs)
    return list(outs[:no]), list(outs[no:])


def _mod_fwd(c_all, w_mod):
    ncol = w_mod.shape[-1]

    def body(c_ref, w_ref, act_ref, out_ref):
        c = c_ref[...]
        act = c * jax.nn.sigmoid(c)
        act_ref[...] = act
        out_ref[0] = _dot_nn(act.astype(BF16), w_ref[0].astype(BF16))

    return pl.pallas_call(
        body, name="mod_fwd", grid=(2,),
        out_shape=[jax.ShapeDtypeStruct((N_DEV, D), F32), jax.ShapeDtypeStruct((2, N_DEV, ncol), F32)],
        in_specs=[pl.BlockSpec((N_DEV, D), lambda l: (0, 0)), pl.BlockSpec((1, D, ncol), lambda l: (l, 0, 0))],
        out_specs=[pl.BlockSpec((N_DEV, D), lambda l: (0, 0)), pl.BlockSpec((1, N_DEV, ncol), lambda l: (l, 0, 0))],
        compiler_params=ARB1,
    )(c_all, w_mod)


def _ffn_fwd(x, p, win_all, wout_all, f, rider=None):
    s = x.shape[0]
    nt = s // TM

    def body(x_ref, p_ref, win_hbm, wout_hbm, xo_ref, h_ref, g_ref, u_ref, y_ref, win, wout, act, sems):
        @pl.when(pl.program_id(0) == 0)
        def _():
            for cp in _load_rows(win_hbm, None, win, sems, 0) + _load_rows(wout_hbm, None, wout, sems, N_DEV):
                cp.start()
            for cp in _load_rows(win_hbm, None, win, sems, 0):
                cp.wait()

        sh, sc, gate, gn = _mod_rows(p_ref)
        x = x_ref[...]
        hb = _modulate(x, gn, sh, sc).astype(BF16)
        h_ref[...] = hb
        for c in range(NCH):
            g = _dot_nt(hb, win[c * HC:(c + 1) * HC, :])
            u = _dot_nt(hb, win[DFF + c * HC:DFF + (c + 1) * HC, :])
            g_ref[c] = g.astype(BF16)
            u_ref[c] = u.astype(BF16)
            act[:, c * HC:(c + 1) * HC] = ((g * jax.nn.sigmoid(g)) * u).astype(BF16)

        @pl.when(pl.program_id(0) == 0)
        def _():
            for cp in _load_rows(wout_hbm, None, wout, sems, N_DEV):
                cp.wait()

        y = _dot_nn(act[...], wout[...])
        y_ref[...] = y
        xo_ref[...] = x + (0.5 * gate) * y

    tile = pl.BlockSpec((TM, D), lambda i: (i, 0))
    chunks = pl.BlockSpec((NCH, TM, HC), lambda i: (0, i, 0))
    return _run(
        body, name=f"ffn_fwd_{f}", grid=(nt,),
        out_shape=[jax.ShapeDtypeStruct((s, D), F32), jax.ShapeDtypeStruct((s, D), BF16),
                   jax.ShapeDtypeStruct((NCH, s, HC), BF16), jax.ShapeDtypeStruct((NCH, s, HC), BF16),
                   jax.ShapeDtypeStruct((s, D), F32)],
        in_specs=[tile, pl.BlockSpec((8, D), lambda i: (0, 0)), ANY, ANY],
        out_specs=[tile, tile, chunks, chunks, tile],
        scratch_shapes=[pltpu.VMEM((2 * DFF, D), BF16), pltpu.VMEM((DFF, D), BF16), pltpu.VMEM((TM, DFF), BF16),
                        pltpu.SemaphoreType.DMA((2 * N_DEV,))],
        args=(x, p, win_all, wout_all), rider=rider)


def _swiglu_bwd_acts(dyb, g_ref, u_ref, wout, a_ref, dgu_ref, dgu):
    for c in range(NCH):
        da = _dot_nt(dyb, wout[c * HC:(c + 1) * HC, :])
        g = g_ref[c].astype(F32)
        u = u_ref[c].astype(F32)
        sg = jax.nn.sigmoid(g)
        si = g * sg
        dg = ((da * u) * (sg * (1.0 + g * (1.0 - sg)))).astype(BF16)
        du = (da * si).astype(BF16)
        a_ref[c] = (si * u).astype(BF16)
        dgu_ref[c] = dg
        dgu_ref[NCH + c] = du
        if dgu is not None:
            dgu[:, c * HC:(c + 1) * HC] = dg
            dgu[:, DFF + c * HC:DFF + (c + 1) * HC] = du


def _ffn_bwd(dxo, x, p, g3, u3, y, win_all, wout_all, f, rider=None):
    s = x.shape[0]
    nt = s // TM

    def body(dxo_ref, x_ref, p_ref, g_ref, u_ref, y_ref, win_hbm, wout_hbm,
             dx_ref, dgu_ref, a_ref, dy_ref, st_ref, win, wout, dgu, stats, sems):
        i = pl.program_id(0)

        @pl.when(i == 0)
        def _():
            for cp in _load_rows(wout_hbm, None, wout, sems, N_DEV) + _load_rows(win_hbm, None, win, sems, 0):
                cp.start()
            stats[...] = jnp.zeros_like(stats)
            for cp in _load_rows(wout_hbm, None, wout, sems, N_DEV):
                cp.wait()

        sh, sc, gate, gn = _mod_rows(p_ref)
        x = x_ref[...]
        dxo = dxo_ref[...]
        dyb = ((0.5 * gate) * dxo).astype(BF16)
        dy_ref[...] = dyb
        stats[2] += _colsum8((0.5 * dxo) * y_ref[...])
        _swiglu_bwd_acts(dyb, g_ref, u_ref, wout, a_ref, dgu_ref, dgu)

        @pl.when(i == 0)
        def _():
            for cp in _load_rows(win_hbm, None, win, sems, 0):
                cp.wait()

        dh = _dot_nn(dgu[...], win[...])
        dx_ref[...] = dxo + _modulate_bwd(dh, x, gn, sc, stats)

        @pl.when(i == nt - 1)
        def _():
            _stats_out(stats, st_ref)

    tile = pl.BlockSpec((TM, D), lambda i: (i, 0))
    chunks = pl.BlockSpec((NCH, TM, HC), lambda i: (0, i, 0))
    small = pl.BlockSpec((8, D), lambda i: (0, 0))
    return _run(
        body, name=f"ffn_bwd_{f}", grid=(nt,),
        out_shape=[jax.ShapeDtypeStruct((s, D), F32), jax.ShapeDtypeStruct((2 * NCH, s, HC), BF16),
                   jax.ShapeDtypeStruct((NCH, s, HC), BF16), jax.ShapeDtypeStruct((s, D), BF16),
                   jax.ShapeDtypeStruct((8, D), F32)],
        in_specs=[tile, tile, small, chunks, chunks, tile, ANY, ANY],
        out_specs=[tile, pl.BlockSpec((2 * NCH, TM, HC), lambda i: (0, i, 0)), chunks, tile, small],
        scratch_shapes=[pltpu.VMEM((2 * DFF, D), BF16), pltpu.VMEM((DFF, D), BF16), pltpu.VMEM((TM, 2 * DFF), BF16),
                        pltpu.VMEM((4, 8, D), F32), pltpu.SemaphoreType.DMA((2 * N_DEV,))],
        args=(dxo, x, p, g3, u3, y, win_all, wout_all), rider=rider)


def _ffn_bwd_acts(dxo, p, g3, u3, y, wout_all, f, rider=None):
    s = dxo.shape[0]
    nt = s // TM

    def body(dxo_ref, p_ref, g_ref, u_ref, y_ref, wout_hbm, dgu_ref, a_ref, dy_ref, st_ref, wout, stat, sems):
        i = pl.program_id(0)

        @pl.when(i == 0)
        def _():
            cps = _load_rows(wout_hbm, None, wout, sems, 0)
            for cp in cps:
                cp.start()
            stat[...] = jnp.zeros_like(stat)
            for cp in cps:
                cp.wait()

        gate = p_ref[2:3, :]
        dxo = dxo_ref[...]
        dyb = ((0.5 * gate) * dxo).astype(BF16)
        dy_ref[...] = dyb
        stat[...] += _colsum8((0.5 * dxo) * y_ref[...])
        _swiglu_bwd_acts(dyb, g_ref, u_ref, wout, a_ref, dgu_ref, None)

        @pl.when(i == nt - 1)
        def _():
            row = jnp.sum(stat[...], axis=0, keepdims=True)
            st_ref[...] = jnp.concatenate([jnp.zeros((2, D), F32), row, jnp.zeros((5, D), F32)], axis=0)

    tile = pl.BlockSpec((TM, D), lambda i: (i, 0))
    chunks = pl.BlockSpec((NCH, TM, HC), lambda i: (0, i, 0))
    small = pl.BlockSpec((8, D), lambda i: (0, 0))
    return _run(
        body, name=f"ffn_bwd_acts_{f}", grid=(nt,),
        out_shape=[jax.ShapeDtypeStruct((2 * NCH, s, HC), BF16), jax.ShapeDtypeStruct((NCH, s, HC), BF16),
                   jax.ShapeDtypeStruct((s, D), BF16), jax.ShapeDtypeStruct((8, D), F32)],
        in_specs=[tile, small, chunks, chunks, tile, ANY],
        out_specs=[pl.BlockSpec((2 * NCH, TM, HC), lambda i: (0, i, 0)), chunks, tile, small],
        scratch_shapes=[pltpu.VMEM((DFF, D), BF16), pltpu.VMEM((8, D), F32), pltpu.SemaphoreType.DMA((N_DEV,))],
        args=(dxo, p, g3, u3, y, wout_all), rider=rider)


def _ffn_bwd_dx(dxo, x, p, dgu3, gate_stats, win_all, f, rider=None):
    s = x.shape[0]
    nt = s // TM

    def body(dxo_ref, x_ref, p_ref, dgu_ref, gs_ref, win_hbm, dx_ref, st_ref, win, dgu, stats, sems):
        i = pl.program_id(0)

        @pl.when(i == 0)
        def _():
            cps = _load_rows(win_hbm, None, win, sems, 0)
            for cp in cps:
                cp.start()
            stats[...] = jnp.zeros_like(stats)
            for cp in cps:
                cp.wait()

        _, sc, _, gn = _mod_rows(p_ref)
        for c in range(2 * NCH):
            dgu[:, c * HC:(c + 1) * HC] = dgu_ref[c]
        dh = _dot_nn(dgu[...], win[...])
        dx_ref[...] = dxo_ref[...] + _modulate_bwd(dh, x_ref[...], gn, sc, stats)

        @pl.when(i == nt - 1)
        def _():
            _stats_out(stats, st_ref)
            st_ref[2:3, :] = gs_ref[2:3, :]

    tile = pl.BlockSpec((TM, D), lambda i: (i, 0))
    small = pl.BlockSpec((8, D), lambda i: (0, 0))
    return _run(
        body, name=f"ffn_bwd_dx_{f}", grid=(nt,),
        out_shape=[jax.ShapeDtypeStruct((s, D), F32), jax.ShapeDtypeStruct((8, D), F32)],
        in_specs=[tile, tile, small, pl.BlockSpec((2 * NCH, TM, HC), lambda i: (0, i, 0)), small, ANY],
        out_specs=[tile, small],
        scratch_shapes=[pltpu.VMEM((2 * DFF, D), BF16), pltpu.VMEM((TM, 2 * DFF), BF16), pltpu.VMEM((4, 8, D), F32),
                        pltpu.SemaphoreType.DMA((N_DEV,))],
        args=(dxo, x, p, dgu3, gate_stats, win_all), rider=rider)


def _wgrad_pair(own3, sib3, own_r, sib_r, sel, n, col_split, name, rider=None):
    nj, s, _ = own3.shape
    nw = own_r.shape[1] // 2 if col_split else own_r.shape[1]
    steps = nj if col_split else n

    def body(sel_ref, lo_ref, ls_ref, ro_ref, rs_ref, o_ref):
        o_ref[...] = (_dot_tn(lo_ref[0], ro_ref[...]) + _dot_tn(ls_ref[0], rs_ref[...])).astype(BF16)

    rspec = pl.BlockSpec((s, nw), lambda j, sel_ref: (0, sel_ref[1]))
    outs, rode = _run(
        body, name=name, grid=(steps,),
        out_shape=[jax.ShapeDtypeStruct((steps * HC, nw), BF16)],
        in_specs=[pl.BlockSpec((1, s, HC), lambda j, sel_ref: (sel_ref[0] + j, 0, 0)),
                  pl.BlockSpec((1, s, HC), lambda j, sel_ref: (j, 0, 0)), rspec, rspec],
        out_specs=[pl.BlockSpec((HC, nw), lambda j, sel_ref: (j, 0))],
        scratch_shapes=[], args=(own3, sib3, own_r, sib_r), rider=rider, prefetch=(sel,))
    return outs[0], rode


def _gating(proj, nv, ws_ref, bst):
    u, v = proj[:, 0:D_A], proj[:, D_A:2 * D_A]
    gu, gv = _gelu(u), _gelu(v)
    mu = jnp.mean(gv, axis=-1, keepdims=True)
    dv = gv - mu
    rstd = lax.rsqrt(jnp.mean(dv * dv, axis=-1, keepdims=True) + EPS)
    vhat = dv * rstd
    vn = vhat * nv
    r = lax.broadcasted_iota(jnp.int32, (CHUNK, CHUNK), 0)
    c = lax.broadcasted_iota(jnp.int32, (CHUNK, CHUNK), 1)
    wm = [jnp.where(r >= c, ws_ref[hd], 0.0).astype(BF16) for hd in range(A_HEADS)]
    vnb = vn.astype(BF16)
    rows = []
    for n in range(proj.shape[0] // CHUNK):
        blocks = []
        for hd in range(A_HEADS):
            blk = vnb[n * CHUNK:(n + 1) * CHUNK, hd * CHUNK:(hd + 1) * CHUNK]
            blocks.append(_dot_nn(wm[hd], blk) + bst[:, hd:hd + 1])
        rows.append(jnp.concatenate(blocks, axis=1))
    z = jnp.concatenate(rows, axis=0)
    return u, v, gu, rstd, vhat, vnb, wm, z


def _conv(proj, cw, prev_xp):
    bg = proj[:, 2 * D_A:2 * D_A + D_B]
    cg = proj[:, 2 * D_A + D_B:2 * D_A + 2 * D_B]
    xb = proj[:, 2 * D_A + 2 * D_B:]
    xp = cg * xb
    x1 = _shift_down(xp, 1, prev_xp)
    x2 = _shift_down(xp, 2, prev_xp)
    conv = cw[0:1, :] * x2 + cw[1:2, :] * x1 + cw[2:3, :] * xp
    return bg, cg, xb, xp, x1, x2, conv


def _ab_fwd(x, p, abin_all, about_all, nv, ws, bst, cw, rider=None):
    s = x.shape[0]
    nt = s // TM

    def body(x_ref, p_ref, abin_hbm, about_hbm, nv_ref, ws_ref, bst_ref, cw_ref,
             xo_ref, h_ref, proj_ref, out_ref, abin, about, prev, sems):
        @pl.when(pl.program_id(0) == 0)
        def _():
            cps = _load_rows(abin_hbm, None, abin, sems, 0) + _load_rows(about_hbm, None, about, sems, N_DEV)
            for cp in cps:
                cp.start()
            prev[...] = jnp.zeros_like(prev)
            for cp in cps:
                cp.wait()

        sh, sc, gate, gn = _mod_rows(p_ref)
        x = x_ref[...]
        hb = _modulate(x, gn, sh, sc).astype(BF16)
        h_ref[...] = hb
        proj = _dot_nt(hb, abin[...])
        proj_ref[...] = proj
        _, _, gu, _, _, _, _, z = _gating(proj, nv_ref[...], ws_ref, bst_ref[...])
        bg, _, _, xp, _, _, conv = _conv(proj, cw_ref[...], prev[...])
        prev[...] = xp[TM - CONV_HALO:, :]
        cat = jnp.concatenate([gu * z, bg * conv], axis=1).astype(BF16)
        out = _dot_nn(cat, about[...])
        out_ref[...] = out
        xo_ref[...] = x + gate * out

    tile = pl.BlockSpec((TM, D), lambda i: (i, 0))
    full = lambda a: pl.BlockSpec(a.shape, lambda i: (0,) * a.ndim)
    return _run(
        body, name="ab_fwd", grid=(nt,),
        out_shape=[jax.ShapeDtypeStruct((s, D), F32), jax.ShapeDtypeStruct((s, D), BF16),
                   jax.ShapeDtypeStruct((s, D_AB), F32), jax.ShapeDtypeStruct((s, D), F32)],
        in_specs=[tile, pl.BlockSpec((8, D), lambda i: (0, 0)), ANY, ANY, full(nv), full(ws), full(bst), full(cw)],
        out_specs=[tile, tile, pl.BlockSpec((TM, D_AB), lambda i: (i, 0)), tile],
        scratch_shapes=[pltpu.VMEM((D_AB, D), BF16), pltpu.VMEM((D, D), BF16), pltpu.VMEM((CONV_HALO, D_B), F32),
                        pltpu.SemaphoreType.DMA((2 * N_DEV,))],
        args=(x, p, abin_all, about_all, nv, ws, bst, cw), rider=rider)


def _ab_bwd(dxo, x, p, proj, out, abin_all, about_all, nv, ws, bst, cw, rider=None):
    s = x.shape[0]
    nt = s // TM
    npj = D_AB // HC

    def body(dxo_ref, x_ref, p_ref, proj_ref, halo_ref, out_ref, abin_hbm, about_hbm, nv_ref, ws_ref, bst_ref, cw_ref,
             dx_ref, dproj_ref, cat_ref, dy_ref, st_ref, dnv_ref, dws_ref, dbs_ref, dcw_ref,
             abin, about, nxt, stats, dnv, dws, dbs, dcw, sems):
        i = pl.program_id(0)
        ti = nt - 1 - i

        @pl.when(i == 0)
        def _():
            cps = _load_rows(abin_hbm, None, abin, sems, 0) + _load_rows(about_hbm, None, about, sems, N_DEV)
            for cp in cps:
                cp.start()
            for z in (nxt, stats, dnv, dws, dbs, dcw):
                z[...] = jnp.zeros_like(z)
            for cp in cps:
                cp.wait()

        sh, sc, gate, gn = _mod_rows(p_ref)
        x = x_ref[...]
        dxo = dxo_ref[...]
        dyb = (gate * dxo).astype(BF16)
        dy_ref[...] = dyb
        stats[2] += _colsum8(dxo * out_ref[...])
        dcat = _dot_nt(dyb, about[...])
        dya, dyb2 = dcat[:, 0:D_A], dcat[:, D_A:]

        proj = proj_ref[...]
        nvv = nv_ref[...]
        u, v, gu, rstd, vhat, vnb, wm, z = _gating(proj, nvv, ws_ref, bst_ref[...])
        dgu = dya * z
        dzb = (dya * gu).astype(BF16)
        dz32 = dya * gu
        rows = []
        for n in range(TM // CHUNK):
            blocks = []
            for hd in range(A_HEADS):
                sl = (slice(n * CHUNK, (n + 1) * CHUNK), slice(hd * CHUNK, (hd + 1) * CHUNK))
                dbs[hd] += dz32[sl]
                dws[hd] += _dot_nt(dzb[sl], vnb[sl])
                blocks.append(_dot_tn(wm[hd], dzb[sl]))
            rows.append(jnp.concatenate(blocks, axis=1))
        dvn = jnp.concatenate(rows, axis=0)
        dnv[...] += _colsum8(dvn * vhat)
        dvh = dvn * nvv
        dgv = rstd * (dvh - jnp.mean(dvh, axis=-1, keepdims=True) - vhat * jnp.mean(dvh * vhat, axis=-1, keepdims=True))
        du = dgu * _gelu_grad(u)
        dv = dgv * _gelu_grad(v)

        halo = halo_ref[...]
        prev_xp = jnp.where(ti > 0, halo[:, 2 * D_A + D_B:2 * D_A + 2 * D_B] * halo[:, 2 * D_A + 2 * D_B:], 0.0)
        cwv = cw_ref[...]
        bg, cg, xb, xp, x1, x2, conv = _conv(proj, cwv, prev_xp)
        dbg = dyb2 * conv
        dconv = dyb2 * bg
        dcw[...] += jnp.concatenate(
            [jnp.sum(_colsum8(dconv * t), axis=0, keepdims=True) for t in (x2, x1, xp)] + [jnp.zeros((5, D_B), F32)], axis=0)
        nx = nxt[...]
        dxp = cwv[2:3, :] * dconv + cwv[1:2, :] * _shift_up(dconv, 1, nx) + cwv[0:1, :] * _shift_up(dconv, 2, nx)
        nxt[...] = dconv[0:CONV_HALO, :]
        dcg = dxp * xb
        dxb = dxp * cg

        dproj = jnp.concatenate([du, dv, dbg, dcg, dxb], axis=1).astype(BF16)
        for k in range(npj):
            dproj_ref[k] = dproj[:, k * HC:(k + 1) * HC]
        cat = jnp.concatenate([gu * z, bg * conv], axis=1).astype(BF16)
        for k in range(D // HC):
            cat_ref[k] = cat[:, k * HC:(k + 1) * HC]
        dh = _dot_nn(dproj, abin[...])
        dx_ref[...] = dxo + _modulate_bwd(dh, x, gn, sc, stats)

        @pl.when(i == nt - 1)
        def _():
            _stats_out(stats, st_ref)
            dnv_ref[...] = jnp.concatenate([jnp.sum(dnv[...], axis=0, keepdims=True), jnp.zeros((7, D_A), F32)], axis=0)
            r = lax.broadcasted_iota(jnp.int32, (CHUNK, CHUNK), 0)
            c = lax.broadcasted_iota(jnp.int32, (CHUNK, CHUNK), 1)
            for hd in range(A_HEADS):
                dws_ref[hd] = jnp.where(r >= c, dws[hd], 0.0)
                dbs_ref[hd] = jnp.broadcast_to(jnp.sum(dbs[hd], axis=1, keepdims=True), (CHUNK, CHUNK))
            dcw_ref[...] = dcw[...]

    rev = pl.BlockSpec((TM, D), lambda i: (nt - 1 - i, 0))
    small = pl.BlockSpec((8, D), lambda i: (0, 0))
    full = lambda a: pl.BlockSpec(a.shape, lambda i: (0,) * a.ndim)
    hpt = TM // CONV_HALO
    fixed = lambda shape: pl.BlockSpec(shape, lambda i: (0,) * len(shape))
    return _run(
        body, name="ab_bwd", grid=(nt,),
        out_shape=[jax.ShapeDtypeStruct((s, D), F32), jax.ShapeDtypeStruct((npj, s, HC), BF16),
                   jax.ShapeDtypeStruct((D // HC, s, HC), BF16), jax.ShapeDtypeStruct((s, D), BF16),
                   jax.ShapeDtypeStruct((8, D), F32), jax.ShapeDtypeStruct((8, D_A), F32),
                   jax.ShapeDtypeStruct((A_HEADS, CHUNK, CHUNK), F32), jax.ShapeDtypeStruct((A_HEADS, CHUNK, CHUNK), F32),
                   jax.ShapeDtypeStruct((8, D_B), F32)],
        in_specs=[rev, rev, small,
                  pl.BlockSpec((TM, D_AB), lambda i: (nt - 1 - i, 0)),
                  pl.BlockSpec((CONV_HALO, D_AB), lambda i: (jnp.maximum((nt - 1 - i) * hpt - 1, 0), 0)),
                  rev, ANY, ANY, full(nv), full(ws), full(bst), full(cw)],
        out_specs=[rev, pl.BlockSpec((npj, TM, HC), lambda i: (0, nt - 1 - i, 0)),
                   pl.BlockSpec((D // HC, TM, HC), lambda i: (0, nt - 1 - i, 0)), rev,
                   small, fixed((8, D_A)), fixed((A_HEADS, CHUNK, CHUNK)), fixed((A_HEADS, CHUNK, CHUNK)), fixed((8, D_B))],
        scratch_shapes=[pltpu.VMEM((D_AB, D), BF16), pltpu.VMEM((D, D), BF16), pltpu.VMEM((CONV_HALO, D_B), F32),
                        pltpu.VMEM((4, 8, D), F32), pltpu.VMEM((8, D_A), F32),
                        pltpu.VMEM((A_HEADS, CHUNK, CHUNK), F32), pltpu.VMEM((A_HEADS, CHUNK, CHUNK), F32),
                        pltpu.VMEM((8, D_B), F32), pltpu.SemaphoreType.DMA((2 * N_DEV,))],
        args=(dxo, x, p, proj, proj, out, abin_all, about_all, nv, ws, bst, cw), rider=rider)


def _pool_counts(first_token, rows):
    t = (first_token + lax.broadcasted_iota(jnp.int32, (rows, 1), 0) + 1).astype(F32)
    lane = lax.broadcasted_iota(jnp.int32, (1, D), 1)
    w = jnp.where(lane < POOL_G, 2.0, jnp.where(lane < 2 * POOL_G, 4.0, jnp.where(lane < 3 * POOL_G, 8.0, 16.0)))
    return jnp.minimum(t, w)


def _window_sums(ext, n_keep, lead, back):
    n = ext.shape[0]
    sh = (lambda v, k: pltpu.roll(v, k, 0)) if back else (lambda v, k: pltpu.roll(v, n - k, 0))
    s2 = ext + sh(ext, 1)
    s4 = s2[:, POOL_G:] + sh(s2[:, POOL_G:], 2)
    s8 = s4[:, POOL_G:] + sh(s4[:, POOL_G:], 4)
    s16 = s8[:, POOL_G:] + sh(s8[:, POOL_G:], 8)
    keep = slice(lead, lead + n_keep)
    return jnp.concatenate([s2[keep, 0:POOL_G], s4[keep, 0:POOL_G], s8[keep, 0:POOL_G], s16[keep, :]], axis=1)


def _pool_fwd(x, p, pool_all, pscale):
    s = x.shape[0]
    nt = s // TM
    ng = D // POOL_G

    def body(x_ref, p_ref, wg_ref, ps_ref, xo_ref, pb_ref, op_ref, prev):
        i = pl.program_id(0)

        @pl.when(i == 0)
        def _():
            prev[...] = jnp.zeros_like(prev)

        sh, sc, gate, gn = _mod_rows(p_ref)
        x = x_ref[...]
        h = _modulate(x, gn, sh, sc)
        win = _window_sums(jnp.concatenate([prev[...], h], axis=0), TM, POOL_HALO, True)
        prev[...] = h[TM - POOL_HALO:, :]
        pb = (win / _pool_counts(i * TM, TM) - h).astype(BF16)
        pb_ref[...] = pb
        op = jnp.concatenate(
            [_dot_nn(pb[:, g * POOL_G:(g + 1) * POOL_G], wg_ref[:, g].reshape(POOL_G, POOL_G)) for g in range(ng)], axis=1)
        op_ref[...] = op
        xo_ref[...] = x + gate * (op * ps_ref[...])

    tile = pl.BlockSpec((TM, D), lambda i: (i, 0))
    return pl.pallas_call(
        body, name="pool_fwd", grid=(nt,),
        out_shape=[jax.ShapeDtypeStruct((s, D), F32), jax.ShapeDtypeStruct((s, D), BF16), jax.ShapeDtypeStruct((s, D), F32)],
        in_specs=[tile, pl.BlockSpec((8, D), lambda i: (0, 0)),
                  pl.BlockSpec(pool_all.shape, lambda i: (0, 0, 0, 0)), pl.BlockSpec((1, D), lambda i: (0, 0))],
        out_specs=[tile, tile, tile],
        scratch_shapes=[pltpu.VMEM((POOL_HALO, D), F32)],
        compiler_params=ARB1,
    )(x, p, pool_all, pscale)


def _pool_bwd(dxo, x, p, pb, op, pool_all, pscale):
    s = x.shape[0]
    nt = s // TM
    ng = D // POOL_G

    def body(dxo_ref, x_ref, p_ref, pb_ref, op_ref, wg_ref, ps_ref,
             dx_ref, st_ref, dps_ref, dwg_ref, nxt, stats, dps, dwg):
        i = pl.program_id(0)
        ti = nt - 1 - i

        @pl.when(i == 0)
        def _():
            for z in (nxt, stats, dps, dwg):
                z[...] = jnp.zeros_like(z)

        sh, sc, gate, gn = _mod_rows(p_ref)
        x = x_ref[...]
        dxo = dxo_ref[...]
        ps = ps_ref[...]
        op = op_ref[...]
        dmo = gate * dxo
        stats[2] += _colsum8(dxo * (op * ps))
        dps[...] += _colsum8(dmo * op)
        dopb = (dmo * ps).astype(BF16)
        pbv = pb_ref[...]
        dps_parts = []
        for g in range(ng):
            sl = slice(g * POOL_G, (g + 1) * POOL_G)
            dps_parts.append(_dot_nt(dopb[:, sl], wg_ref[:, g].reshape(POOL_G, POOL_G)))
            dwg[g] += _dot_tn(pbv[:, sl], dopb[:, sl])
        dp = jnp.concatenate(dps_parts, axis=1)
        q = dp / _pool_counts(ti * TM, TM)
        wsum = _window_sums(jnp.concatenate([q, nxt[...]], axis=0), TM, 0, False)
        nxt[...] = q[0:POOL_HALO, :]
        dx_ref[...] = dxo + _modulate_bwd(wsum - dp, x, gn, sc, stats)

        @pl.when(i == nt - 1)
        def _():
            _stats_out(stats, st_ref)
            dps_ref[...] = jnp.concatenate([jnp.sum(dps[...], axis=0, keepdims=True), jnp.zeros((7, D), F32)], axis=0)
            dwg_ref[...] = dwg[...].astype(BF16)

    rev = pl.BlockSpec((TM, D), lambda i: (nt - 1 - i, 0))
    small = pl.BlockSpec((8, D), lambda i: (0, 0))
    return pl.pallas_call(
        body, name="pool_bwd", grid=(nt,),
        out_shape=[jax.ShapeDtypeStruct((s, D), F32), jax.ShapeDtypeStruct((8, D), F32), jax.ShapeDtypeStruct((8, D), F32),
                   jax.ShapeDtypeStruct((ng, POOL_G, POOL_G), BF16)],
        in_specs=[rev, rev, small, rev, rev,
                  pl.BlockSpec(pool_all.shape, lambda i: (0, 0, 0, 0)), pl.BlockSpec((1, D), lambda i: (0, 0))],
        out_specs=[rev, small, small, pl.BlockSpec((ng, POOL_G, POOL_G), lambda i: (0, 0, 0))],
        scratch_shapes=[pltpu.VMEM((POOL_HALO, D), F32), pltpu.VMEM((4, 8, D), F32), pltpu.VMEM((8, D), F32),
                        pltpu.VMEM((ng, POOL_G, POOL_G), F32)],
        compiler_params=ARB1,
    )(dxo, x, p, pb, op, pool_all, pscale)


def _head(x, fg, tgt):
    s = x.shape[0]
    nt = s // TM

    def body(x_ref, fg_ref, t_ref, dx_ref, loss_ref, dfg_ref, sq, dfg):
        i = pl.program_id(0)

        @pl.when(i == 0)
        def _():
            sq[...] = jnp.zeros_like(sq)
            dfg[...] = jnp.zeros_like(dfg)

        x = x_ref[...]
        g = fg_ref[...]
        r = lax.rsqrt(jnp.mean(x * x, axis=-1, keepdims=True) + EPS)
        xn = x * r
        e = xn * g - t_ref[...]
        sq[...] += _colsum8(e * e)
        dy = e * (1.0 / D)
        dfg[...] += _colsum8(dy * xn)
        dxn = dy * g
        dx_ref[...] = r * (dxn - xn * jnp.mean(dxn * xn, axis=-1, keepdims=True))

        @pl.when(i == nt - 1)
        def _():
            total = jnp.sum(jnp.sum(sq[...], axis=0, keepdims=True), axis=1, keepdims=True)
            loss_ref[...] = jnp.broadcast_to(total * (0.5 / D), loss_ref.shape)
            dfg_ref[...] = jnp.concatenate([jnp.sum(dfg[...], axis=0, keepdims=True), jnp.zeros((7, D), F32)], axis=0)

    tile = pl.BlockSpec((TM, D), lambda i: (i, 0))
    return pl.pallas_call(
        body, name="head", grid=(nt,),
        out_shape=[jax.ShapeDtypeStruct((s, D), F32), jax.ShapeDtypeStruct((8, 128), F32), jax.ShapeDtypeStruct((8, D), F32)],
        in_specs=[tile, pl.BlockSpec((1, D), lambda i: (0, 0)), tile],
        out_specs=[tile, pl.BlockSpec((8, 128), lambda i: (0, 0)), pl.BlockSpec((8, D), lambda i: (0, 0))],
        scratch_shapes=[pltpu.VMEM((8, D), F32), pltpu.VMEM((8, D), F32)],
        compiler_params=ARB1,
    )(x, fg, tgt)


def _adamw_math(w, g, m, v):
    m = ADAM_B1 * m + (1.0 - ADAM_B1) * g
    v = ADAM_B2 * v + (1.0 - ADAM_B2) * (g * g)
    m_hat = m / (1.0 - ADAM_B1 ** ADAM_STEP)
    v_hat = v / (1.0 - ADAM_B2 ** ADAM_STEP)
    delta = -ADAM_LR * (m_hat / (jnp.sqrt(v_hat) + ADAM_EPS) + ADAM_WD * w)
    return delta, m, v


def _finish(parts, w, m, v, rb, name, halves=False, rider=None):
    nf, r, c = w.shape
    npart = parts[0].shape[0]

    def body(*refs):
        p_refs = refs[:nf]
        w_ref, m_ref, v_ref, g_ref, d_ref, mo_ref, vo_ref = refs[nf:]
        for f in range(nf):
            @pl.when(pl.program_id(0) == f)
            def _():
                g = p_refs[f][0].astype(F32)
                for k in range(1, npart):
                    g = g + p_refs[f][k].astype(F32)
                if halves:
                    g = jnp.concatenate([g[0], g[1]], axis=1)
                g_ref[0] = g
                d_ref[0], mo_ref[0], vo_ref[0] = _adamw_math(w_ref[0], g, m_ref[0], v_ref[0])

    blk = pl.BlockSpec((1, rb, c), lambda f, i: (f, i, 0))

    def pblk(mine):
        if halves:
            return pl.BlockSpec((npart, 2, rb, c // 2), lambda f, i: (0, 0, jnp.where(f == mine, i, 0), 0))
        return pl.BlockSpec((npart, rb, c), lambda f, i: (0, jnp.where(f == mine, i, 0), 0))

    return _run(
        body, name=name, grid=(nf, r // rb),
        out_shape=[jax.ShapeDtypeStruct(w.shape, F32)] * 4,
        in_specs=[pblk(f) for f in range(nf)] + [blk, blk, blk], out_specs=[blk] * 4, scratch_shapes=[],
        args=(*parts, w, m, v), rider=rider,
        params=pltpu.CompilerParams(dimension_semantics=("arbitrary", "arbitrary"), vmem_limit_bytes=VMEM_LIMIT))


def _sum_small(parts, late, late_rows, narrow):
    def body(p_ref, l_ref, n_ref, o_ref, on_ref):
        acc, acc_l, acc_n = p_ref[0], l_ref[0], n_ref[0].astype(F32)
        for k in range(1, N_DEV):
            acc, acc_l, acc_n = acc + p_ref[k], acc_l + l_ref[k], acc_n + n_ref[k].astype(F32)
        o_ref[...] = acc
        for a, b, r in late_rows:
            o_ref[r:r + b - a, :] += acc_l[a:b, :]
        on_ref[...] = acc_n

    return pl.pallas_call(
        body, name="sum_small",
        out_shape=[jax.ShapeDtypeStruct(parts.shape[1:], F32), jax.ShapeDtypeStruct(narrow.shape[1:], F32)],
        in_specs=[VMEM_SPEC] * 3, out_specs=[VMEM_SPEC] * 2)(parts, late, narrow)


def _adamw(w, g, m, v, name):
    def body(w_ref, g_ref, m_ref, v_ref, d_ref, mo_ref, vo_ref):
        d_ref[...], mo_ref[...], vo_ref[...] = _adamw_math(w_ref[...], g_ref[...], m_ref[...], v_ref[...])

    return pl.pallas_call(
        body, name=name, out_shape=[jax.ShapeDtypeStruct(w.shape, F32)] * 3,
        in_specs=[VMEM_SPEC] * 4, out_specs=[VMEM_SPEC] * 3,
    )(w, g, m, v)


def _wmod_finish(act_t, dmod_cols, w, m, v):
    rb = 256
    ncol = w.shape[-1]

    def body(a_ref, dm_ref, w_ref, m_ref, v_ref, g_ref, d_ref, mo_ref, vo_ref):
        g = a_ref[:, 0:1] * dm_ref[0, 0:1, :]
        for k in range(1, N_DEV):
            g = g + a_ref[:, k:k + 1] * dm_ref[0, k:k + 1, :]
        g_ref[0] = g
        d_ref[0], mo_ref[0], vo_ref[0] = _adamw_math(w_ref[0], g, m_ref[0], v_ref[0])

    blk = pl.BlockSpec((1, rb, ncol), lambda l, i: (l, i, 0))
    return pl.pallas_call(
        body, name="wmod_finish", grid=(2, D // rb),
        out_shape=[jax.ShapeDtypeStruct(w.shape, F32)] * 4,
        in_specs=[pl.BlockSpec((rb, N_DEV), lambda l, i: (i, 0)), pl.BlockSpec((1, N_DEV, ncol), lambda l, i: (l, 0, 0)),
                  blk, blk, blk],
        out_specs=[blk] * 4,
        compiler_params=pltpu.CompilerParams(dimension_semantics=("arbitrary", "arbitrary"), vmem_limit_bytes=VMEM_LIMIT),
    )(act_t, dmod_cols, w, m, v)


def _pack(pieces):
    flat, offs, at = [], [], 0
    for a in pieces:
        a = a.reshape(-1)
        n = -(-a.shape[0] // 128) * 128
        flat.append(jnp.pad(a, (0, n - a.shape[0])))
        offs.append(at)
        at += n
    return jnp.concatenate(flat).reshape(-1, 128), offs


def _param_block(mod_l, sub, gn):
    return jnp.concatenate([mod_l[sub], gn[None, :], jnp.zeros((4, D), F32)], axis=0)


def kernel(x, c, norm_g, w_mod, b_mod, w_ffn_in, w_ffn_out, ab_w_in, ab_norm_v, ab_w_s, ab_b_s, ab_conv_w, ab_w_out, pool_w_grp, pool_scale, final_g, loss_target, m_norm_g, m_w_mod, m_b_mod, m_w_ffn_in, m_w_ffn_out, m_ab_w_in, m_ab_norm_v, m_ab_w_s, m_ab_b_s, m_ab_conv_w, m_ab_w_out, m_pool_w_grp, m_pool_scale, m_final_g, v_norm_g, v_w_mod, v_b_mod, v_w_ffn_in, v_w_ffn_out, v_ab_w_in, v_ab_norm_v, v_ab_w_s, v_ab_b_s, v_ab_conv_w, v_ab_w_out, v_pool_w_grp, v_pool_scale, v_final_g):
    me = 4 * lax.axis_index("x") + 2 * lax.axis_index("y") + lax.axis_index("c")
    x0 = x[0]
    tgt = loss_target[0]
    n_in = w_ffn_in.shape[-1]
    n_out = w_ffn_out.shape[-2]
    n_abin = ab_w_in.shape[-1]
    n_about = ab_w_out.shape[-2]
    n_pool = pool_w_grp.shape[-2]
    n_mod = w_mod.shape[-1]
    n_ng = norm_g.shape[-1]
    n_cw = ab_conv_w.shape[-1]
    n_ps = pool_scale.shape[-1]

    win_sh = jnp.swapaxes(w_ffn_in.reshape(4, D, n_in), 1, 2).astype(BF16)
    wout_sh = w_ffn_out.reshape(4, n_out, D).astype(BF16)
    abin_sh = ab_w_in[0].T.astype(BF16)
    about_sh = ab_w_out[0].astype(BF16)
    pool_sh = pool_w_grp[0].astype(BF16)
    win, wout = [None] * 4, [None] * 4

    pack, offs = _pack([c, norm_g, ab_conv_w, pool_scale])
    got, (win[0],) = _exchange_small(pack, "gather_small", _GatherRider([(win_sh, 0)]))
    got = got.reshape(N_DEV, -1)
    c_all = got[:, offs[0]:offs[0] + D]
    ng_full = got[:, offs[1]:offs[1] + 6 * n_ng].reshape(N_DEV, 2, 3, n_ng).transpose(1, 2, 0, 3).reshape(2, 3, D)
    cw_full = got[:, offs[2]:offs[2] + 3 * n_cw].reshape(N_DEV, 3, n_cw).transpose(1, 0, 2).reshape(3, D_B)
    ps_full = got[:, offs[3]:offs[3] + n_ps].reshape(1, D)

    act_all, mod_cols = _mod_fwd(c_all, w_mod)
    mod_got, (wout[0],) = _exchange_small(mod_cols.reshape(-1, 128), "gather_mod", _GatherRider([(wout_sh, 0)]))
    mod_got = mod_got.reshape(N_DEV, 2, N_DEV, n_mod)
    mod = lax.dynamic_index_in_dim(mod_got, me, axis=2, keepdims=False).transpose(1, 0, 2).reshape(2, 9 * D) + b_mod
    mod = mod.reshape(2, 3, 3, D)
    nv = ab_norm_v
    ws = ab_w_s[0]
    bst = ab_b_s[0].T
    cw8 = jnp.concatenate([cw_full, jnp.zeros((5, D_B), F32)], axis=0)

    pb = [[_param_block(mod[l], s, ng_full[l, s]) for s in range(3)] for l in range(2)]
    (x1, h00, g00, u00, y00), (abin_all, about_all, win[1]) = _ffn_fwd(
        x0, pb[0][0], win[0], wout[0], 0, _GatherRider([abin_sh, about_sh, (win_sh, 1)]))
    (x2, h01, proj, ab_out), (wout[1],) = _ab_fwd(
        x1, pb[0][1], abin_all, about_all, nv, ws, bst, cw8, _GatherRider([(wout_sh, 1)]))
    (x3, h02, g02, u02, y02), (win[2], wout[2]) = _ffn_fwd(
        x2, pb[0][2], win[1], wout[1], 1, _GatherRider([(win_sh, 2), (wout_sh, 2)]))
    (x4, h10, g10, u10, y10), (pool_all, win[3], wout[3]) = _ffn_fwd(
        x3, pb[1][0], win[2], wout[2], 2, _GatherRider([pool_sh, (win_sh, 3), (wout_sh, 3)]))
    x5, pooled, pool_out = _pool_fwd(x4, pb[1][1], pool_all, ps_full)
    (x6, h12, g12, u12, y12), _ = _ffn_fwd(x5, pb[1][2], win[3], wout[3], 3)
    dx6, loss_blk, dfg = _head(x6, final_g.reshape(1, D), tgt)

    core = lax.axis_index("c")
    sel_rows = lambda n: jnp.stack([core * n, 0]).astype(jnp.int32)
    sel_cols = jnp.stack([0, core]).astype(jnp.int32)
    p_in, p_out = [None] * 4, [None] * 4

    def exchange(dgu, a, h, dy):
        return _SiblingRider([(dgu, NCH), (a, None), (h, None), (dy, None)])

    def ffn_wgrads(f, dgu, a, h, dy, got, ride_out=None):
        s_dgu, s_a, s_h, s_dy = got
        g_out, rode = _wgrad_pair(a, s_a, dy, s_dy, sel_cols, NCH, True, f"wgrad_out_{f}", ride_out)
        g_in, (p_out[f],) = _wgrad_pair(dgu, s_dgu, h, s_h, sel_rows(NCH), NCH, False, f"wgrad_in_{f}",
                                        _ChipScatterRider([], [g_out]))
        return g_in, rode

    (dx5, dgu12, a12, dy12, st12), _ = _ffn_bwd(dx6, x5, pb[1][2], g12, u12, y12, win[3], wout[3], 3)
    dx4, st11, dps, gw_pool = _pool_bwd(dx5, x4, pb[1][1], pooled, pool_out, pool_all, ps_full)
    (dx3, dgu10, a10, dy10, st10), got3 = _ffn_bwd(
        dx4, x3, pb[1][0], g10, u10, y10, win[2], wout[2], 2, exchange(dgu12, a12, h12, dy12))
    g_in3, (p_pool,) = ffn_wgrads(3, dgu12, a12, h12, dy12, got3, _ScatterRider([gw_pool]))
    ride = _Riders([exchange(dgu10, a10, h10, dy10), _ChipScatterRider([g_in3], [])])
    (dx2, dgu02, a02, dy02, st02), rode = _ffn_bwd(dx3, x2, pb[0][2], g02, u02, y02, win[1], wout[1], 1, ride)
    got2, (p_in[3],) = ride.split(rode)
    g_in2, _ = ffn_wgrads(2, dgu10, a10, h10, dy10, got2)
    ride = _Riders([exchange(dgu02, a02, h02, dy02), _ChipScatterRider([g_in2], [])])
    (dx1, dproj, cat, dy01, st01, dnv, dws, dbs, dcw), rode = _ab_bwd(
        dx2, x1, pb[0][1], proj, ab_out, abin_all, about_all, nv, ws, bst, cw8, ride)
    got1, (p_in[2],) = ride.split(rode)
    g_in1, _ = ffn_wgrads(1, dgu02, a02, h02, dy02, got1)
    zero_stats = jnp.zeros((8, D), F32)
    stats = [[zero_stats, st01, st02], [st10, st11, st12]]
    dmod = jnp.stack([jnp.concatenate([stats[l][s][0:3].reshape(-1) for s in range(3)]) for l in range(2)])
    dng = jnp.stack([jnp.stack([stats[l][s][3] for s in range(3)]) for l in range(2)])
    spack, so = _pack([dmod, dng, dnv[0], dbs[:, :, 0], dcw[0:3], dps[0], dfg[0], loss_blk[0]])
    ride = _Riders([_SiblingRider([(dproj, 5), (cat, 2), (h01, None), (dy01, None)]), _SmallGatherRider(spack),
                    _SmallGatherRider(dws.astype(BF16))])
    (dgu00, a00, dy00, gst00), rode = _ffn_bwd_acts(dx1, pb[0][0], g00, u00, y00, wout[0], 0, ride)
    (s_dproj, s_cat, s_h01, s_dy01), (sgot,), (got_ws,) = ride.split(rode)
    g_about, _ = _wgrad_pair(cat, s_cat, dy01, s_dy01, sel_rows(2), 2, False, "wgrad_ab_out")
    g_abin, (p_about,) = _wgrad_pair(dproj, s_dproj, h01, s_h01, sel_rows(5), 5, False, "wgrad_ab_in",
                                     _ChipScatterRider([g_about], []))
    ride = _Riders([exchange(dgu00, a00, h00, dy00), _ChipScatterRider([g_in1], [])])
    (dx0, st00), rode = _ffn_bwd_dx(dx1, x0, pb[0][0], dgu00, gst00, win[0], 0, ride)
    got0, (p_in[1],) = ride.split(rode)
    grad_x = dx0[None]
    ride = _Riders([_ChipScatterRider([g_abin], []), _SmallGatherRider(st00[0:4].reshape(-1, 128))])
    g_in0, rode = ffn_wgrads(0, dgu00, a00, h00, dy00, got0, ride)
    (p_abin,), (sgot0,) = ride.split(rode)

    shape_in, shape_out = w_ffn_in.shape, w_ffn_out.shape
    fin = lambda a: jnp.swapaxes(a.reshape(4, D, n_in), 1, 2)
    fabin = lambda a: jnp.swapaxes(a, 1, 2)
    fout = lambda a: a.reshape(4, n_out, D)
    fpool = lambda a: a.reshape(1, 4 * n_pool, POOL_G)
    r_out, (p_in[0],) = _finish(p_out, fout(w_ffn_out), fout(m_w_ffn_out), fout(v_w_ffn_out), n_out // 2, "finish_ffn_out",
                                halves=True, rider=_ChipScatterRider([g_in0], []))
    r_in, _ = _finish(p_in, fin(w_ffn_in), fin(m_w_ffn_in), fin(v_w_ffn_in), n_in // 4, "finish_ffn_in")
    r_abin, _ = _finish([p_abin], fabin(ab_w_in), fabin(m_ab_w_in), fabin(v_ab_w_in), n_abin, "finish_ab_in")
    r_about, _ = _finish([p_about], ab_w_out, m_ab_w_out, v_ab_w_out, n_about, "finish_ab_out")
    r_pool, _ = _finish([p_pool.reshape(N_DEV, 4 * n_pool, POOL_G)], fpool(pool_w_grp), fpool(m_pool_w_grp),
                        fpool(v_pool_w_grp), 4 * n_pool, "finish_pool")
    r_in = [jnp.swapaxes(a, 1, 2).reshape(shape_in) for a in r_in]
    r_abin = [jnp.swapaxes(a, 1, 2) for a in r_abin]
    r_out = [a.reshape(shape_out) for a in r_out]
    r_pool = [a.reshape(pool_w_grp.shape) for a in r_pool]

    rows_mod, rows_gn = 3 * D // 128, D // 128
    ssum, g_ws = _sum_small(sgot, sgot0, [(0, rows_mod, so[0] // 128), (rows_mod, rows_mod + rows_gn, so[1] // 128)], got_ws)
    ssum = ssum.reshape(-1)
    loss = ssum[so[7]]
    take = lambda i, n: lax.dynamic_slice_in_dim(ssum, so[i], n)
    g_bmod = take(0, 2 * 9 * D).reshape(2, 9 * D)
    g_ng = lax.dynamic_slice_in_dim(take(1, 6 * D).reshape(2, 3, D), me * n_ng, n_ng, axis=2)
    g_nv = take(2, D_A).reshape(1, D_A)
    g_ws = g_ws[None]
    g_bs = take(3, A_HEADS * CHUNK).reshape(1, A_HEADS, CHUNK)
    g_cw = lax.dynamic_slice_in_dim(take(4, 3 * D_B).reshape(1, 3, D_B), me * n_cw, n_cw, axis=2)
    g_ps = lax.dynamic_slice_in_dim(take(5, D).reshape(1, D), me * n_ps, n_ps, axis=1)
    g_fg = take(6, D)

    dmod_all = sgot.reshape(N_DEV, -1)[:, so[0]:so[0] + 2 * 9 * D].reshape(N_DEV, 2, 9 * D)
    dmod_all = dmod_all.at[:, 0, 0:3 * D].set(sgot0[:, 0:rows_mod].reshape(N_DEV, 3 * D))
    dmod_cols = lax.dynamic_slice_in_dim(dmod_all, me * n_mod, n_mod, axis=2).transpose(1, 0, 2)
    r_wmod = _wmod_finish(act_all.T, dmod_cols, w_mod, m_w_mod, v_w_mod)

    small_w = [b_mod, norm_g, ab_norm_v, ab_w_s, ab_b_s, ab_conv_w, pool_scale, final_g]
    small_g = [g_bmod, g_ng, g_nv, g_ws, g_bs, g_cw, g_ps, g_fg]
    small_m = [m_b_mod, m_norm_g, m_ab_norm_v, m_ab_w_s, m_ab_b_s, m_ab_conv_w, m_pool_scale, m_final_g]
    small_v = [v_b_mod, v_norm_g, v_ab_norm_v, v_ab_w_s, v_ab_b_s, v_ab_conv_w, v_pool_scale, v_final_g]
    pw, po = _pack(small_w)
    pv = jnp.concatenate([jnp.pad(a.reshape(-1), (0, -a.size % 128), constant_values=1.0) for a in small_v]).reshape(-1, 128)
    sd, sm, sv = _adamw(pw, _pack(small_g)[0], _pack(small_m)[0], pv, "adamw_small")
    unpack = lambda packed: [packed.reshape(-1)[po[i]:po[i] + a.size].reshape(a.shape) for i, a in enumerate(small_w)]
    d_s, m_s, v_s = unpack(sd), unpack(sm), unpack(sv)

    def ordered(k, small):
        return [small[1], r_wmod[k], small[0], r_in[k], r_out[k], r_abin[k], small[2], small[3], small[4], small[5],
                r_about[k], r_pool[k], small[6], small[7]]

    grads = ordered(0, small_g)
    deltas = ordered(1, d_s)
    new_m = ordered(2, m_s)
    new_v = ordered(3, v_s)
    return (loss, grad_x, *grads, *deltas, *new_m, *new_v)
```

```python
import functools
import math

import jax
import jax.numpy as jnp
from jax import lax
from jax.experimental import pallas as pl
from jax.experimental.pallas import tpu as pltpu

F32 = jnp.float32
BF16 = jnp.bfloat16

N_DEV = 8
D = 1024
DFF = 2816
HC = 256
NCH = DFF // HC
D_A = 512
D_B = 512
D_AB = 2 * D_A + 3 * D_B
CHUNK = 128
A_HEADS = 4
POOL_G = 256
POOL_HALO = 16
CONV_HALO = 8
EPS = 1e-6
TM = 256
GELU_K = math.sqrt(2.0 / math.pi)
GELU_C = 0.044715

ADAM_LR = 0.001
ADAM_B1 = 0.9
ADAM_B2 = 0.999
ADAM_EPS = 1e-08
ADAM_WD = 0.01
ADAM_STEP = 10

VMEM_LIMIT = 56 * 1024 * 1024
MESH_ID = pl.DeviceIdType.MESH
ANY = pl.BlockSpec(memory_space=pl.ANY)
VMEM_SPEC = pl.BlockSpec(memory_space=pltpu.VMEM)
ARB1 = pltpu.CompilerParams(dimension_semantics=("arbitrary",), vmem_limit_bytes=VMEM_LIMIT)


def _dot_nt(a, b):
    return lax.dot_general(a, b, (((1,), (1,)), ((), ())), preferred_element_type=F32)


def _dot_nn(a, b):
    return lax.dot_general(a, b, (((1,), (0,)), ((), ())), preferred_element_type=F32)


def _dot_tn(a, b):
    return lax.dot_general(a, b, (((0,), (0,)), ((), ())), preferred_element_type=F32)


def _colsum8(v):
    r, n = v.shape
    return jnp.sum(v.reshape(r // 8, 8, n), axis=0)


def _gelu(x):
    return 0.5 * x * (1.0 + jnp.tanh(GELU_K * (x + GELU_C * x * x * x)))


def _gelu_grad(x):
    t = jnp.tanh(GELU_K * (x + GELU_C * x * x * x))
    return 0.5 * (1.0 + t) + 0.5 * x * (1.0 - t * t) * (GELU_K * (1.0 + 3.0 * GELU_C * x * x))


def _mod_rows(p_ref):
    return p_ref[0:1, :], p_ref[1:2, :], p_ref[2:3, :], p_ref[3:4, :]


def _modulate(x, gn, sh, sc):
    r = lax.rsqrt(jnp.mean(x * x, axis=-1, keepdims=True) + EPS)
    return ((x * r) * gn) * (1.0 + sc) + sh


def _modulate_bwd(dh, x, gn, sc, stats):
    r = lax.rsqrt(jnp.mean(x * x, axis=-1, keepdims=True) + EPS)
    xn = x * r
    stats[0] += _colsum8(dh)
    stats[1] += _colsum8(dh * (xn * gn))
    dy0 = dh * (1.0 + sc)
    stats[3] += _colsum8(dy0 * xn)
    dxn = dy0 * gn
    return r * (dxn - xn * jnp.mean(dxn * xn, axis=-1, keepdims=True))


def _stats_out(stats, out_ref):
    rows = [jnp.sum(stats[k], axis=0, keepdims=True) for k in range(4)]
    out_ref[...] = jnp.concatenate(rows + [jnp.zeros((4, stats.shape[-1]), F32)], axis=0)


def _shift_down(v, k, prev):
    n = v.shape[0]
    row = lax.broadcasted_iota(jnp.int32, v.shape, 0)
    out = pltpu.roll(v, k, 0)
    for j in range(k):
        out = jnp.where(row == j, prev[prev.shape[0] - k + j:prev.shape[0] - k + j + 1, :], out)
    return out


def _shift_up(v, k, nxt):
    n = v.shape[0]
    row = lax.broadcasted_iota(jnp.int32, v.shape, 0)
    out = pltpu.roll(v, n - k, 0)
    for j in range(k):
        out = jnp.where(row == n - k + j, nxt[j:j + 1, :], out)
    return out


def _load_rows(w_hbm, sel, dst, sems, base):
    n = dst.shape[0] // N_DEV
    cps = []
    for k in range(N_DEV):
        src = w_hbm.at[k] if sel is None else w_hbm.at[k, sel]
        cps.append(pltpu.make_async_copy(src, dst.at[pl.ds(k * n, n)], sems.at[base + k]))
    return cps


def _my_pos():
    return lax.axis_index("x"), lax.axis_index("y"), lax.axis_index("c")


def _peer(j):
    x, y, c = _my_pos()
    return (1 - x if j & 4 else x, 1 - y if j & 2 else y, 1 - c if j & 1 else c)


def _index(pos):
    return 4 * pos[0] + 2 * pos[1] + pos[2]


def _exchange_small(v, name, rider):
    rows = v.shape[0]
    ri, ro = len(rider.inputs), len(rider.out_shapes)

    def body(v_ref, *refs):
        r_in, out_ref, r_out, refs = refs[:ri], refs[ri], refs[ri + 1:ri + 1 + ro], refs[ri + 1 + ro:]
        send_sems, recv_sems, local_sem = refs[:3]
        rider.first(r_in, r_out, refs[3:])
        me = _index(_my_pos())

        def copy(j, slot):
            return pltpu.make_async_remote_copy(
                src_ref=v_ref, dst_ref=out_ref.at[slot], send_sem=send_sems.at[j - 1], recv_sem=recv_sems.at[j - 1],
                device_id=_peer(j), device_id_type=MESH_ID)

        mine = pltpu.make_async_copy(v_ref, out_ref.at[me], local_sem)
        mine.start()
        sends = [copy(j, me) for j in range(1, N_DEV)]
        for cp in sends:
            cp.start()
        for j in range(1, N_DEV):
            copy(j, _index(_peer(j))).wait_recv()
        for cp in sends:
            cp.wait_send()
        mine.wait()
        if rider.has_middle:
            rider.middle(r_in, r_out, refs[3:])
        rider.last(r_in, r_out, refs[3:])

    scratch = [pltpu.SemaphoreType.DMA((N_DEV - 1,)), pltpu.SemaphoreType.DMA((N_DEV - 1,)), pltpu.SemaphoreType.DMA(())]
    res = pl.pallas_call(
        body, name=name, out_shape=[jax.ShapeDtypeStruct((N_DEV, rows, 128), F32)] + rider.out_shapes,
        in_specs=[VMEM_SPEC] + [ANY] * ri, out_specs=[VMEM_SPEC] + [ANY] * ro,
        scratch_shapes=scratch + rider.scratch)(v, *rider.inputs)
    return res[0], list(res[1:])


class _SmallGatherRider:
    has_middle = False

    def __init__(self, v):
        self.inputs = [v]
        self.out_shapes = [jax.ShapeDtypeStruct((N_DEV,) + v.shape, v.dtype)]
        self.scratch = [pltpu.SemaphoreType.DMA((N_DEV - 1,)), pltpu.SemaphoreType.DMA((N_DEV - 1,)), pltpu.SemaphoreType.DMA(())]

    def _copy(self, ins, outs, scr, j, slot):
        return pltpu.make_async_remote_copy(
            src_ref=ins[0], dst_ref=outs[0].at[slot], send_sem=scr[0].at[j - 1], recv_sem=scr[1].at[j - 1],
            device_id=_peer(j), device_id_type=MESH_ID)

    def first(self, ins, outs, scr):
        me = _index(_my_pos())
        pltpu.make_async_copy(ins[0], outs[0].at[me], scr[2]).start()
        for j in range(1, N_DEV):
            self._copy(ins, outs, scr, j, me).start()

    def last(self, ins, outs, scr):
        me = _index(_my_pos())
        for j in range(1, N_DEV):
            self._copy(ins, outs, scr, j, _index(_peer(j))).wait_recv()
        for j in range(1, N_DEV):
            self._copy(ins, outs, scr, j, me).wait_send()
        pltpu.make_async_copy(ins[0], outs[0].at[me], scr[2]).wait()


class _GatherRider:
    has_middle = True

    def __init__(self, shards):
        pairs = [s if isinstance(s, tuple) else (s, None) for s in shards]
        self.inputs = [a for a, _ in pairs]
        self.picks = [i for _, i in pairs]
        shapes = [a.shape if i is None else a.shape[1:] for a, i in pairs]
        self.out_shapes = [jax.ShapeDtypeStruct((N_DEV,) + s, a.dtype) for s, (a, _) in zip(shapes, pairs)]
        self.pieces = [4 if (len(s) > 2 or s[0] % 64 == 0) else 2 for s in shapes]
        self.base = [sum(1 + 6 * n for n in self.pieces[:a]) for a in range(len(pairs))]
        total = sum(1 + 6 * n for n in self.pieces)
        self.scratch = [pltpu.SemaphoreType.DMA((total,)), pltpu.SemaphoreType.DMA((total,)),
                        pltpu.SemaphoreType.DMA((len(pairs),))]

    def _src(self, ins):
        return [r if i is None else r.at[i] for r, i in zip(ins, self.picks)]

    def _ctx(self, outs, scr):
        send, recv, _ = scr
        x, y, c = _my_pos()
        chips = [(1 - x, y), (x, 1 - y), (1 - x, 1 - y)]

        def copy(a, k, block, to, src=None, piece=None):
            slot = outs[a].at[_index(block)]
            src = slot if src is None else src
            if piece is not None:
                rows = slot.shape[0] // self.pieces[a]
                slot, src = slot.at[pl.ds(piece * rows, rows)], src.at[pl.ds(piece * rows, rows)]
            return pltpu.make_async_remote_copy(
                src_ref=src, dst_ref=slot, send_sem=send.at[self.base[a] + k], recv_sem=recv.at[self.base[a] + k],
                device_id=to, device_id_type=MESH_ID)

        return (x, y, c), (x, y, 1 - c), chips, copy

    def _sends(self, ins, outs, scr):
        me, sib, chips, copy = self._ctx(outs, scr)
        srcs = self._src(ins)
        out = [copy(a, 0, me, sib, src=srcs[a]) for a in range(len(ins))]
        for s in range(4):
            for a in range(len(ins)):
                if s < self.pieces[a]:
                    out += [copy(a, 1 + j * self.pieces[a] + s, me, (*chips[j], me[2]), src=srcs[a], piece=s) for j in range(3)]
        return out

    def first(self, ins, outs, scr):
        me = _index(_my_pos())
        for a, src in enumerate(self._src(ins)):
            pltpu.make_async_copy(src, outs[a].at[me], scr[2].at[a]).start()
        for cp in self._sends(ins, outs, scr):
            cp.start()

    def _forwards(self, ins, outs, scr, core):
        me, sib, chips, copy = self._ctx(outs, scr)
        out = []
        for s in range(4):
            for a in range(len(ins)):
                n = self.pieces[a]
                if s < n:
                    for j in range(3):
                        out.append((copy(a, 1 + j * n + s, (*chips[j], core), me, piece=s),
                                    copy(a, 1 + 3 * n + j * n + s, (*chips[j], core), sib, piece=s)))
        return out

    def middle(self, ins, outs, scr):
        me = _my_pos()
        for arrival, forward in self._forwards(ins, outs, scr, me[2]):
            arrival.wait_recv()
            forward.start()

    def last(self, ins, outs, scr):
        me, sib, chips, copy = self._ctx(outs, scr)
        for a in range(len(ins)):
            copy(a, 0, sib, me).wait_recv()
        for _, forward in self._forwards(ins, outs, scr, sib[2]):
            forward.wait_recv()
        for cp in self._sends(ins, outs, scr):
            cp.wait_send()
        for _, forward in self._forwards(ins, outs, scr, me[2]):
            forward.wait_send()
        for a, src in enumerate(self._src(ins)):
            pltpu.make_async_copy(src, outs[a].at[_index(me)], scr[2].at[a]).wait()


class _ScatterRider:
    has_middle = False

    def __init__(self, grads):
        n = len(grads)
        self.inputs = list(grads)
        self.out_shapes = []
        for g in grads:
            if g.ndim == 3:
                self.out_shapes.append(jax.ShapeDtypeStruct((N_DEV, g.shape[0], g.shape[1] // N_DEV, g.shape[2]), g.dtype))
            else:
                self.out_shapes.append(jax.ShapeDtypeStruct((N_DEV, g.shape[0] // N_DEV, g.shape[1]), g.dtype))
        self.scratch = [pltpu.SemaphoreType.DMA((7 * n,)), pltpu.SemaphoreType.DMA((7 * n,)), pltpu.SemaphoreType.DMA((n,))]

    @staticmethod
    def _part(ref, k):
        if ref.ndim == 3:
            n = ref.shape[1] // N_DEV
            return ref.at[:, pl.ds(pl.multiple_of(k * n, 16), n)]
        n = ref.shape[0] // N_DEV
        return ref.at[pl.ds(pl.multiple_of(k * n, 16), n)]

    def _copy(self, ins, outs, scr, g, j, to, src_dev):
        return pltpu.make_async_remote_copy(
            src_ref=self._part(ins[g], to), dst_ref=outs[g].at[src_dev], send_sem=scr[0].at[7 * g + j - 1],
            recv_sem=scr[1].at[7 * g + j - 1], device_id=_peer(j), device_id_type=MESH_ID)

    def first(self, ins, outs, scr):
        me = _index(_my_pos())
        for g in range(len(ins)):
            pltpu.make_async_copy(self._part(ins[g], me), outs[g].at[me], scr[2].at[g]).start()
        for j in range(1, N_DEV):
            for g in range(len(ins)):
                self._copy(ins, outs, scr, g, j, _index(_peer(j)), me).start()

    def last(self, ins, outs, scr):
        me = _index(_my_pos())
        for j in range(1, N_DEV):
            for g in range(len(ins)):
                self._copy(ins, outs, scr, g, j, me, _index(_peer(j))).wait_recv()
        for j in range(1, N_DEV):
            for g in range(len(ins)):
                self._copy(ins, outs, scr, g, j, _index(_peer(j)), me).wait_send()
        for g in range(len(ins)):
            pltpu.make_async_copy(self._part(ins[g], me), outs[g].at[me], scr[2].at[g]).wait()


class _SiblingRider:
    has_middle = False

    def __init__(self, items):
        self.inputs = [a for a, _ in items]
        self.counts = [n for _, n in items]
        self.out_shapes = [jax.ShapeDtypeStruct(a.shape if n is None else (n,) + a.shape[1:], a.dtype) for a, n in items]
        self.scratch = [pltpu.SemaphoreType.DMA((len(items),)), pltpu.SemaphoreType.DMA((len(items),))]

    def _copies(self, ins, outs, scr):
        x, y, c = _my_pos()
        out = []
        for i, (ref, n) in enumerate(zip(ins, self.counts)):
            src = ref if n is None else ref.at[pl.ds((1 - c) * n, n)]
            out.append(pltpu.make_async_remote_copy(
                src_ref=src, dst_ref=outs[i], send_sem=scr[0].at[i], recv_sem=scr[1].at[i],
                device_id=(x, y, 1 - c), device_id_type=MESH_ID))
        return out

    def first(self, ins, outs, scr):
        for cp in self._copies(ins, outs, scr):
            cp.start()

    def last(self, ins, outs, scr):
        for cp in self._copies(ins, outs, scr):
            cp.wait()


class _ChipScatterRider:
    has_middle = False

    def __init__(self, rows, cols):
        self.nr, self.nc = len(rows), len(cols)
        self.inputs = list(rows) + list(cols)
        self.out_shapes = [jax.ShapeDtypeStruct((4, a.shape[0] // 4, a.shape[1]), a.dtype) for a in rows]
        self.out_shapes += [jax.ShapeDtypeStruct((4, 2, a.shape[0] // N_DEV, a.shape[1]), a.dtype) for a in cols]
        n = 4 * self.nr + N_DEV * self.nc
        self.scratch = [pltpu.SemaphoreType.DMA((n,)), pltpu.SemaphoreType.DMA((n,)), pltpu.SemaphoreType.DMA((n,))]

    def _pieces(self, ins, outs):
        x, y, c = _my_pos()
        q = 2 * x + y
        out = []
        for a in range(self.nr):
            n = ins[a].shape[0] // 4
            for j in range(4):
                out.append((4 * a + j, ins[a].at[pl.ds(j * n, n)], (c, j >> 1, j & 1), outs[a].at[q], 4 * a + q))
        for a in range(self.nc):
            ref, base = ins[self.nr + a], 4 * self.nr + N_DEV * a
            n = ref.shape[0] // N_DEV
            for k in range(N_DEV):
                out.append((base + k, ref.at[pl.ds(k * n, n)], (k >> 2, (k >> 1) & 1, k & 1),
                            outs[self.nr + a].at[q, c], base + 2 * q + c))
        return out

    def first(self, ins, outs, scr):
        send, recv, local = scr
        me = _index(_my_pos())
        for s, src, to, slot, r in self._pieces(ins, outs):
            mine = _index(to) == me

            @pl.when(mine)
            def _():
                pltpu.make_async_copy(src, slot, local.at[s]).start()

            @pl.when(jnp.logical_not(mine))
            def _():
                pltpu.make_async_remote_copy(src_ref=src, dst_ref=slot, send_sem=send.at[s], recv_sem=recv.at[r],
                                             device_id=to, device_id_type=MESH_ID).start()

    def last(self, ins, outs, scr):
        send, recv, local = scr
        x, y, c = _my_pos()
        me = _index((x, y, c))
        arrivals = []
        for a in range(self.nr):
            n = ins[a].shape[0] // 4
            for q in range(4):
                arrivals.append((4 * a + q, (q >> 1, q & 1, x), ins[a].at[pl.ds(0, n)], outs[a].at[q], 4 * a + 2 * y + c))
        for a in range(self.nc):
            ref, base = ins[self.nr + a], 4 * self.nr + N_DEV * a
            n = ref.shape[0] // N_DEV
            for k in range(N_DEV):
                arrivals.append((base + k, (k >> 2, (k >> 1) & 1, k & 1), ref.at[pl.ds(0, n)],
                                 outs[self.nr + a].at[k >> 1, k & 1], base + me))
        for r, sender, src, slot, s_local in arrivals:
            mine = _index(sender) == me

            @pl.when(mine)
            def _():
                pltpu.make_async_copy(src, slot, local.at[s_local]).wait()

            @pl.when(jnp.logical_not(mine))
            def _():
                pltpu.make_async_remote_copy(src_ref=src, dst_ref=slot, send_sem=send.at[r], recv_sem=recv.at[r],
                                             device_id=sender, device_id_type=MESH_ID).wait_recv()

        for s, src, to, slot, r in self._pieces(ins, outs):
            @pl.when(_index(to) != me)
            def _():
                pltpu.make_async_remote_copy(src_ref=src, dst_ref=slot, send_sem=send.at[s], recv_sem=recv.at[r],
                                             device_id=to, device_id_type=MESH_ID).wait_send()


class _Riders:
    def __init__(self, riders):
        self.riders = list(riders)
        self.inputs = [a for r in self.riders for a in r.inputs]
        self.out_shapes = [s for r in self.riders for s in r.out_shapes]
        self.scratch = [s for r in self.riders for s in r.scratch]
        self.has_middle = any(r.has_middle for r in self.riders)

    def _each(self, ins, outs, scr):
        i = o = s = 0
        for r in self.riders:
            ni, no, ns = len(r.inputs), len(r.out_shapes), len(r.scratch)
            yield r, ins[i:i + ni], outs[o:o + no], scr[s:s + ns]
            i, o, s = i + ni, o + no, s + ns

    def first(self, ins, outs, scr):
        for r, a, b, c in self._each(ins, outs, scr):
            r.first(a, b, c)

    def middle(self, ins, outs, scr):
        for r, a, b, c in self._each(ins, outs, scr):
            if r.has_middle:
                r.middle(a, b, c)

    def last(self, ins, outs, scr):
        for r, a, b, c in self._each(ins, outs, scr):
            r.last(a, b, c)

    def split(self, outs):
        res, o = [], 0
        for r in self.riders:
            res.append(list(outs[o:o + len(r.out_shapes)]))
            o += len(r.out_shapes)
        return res


def _run(body, *, name, grid, in_specs, out_specs, out_shape, scratch_shapes, args, rider=None, params=None, prefetch=()):
    params = ARB1 if params is None else params
    npf = len(prefetch)
    ni, no, ns = len(in_specs), len(out_shape), len(scratch_shapes)
    ri, ro = (len(rider.inputs), len(rider.out_shapes)) if rider is not None else (0, 0)

    def wrapped(*refs):
        pf, refs = refs[:npf], refs[npf:]
        cut = [ni, ni + ri, ni + ri + no, ni + ri + no + ro, ni + ri + no + ro + ns]
        a, b, c, d, e, f = (refs[lo:hi] for lo, hi in zip([0] + cut, cut + [len(refs)]))
        if rider is not None:
            ids = [pl.program_id(k) for k in range(len(grid))]
            at_first = functools.reduce(jnp.logical_and, [i == 0 for i in ids])
            at_last = functools.reduce(jnp.logical_and, [i == n - 1 for i, n in zip(ids, grid)])

            @pl.when(at_first)
            def _():
                rider.first(b, d, f)

            if rider.has_middle:
                @pl.when(at_last)
                def _():
                    rider.middle(b, d, f)

        body(*pf, *a, *c, *e)

        if rider is not None:
            @pl.when(at_last)
            def _():
                rider.last(b, d, f)

    extra_shapes = rider.out_shapes if rider is not None else []
    extra_scratch = rider.scratch if rider is not None else []
    extra_inputs = rider.inputs if rider is not None else []
    all_in, all_out = list(in_specs) + [ANY] * ri, list(out_specs) + [ANY] * ro
    all_scratch = list(scratch_shapes) + extra_scratch
    if npf:
        outs = pl.pallas_call(
            wrapped, name=name, out_shape=list(out_shape) + extra_shapes,
            grid_spec=pltpu.PrefetchScalarGridSpec(num_scalar_prefetch=npf, grid=grid, in_specs=all_in, out_specs=all_out,
                                                   scratch_shapes=all_scratch),
            compiler_params=params)(*prefetch, *args, *extra_inputs)
    else:
        outs = pl.pallas_call(
            wrapped, name=name, grid=grid, in_specs=all_in, out_specs=all_out, out_shape=list(out_shape) + extra_shapes,
            scratch_shapes=all_scratch, compiler_params=params)(*args, *extra_inputs)
    return list(outs[:no]), list(outs[no:])


def _mod_fwd(c_all, w_mod):
    ncol = w_mod.shape[-1]

    def body(c_ref, w_ref, act_ref, out_ref):
        c = c_ref[...]
        act = c * jax.nn.sigmoid(c)
        act_ref[...] = act
        out_ref[0] = _dot_nn(act.astype(BF16), w_ref[0].astype(BF16))

    return pl.pallas_call(
        body, name="mod_fwd", grid=(2,),
        out_shape=[jax.ShapeDtypeStruct((N_DEV, D), F32), jax.ShapeDtypeStruct((2, N_DEV, ncol), F32)],
        in_specs=[pl.BlockSpec((N_DEV, D), lambda l: (0, 0)), pl.BlockSpec((1, D, ncol), lambda l: (l, 0, 0))],
        out_specs=[pl.BlockSpec((N_DEV, D), lambda l: (0, 0)), pl.BlockSpec((1, N_DEV, ncol), lambda l: (l, 0, 0))],
        compiler_params=ARB1,
    )(c_all, w_mod)


def _head_tile(x, g, tgt, sq, dfg):
    r = lax.rsqrt(jnp.mean(x * x, axis=-1, keepdims=True) + EPS)
    xn = x * r
    e = xn * g - tgt
    sq[...] += _colsum8(e * e)
    dy = e * (1.0 / D)
    dfg[...] += _colsum8(dy * xn)
    dxn = dy * g
    return r * (dxn - xn * jnp.mean(dxn * xn, axis=-1, keepdims=True))


def _head_out(sq, dfg, loss_ref, dfg_ref):
    total = jnp.sum(jnp.sum(sq[...], axis=0, keepdims=True), axis=1, keepdims=True)
    loss_ref[...] = jnp.broadcast_to(total * (0.5 / D), loss_ref.shape)
    dfg_ref[...] = jnp.concatenate([jnp.sum(dfg[...], axis=0, keepdims=True), jnp.zeros((7, D), F32)], axis=0)


def _ffn_fwd(x, p, win_all, wout_all, f, rider=None, head=None):
    s = x.shape[0]
    nt = s // TM
    nh = 0 if head is None else 2

    def body(x_ref, p_ref, win_hbm, wout_hbm, *refs):
        head_in, refs = refs[:nh], refs[nh:]
        xo_ref, h_ref, g_ref, u_ref, y_ref, a_ref = refs[:6]
        head_out, (win, wout, act, sems), head_acc = refs[6:6 + nh], refs[6 + nh:10 + nh], refs[10 + nh:]
        i = pl.program_id(0)

        @pl.when(i == 0)
        def _():
            cps = _load_rows(win_hbm, None, win, sems, 0) + _load_rows(wout_hbm, None, wout, sems, N_DEV)
            for cp in cps:
                cp.start()
            for z in head_acc:
                z[...] = jnp.zeros_like(z)
            for cp in cps:
                cp.wait()

        sh, sc, gate, gn = _mod_rows(p_ref)
        x = x_ref[...]
        hb = _modulate(x, gn, sh, sc).astype(BF16)
        h_ref[...] = hb
        for c in range(NCH):
            g = _dot_nt(hb, win[c * HC:(c + 1) * HC, :])
            u = _dot_nt(hb, win[DFF + c * HC:DFF + (c + 1) * HC, :])
            g_ref[c] = g.astype(BF16)
            u_ref[c] = u.astype(BF16)
            a = ((g * jax.nn.sigmoid(g)) * u).astype(BF16)
            a_ref[c] = a
            act[:, c * HC:(c + 1) * HC] = a
        y = _dot_nn(act[...], wout[...])
        y_ref[...] = y
        xo = x + (0.5 * gate) * y
        if head is None:
            xo_ref[...] = xo
        else:
            xo_ref[...] = _head_tile(xo, head_in[0][...], head_in[1][...], *head_acc)

            @pl.when(i == nt - 1)
            def _():
                _head_out(*head_acc, *head_out)

    tile = pl.BlockSpec((TM, D), lambda i: (i, 0))
    chunks = pl.BlockSpec((NCH, TM, HC), lambda i: (0, i, 0))
    small = pl.BlockSpec((8, D), lambda i: (0, 0))
    head_specs = [] if head is None else [pl.BlockSpec((1, D), lambda i: (0, 0)), tile]
    return _run(
        body, name=f"ffn_fwd_{f}", grid=(nt,),
        out_shape=[jax.ShapeDtypeStruct((s, D), F32), jax.ShapeDtypeStruct((s, D), BF16),
                   jax.ShapeDtypeStruct((NCH, s, HC), BF16), jax.ShapeDtypeStruct((NCH, s, HC), BF16),
                   jax.ShapeDtypeStruct((s, D), F32), jax.ShapeDtypeStruct((NCH, s, HC), BF16)]
        + ([] if head is None else [jax.ShapeDtypeStruct((8, 128), F32), jax.ShapeDtypeStruct((8, D), F32)]),
        in_specs=[tile, small, ANY, ANY] + head_specs,
        out_specs=[tile, tile, chunks, chunks, tile, chunks]
        + ([] if head is None else [pl.BlockSpec((8, 128), lambda i: (0, 0)), small]),
        scratch_shapes=[pltpu.VMEM((2 * DFF, D), BF16), pltpu.VMEM((DFF, D), BF16), pltpu.VMEM((TM, DFF), BF16),
                        pltpu.SemaphoreType.DMA((2 * N_DEV,))]
        + ([] if head is None else [pltpu.VMEM((8, D), F32), pltpu.VMEM((8, D), F32)]),
        args=(x, p, win_all, wout_all) + (() if head is None else tuple(head)), rider=rider)


def _swiglu_bwd_acts(dyb, g_ref, u_ref, wout, dgu_ref, dgu):
    for c in range(NCH):
        da = _dot_nt(dyb, wout[c * HC:(c + 1) * HC, :])
        g = g_ref[c].astype(F32)
        u = u_ref[c].astype(F32)
        sg = jax.nn.sigmoid(g)
        dg = ((da * u) * (sg * (1.0 + g * (1.0 - sg)))).astype(BF16)
        du = (da * (g * sg)).astype(BF16)
        dgu_ref[c] = dg
        dgu_ref[NCH + c] = du
        if dgu is not None:
            dgu[:, c * HC:(c + 1) * HC] = dg
            dgu[:, DFF + c * HC:DFF + (c + 1) * HC] = du


def _ffn_bwd(dxo, x, p, g3, u3, y, win_all, wout_all, f, rider=None):
    s = x.shape[0]
    nt = s // TM

    def body(dxo_ref, x_ref, p_ref, g_ref, u_ref, y_ref, win_hbm, wout_hbm,
             dx_ref, dgu_ref, dy_ref, st_ref, win, wout, dgu, stats, sems):
        i = pl.program_id(0)

        @pl.when(i == 0)
        def _():
            cps = _load_rows(win_hbm, None, win, sems, 0) + _load_rows(wout_hbm, None, wout, sems, N_DEV)
            for cp in cps:
                cp.start()
            stats[...] = jnp.zeros_like(stats)
            for cp in cps:
                cp.wait()

        sh, sc, gate, gn = _mod_rows(p_ref)
        x = x_ref[...]
        dxo = dxo_ref[...]
        dyb = ((0.5 * gate) * dxo).astype(BF16)
        dy_ref[...] = dyb
        stats[2] += _colsum8((0.5 * dxo) * y_ref[...])
        _swiglu_bwd_acts(dyb, g_ref, u_ref, wout, dgu_ref, dgu)
        dh = _dot_nn(dgu[...], win[...])
        dx_ref[...] = dxo + _modulate_bwd(dh, x, gn, sc, stats)

        @pl.when(i == nt - 1)
        def _():
            _stats_out(stats, st_ref)

    tile = pl.BlockSpec((TM, D), lambda i: (i, 0))
    chunks = pl.BlockSpec((NCH, TM, HC), lambda i: (0, i, 0))
    small = pl.BlockSpec((8, D), lambda i: (0, 0))
    return _run(
        body, name=f"ffn_bwd_{f}", grid=(nt,),
        out_shape=[jax.ShapeDtypeStruct((s, D), F32), jax.ShapeDtypeStruct((2 * NCH, s, HC), BF16),
                   jax.ShapeDtypeStruct((s, D), BF16), jax.ShapeDtypeStruct((8, D), F32)],
        in_specs=[tile, tile, small, chunks, chunks, tile, ANY, ANY],
        out_specs=[tile, pl.BlockSpec((2 * NCH, TM, HC), lambda i: (0, i, 0)), tile, small],
        scratch_shapes=[pltpu.VMEM((2 * DFF, D), BF16), pltpu.VMEM((DFF, D), BF16), pltpu.VMEM((TM, 2 * DFF), BF16),
                        pltpu.VMEM((4, 8, D), F32), pltpu.SemaphoreType.DMA((2 * N_DEV,))],
        args=(dxo, x, p, g3, u3, y, win_all, wout_all), rider=rider)


def _ffn_bwd_acts(dxo, p, g3, u3, y, wout_all, f, rider=None):
    s = dxo.shape[0]
    nt = s // TM

    def body(dxo_ref, p_ref, g_ref, u_ref, y_ref, wout_hbm, dgu_ref, dy_ref, st_ref, wout, stat, sems):
        i = pl.program_id(0)

        @pl.when(i == 0)
        def _():
            cps = _load_rows(wout_hbm, None, wout, sems, 0)
            for cp in cps:
                cp.start()
            stat[...] = jnp.zeros_like(stat)
            for cp in cps:
                cp.wait()

        gate = p_ref[2:3, :]
        dxo = dxo_ref[...]
        dyb = ((0.5 * gate) * dxo).astype(BF16)
        dy_ref[...] = dyb
        stat[...] += _colsum8((0.5 * dxo) * y_ref[...])
        _swiglu_bwd_acts(dyb, g_ref, u_ref, wout, dgu_ref, None)

        @pl.when(i == nt - 1)
        def _():
            row = jnp.sum(stat[...], axis=0, keepdims=True)
            st_ref[...] = jnp.concatenate([jnp.zeros((2, D), F32), row, jnp.zeros((5, D), F32)], axis=0)

    tile = pl.BlockSpec((TM, D), lambda i: (i, 0))
    chunks = pl.BlockSpec((NCH, TM, HC), lambda i: (0, i, 0))
    small = pl.BlockSpec((8, D), lambda i: (0, 0))
    return _run(
        body, name=f"ffn_bwd_acts_{f}", grid=(nt,),
        out_shape=[jax.ShapeDtypeStruct((2 * NCH, s, HC), BF16), jax.ShapeDtypeStruct((s, D), BF16),
                   jax.ShapeDtypeStruct((8, D), F32)],
        in_specs=[tile, small, chunks, chunks, tile, ANY],
        out_specs=[pl.BlockSpec((2 * NCH, TM, HC), lambda i: (0, i, 0)), tile, small],
        scratch_shapes=[pltpu.VMEM((DFF, D), BF16), pltpu.VMEM((8, D), F32), pltpu.SemaphoreType.DMA((N_DEV,))],
        args=(dxo, p, g3, u3, y, wout_all), rider=rider)


def _ffn_bwd_dx(dxo, x, p, dgu3, gate_stats, win_all, f, rider=None):
    s = x.shape[0]
    nt = s // TM

    def body(dxo_ref, x_ref, p_ref, dgu_ref, gs_ref, win_hbm, dx_ref, st_ref, win, dgu, stats, sems):
        i = pl.program_id(0)

        @pl.when(i == 0)
        def _():
            cps = _load_rows(win_hbm, None, win, sems, 0)
            for cp in cps:
                cp.start()
            stats[...] = jnp.zeros_like(stats)
            for cp in cps:
                cp.wait()

        _, sc, _, gn = _mod_rows(p_ref)
        for c in range(2 * NCH):
            dgu[:, c * HC:(c + 1) * HC] = dgu_ref[c]
        dh = _dot_nn(dgu[...], win[...])
        dx_ref[...] = dxo_ref[...] + _modulate_bwd(dh, x_ref[...], gn, sc, stats)

        @pl.when(i == nt - 1)
        def _():
            _stats_out(stats, st_ref)
            st_ref[2:3, :] = gs_ref[2:3, :]

    tile = pl.BlockSpec((TM, D), lambda i: (i, 0))
    small = pl.BlockSpec((8, D), lambda i: (0, 0))
    return _run(
        body, name=f"ffn_bwd_dx_{f}", grid=(nt,),
        out_shape=[jax.ShapeDtypeStruct((s, D), F32), jax.ShapeDtypeStruct((8, D), F32)],
        in_specs=[tile, tile, small, pl.BlockSpec((2 * NCH, TM, HC), lambda i: (0, i, 0)), small, ANY],
        out_specs=[tile, small],
        scratch_shapes=[pltpu.VMEM((2 * DFF, D), BF16), pltpu.VMEM((TM, 2 * DFF), BF16), pltpu.VMEM((4, 8, D), F32),
                        pltpu.SemaphoreType.DMA((N_DEV,))],
        args=(dxo, x, p, dgu3, gate_stats, win_all), rider=rider)


def _wgrad_pair(own3, sib3, own_r, sib_r, sel, n, col_split, name, rider=None):
    nj, s, _ = own3.shape
    nw = own_r.shape[1] // 2 if col_split else own_r.shape[1]
    steps = nj if col_split else n

    def body(sel_ref, lo_ref, ls_ref, ro_ref, rs_ref, o_ref):
        o_ref[...] = (_dot_tn(lo_ref[0], ro_ref[...]) + _dot_tn(ls_ref[0], rs_ref[...])).astype(BF16)

    rspec = pl.BlockSpec((s, nw), lambda j, sel_ref: (0, sel_ref[1]))
    outs, rode = _run(
        body, name=name, grid=(steps,),
        out_shape=[jax.ShapeDtypeStruct((steps * HC, nw), BF16)],
        in_specs=[pl.BlockSpec((1, s, HC), lambda j, sel_ref: (sel_ref[0] + j, 0, 0)),
                  pl.BlockSpec((1, s, HC), lambda j, sel_ref: (j, 0, 0)), rspec, rspec],
        out_specs=[pl.BlockSpec((HC, nw), lambda j, sel_ref: (j, 0))],
        scratch_shapes=[], args=(own3, sib3, own_r, sib_r), rider=rider, prefetch=(sel,))
    return outs[0], rode


def _gating(proj, nv, ws_ref, bst):
    u, v = proj[:, 0:D_A], proj[:, D_A:2 * D_A]
    gu, gv = _gelu(u), _gelu(v)
    mu = jnp.mean(gv, axis=-1, keepdims=True)
    dv = gv - mu
    rstd = lax.rsqrt(jnp.mean(dv * dv, axis=-1, keepdims=True) + EPS)
    vhat = dv * rstd
    vn = vhat * nv
    r = lax.broadcasted_iota(jnp.int32, (CHUNK, CHUNK), 0)
    c = lax.broadcasted_iota(jnp.int32, (CHUNK, CHUNK), 1)
    wm = [jnp.where(r >= c, ws_ref[hd], 0.0).astype(BF16) for hd in range(A_HEADS)]
    vnb = vn.astype(BF16)
    rows = []
    for n in range(proj.shape[0] // CHUNK):
        blocks = []
        for hd in range(A_HEADS):
            blk = vnb[n * CHUNK:(n + 1) * CHUNK, hd * CHUNK:(hd + 1) * CHUNK]
            blocks.append(_dot_nn(wm[hd], blk) + bst[:, hd:hd + 1])
        rows.append(jnp.concatenate(blocks, axis=1))
    z = jnp.concatenate(rows, axis=0)
    return u, v, gu, rstd, vhat, vnb, wm, z


def _conv(proj, cw, prev_xp):
    bg = proj[:, 2 * D_A:2 * D_A + D_B]
    cg = proj[:, 2 * D_A + D_B:2 * D_A + 2 * D_B]
    xb = proj[:, 2 * D_A + 2 * D_B:]
    xp = cg * xb
    x1 = _shift_down(xp, 1, prev_xp)
    x2 = _shift_down(xp, 2, prev_xp)
    conv = cw[0:1, :] * x2 + cw[1:2, :] * x1 + cw[2:3, :] * xp
    return bg, cg, xb, xp, x1, x2, conv


def _ab_fwd(x, p, abin_all, about_all, nv, ws, bst, cw, rider=None):
    s = x.shape[0]
    nt = s // TM

    def body(x_ref, p_ref, abin_hbm, about_hbm, nv_ref, ws_ref, bst_ref, cw_ref,
             xo_ref, h_ref, proj_ref, out_ref, abin, about, prev, sems):
        @pl.when(pl.program_id(0) == 0)
        def _():
            cps = _load_rows(abin_hbm, None, abin, sems, 0) + _load_rows(about_hbm, None, about, sems, N_DEV)
            for cp in cps:
                cp.start()
            prev[...] = jnp.zeros_like(prev)
            for cp in cps:
                cp.wait()

        sh, sc, gate, gn = _mod_rows(p_ref)
        x = x_ref[...]
        hb = _modulate(x, gn, sh, sc).astype(BF16)
        h_ref[...] = hb
        proj = _dot_nt(hb, abin[...])
        proj_ref[...] = proj
        _, _, gu, _, _, _, _, z = _gating(proj, nv_ref[...], ws_ref, bst_ref[...])
        bg, _, _, xp, _, _, conv = _conv(proj, cw_ref[...], prev[...])
        prev[...] = xp[TM - CONV_HALO:, :]
        cat = jnp.concatenate([gu * z, bg * conv], axis=1).astype(BF16)
        out = _dot_nn(cat, about[...])
        out_ref[...] = out
        xo_ref[...] = x + gate * out

    tile = pl.BlockSpec((TM, D), lambda i: (i, 0))
    full = lambda a: pl.BlockSpec(a.shape, lambda i: (0,) * a.ndim)
    return _run(
        body, name="ab_fwd", grid=(nt,),
        out_shape=[jax.ShapeDtypeStruct((s, D), F32), jax.ShapeDtypeStruct((s, D), BF16),
                   jax.ShapeDtypeStruct((s, D_AB), F32), jax.ShapeDtypeStruct((s, D), F32)],
        in_specs=[tile, pl.BlockSpec((8, D), lambda i: (0, 0)), ANY, ANY, full(nv), full(ws), full(bst), full(cw)],
        out_specs=[tile, tile, pl.BlockSpec((TM, D_AB), lambda i: (i, 0)), tile],
        scratch_shapes=[pltpu.VMEM((D_AB, D), BF16), pltpu.VMEM((D, D), BF16), pltpu.VMEM((CONV_HALO, D_B), F32),
                        pltpu.SemaphoreType.DMA((2 * N_DEV,))],
        args=(x, p, abin_all, about_all, nv, ws, bst, cw), rider=rider)


def _ab_bwd(dxo, x, p, proj, out, abin_all, about_all, nv, ws, bst, cw, rider=None):
    s = x.shape[0]
    nt = s // TM
    npj = D_AB // HC

    def body(dxo_ref, x_ref, p_ref, proj_ref, halo_ref, out_ref, abin_hbm, about_hbm, nv_ref, ws_ref, bst_ref, cw_ref,
             dx_ref, dproj_ref, cat_ref, dy_ref, st_ref, dnv_ref, dws_ref, dbs_ref, dcw_ref,
             abin, about, nxt, stats, dnv, dws, dbs, dcw, sems):
        i = pl.program_id(0)
        ti = nt - 1 - i

        @pl.when(i == 0)
        def _():
            cps = _load_rows(abin_hbm, None, abin, sems, 0) + _load_rows(about_hbm, None, about, sems, N_DEV)
            for cp in cps:
                cp.start()
            for z in (nxt, stats, dnv, dws, dbs, dcw):
                z[...] = jnp.zeros_like(z)
            for cp in cps:
                cp.wait()

        sh, sc, gate, gn = _mod_rows(p_ref)
        x = x_ref[...]
        dxo = dxo_ref[...]
        dyb = (gate * dxo).astype(BF16)
        dy_ref[...] = dyb
        stats[2] += _colsum8(dxo * out_ref[...])
        dcat = _dot_nt(dyb, about[...])
        dya, dyb2 = dcat[:, 0:D_A], dcat[:, D_A:]

        proj = proj_ref[...]
        nvv = nv_ref[...]
        u, v, gu, rstd, vhat, vnb, wm, z = _gating(proj, nvv, ws_ref, bst_ref[...])
        dgu = dya * z
        dzb = (dya * gu).astype(BF16)
        dz32 = dya * gu
        rows = []
        for n in range(TM // CHUNK):
            blocks = []
            for hd in range(A_HEADS):
                sl = (slice(n * CHUNK, (n + 1) * CHUNK), slice(hd * CHUNK, (hd + 1) * CHUNK))
                dbs[hd] += dz32[sl]
                dws[hd] += _dot_nt(dzb[sl], vnb[sl])
                blocks.append(_dot_tn(wm[hd], dzb[sl]))
            rows.append(jnp.concatenate(blocks, axis=1))
        dvn = jnp.concatenate(rows, axis=0)
        dnv[...] += _colsum8(dvn * vhat)
        dvh = dvn * nvv
        dgv = rstd * (dvh - jnp.mean(dvh, axis=-1, keepdims=True) - vhat * jnp.mean(dvh * vhat, axis=-1, keepdims=True))
        du = dgu * _gelu_grad(u)
        dv = dgv * _gelu_grad(v)

        halo = halo_ref[...]
        prev_xp = jnp.where(ti > 0, halo[:, 2 * D_A + D_B:2 * D_A + 2 * D_B] * halo[:, 2 * D_A + 2 * D_B:], 0.0)
        cwv = cw_ref[...]
        bg, cg, xb, xp, x1, x2, conv = _conv(proj, cwv, prev_xp)
        dbg = dyb2 * conv
        dconv = dyb2 * bg
        dcw[...] += jnp.concatenate(
            [jnp.sum(_colsum8(dconv * t), axis=0, keepdims=True) for t in (x2, x1, xp)] + [jnp.zeros((5, D_B), F32)], axis=0)
        nx = nxt[...]
        dxp = cwv[2:3, :] * dconv + cwv[1:2, :] * _shift_up(dconv, 1, nx) + cwv[0:1, :] * _shift_up(dconv, 2, nx)
        nxt[...] = dconv[0:CONV_HALO, :]
        dcg = dxp * xb
        dxb = dxp * cg

        dproj = jnp.concatenate([du, dv, dbg, dcg, dxb], axis=1).astype(BF16)
        for k in range(npj):
            dproj_ref[k] = dproj[:, k * HC:(k + 1) * HC]
        cat = jnp.concatenate([gu * z, bg * conv], axis=1).astype(BF16)
        for k in range(D // HC):
            cat_ref[k] = cat[:, k * HC:(k + 1) * HC]
        dh = _dot_nn(dproj, abin[...])
        dx_ref[...] = dxo + _modulate_bwd(dh, x, gn, sc, stats)

        @pl.when(i == nt - 1)
        def _():
            _stats_out(stats, st_ref)
            dnv_ref[...] = jnp.concatenate([jnp.sum(dnv[...], axis=0, keepdims=True), jnp.zeros((7, D_A), F32)], axis=0)
            r = lax.broadcasted_iota(jnp.int32, (CHUNK, CHUNK), 0)
            c = lax.broadcasted_iota(jnp.int32, (CHUNK, CHUNK), 1)
            for hd in range(A_HEADS):
                dws_ref[hd] = jnp.where(r >= c, dws[hd], 0.0)
                dbs_ref[hd] = jnp.broadcast_to(jnp.sum(dbs[hd], axis=1, keepdims=True), (CHUNK, CHUNK))
            dcw_ref[...] = dcw[...]

    rev = pl.BlockSpec((TM, D), lambda i: (nt - 1 - i, 0))
    small = pl.BlockSpec((8, D), lambda i: (0, 0))
    full = lambda a: pl.BlockSpec(a.shape, lambda i: (0,) * a.ndim)
    hpt = TM // CONV_HALO
    fixed = lambda shape: pl.BlockSpec(shape, lambda i: (0,) * len(shape))
    return _run(
        body, name="ab_bwd", grid=(nt,),
        out_shape=[jax.ShapeDtypeStruct((s, D), F32), jax.ShapeDtypeStruct((npj, s, HC), BF16),
                   jax.ShapeDtypeStruct((D // HC, s, HC), BF16), jax.ShapeDtypeStruct((s, D), BF16),
                   jax.ShapeDtypeStruct((8, D), F32), jax.ShapeDtypeStruct((8, D_A), F32),
                   jax.ShapeDtypeStruct((A_HEADS, CHUNK, CHUNK), F32), jax.ShapeDtypeStruct((A_HEADS, CHUNK, CHUNK), F32),
                   jax.ShapeDtypeStruct((8, D_B), F32)],
        in_specs=[rev, rev, small,
                  pl.BlockSpec((TM, D_AB), lambda i: (nt - 1 - i, 0)),
                  pl.BlockSpec((CONV_HALO, D_AB), lambda i: (jnp.maximum((nt - 1 - i) * hpt - 1, 0), 0)),
                  rev, ANY, ANY, full(nv), full(ws), full(bst), full(cw)],
        out_specs=[rev, pl.BlockSpec((npj, TM, HC), lambda i: (0, nt - 1 - i, 0)),
                   pl.BlockSpec((D // HC, TM, HC), lambda i: (0, nt - 1 - i, 0)), rev,
                   small, fixed((8, D_A)), fixed((A_HEADS, CHUNK, CHUNK)), fixed((A_HEADS, CHUNK, CHUNK)), fixed((8, D_B))],
        scratch_shapes=[pltpu.VMEM((D_AB, D), BF16), pltpu.VMEM((D, D), BF16), pltpu.VMEM((CONV_HALO, D_B), F32),
                        pltpu.VMEM((4, 8, D), F32), pltpu.VMEM((8, D_A), F32),
                        pltpu.VMEM((A_HEADS, CHUNK, CHUNK), F32), pltpu.VMEM((A_HEADS, CHUNK, CHUNK), F32),
                        pltpu.VMEM((8, D_B), F32), pltpu.SemaphoreType.DMA((2 * N_DEV,))],
        args=(dxo, x, p, proj, proj, out, abin_all, about_all, nv, ws, bst, cw), rider=rider)


def _pool_counts(first_token, rows):
    t = (first_token + lax.broadcasted_iota(jnp.int32, (rows, 1), 0) + 1).astype(F32)
    lane = lax.broadcasted_iota(jnp.int32, (1, D), 1)
    w = jnp.where(lane < POOL_G, 2.0, jnp.where(lane < 2 * POOL_G, 4.0, jnp.where(lane < 3 * POOL_G, 8.0, 16.0)))
    return jnp.minimum(t, w)


def _window_sums(ext, n_keep, lead, back):
    n = ext.shape[0]
    sh = (lambda v, k: pltpu.roll(v, k, 0)) if back else (lambda v, k: pltpu.roll(v, n - k, 0))
    s2 = ext + sh(ext, 1)
    s4 = s2[:, POOL_G:] + sh(s2[:, POOL_G:], 2)
    s8 = s4[:, POOL_G:] + sh(s4[:, POOL_G:], 4)
    s16 = s8[:, POOL_G:] + sh(s8[:, POOL_G:], 8)
    keep = slice(lead, lead + n_keep)
    return jnp.concatenate([s2[keep, 0:POOL_G], s4[keep, 0:POOL_G], s8[keep, 0:POOL_G], s16[keep, :]], axis=1)


def _pool_fwd(x, p, pool_all, pscale):
    s = x.shape[0]
    nt = s // TM
    ng = D // POOL_G

    def body(x_ref, p_ref, wg_ref, ps_ref, xo_ref, pb_ref, op_ref, prev):
        i = pl.program_id(0)

        @pl.when(i == 0)
        def _():
            prev[...] = jnp.zeros_like(prev)

        sh, sc, gate, gn = _mod_rows(p_ref)
        x = x_ref[...]
        h = _modulate(x, gn, sh, sc)
        win = _window_sums(jnp.concatenate([prev[...], h], axis=0), TM, POOL_HALO, True)
        prev[...] = h[TM - POOL_HALO:, :]
        pb = (win / _pool_counts(i * TM, TM) - h).astype(BF16)
        pb_ref[...] = pb
        op = jnp.concatenate(
            [_dot_nn(pb[:, g * POOL_G:(g + 1) * POOL_G], wg_ref[:, g].reshape(POOL_G, POOL_G)) for g in range(ng)], axis=1)
        op_ref[...] = op
        xo_ref[...] = x + gate * (op * ps_ref[...])

    tile = pl.BlockSpec((TM, D), lambda i: (i, 0))
    return pl.pallas_call(
        body, name="pool_fwd", grid=(nt,),
        out_shape=[jax.ShapeDtypeStruct((s, D), F32), jax.ShapeDtypeStruct((s, D), BF16), jax.ShapeDtypeStruct((s, D), F32)],
        in_specs=[tile, pl.BlockSpec((8, D), lambda i: (0, 0)),
                  pl.BlockSpec(pool_all.shape, lambda i: (0, 0, 0, 0)), pl.BlockSpec((1, D), lambda i: (0, 0))],
        out_specs=[tile, tile, tile],
        scratch_shapes=[pltpu.VMEM((POOL_HALO, D), F32)],
        compiler_params=ARB1,
    )(x, p, pool_all, pscale)


def _pool_bwd(dxo, x, p, pb, op, pool_all, pscale):
    s = x.shape[0]
    nt = s // TM
    ng = D // POOL_G

    def body(dxo_ref, x_ref, p_ref, pb_ref, op_ref, wg_ref, ps_ref,
             dx_ref, st_ref, dps_ref, dwg_ref, nxt, stats, dps, dwg):
        i = pl.program_id(0)
        ti = nt - 1 - i

        @pl.when(i == 0)
        def _():
            for z in (nxt, stats, dps, dwg):
                z[...] = jnp.zeros_like(z)

        sh, sc, gate, gn = _mod_rows(p_ref)
        x = x_ref[...]
        dxo = dxo_ref[...]
        ps = ps_ref[...]
        op = op_ref[...]
        dmo = gate * dxo
        stats[2] += _colsum8(dxo * (op * ps))
        dps[...] += _colsum8(dmo * op)
        dopb = (dmo * ps).astype(BF16)
        pbv = pb_ref[...]
        dps_parts = []
        for g in range(ng):
            sl = slice(g * POOL_G, (g + 1) * POOL_G)
            dps_parts.append(_dot_nt(dopb[:, sl], wg_ref[:, g].reshape(POOL_G, POOL_G)))
            dwg[g] += _dot_tn(pbv[:, sl], dopb[:, sl])
        dp = jnp.concatenate(dps_parts, axis=1)
        q = dp / _pool_counts(ti * TM, TM)
        wsum = _window_sums(jnp.concatenate([q, nxt[...]], axis=0), TM, 0, False)
        nxt[...] = q[0:POOL_HALO, :]
        dx_ref[...] = dxo + _modulate_bwd(wsum - dp, x, gn, sc, stats)

        @pl.when(i == nt - 1)
        def _():
            _stats_out(stats, st_ref)
            dps_ref[...] = jnp.concatenate([jnp.sum(dps[...], axis=0, keepdims=True), jnp.zeros((7, D), F32)], axis=0)
            dwg_ref[...] = dwg[...].astype(BF16)

    rev = pl.BlockSpec((TM, D), lambda i: (nt - 1 - i, 0))
    small = pl.BlockSpec((8, D), lambda i: (0, 0))
    return pl.pallas_call(
        body, name="pool_bwd", grid=(nt,),
        out_shape=[jax.ShapeDtypeStruct((s, D), F32), jax.ShapeDtypeStruct((8, D), F32), jax.ShapeDtypeStruct((8, D), F32),
                   jax.ShapeDtypeStruct((ng, POOL_G, POOL_G), BF16)],
        in_specs=[rev, rev, small, rev, rev,
                  pl.BlockSpec(pool_all.shape, lambda i: (0, 0, 0, 0)), pl.BlockSpec((1, D), lambda i: (0, 0))],
        out_specs=[rev, small, small, pl.BlockSpec((ng, POOL_G, POOL_G), lambda i: (0, 0, 0))],
        scratch_shapes=[pltpu.VMEM((POOL_HALO, D), F32), pltpu.VMEM((4, 8, D), F32), pltpu.VMEM((8, D), F32),
                        pltpu.VMEM((ng, POOL_G, POOL_G), F32)],
        compiler_params=ARB1,
    )(dxo, x, p, pb, op, pool_all, pscale)


def _adamw_math(w, g, m, v):
    m = ADAM_B1 * m + (1.0 - ADAM_B1) * g
    v = ADAM_B2 * v + (1.0 - ADAM_B2) * (g * g)
    m_hat = m / (1.0 - ADAM_B1 ** ADAM_STEP)
    v_hat = v / (1.0 - ADAM_B2 ** ADAM_STEP)
    delta = -ADAM_LR * (m_hat / (jnp.sqrt(v_hat) + ADAM_EPS) + ADAM_WD * w)
    return delta, m, v


def _finish(parts, w, m, v, rb, name, halves=False, rider=None):
    nf, r, c = w.shape
    npart = parts[0].shape[0]

    def body(*refs):
        p_refs = refs[:nf]
        w_ref, m_ref, v_ref, g_ref, d_ref, mo_ref, vo_ref = refs[nf:]
        for f in range(nf):
            @pl.when(pl.program_id(0) == f)
            def _():
                g = p_refs[f][0].astype(F32)
                for k in range(1, npart):
                    g = g + p_refs[f][k].astype(F32)
                if halves:
                    g = jnp.concatenate([g[0], g[1]], axis=1)
                g_ref[0] = g
                d_ref[0], mo_ref[0], vo_ref[0] = _adamw_math(w_ref[0], g, m_ref[0], v_ref[0])

    blk = pl.BlockSpec((1, rb, c), lambda f, i: (f, i, 0))

    def pblk(mine):
        if halves:
            return pl.BlockSpec((npart, 2, rb, c // 2), lambda f, i: (0, 0, jnp.where(f == mine, i, 0), 0))
        return pl.BlockSpec((npart, rb, c), lambda f, i: (0, jnp.where(f == mine, i, 0), 0))

    return _run(
        body, name=name, grid=(nf, r // rb),
        out_shape=[jax.ShapeDtypeStruct(w.shape, F32)] * 4,
        in_specs=[pblk(f) for f in range(nf)] + [blk, blk, blk], out_specs=[blk] * 4, scratch_shapes=[],
        args=(*parts, w, m, v), rider=rider,
        params=pltpu.CompilerParams(dimension_semantics=("arbitrary", "arbitrary"), vmem_limit_bytes=VMEM_LIMIT))


def _sum_small(parts, late, late_rows, narrow):
    def body(p_ref, l_ref, n_ref, o_ref, on_ref):
        acc, acc_l, acc_n = p_ref[0], l_ref[0], n_ref[0].astype(F32)
        for k in range(1, N_DEV):
            acc, acc_l, acc_n = acc + p_ref[k], acc_l + l_ref[k], acc_n + n_ref[k].astype(F32)
        o_ref[...] = acc
        for a, b, r in late_rows:
            o_ref[r:r + b - a, :] += acc_l[a:b, :]
        on_ref[...] = acc_n

    return pl.pallas_call(
        body, name="sum_small",
        out_shape=[jax.ShapeDtypeStruct(parts.shape[1:], F32), jax.ShapeDtypeStruct(narrow.shape[1:], F32)],
        in_specs=[VMEM_SPEC] * 3, out_specs=[VMEM_SPEC] * 2)(parts, late, narrow)


def _adamw(w, g, m, v, name):
    def body(w_ref, g_ref, m_ref, v_ref, d_ref, mo_ref, vo_ref):
        d_ref[...], mo_ref[...], vo_ref[...] = _adamw_math(w_ref[...], g_ref[...], m_ref[...], v_ref[...])

    return pl.pallas_call(
        body, name=name, out_shape=[jax.ShapeDtypeStruct(w.shape, F32)] * 3,
        in_specs=[VMEM_SPEC] * 4, out_specs=[VMEM_SPEC] * 3,
    )(w, g, m, v)


def _wmod_finish(act_t, dmod_cols, w, m, v):
    rb = 256
    ncol = w.shape[-1]

    def body(a_ref, dm_ref, w_ref, m_ref, v_ref, g_ref, d_ref, mo_ref, vo_ref):
        g = a_ref[:, 0:1] * dm_ref[0, 0:1, :]
        for k in range(1, N_DEV):
            g = g + a_ref[:, k:k + 1] * dm_ref[0, k:k + 1, :]
        g_ref[0] = g
        d_ref[0], mo_ref[0], vo_ref[0] = _adamw_math(w_ref[0], g, m_ref[0], v_ref[0])

    blk = pl.BlockSpec((1, rb, ncol), lambda l, i: (l, i, 0))
    return pl.pallas_call(
        body, name="wmod_finish", grid=(2, D // rb),
        out_shape=[jax.ShapeDtypeStruct(w.shape, F32)] * 4,
        in_specs=[pl.BlockSpec((rb, N_DEV), lambda l, i: (i, 0)), pl.BlockSpec((1, N_DEV, ncol), lambda l, i: (l, 0, 0)),
                  blk, blk, blk],
        out_specs=[blk] * 4,
        compiler_params=pltpu.CompilerParams(dimension_semantics=("arbitrary", "arbitrary"), vmem_limit_bytes=VMEM_LIMIT),
    )(act_t, dmod_cols, w, m, v)


def _pack(pieces):
    flat, offs, at = [], [], 0
    for a in pieces:
        a = a.reshape(-1)
        n = -(-a.shape[0] // 128) * 128
        flat.append(jnp.pad(a, (0, n - a.shape[0])))
        offs.append(at)
        at += n
    return jnp.concatenate(flat).reshape(-1, 128), offs


def _param_block(mod_l, sub, gn):
    return jnp.concatenate([mod_l[sub], gn[None, :], jnp.zeros((4, D), F32)], axis=0)


def kernel(x, c, norm_g, w_mod, b_mod, w_ffn_in, w_ffn_out, ab_w_in, ab_norm_v, ab_w_s, ab_b_s, ab_conv_w, ab_w_out, pool_w_grp, pool_scale, final_g, loss_target, m_norm_g, m_w_mod, m_b_mod, m_w_ffn_in, m_w_ffn_out, m_ab_w_in, m_ab_norm_v, m_ab_w_s, m_ab_b_s, m_ab_conv_w, m_ab_w_out, m_pool_w_grp, m_pool_scale, m_final_g, v_norm_g, v_w_mod, v_b_mod, v_w_ffn_in, v_w_ffn_out, v_ab_w_in, v_ab_norm_v, v_ab_w_s, v_ab_b_s, v_ab_conv_w, v_ab_w_out, v_pool_w_grp, v_pool_scale, v_final_g):
    me = 4 * lax.axis_index("x") + 2 * lax.axis_index("y") + lax.axis_index("c")
    x0 = x[0]
    tgt = loss_target[0]
    n_in = w_ffn_in.shape[-1]
    n_out = w_ffn_out.shape[-2]
    n_abin = ab_w_in.shape[-1]
    n_about = ab_w_out.shape[-2]
    n_pool = pool_w_grp.shape[-2]
    n_mod = w_mod.shape[-1]
    n_ng = norm_g.shape[-1]
    n_cw = ab_conv_w.shape[-1]
    n_ps = pool_scale.shape[-1]

    win_sh = jnp.swapaxes(w_ffn_in.reshape(4, D, n_in), 1, 2).astype(BF16)
    wout_sh = w_ffn_out.reshape(4, n_out, D).astype(BF16)
    abin_sh = ab_w_in[0].T.astype(BF16)
    about_sh = ab_w_out[0].astype(BF16)
    pool_sh = pool_w_grp[0].astype(BF16)
    win, wout = [None] * 4, [None] * 4

    pack, offs = _pack([c, norm_g, ab_conv_w, pool_scale])
    got, (win[0],) = _exchange_small(pack, "gather_small", _GatherRider([(win_sh, 0)]))
    got = got.reshape(N_DEV, -1)
    c_all = got[:, offs[0]:offs[0] + D]
    ng_full = got[:, offs[1]:offs[1] + 6 * n_ng].reshape(N_DEV, 2, 3, n_ng).transpose(1, 2, 0, 3).reshape(2, 3, D)
    cw_full = got[:, offs[2]:offs[2] + 3 * n_cw].reshape(N_DEV, 3, n_cw).transpose(1, 0, 2).reshape(3, D_B)
    ps_full = got[:, offs[3]:offs[3] + n_ps].reshape(1, D)

    act_all, mod_cols = _mod_fwd(c_all, w_mod)
    mod_got, (wout[0],) = _exchange_small(mod_cols.reshape(-1, 128), "gather_mod", _GatherRider([(wout_sh, 0)]))
    mod_got = mod_got.reshape(N_DEV, 2, N_DEV, n_mod)
    mod = lax.dynamic_index_in_dim(mod_got, me, axis=2, keepdims=False).transpose(1, 0, 2).reshape(2, 9 * D) + b_mod
    mod = mod.reshape(2, 3, 3, D)
    nv = ab_norm_v
    ws = ab_w_s[0]
    bst = ab_b_s[0].T
    cw8 = jnp.concatenate([cw_full, jnp.zeros((5, D_B), F32)], axis=0)

    pb = [[_param_block(mod[l], s, ng_full[l, s]) for s in range(3)] for l in range(2)]
    fwd_pair = lambda a, h: _SiblingRider([(a, None), (h, None)])
    (x1, h00, g00, u00, y00, a00), (abin_all, about_all, win[1]) = _ffn_fwd(
        x0, pb[0][0], win[0], wout[0], 0, _GatherRider([abin_sh, about_sh, (win_sh, 1)]))
    ride = _Riders([fwd_pair(a00, h00), _GatherRider([(wout_sh, 1)])])
    (x2, h01, proj, ab_out), rode = _ab_fwd(x1, pb[0][1], abin_all, about_all, nv, ws, bst, cw8, ride)
    sib0, (wout[1],) = ride.split(rode)
    (x3, h02, g02, u02, y02, a02), (win[2], wout[2]) = _ffn_fwd(
        x2, pb[0][2], win[1], wout[1], 1, _GatherRider([(win_sh, 2), (wout_sh, 2)]))
    ride = _Riders([fwd_pair(a02, h02), _GatherRider([pool_sh, (win_sh, 3), (wout_sh, 3)])])
    (x4, h10, g10, u10, y10, a10), rode = _ffn_fwd(x3, pb[1][0], win[2], wout[2], 2, ride)
    sib1, (pool_all, win[3], wout[3]) = ride.split(rode)
    x5, pooled, pool_out = _pool_fwd(x4, pb[1][1], pool_all, ps_full)
    (dx6, h12, g12, u12, y12, a12, loss_blk, dfg), sib2 = _ffn_fwd(
        x5, pb[1][2], win[3], wout[3], 3, fwd_pair(a10, h10), head=(final_g.reshape(1, D), tgt))

    core = lax.axis_index("c")
    sel_rows = lambda n: jnp.stack([core * n, 0]).astype(jnp.int32)
    sel_cols = jnp.stack([0, core]).astype(jnp.int32)
    p_in, p_out = [None] * 4, [None] * 4

    def exchange(dgu, dy):
        return _SiblingRider([(dgu, NCH), (dy, None)])

    def ffn_wgrads(f, dgu, a, h, dy, got, sib, ride_out=None):
        (s_dgu, s_dy), (s_a, s_h) = got, sib
        g_out, rode = _wgrad_pair(a, s_a, dy, s_dy, sel_cols, NCH, True, f"wgrad_out_{f}", ride_out)
        g_in, (p_out[f],) = _wgrad_pair(dgu, s_dgu, h, s_h, sel_rows(NCH), NCH, False, f"wgrad_in_{f}",
                                        _ChipScatterRider([], [g_out]))
        return g_in, rode

    (dx5, dgu12, dy12, st12), sib3 = _ffn_bwd(dx6, x5, pb[1][2], g12, u12, y12, win[3], wout[3], 3, fwd_pair(a12, h12))
    dx4, st11, dps, gw_pool = _pool_bwd(dx5, x4, pb[1][1], pooled, pool_out, pool_all, ps_full)
    (dx3, dgu10, dy10, st10), got3 = _ffn_bwd(dx4, x3, pb[1][0], g10, u10, y10, win[2], wout[2], 2, exchange(dgu12, dy12))
    g_in3, (p_pool,) = ffn_wgrads(3, dgu12, a12, h12, dy12, got3, sib3, _ScatterRider([gw_pool]))
    ride = _Riders([exchange(dgu10, dy10), _ChipScatterRider([g_in3], [])])
    (dx2, dgu02, dy02, st02), rode = _ffn_bwd(dx3, x2, pb[0][2], g02, u02, y02, win[1], wout[1], 1, ride)
    got2, (p_in[3],) = ride.split(rode)
    g_in2, _ = ffn_wgrads(2, dgu10, a10, h10, dy10, got2, sib2)
    ride = _Riders([exchange(dgu02, dy02), _ChipScatterRider([g_in2], [])])
    (dx1, dproj, cat, dy01, st01, dnv, dws, dbs, dcw), rode = _ab_bwd(
        dx2, x1, pb[0][1], proj, ab_out, abin_all, about_all, nv, ws, bst, cw8, ride)
    got1, (p_in[2],) = ride.split(rode)
    g_in1, _ = ffn_wgrads(1, dgu02, a02, h02, dy02, got1, sib1)
    zero_stats = jnp.zeros((8, D), F32)
    stats = [[zero_stats, st01, st02], [st10, st11, st12]]
    dmod = jnp.stack([jnp.concatenate([stats[l][s][0:3].reshape(-1) for s in range(3)]) for l in range(2)])
    dng = jnp.stack([jnp.stack([stats[l][s][3] for s in range(3)]) for l in range(2)])
    spack, so = _pack([dmod, dng, dnv[0], dbs[:, :, 0], dcw[0:3], dps[0], dfg[0], loss_blk[0]])
    ride = _Riders([_SiblingRider([(dproj, 5), (cat, 2), (h01, None), (dy01, None)]), _SmallGatherRider(spack),
                    _SmallGatherRider(dws.astype(BF16))])
    (dgu00, dy00, gst00), rode = _ffn_bwd_acts(dx1, pb[0][0], g00, u00, y00, wout[0], 0, ride)
    (s_dproj, s_cat, s_h01, s_dy01), (sgot,), (got_ws,) = ride.split(rode)
    g_about, _ = _wgrad_pair(cat, s_cat, dy01, s_dy01, sel_rows(2), 2, False, "wgrad_ab_out")
    g_abin, (p_about,) = _wgrad_pair(dproj, s_dproj, h01, s_h01, sel_rows(5), 5, False, "wgrad_ab_in",
                                     _ChipScatterRider([g_about], []))
    ride = _Riders([exchange(dgu00, dy00), _ChipScatterRider([g_in1], [])])
    (dx0, st00), rode = _ffn_bwd_dx(dx1, x0, pb[0][0], dgu00, gst00, win[0], 0, ride)
    got0, (p_in[1],) = ride.split(rode)
    grad_x = dx0[None]
    ride = _Riders([_ChipScatterRider([g_abin], []), _SmallGatherRider(st00[0:4].reshape(-1, 128))])
    g_in0, rode = ffn_wgrads(0, dgu00, a00, h00, dy00, got0, sib0, ride)
    (p_abin,), (sgot0,) = ride.split(rode)

    shape_in, shape_out = w_ffn_in.shape, w_ffn_out.shape
    fin = lambda a: jnp.swapaxes(a.reshape(4, D, n_in), 1, 2)
    fabin = lambda a: jnp.swapaxes(a, 1, 2)
    fout = lambda a: a.reshape(4, n_out, D)
    fpool = lambda a: a.reshape(1, 4 * n_pool, POOL_G)
    r_out, (p_in[0],) = _finish(p_out, fout(w_ffn_out), fout(m_w_ffn_out), fout(v_w_ffn_out), n_out // 2, "finish_ffn_out",
                                halves=True, rider=_ChipScatterRider([g_in0], []))
    r_in, _ = _finish(p_in, fin(w_ffn_in), fin(m_w_ffn_in), fin(v_w_ffn_in), n_in // 4, "finish_ffn_in")
    r_abin, _ = _finish([p_abin], fabin(ab_w_in), fabin(m_ab_w_in), fabin(v_ab_w_in), n_abin, "finish_ab_in")
    r_about, _ = _finish([p_about], ab_w_out, m_ab_w_out, v_ab_w_out, n_about, "finish_ab_out")
    r_pool, _ = _finish([p_pool.reshape(N_DEV, 4 * n_pool, POOL_G)], fpool(pool_w_grp), fpool(m_pool_w_grp),
                        fpool(v_pool_w_grp), 4 * n_pool, "finish_pool")
    r_in = [jnp.swapaxes(a, 1, 2).reshape(shape_in) for a in r_in]
    r_abin = [jnp.swapaxes(a, 1, 2) for a in r_abin]
    r_out = [a.reshape(shape_out) for a in r_out]
    r_pool = [a.reshape(pool_w_grp.shape) for a in r_pool]

    rows_mod, rows_gn = 3 * D // 128, D // 128
    ssum, g_ws = _sum_small(sgot, sgot0, [(0, rows_mod, so[0] // 128), (rows_mod, rows_mod + rows_gn, so[1] // 128)], got_ws)
    ssum = ssum.reshape(-1)
    loss = ssum[so[7]]
    take = lambda i, n: lax.dynamic_slice_in_dim(ssum, so[i], n)
    g_bmod = take(0, 2 * 9 * D).reshape(2, 9 * D)
    g_ng = lax.dynamic_slice_in_dim(take(1, 6 * D).reshape(2, 3, D), me * n_ng, n_ng, axis=2)
    g_nv = take(2, D_A).reshape(1, D_A)
    g_ws = g_ws[None]
    g_bs = take(3, A_HEADS * CHUNK).reshape(1, A_HEADS, CHUNK)
    g_cw = lax.dynamic_slice_in_dim(take(4, 3 * D_B).reshape(1, 3, D_B), me * n_cw, n_cw, axis=2)
    g_ps = lax.dynamic_slice_in_dim(take(5, D).reshape(1, D), me * n_ps, n_ps, axis=1)
    g_fg = take(6, D)

    dmod_all = sgot.reshape(N_DEV, -1)[:, so[0]:so[0] + 2 * 9 * D].reshape(N_DEV, 2, 9 * D)
    dmod_all = dmod_all.at[:, 0, 0:3 * D].set(sgot0[:, 0:rows_mod].reshape(N_DEV, 3 * D))
    dmod_cols = lax.dynamic_slice_in_dim(dmod_all, me * n_mod, n_mod, axis=2).transpose(1, 0, 2)
    r_wmod = _wmod_finish(act_all.T, dmod_cols, w_mod, m_w_mod, v_w_mod)

    small_w = [b_mod, norm_g, ab_norm_v, ab_w_s, ab_b_s, ab_conv_w, pool_scale, final_g]
    small_g = [g_bmod, g_ng, g_nv, g_ws, g_bs, g_cw, g_ps, g_fg]
    small_m = [m_b_mod, m_norm_g, m_ab_norm_v, m_ab_w_s, m_ab_b_s, m_ab_conv_w, m_pool_scale, m_final_g]
    small_v = [v_b_mod, v_norm_g, v_ab_norm_v, v_ab_w_s, v_ab_b_s, v_ab_conv_w, v_pool_scale, v_final_g]
    pw, po = _pack(small_w)
    pv = jnp.concatenate([jnp.pad(a.reshape(-1), (0, -a.size % 128), constant_values=1.0) for a in small_v]).reshape(-1, 128)
    sd, sm, sv = _adamw(pw, _pack(small_g)[0], _pack(small_m)[0], pv, "adamw_small")
    unpack = lambda packed: [packed.reshape(-1)[po[i]:po[i] + a.size].reshape(a.shape) for i, a in enumerate(small_w)]
    d_s, m_s, v_s = unpack(sd), unpack(sm), unpack(sv)

    def ordered(k, small):
        return [small[1], r_wmod[k], small[0], r_in[k], r_out[k], r_abin[k], small[2], small[3], small[4], small[5],
                r_about[k], r_pool[k], small[6], small[7]]

    grads = ordered(0, small_g)
    deltas = ordered(1, d_s)
    new_m = ordered(2, m_s)
    new_v = ordered(3, v_s)
    return (loss, grad_x, *grads, *deltas, *new_m, *new_v)
```
